```python
import jax
import jax.numpy as jnp
from jax import lax
import numpy as np

D_MODEL = 1024
BATCH = 8
SEQ = 8192
DEPTH = 2
DEC_BATCH = 4
DEC_SEQ = 8192
PAST_LEN = 128

HG_DK = 128
HG_DV = 128
HG_WIDTH = D_MODEL // 2
HG_HEADS = HG_WIDTH // HG_DK
HG_CHUNK = 64
ATT_HD = 64
ATT_WIDTH = D_MODEL // 2
ATT_HEADS = ATT_WIDTH // ATT_HD
ATT_KV = 2
ATT_GROUP = ATT_HEADS // ATT_KV
KV_WIDTH = ATT_KV * ATT_HD
WINDOW = 128
ATT_BLOCK = 128
N_BUCKETS = 32
REL_MAX_DIST = 128
N_EXPERTS = 16
N_GROUPS = 4
EXPERTS_PER_GROUP = N_EXPERTS // N_GROUPS
TOP_K = 2
D_EXPERT = D_MODEL // 2
N_MOD = 6
IN_COLS = 5 * HG_WIDTH + ATT_WIDTH + 2 * KV_WIDTH + 2 * D_MODEL
EPS = 1e-6
NEG_BIG = -1e30
TINY = 1e-30

kernel_name = "hybrid_hgrn2_swa_moe_encoder"


def _rmsnorm(x, g):
    xf = x.astype(jnp.float32)
    y = xf * lax.rsqrt(jnp.mean(xf * xf, axis=-1, keepdims=True) + EPS)
    return (y * g.astype(jnp.float32)).astype(x.dtype)


def _lower_bounds(gamma):
    p = jax.nn.softmax(gamma.astype(jnp.float32), axis=0)
    return jnp.clip(jnp.cumsum(p, axis=0) - p[0], 0.0, 1.0)


def _forget(z, lb):
    z = z.astype(jnp.float32)
    f = lb + (1.0 - lb) * jax.nn.sigmoid(z)
    log_f = jnp.log(jnp.maximum(f, TINY))
    k = (1.0 - lb) * jax.nn.sigmoid(-z)
    return log_f, k


def _hgrn_chunk_scan(q, k, v, log_f):
    B, S, H, DK = q.shape
    DV = v.shape[-1]
    C = HG_CHUNK
    n = S // C

    def chunks(a):
        return a.reshape(B, n, C, H, a.shape[-1]).transpose(1, 0, 3, 2, 4)

    lower = jnp.asarray(np.tril(np.ones((C, C), dtype=bool)))[:, :, None]

    def step(state, inp):
        qc, kc, vc, gc = inp
        b = jnp.cumsum(gc, axis=2)
        o_inter = jnp.einsum("bhtd,bhdv->bhtv", qc * jnp.exp(b), state)
        diff = b[:, :, :, None, :] - b[:, :, None, :, :]
        decay = jnp.exp(jnp.where(lower, diff, NEG_BIG))
        a = jnp.einsum("bhtsd,bhsd->bhts", decay * qc[:, :, :, None, :], kc)
        o_intra = jnp.einsum("bhts,bhsv->bhtv", a, vc)
        b_last = b[:, :, -1, :]
        k_dec = kc * jnp.exp(b_last[:, :, None, :] - b)
        state = jnp.exp(b_last)[..., None] * state + jnp.einsum("bhsd,bhsv->bhdv", k_dec, vc)
        return state, o_inter + o_intra

    s0 = jnp.zeros((B, H, DK, DV), jnp.float32)
    _, o = lax.scan(step, s0, (chunks(q), chunks(k), chunks(v), chunks(log_f)))
    return o.transpose(1, 0, 3, 2, 4).reshape(B, S, H, DV)


def _t5_buckets(rel):
    half = N_BUCKETS // 2
    ret = np.where(rel > 0, half, 0)
    n = np.abs(rel)
    max_exact = half // 2
    large = max_exact + (np.log(np.maximum(n, 1) / max_exact)
                         / np.log(REL_MAX_DIST / max_exact) * (half - max_exact)).astype(np.int32)
    large = np.minimum(large, half - 1)
    return (ret + np.where(n < max_exact, n, large)).astype(np.int32)


def _window_attention(q, k, v, rel_bias, sink):
    B, S = q.shape[0], q.shape[1]
    C = ATT_BLOCK
    nb = S // C
    qb = q.reshape(B, nb, C, ATT_KV, ATT_GROUP, ATT_HD)

    def band(a):
        ap = jnp.pad(a, ((0, 0), (C, C), (0, 0), (0, 0)))
        ab = ap.reshape(B, nb + 2, C, ATT_KV, ATT_HD)
        return jnp.concatenate([ab[:, :-2], ab[:, 1:-1], ab[:, 2:]], axis=2)

    kb, vb = band(k), band(v)
    rel = np.arange(3 * C)[None, :] - C - np.arange(C)[:, None]
    kpos = np.arange(nb)[:, None] * C - C + np.arange(3 * C)[None, :]
    mask = (np.abs(rel) <= WINDOW)[None] & ((kpos >= 0) & (kpos < S))[:, None, :]
    bias = rel_bias.astype(jnp.float32)[_t5_buckets(rel)]
    bias = bias.transpose(2, 0, 1).reshape(ATT_KV, ATT_GROUP, C, 3 * C)
    logits = jnp.einsum("bnqhgd,bnjhd->bnhgqj", qb, kb).astype(jnp.float32) * (ATT_HD ** -0.5) + bias
    logits = jnp.where(mask[None, :, None, None], logits, NEG_BIG)
    s = sink.astype(jnp.float32).reshape(ATT_KV, ATT_GROUP)[:, :, None, None]
    m = jnp.maximum(jnp.max(logits, axis=-1, keepdims=True), s)
    p = jnp.exp(logits - m)
    probs = p / (jnp.sum(p, axis=-1, keepdims=True) + jnp.exp(s - m))
    o = jnp.einsum("bnhgqj,bnjhd->bnqhgd", probs.astype(v.dtype), vb)
    return o.reshape(B, S, ATT_WIDTH)


def _mixer(h, l, w_in, hg_lb_fwd, hg_lb_bwd, hg_norm, att_sink, rel_bias, w_br_hgrn, w_br_att, w_out):
    B, S, _ = h.shape
    proj = jnp.einsum("bsd,de->bse", h, w_in[l])
    sizes = [HG_WIDTH] * 5 + [ATT_WIDTH, KV_WIDTH, KV_WIDTH, D_MODEL, D_MODEL]
    offs = np.cumsum(sizes)[:-1].tolist()
    hq, hf_fwd, hf_bwd, hi, hg, aq, ak, av, gate_h, gate_a = jnp.split(proj, offs, axis=-1)

    def heads(a):
        return a.reshape(B, S, HG_HEADS, HG_DK).astype(jnp.float32)

    q = jax.nn.silu(heads(hq)) * (HG_DK ** -0.5)
    v = heads(hi)
    lb_f = _lower_bounds(hg_lb_fwd)[l].reshape(HG_HEADS, HG_DK)
    lb_b = _lower_bounds(hg_lb_bwd)[l].reshape(HG_HEADS, HG_DK)
    logf_f, k_f = _forget(heads(hf_fwd), lb_f)
    logf_b, k_b = _forget(heads(hf_bwd), lb_b)
    o_f = _hgrn_chunk_scan(q, k_f, v, logf_f)

    def flip(a):
        return jnp.flip(a, axis=1)

    o_b = flip(_hgrn_chunk_scan(flip(q), flip(k_b), flip(v), flip(logf_b)))
    o = _rmsnorm(o_f + o_b, hg_norm[l]) * jax.nn.silu(heads(hg))
    o_h = o.reshape(B, S, HG_WIDTH).astype(h.dtype)

    o_a = _window_attention(aq.reshape(B, S, ATT_HEADS, ATT_HD), ak.reshape(B, S, ATT_KV, ATT_HD),
                            av.reshape(B, S, ATT_KV, ATT_HD), rel_bias, att_sink[l])

    merged = (jax.nn.sigmoid(gate_h) * jnp.einsum("bse,ed->bsd", o_h, w_br_hgrn[l])
              + jax.nn.sigmoid(gate_a) * jnp.einsum("bse,ed->bsd", o_a, w_br_att[l]))
    return jnp.einsum("bsd,de->bse", merged, w_out[l])


def _moe(h, w_router, router_bias, w_gate, w_up, w_down):
    B, S, D = h.shape
    t = h.reshape(B * S, D)
    scores = jax.nn.softmax(jnp.dot(t, w_router).astype(jnp.float32), axis=-1)
    sel = (scores + router_bias.astype(jnp.float32)).reshape(-1, N_GROUPS, EXPERTS_PER_GROUP)
    group_score = jnp.sum(lax.top_k(sel, TOP_K)[0], axis=-1)
    g = jnp.argmax(group_score, axis=-1)
    in_group = jnp.take_along_axis(sel, g[:, None, None], axis=1)[:, 0]
    _, local = lax.top_k(in_group, TOP_K)
    idx = g[:, None] * EXPERTS_PER_GROUP + local
    w = jnp.take_along_axis(scores, idx, axis=-1)
    w = w / jnp.sum(w, axis=-1, keepdims=True)
    combine = jnp.sum(jax.nn.one_hot(idx, N_EXPERTS, dtype=jnp.float32) * w[..., None], axis=1)
    y = jnp.zeros(t.shape, jnp.float32)
    for e in range(N_EXPERTS):
        he = jax.nn.silu(t @ w_gate[e]) * (t @ w_up[e])
        y = y + combine[:, e:e + 1] * (he @ w_down[e]).astype(jnp.float32)
    return y.astype(h.dtype).reshape(B, S, D)


def _trunk(x, c, w_ada, b_ada, norm_mix, norm_ffn, norm_final, w_in, hg_lb_fwd, hg_lb_bwd, hg_norm,
           att_sink, rel_bias, w_br_hgrn, w_br_att, w_out, w_router, router_bias, w_gate, w_up, w_down):
    for l in range(DEPTH):
        mod = jnp.einsum("bd,de->be", jax.nn.silu(c), w_ada[l]) + b_ada[l]
        sh1, sc1, g1, sh2, sc2, g2 = [m[:, None, :] for m in jnp.split(mod, N_MOD, axis=-1)]
        h = _rmsnorm(x, norm_mix[l]) * (1 + sc1) + sh1
        x = x + g1 * _mixer(h, l, w_in, hg_lb_fwd, hg_lb_bwd, hg_norm, att_sink, rel_bias,
                            w_br_hgrn, w_br_att, w_out)
        h = _rmsnorm(x, norm_ffn[l]) * (1 + sc2) + sh2
        x = x + g2 * _moe(h, w_router, router_bias, w_gate[l], w_up[l], w_down[l])
    return _rmsnorm(x, norm_final)


def setup_inputs(seed: int = 0) -> dict:
    key = jax.random.key(seed)
    ks = jax.random.split(key, 24)

    def nrm(k, shape, s):
        return jax.random.normal(k, shape, jnp.float32) * s

    return {
        "x_prompt": nrm(ks[0], (BATCH, SEQ, D_MODEL), 1.0),
        "x_sample": nrm(ks[1], (DEC_BATCH, DEC_SEQ, D_MODEL), 1.0),
        "c_prompt": nrm(ks[2], (BATCH, D_MODEL), 1.0),
        "c_sample": nrm(ks[3], (DEC_BATCH, D_MODEL), 1.0),
        "w_ada": nrm(ks[4], (DEPTH, D_MODEL, N_MOD * D_MODEL), 0.5 * D_MODEL ** -0.5),
        "b_ada": nrm(ks[5], (DEPTH, N_MOD * D_MODEL), 0.02),
        "norm_mix": 1.0 + nrm(ks[6], (DEPTH, D_MODEL), 0.05),
        "norm_ffn": 1.0 + nrm(ks[7], (DEPTH, D_MODEL), 0.05),
        "norm_final": 1.0 + nrm(ks[8], (D_MODEL,), 0.05),
        "w_in": nrm(ks[9], (DEPTH, D_MODEL, IN_COLS), D_MODEL ** -0.5),
        "hg_lb_fwd": nrm(ks[10], (DEPTH, HG_WIDTH), 1.0),
        "hg_lb_bwd": nrm(ks[11], (DEPTH, HG_WIDTH), 1.0),
        "hg_norm": 1.0 + nrm(ks[12], (DEPTH, HG_DV), 0.05),
        "att_sink": nrm(ks[13], (DEPTH, ATT_HEADS), 1.0),
        "rel_bias": nrm(ks[14], (N_BUCKETS, ATT_HEADS), 0.5),
        "w_br_hgrn": nrm(ks[15], (DEPTH, HG_WIDTH, D_MODEL), HG_WIDTH ** -0.5),
        "w_br_att": nrm(ks[16], (DEPTH, ATT_WIDTH, D_MODEL), ATT_WIDTH ** -0.5),
        "w_out": nrm(ks[17], (DEPTH, D_MODEL, D_MODEL), D_MODEL ** -0.5),
        "w_router": nrm(ks[18], (D_MODEL, N_EXPERTS), D_MODEL ** -0.5),
        "router_bias": nrm(ks[19], (N_EXPERTS,), 0.01),
        "w_gate": nrm(ks[20], (DEPTH, N_EXPERTS, D_MODEL, D_EXPERT), D_MODEL ** -0.5),
        "w_up": nrm(ks[21], (DEPTH, N_EXPERTS, D_MODEL, D_EXPERT), D_MODEL ** -0.5),
        "w_down": nrm(ks[22], (DEPTH, N_EXPERTS, D_EXPERT, D_MODEL), D_EXPERT ** -0.5),
    }


def reference(x_prompt, x_sample, c_prompt, c_sample, w_ada, b_ada, norm_mix, norm_ffn, norm_final,
              w_in, hg_lb_fwd, hg_lb_bwd, hg_norm, att_sink, rel_bias, w_br_hgrn, w_br_att, w_out,
              w_router, router_bias, w_gate, w_up, w_down):
    y_prompt = _trunk(x_prompt, c_prompt, w_ada, b_ada, norm_mix, norm_ffn, norm_final, w_in,
                      hg_lb_fwd, hg_lb_bwd, hg_norm, att_sink, rel_bias, w_br_hgrn, w_br_att, w_out,
                      w_router, router_bias, w_gate, w_up, w_down)
    y_sample = _trunk(x_sample, c_sample, w_ada, b_ada, norm_mix, norm_ffn, norm_final, w_in,
                      hg_lb_fwd, hg_lb_bwd, hg_norm, att_sink, rel_bias, w_br_hgrn, w_br_att, w_out,
                      w_router, router_bias, w_gate, w_up, w_down)
    return (y_prompt, y_sample)
```

```python
import functools

import numpy as np
import jax
import jax.numpy as jnp
from jax import lax
from jax.experimental import pallas as pl
from jax.experimental.pallas import tpu as pltpu

D_MODEL = 1024
HG_DK = 128
HG_WIDTH = 512
HG_HEADS = 4
HG_SUB = 64
HG_LEVELS = 6
ATT_HD = 64
ATT_HEADS = 8
ATT_KV = 2
ATT_GROUP = 4
ATT_WIDTH = 512
KV_WIDTH = 128
WINDOW = 128
ATT_BLOCK = 128
N_BUCKETS = 32
REL_MAX_DIST = 128
N_EXPERTS = 16
N_GROUPS = 4
EXPERTS_PER_GROUP = 4
D_EXPERT = 512
N_MOD = 6
IN_COLS = 5376
EPS = 1e-6
NEG_BIG = -1e30
TINY = 1e-30

COL_GATE_H = 0
COL_GATE_A = 1024
COL_HQ = 2048
COL_HF_FWD = 2560
COL_HF_BWD = 3072
COL_HI = 3584
COL_HG = 4096
COL_AQ = 4608
COL_AK = 5120
COL_AV = 5248

V7X_VMEM_LIMIT = 56 * 1024 * 1024

F32 = jnp.float32
BF16 = jnp.bfloat16
HIGHEST = lax.Precision.HIGHEST


def _params(sem):
    return pltpu.CompilerParams(dimension_semantics=sem, vmem_limit_bytes=V7X_VMEM_LIMIT)


def _dot(a, b):
    return jnp.dot(a, b, preferred_element_type=F32)


def _dot_nt(a, b):
    return lax.dot_general(a, b, (((1,), (1,)), ((), ())), preferred_element_type=F32)


def _dot_tn(a, b):
    return lax.dot_general(a, b, (((0,), (0,)), ((), ())), preferred_element_type=F32)


def _sigmoid(x):
    return 1.0 / (1.0 + jnp.exp(-x))


def _silu(x):
    return x * _sigmoid(x)


def _rms(x, g):
    return x * lax.rsqrt(jnp.mean(x * x, axis=-1, keepdims=True) + EPS) * g


def _ada_kernel(c_ref, w_ref, b_ref, o_ref):
    c = c_ref[...]
    o_ref[...] = jnp.dot(_silu(c), w_ref[...], precision=HIGHEST, preferred_element_type=F32) + b_ref[...]


def _ada(c_all, w_ada, b_ada):
    depth = w_ada.shape[0]
    rows = c_all.shape[0]
    ncol = w_ada.shape[2]
    tn = 1024
    return pl.pallas_call(
        _ada_kernel,
        grid=(depth, ncol // tn),
        in_specs=[
            pl.BlockSpec((rows, D_MODEL), lambda l, j: (0, 0)),
            pl.BlockSpec((None, D_MODEL, tn), lambda l, j: (l, 0, j)),
            pl.BlockSpec((None, 1, tn), lambda l, j: (l, 0, j)),
        ],
        out_specs=pl.BlockSpec((None, rows, tn), lambda l, j: (l, 0, j)),
        out_shape=jax.ShapeDtypeStruct((depth, rows, ncol), F32),
        compiler_params=_params(("arbitrary", "arbitrary")),
        name="ada_mod",
    )(c_all, w_ada, b_ada.reshape(depth, 1, ncol))


def _inproj_kernel(x_ref, mod_ref, g_ref, w_ref, o_ref):
    x = x_ref[...]
    h = _rms(x, g_ref[...]) * (1.0 + mod_ref[1:2, :]) + mod_ref[0:1, :]
    o_ref[...] = _dot(h.astype(BF16), w_ref[...])


def _inproj(x2d, mod, g, w, seq):
    n = x2d.shape[0]
    tm = min(512, seq)
    tn = IN_COLS // 2
    per_seq = seq // tm
    return pl.pallas_call(
        _inproj_kernel,
        grid=(IN_COLS // tn, n // tm),
        in_specs=[
            pl.BlockSpec((tm, D_MODEL), lambda j, i: (i, 0)),
            pl.BlockSpec((None, N_MOD, D_MODEL), lambda j, i: (i // per_seq, 0, 0)),
            pl.BlockSpec((1, D_MODEL), lambda j, i: (0, 0)),
            pl.BlockSpec((D_MODEL, tn), lambda j, i: (0, j)),
        ],
        out_specs=pl.BlockSpec((tm, tn), lambda j, i: (i, j)),
        out_shape=jax.ShapeDtypeStruct((n, IN_COLS), F32),
        compiler_params=_params(("arbitrary", "arbitrary")),
        name="inproj",
    )(x2d, mod, g, w)


def _lower_bound_row(gam_ref, layer):
    rows = [gam_ref[d:d + 1, :] for d in range(gam_ref.shape[0])]
    m = functools.reduce(jnp.maximum, rows)
    es = [jnp.exp(r - m) for r in rows]
    tot = functools.reduce(lambda a, b: a + b, es)
    ps = [e / tot for e in es]
    cum = ps[0]
    for d in range(1, layer + 1):
        cum = cum + ps[d]
    return jnp.clip(cum - ps[0], 0.0, 1.0)


def _hgrn_tables(rev):
    c = HG_SUB
    r = lax.broadcasted_iota(jnp.int32, (c, c), 0)
    s = lax.broadcasted_iota(jnp.int32, (c, c), 1)
    tri = jnp.where((s >= r) if rev else (s <= r), 1.0, 0.0).astype(F32)
    row = lax.broadcasted_iota(jnp.int32, (c, HG_WIDTH), 0)
    sels, qsides, pairs = [], [], []
    for lev in range(HG_LEVELS):
        half = 1 << lev
        blk = 2 * half
        r_up = (r & (blk - 1)) >= half
        s_up = (s & (blk - 1)) >= half
        base = r - (r & (blk - 1))
        mrow = base + (half if rev else half - 1)
        sels.append(jnp.where(s == mrow, 1.0, 0.0).astype(F32))
        row_up = (row & (blk - 1)) >= half
        qsides.append(~row_up if rev else row_up)
        same = (r >> (lev + 1)) == (s >> (lev + 1))
        pairs.append(same & ((~r_up & s_up) if rev else (r_up & ~s_up)))
    return tri, jnp.concatenate(sels, axis=0), qsides, pairs, r == s


def _hgrn_kernel(*refs, layer, rev):
    if rev:
        q_ref, f_ref, v_ref, gam_ref, hg_ref, of_ref, nrm_ref, o_ref, st_ref = refs
    else:
        q_ref, f_ref, v_ref, gam_ref, o_ref, st_ref = refs

    @pl.when(pl.program_id(1) == 0)
    def _():
        st_ref[...] = jnp.zeros_like(st_ref)

    c = HG_SUB
    nsub = q_ref.shape[0] // c
    lb = _lower_bound_row(gam_ref, layer)
    tri, sel_all, qsides, pairs, eye = _hgrn_tables(rev)
    last_row = 0 if rev else c - 1

    def sub_chunk(i, carry):
        ci = (nsub - 1 - i) if rev else i
        sl = pl.ds(pl.multiple_of(ci * c, c), c)
        zq = q_ref[sl, :]
        zf = f_ref[sl, :]
        v = v_ref[sl, :]
        q = _silu(zq) * (HG_DK ** -0.5)
        f = lb + (1.0 - lb) * _sigmoid(zf)
        g = jnp.log(jnp.maximum(f, TINY))
        k = (1.0 - lb) * _sigmoid(-zf)
        b = jnp.dot(tri, g, precision=HIGHEST, preferred_element_type=F32)
        bl = b[last_row:last_row + 1, :]
        q_in = (q * jnp.exp(b)).astype(BF16)
        k_dec = (k * jnp.exp(bl - b)).astype(BF16)
        dec = jnp.exp(bl)
        bref_all = jnp.dot(sel_all, b, precision=HIGHEST, preferred_element_type=F32)
        vb = v.astype(BF16)
        qb = q.astype(BF16)
        kb = k.astype(BF16)
        a = [jnp.where(eye, _dot_nt(qb[:, h * HG_DK:(h + 1) * HG_DK], kb[:, h * HG_DK:(h + 1) * HG_DK]), 0.0)
             for h in range(HG_HEADS)]
        for lev in range(HG_LEVELS):
            bref = bref_all[lev * c:(lev + 1) * c, :]
            qs = qsides[lev]
            x = jnp.exp(jnp.where(qs, b - bref, bref - b))
            ql = jnp.where(qs, q * x, 0.0).astype(BF16)
            kl = jnp.where(qs, 0.0, k * x).astype(BF16)
            for h in range(HG_HEADS):
                hs = slice(h * HG_DK, (h + 1) * HG_DK)
                a[h] = a[h] + jnp.where(pairs[lev], _dot_nt(ql[:, hs], kl[:, hs]), 0.0)
        outs = []
        for h in range(HG_HEADS):
            hs = slice(h * HG_DK, (h + 1) * HG_DK)
            st = st_ref[h]
            o = _dot_nt(q_in[:, hs], st.astype(BF16)) + _dot(a[h].astype(BF16), vb[:, hs])
            st_ref[h] = st * dec[:, hs] + _dot_tn(vb[:, hs], k_dec[:, hs])
            outs.append(o)
        o_all = jnp.concatenate(outs, axis=1)
        if rev:
            tot = of_ref[sl, :] + o_all
            zg = hg_ref[sl, :]
            nrm = nrm_ref[...]
            ys = [_rms(tot[:, h * HG_DK:(h + 1) * HG_DK], nrm) for h in range(HG_HEADS)]
            o_ref[sl, :] = (jnp.concatenate(ys, axis=1) * _silu(zg)).astype(o_ref.dtype)
        else:
            o_ref[sl, :] = o_all
        return carry

    lax.fori_loop(0, nsub, sub_chunk, 0)


def _hgrn(proj3, gamma, layer, rev, o_fwd=None, nrm=None):
    bsz, seq, _ = proj3.shape
    cb = min(256, seq)
    nc = seq // cb
    wblk = HG_WIDTH

    def cmap(col):
        if rev:
            return lambda b, c: (b, nc - 1 - c, col // wblk)
        return lambda b, c: (b, c, col // wblk)

    in_specs = [
        pl.BlockSpec((None, cb, wblk), cmap(COL_HQ)),
        pl.BlockSpec((None, cb, wblk), cmap(COL_HF_BWD if rev else COL_HF_FWD)),
        pl.BlockSpec((None, cb, wblk), cmap(COL_HI)),
        pl.BlockSpec(gamma.shape, lambda b, c: (0, 0)),
    ]
    args = [proj3, proj3, proj3, gamma]
    if rev:
        in_specs += [
            pl.BlockSpec((None, cb, wblk), cmap(COL_HG)),
            pl.BlockSpec((None, cb, wblk), cmap(0)),
            pl.BlockSpec((1, HG_DK), lambda b, c: (0, 0)),
        ]
        args += [proj3, o_fwd, nrm]
    out_dtype = BF16 if rev else F32
    return pl.pallas_call(
        functools.partial(_hgrn_kernel, layer=layer, rev=rev),
        grid=(bsz, nc),
        in_specs=in_specs,
        out_specs=pl.BlockSpec((None, cb, wblk), cmap(0)),
        out_shape=jax.ShapeDtypeStruct((bsz, seq, HG_WIDTH), out_dtype),
        scratch_shapes=[pltpu.VMEM((HG_HEADS, HG_DK, HG_DK), F32)],
        compiler_params=_params(("arbitrary", "arbitrary")),
        name="hgrn_bwd" if rev else "hgrn_fwd",
    )(*args)


def _t5_buckets(rel):
    half = N_BUCKETS // 2
    ret = np.where(rel > 0, half, 0)
    n = np.abs(rel)
    max_exact = half // 2
    large = max_exact + (np.log(np.maximum(n, 1) / max_exact)
                         / np.log(REL_MAX_DIST / max_exact) * (half - max_exact)).astype(np.int32)
    large = np.minimum(large, half - 1)
    return (ret + np.where(n < max_exact, n, large)).astype(np.int32)


def _attn_kernel(q_ref, kp_ref, kc_ref, kn_ref, vp_ref, vc_ref, vn_ref, bias_ref, sink_ref, o_ref):
    n = pl.program_id(1)
    nb = pl.num_programs(1)
    c = ATT_BLOCK
    q = q_ref[...].astype(BF16)
    kband = jnp.concatenate([kp_ref[...], kc_ref[...], kn_ref[...]], axis=0).astype(BF16)
    vband = jnp.concatenate([vp_ref[...], vc_ref[...], vn_ref[...]], axis=0).astype(BF16)
    qi = lax.broadcasted_iota(jnp.int32, (c, 3 * c), 0)
    kj = lax.broadcasted_iota(jnp.int32, (c, 3 * c), 1)
    rel = kj - c - qi
    valid = ((kj >= c) | (n > 0)) & ((kj < 2 * c) | (n < nb - 1))
    mask = (rel <= WINDOW) & (rel >= -WINDOW) & valid
    outs = []
    for h in range(ATT_HEADS):
        kv = h // ATT_GROUP
        k = kband[:, kv * ATT_HD:(kv + 1) * ATT_HD]
        v = vband[:, kv * ATT_HD:(kv + 1) * ATT_HD]
        logits = _dot_nt(q[:, h * ATT_HD:(h + 1) * ATT_HD], k) * (ATT_HD ** -0.5) + bias_ref[h]
        logits = jnp.where(mask, logits, NEG_BIG)
        s = sink_ref[h:h + 1, 0:1]
        m = jnp.maximum(jnp.max(logits, axis=-1, keepdims=True), s)
        p = jnp.exp(logits - m)
        probs = p / (jnp.sum(p, axis=-1, keepdims=True) + jnp.exp(s - m))
        outs.append(_dot(probs.astype(BF16), v))
    o_ref[...] = jnp.concatenate(outs, axis=1).astype(o_ref.dtype)


def _attn(proj3, bias, sink):
    bsz, seq, _ = proj3.shape
    c = ATT_BLOCK
    nb = seq // c
    kcol = COL_AK // KV_WIDTH
    vcol = COL_AV // KV_WIDTH

    def band(col):
        return [
            pl.BlockSpec((None, c, KV_WIDTH), lambda b, n: (b, jnp.maximum(n - 1, 0), col)),
            pl.BlockSpec((None, c, KV_WIDTH), lambda b, n: (b, n, col)),
            pl.BlockSpec((None, c, KV_WIDTH), lambda b, n: (b, jnp.minimum(n + 1, nb - 1), col)),
        ]

    return pl.pallas_call(
        _attn_kernel,
        grid=(bsz, nb),
        in_specs=[pl.BlockSpec((None, c, ATT_WIDTH), lambda b, n: (b, n, COL_AQ // ATT_WIDTH))]
        + band(kcol) + band(vcol)
        + [pl.BlockSpec(bias.shape, lambda b, n: (0, 0, 0)),
           pl.BlockSpec(sink.shape, lambda b, n: (0, 0))],
        out_specs=pl.BlockSpec((None, c, ATT_WIDTH), lambda b, n: (b, n, 0)),
        out_shape=jax.ShapeDtypeStruct((bsz, seq, ATT_WIDTH), BF16),
        compiler_params=_params(("arbitrary", "arbitrary")),
        name="window_attn",
    )(proj3, proj3, proj3, proj3, proj3, proj3, proj3, bias, sink)


def _first_argmax(vals):
    best, idx = vals[0], jnp.zeros(vals[0].shape, jnp.int32)
    for j in range(1, len(vals)):
        upd = vals[j] > best
        idx = jnp.where(upd, j, idx)
        best = jnp.where(upd, vals[j], best)
    return best, idx


def _select(vals, idx):
    out = vals[0]
    for j in range(1, len(vals)):
        out = jnp.where(idx == j, vals[j], out)
    return out


def _route(logits_t, rbias):
    m = jnp.max(logits_t, axis=0, keepdims=True)
    e = jnp.exp(logits_t - m)
    scores = e / jnp.sum(e, axis=0, keepdims=True)
    sel = scores + rbias
    srow = [scores[i:i + 1, :] for i in range(N_EXPERTS)]
    lrow = [sel[i:i + 1, :] for i in range(N_EXPERTS)]
    gscore = []
    for g in range(N_GROUPS):
        a, b, c, d = lrow[4 * g:4 * g + 4]
        hi1, lo1 = jnp.maximum(a, b), jnp.minimum(a, b)
        hi2, lo2 = jnp.maximum(c, d), jnp.minimum(c, d)
        gscore.append(jnp.maximum(hi1, hi2) + jnp.maximum(jnp.minimum(hi1, hi2), jnp.maximum(lo1, lo2)))
    _, gi = _first_argmax(gscore)
    ing = [_select([lrow[4 * g + j] for g in range(N_GROUPS)], gi) for j in range(EXPERTS_PER_GROUP)]
    sg = [_select([srow[4 * g + j] for g in range(N_GROUPS)], gi) for j in range(EXPERTS_PER_GROUP)]
    _, i1 = _first_argmax(ing)
    rest = [jnp.where(i1 == j, -jnp.inf, ing[j]) for j in range(EXPERTS_PER_GROUP)]
    _, i2 = _first_argmax(rest)
    s1, s2 = _select(sg, i1), _select(sg, i2)
    tot = s1 + s2
    w1, w2 = s1 / tot, s2 / tot
    idx1 = gi * EXPERTS_PER_GROUP + i1
    idx2 = gi * EXPERTS_PER_GROUP + i2
    erow = lax.broadcasted_iota(jnp.int32, logits_t.shape, 0)
    return jnp.where(erow == idx1, w1, 0.0) + jnp.where(erow == idx2, w2, 0.0)


def _merge_kernel(oh_ref, oa_ref, gh_ref, ga_ref, x_ref, mod_ref, nf_ref, wbh_ref, wba_ref, wo_ref,
                  wrt_ref, rb_ref, x1_ref, h2_ref, comb_ref):
    mh = _dot(oh_ref[...], wbh_ref[...])
    ma = _dot(oa_ref[...], wba_ref[...])
    merged = _sigmoid(gh_ref[...]) * mh + _sigmoid(ga_ref[...]) * ma
    out = _dot(merged.astype(BF16), wo_ref[...])
    x1 = x_ref[...] + mod_ref[2:3, :] * out
    x1_ref[...] = x1
    h2 = _rms(x1, nf_ref[...]) * (1.0 + mod_ref[4:5, :]) + mod_ref[3:4, :]
    h2_ref[...] = h2.astype(BF16)
    logits_t = lax.dot_general(wrt_ref[...], h2, (((1,), (1,)), ((), ())), precision=HIGHEST,
                               preferred_element_type=F32)
    comb_ref[...] = _route(logits_t, rb_ref[:, 0:1])


def _merge(o_h, o_a, proj, x2d, mod, nf, wbh, wba, wo, wrt, rb, seq):
    n = x2d.shape[0]
    tm = min(512, seq)
    per_seq = seq // tm
    const = lambda i: (0, 0)
    return pl.pallas_call(
        _merge_kernel,
        grid=(n // tm,),
        in_specs=[
            pl.BlockSpec((tm, HG_WIDTH), lambda i: (i, 0)),
            pl.BlockSpec((tm, ATT_WIDTH), lambda i: (i, 0)),
            pl.BlockSpec((tm, D_MODEL), lambda i: (i, COL_GATE_H // D_MODEL)),
            pl.BlockSpec((tm, D_MODEL), lambda i: (i, COL_GATE_A // D_MODEL)),
            pl.BlockSpec((tm, D_MODEL), lambda i: (i, 0)),
            pl.BlockSpec((None, N_MOD, D_MODEL), lambda i: (i // per_seq, 0, 0)),
            pl.BlockSpec((1, D_MODEL), const),
            pl.BlockSpec(wbh.shape, const),
            pl.BlockSpec(wba.shape, const),
            pl.BlockSpec(wo.shape, const),
            pl.BlockSpec(wrt.shape, const),
            pl.BlockSpec(rb.shape, const),
        ],
        out_specs=[
            pl.BlockSpec((tm, D_MODEL), lambda i: (i, 0)),
            pl.BlockSpec((tm, D_MODEL), lambda i: (i, 0)),
            pl.BlockSpec((N_EXPERTS, tm), lambda i: (0, i)),
        ],
        out_shape=[
            jax.ShapeDtypeStruct((n, D_MODEL), F32),
            jax.ShapeDtypeStruct((n, D_MODEL), BF16),
            jax.ShapeDtypeStruct((N_EXPERTS, n), F32),
        ],
        compiler_params=_params(("arbitrary",)),
        name="merge_router",
    )(o_h, o_a, proj, proj, x2d, mod, nf, wbh, wba, wo, wrt, rb)


def _moe_kernel(h_ref, comb_ref, wg_ref, wu_ref, wd_ref, x1_ref, mod_ref, nfin_ref, o_ref, acc_ref, *, last):
    e = pl.program_id(1)

    @pl.when(e == 0)
    def _():
        acc_ref[...] = jnp.zeros_like(acc_ref)

    h = h_ref[...]
    he = _silu(_dot(h, wg_ref[...])) * _dot(h, wu_ref[...])
    comb = comb_ref[...]
    lane = lax.broadcasted_iota(jnp.int32, comb.shape, 1)
    ce = jnp.sum(jnp.where(lane == e, comb, 0.0), axis=-1, keepdims=True)
    acc_ref[...] += _dot((he * ce).astype(BF16), wd_ref[...])

    @pl.when(e == pl.num_programs(1) - 1)
    def _():
        x2 = x1_ref[...] + mod_ref[5:6, :] * acc_ref[...]
        if last:
            x2 = _rms(x2, nfin_ref[...])
        o_ref[...] = x2


def _moe(h2, comb, wg, wu, wd, x1, mod, nfin, seq, last):
    n = h2.shape[0]
    tm = min(1024, seq)
    per_seq = seq // tm
    return pl.pallas_call(
        functools.partial(_moe_kernel, last=last),
        grid=(n // tm, N_EXPERTS),
        in_specs=[
            pl.BlockSpec((tm, D_MODEL), lambda i, e: (i, 0)),
            pl.BlockSpec((tm, N_EXPERTS), lambda i, e: (i, 0)),
            pl.BlockSpec((None, D_MODEL, D_EXPERT), lambda i, e: (e, 0, 0)),
            pl.BlockSpec((None, D_MODEL, D_EXPERT), lambda i, e: (e, 0, 0)),
            pl.BlockSpec((None, D_EXPERT, D_MODEL), lambda i, e: (e, 0, 0)),
            pl.BlockSpec((tm, D_MODEL), lambda i, e: (i, 0)),
            pl.BlockSpec((None, N_MOD, D_MODEL), lambda i, e: (i // per_seq, 0, 0)),
            pl.BlockSpec((1, D_MODEL), lambda i, e: (0, 0)),
        ],
        out_specs=pl.BlockSpec((tm, D_MODEL), lambda i, e: (i, 0)),
        out_shape=jax.ShapeDtypeStruct((n, D_MODEL), F32),
        scratch_shapes=[pltpu.VMEM((tm, D_MODEL), F32)],
        compiler_params=_params(("arbitrary", "arbitrary")),
        name="moe_dense",
    )(h2, comb, wg, wu, wd, x1, mod, nfin)


def _permute_w_in(w):
    hg5, att, gates = w[..., :2560], w[..., 2560:3328], w[..., 3328:]
    return jnp.concatenate([gates, hg5, att], axis=-1).astype(BF16)


def _bias_table(rel_bias):
    c = ATT_BLOCK
    rel = np.arange(3 * c)[None, :] - c - np.arange(c)[:, None]
    return rel_bias.astype(F32)[_t5_buckets(rel)].transpose(2, 0, 1)


def _trunk(x, mod, wts):
    bsz, seq, _ = x.shape
    n = bsz * seq
    depth = wts["w_in"].shape[0]
    x2d = x.reshape(n, D_MODEL)
    for l in range(depth):
        mod_l = mod[l]
        proj = _inproj(x2d, mod_l, wts["norm_mix"][l:l + 1], wts["w_in"][l], seq)
        proj3 = proj.reshape(bsz, seq, IN_COLS)
        o_f = _hgrn(proj3, wts["hg_lb_fwd"], l, False)
        o_h = _hgrn(proj3, wts["hg_lb_bwd"], l, True, o_f, wts["hg_norm"][l:l + 1])
        o_a = _attn(proj3, wts["bias"], wts["sink"][l])
        x1, h2, comb_t = _merge(o_h.reshape(n, HG_WIDTH), o_a.reshape(n, ATT_WIDTH), proj, x2d, mod_l,
                                wts["norm_ffn"][l:l + 1], wts["w_br_hgrn"][l], wts["w_br_att"][l],
                                wts["w_out"][l], wts["w_router_t"], wts["router_bias"], seq)
        x2d = _moe(h2, comb_t.T, wts["w_gate"][l], wts["w_up"][l], wts["w_down"][l], x1, mod_l,
                   wts["norm_final"], seq, l == depth - 1)
    return x2d.reshape(bsz, seq, D_MODEL)


def kernel(x_prompt, x_sample, c_prompt, c_sample, w_ada, b_ada, norm_mix, norm_ffn, norm_final, w_in, hg_lb_fwd, hg_lb_bwd, hg_norm, att_sink, rel_bias, w_br_hgrn, w_br_att, w_out, w_router, router_bias, w_gate, w_up, w_down):
    depth = w_in.shape[0]
    bp, bs = c_prompt.shape[0], c_sample.shape[0]
    rows = -(-(bp + bs) // 8) * 8
    c_all = jnp.concatenate([c_prompt, c_sample, jnp.zeros((rows - bp - bs, D_MODEL), F32)], axis=0)
    mod = _ada(c_all, w_ada, b_ada).reshape(depth, rows, N_MOD, D_MODEL)
    wts = {
        "norm_mix": norm_mix, "norm_ffn": norm_ffn, "norm_final": norm_final.reshape(1, D_MODEL),
        "w_in": _permute_w_in(w_in),
        "hg_lb_fwd": hg_lb_fwd, "hg_lb_bwd": hg_lb_bwd, "hg_norm": hg_norm,
        "sink": jnp.broadcast_to(att_sink[:, :, None], att_sink.shape + (128,)),
        "bias": _bias_table(rel_bias),
        "w_br_hgrn": w_br_hgrn.astype(BF16), "w_br_att": w_br_att.astype(BF16), "w_out": w_out.astype(BF16),
        "w_router_t": w_router.T,
        "router_bias": jnp.broadcast_to(router_bias[:, None], (N_EXPERTS, 128)),
        "w_gate": w_gate.astype(BF16), "w_up": w_up.astype(BF16), "w_down": w_down.astype(BF16),
    }
    y_prompt = _trunk(x_prompt, mod[:, :bp], wts)
    y_sample = _trunk(x_sample, mod[:, bp:bp + bs], wts)
    return (y_prompt, y_sample)
```

```python
import functools

import numpy as np
import jax
import jax.numpy as jnp
from jax import lax
from jax.experimental import pallas as pl
from jax.experimental.pallas import tpu as pltpu

D_MODEL = 1024
HG_DK = 128
HG_WIDTH = 512
HG_HEADS = 4
HG_SUB = 64
HG_LEVELS = 6
HG_GROUP = 32
HG_SAFE_SPAN = 80.0
ATT_HD = 64
ATT_HEADS = 8
ATT_KV = 2
ATT_GROUP = 4
ATT_WIDTH = 512
KV_WIDTH = 128
WINDOW = 128
ATT_BLOCK = 128
N_BUCKETS = 32
REL_MAX_DIST = 128
N_EXPERTS = 16
N_GROUPS = 4
EXPERTS_PER_GROUP = 4
D_EXPERT = 512
N_MOD = 6
IN_COLS = 5376
EPS = 1e-6
NEG_BIG = -1e30
TINY = 1e-30

COL_GATE_H = 0
COL_GATE_A = 1024
COL_HQ = 2048
COL_HF_FWD = 2560
COL_HF_BWD = 3072
COL_HI = 3584
COL_HG = 4096
COL_AQ = 4608
COL_AK = 5120
COL_AV = 5248

V7X_VMEM_LIMIT = 56 * 1024 * 1024

F32 = jnp.float32
BF16 = jnp.bfloat16
HIGHEST = lax.Precision.HIGHEST


def _params(sem):
    return pltpu.CompilerParams(dimension_semantics=sem, vmem_limit_bytes=V7X_VMEM_LIMIT)


def _dot(a, b):
    return jnp.dot(a, b, preferred_element_type=F32)


def _dot_nt(a, b):
    return lax.dot_general(a, b, (((1,), (1,)), ((), ())), preferred_element_type=F32)


def _dot_tn(a, b):
    return lax.dot_general(a, b, (((0,), (0,)), ((), ())), preferred_element_type=F32)


def _sigmoid(x):
    return 1.0 / (1.0 + jnp.exp(-x))


def _silu(x):
    return x * _sigmoid(x)


def _rms(x, g):
    return x * lax.rsqrt(jnp.mean(x * x, axis=-1, keepdims=True) + EPS) * g


def _ada_kernel(c_ref, w_ref, b_ref, o_ref):
    c = c_ref[...]
    o_ref[...] = jnp.dot(_silu(c), w_ref[...], precision=HIGHEST, preferred_element_type=F32) + b_ref[...]


def _ada(c_all, w_ada, b_ada):
    depth = w_ada.shape[0]
    rows = c_all.shape[0]
    ncol = w_ada.shape[2]
    tn = 1024
    return pl.pallas_call(
        _ada_kernel,
        grid=(depth, ncol // tn),
        in_specs=[
            pl.BlockSpec((rows, D_MODEL), lambda l, j: (0, 0)),
            pl.BlockSpec((None, D_MODEL, tn), lambda l, j: (l, 0, j)),
            pl.BlockSpec((None, 1, tn), lambda l, j: (l, 0, j)),
        ],
        out_specs=pl.BlockSpec((None, rows, tn), lambda l, j: (l, 0, j)),
        out_shape=jax.ShapeDtypeStruct((depth, rows, ncol), F32),
        compiler_params=_params(("arbitrary", "arbitrary")),
        name="ada_mod",
    )(c_all, w_ada, b_ada.reshape(depth, 1, ncol))


def _inproj_kernel(x_ref, mod_ref, g_ref, w_ref, o_ref):
    x = x_ref[...]
    h = _rms(x, g_ref[...]) * (1.0 + mod_ref[1:2, :]) + mod_ref[0:1, :]
    o_ref[...] = _dot(h.astype(BF16), w_ref[...])


def _inproj(x2d, mod, g, w, seq):
    n = x2d.shape[0]
    tm = min(512, seq)
    tn = IN_COLS // 2
    per_seq = seq // tm
    return pl.pallas_call(
        _inproj_kernel,
        grid=(IN_COLS // tn, n // tm),
        in_specs=[
            pl.BlockSpec((tm, D_MODEL), lambda j, i: (i, 0)),
            pl.BlockSpec((None, N_MOD, D_MODEL), lambda j, i: (i // per_seq, 0, 0)),
            pl.BlockSpec((1, D_MODEL), lambda j, i: (0, 0)),
            pl.BlockSpec((D_MODEL, tn), lambda j, i: (0, j)),
        ],
        out_specs=pl.BlockSpec((tm, tn), lambda j, i: (i, j)),
        out_shape=jax.ShapeDtypeStruct((n, IN_COLS), F32),
        compiler_params=_params(("arbitrary", "arbitrary")),
        name="inproj",
    )(x2d, mod, g, w)


def _lower_bound_row(gam_ref, layer):
    rows = [gam_ref[d:d + 1, :] for d in range(gam_ref.shape[0])]
    m = functools.reduce(jnp.maximum, rows)
    es = [jnp.exp(r - m) for r in rows]
    tot = functools.reduce(lambda a, b: a + b, es)
    ps = [e / tot for e in es]
    cum = ps[0]
    for d in range(1, layer + 1):
        cum = cum + ps[d]
    return jnp.clip(cum - ps[0], 0.0, 1.0)


def _hgrn_level_tables(rev):
    c = HG_SUB
    r = lax.broadcasted_iota(jnp.int32, (c, c), 0)
    s = lax.broadcasted_iota(jnp.int32, (c, c), 1)
    row = lax.broadcasted_iota(jnp.int32, (c, HG_WIDTH), 0)
    sels, qsides, pairs = [], [], []
    for lev in range(HG_LEVELS):
        half = 1 << lev
        blk = 2 * half
        r_up = (r & (blk - 1)) >= half
        s_up = (s & (blk - 1)) >= half
        base = r - (r & (blk - 1))
        mrow = base + (half if rev else half - 1)
        sels.append(jnp.where(s == mrow, 1.0, 0.0).astype(F32))
        row_up = (row & (blk - 1)) >= half
        qsides.append(~row_up if rev else row_up)
        same = (r >> (lev + 1)) == (s >> (lev + 1))
        pairs.append(same & ((~r_up & s_up) if rev else (r_up & ~s_up)))
    return jnp.concatenate(sels, axis=0), qsides, pairs, r == s


def _hgrn_kernel(*refs, layer, rev):
    if rev:
        q_ref, f_ref, v_ref, gam_ref, hg_ref, of_ref, nrm_ref, o_ref, st_ref, qs_ref, ks_ref, bs_ref = refs
    else:
        q_ref, f_ref, v_ref, gam_ref, o_ref, st_ref, qs_ref, ks_ref, bs_ref = refs

    @pl.when(pl.program_id(1) == 0)
    def _():
        st_ref[...] = jnp.zeros_like(st_ref)

    c = HG_SUB
    w = HG_WIDTH
    nsub = q_ref.shape[0] // c
    lb = _lower_bound_row(gam_ref, layer)
    r_i = lax.broadcasted_iota(jnp.int32, (c, c), 0)
    s_i = lax.broadcasted_iota(jnp.int32, (c, c), 1)
    causal = (s_i >= r_i) if rev else (s_i <= r_i)
    tri = jnp.where(causal, 1.0, 0.0).astype(BF16)
    row = lax.broadcasted_iota(jnp.int32, (c, w), 0)
    far = (row < HG_GROUP) if rev else (row >= HG_GROUP)
    ref_row = HG_GROUP if rev else HG_GROUP - 1
    last_row = 0 if rev else c - 1
    heads = [slice(h * HG_DK, (h + 1) * HG_DK) for h in range(HG_HEADS)]

    def rows_of(i):
        ci = (nsub - 1 - i) if rev else i
        return pl.ds(pl.multiple_of(ci * c, c), c)

    def prep(i, worst):
        sl = rows_of(i)
        zq = q_ref[sl, :]
        zf = f_ref[sl, :]
        f = lb + (1.0 - lb) * _sigmoid(zf)
        g = jnp.log(jnp.maximum(f, TINY))
        qs_ref[sl, :] = _silu(zq) * (HG_DK ** -0.5)
        ks_ref[sl, :] = (1.0 - lb) * _sigmoid(-zf)
        g1 = g.astype(BF16)
        r1 = g - g1.astype(F32)
        g2 = r1.astype(BF16)
        g3 = (r1 - g2.astype(F32)).astype(BF16)
        bb = _dot(tri, jnp.concatenate([g1, g2, g3], axis=1))
        b = (bb[:, :w] + bb[:, w:2 * w]) + bb[:, 2 * w:]
        bs_ref[sl, :] = b
        r = b[ref_row:ref_row + 1, :]
        bl = b[last_row:last_row + 1, :]
        return jnp.maximum(worst, jnp.maximum(-r, r - bl))

    worst = lax.fori_loop(0, nsub, prep, jnp.zeros((1, w), F32))
    safe = jnp.max(worst) <= HG_SAFE_SPAN

    def finish(sl, a, q_in, k_dec, dec, vb):
        outs = []
        for h, hs in enumerate(heads):
            st = st_ref[h]
            outs.append(_dot_nt(q_in[:, hs], st.astype(BF16)) + _dot(a[h].astype(BF16), vb[:, hs]))
            st_ref[h] = st * dec[:, hs] + _dot_tn(vb[:, hs], k_dec[:, hs])
        o_all = jnp.concatenate(outs, axis=1)
        if rev:
            tot = of_ref[sl, :] + o_all
            nrm = nrm_ref[...]
            ys = [_rms(tot[:, hs], nrm) for hs in heads]
            o_ref[sl, :] = (jnp.concatenate(ys, axis=1) * _silu(hg_ref[sl, :])).astype(o_ref.dtype)
        else:
            o_ref[sl, :] = o_all

    def factored(i, carry):
        sl = rows_of(i)
        q, k, b = qs_ref[sl, :], ks_ref[sl, :], bs_ref[sl, :]
        vb = v_ref[sl, :].astype(BF16)
        r = b[ref_row:ref_row + 1, :]
        bl = b[last_row:last_row + 1, :]
        rg = jnp.where(far, r, 0.0)
        qt = q * jnp.exp(b - rg)
        kt = k * jnp.exp(rg - b)
        er = jnp.exp(r)
        qn = jnp.where(far, 0.0, qt).astype(BF16)
        qf = jnp.where(far, qt, 0.0).astype(BF16)
        kc = jnp.where(far, kt, kt * er).astype(BF16)
        ktb = kt.astype(BF16)
        q_in = jnp.where(far, qt * er, qt).astype(BF16)
        k_dec = (kt * jnp.where(far, jnp.exp(bl - r), jnp.exp(bl))).astype(BF16)
        a = []
        for hs in heads:
            lhs = jnp.concatenate([qn[:, hs], qf[:, hs]], axis=1)
            rhs = jnp.concatenate([ktb[:, hs], kc[:, hs]], axis=1)
            a.append(jnp.where(causal, _dot_nt(lhs, rhs), 0.0))
        finish(sl, a, q_in, k_dec, jnp.exp(bl), vb)
        return carry

    def levels(i, carry):
        sl = rows_of(i)
        q, k, b = qs_ref[sl, :], ks_ref[sl, :], bs_ref[sl, :]
        vb = v_ref[sl, :].astype(BF16)
        sel_all, qsides, pairs, eye = _hgrn_level_tables(rev)
        bl = b[last_row:last_row + 1, :]
        q_in = (q * jnp.exp(b)).astype(BF16)
        k_dec = (k * jnp.exp(bl - b)).astype(BF16)
        bref_all = jnp.dot(sel_all, b, precision=HIGHEST, preferred_element_type=F32)
        qb = q.astype(BF16)
        kb = k.astype(BF16)
        a = [jnp.where(eye, _dot_nt(qb[:, hs], kb[:, hs]), 0.0) for hs in heads]
        for lev in range(HG_LEVELS):
            bref = bref_all[lev * c:(lev + 1) * c, :]
            qs = qsides[lev]
            x = jnp.exp(jnp.where(qs, b - bref, bref - b))
            ql = jnp.where(qs, q * x, 0.0).astype(BF16)
            kl = jnp.where(qs, 0.0, k * x).astype(BF16)
            for h, hs in enumerate(heads):
                a[h] = a[h] + jnp.where(pairs[lev], _dot_nt(ql[:, hs], kl[:, hs]), 0.0)
        finish(sl, a, q_in, k_dec, jnp.exp(bl), vb)
        return carry

    @pl.when(safe)
    def _():
        lax.fori_loop(0, nsub, factored, 0)

    @pl.when(jnp.logical_not(safe))
    def _():
        lax.fori_loop(0, nsub, levels, 0)


def _hgrn(proj3, gamma, layer, rev, o_fwd=None, nrm=None):
    bsz, seq, _ = proj3.shape
    cb = min(512, seq)
    nc = seq // cb
    wblk = HG_WIDTH

    def cmap(col):
        if rev:
            return lambda b, c: (b, nc - 1 - c, col // wblk)
        return lambda b, c: (b, c, col // wblk)

    in_specs = [
        pl.BlockSpec((None, cb, wblk), cmap(COL_HQ)),
        pl.BlockSpec((None, cb, wblk), cmap(COL_HF_BWD if rev else COL_HF_FWD)),
        pl.BlockSpec((None, cb, wblk), cmap(COL_HI)),
        pl.BlockSpec(gamma.shape, lambda b, c: (0, 0)),
    ]
    args = [proj3, proj3, proj3, gamma]
    if rev:
        in_specs += [
            pl.BlockSpec((None, cb, wblk), cmap(COL_HG)),
            pl.BlockSpec((None, cb, wblk), cmap(0)),
            pl.BlockSpec((1, HG_DK), lambda b, c: (0, 0)),
        ]
        args += [proj3, o_fwd, nrm]
    out_dtype = BF16 if rev else F32
    return pl.pallas_call(
        functools.partial(_hgrn_kernel, layer=layer, rev=rev),
        grid=(bsz, nc),
        in_specs=in_specs,
        out_specs=pl.BlockSpec((None, cb, wblk), cmap(0)),
        out_shape=jax.ShapeDtypeStruct((bsz, seq, HG_WIDTH), out_dtype),
        scratch_shapes=[pltpu.VMEM((HG_HEADS, HG_DK, HG_DK), F32)] + [pltpu.VMEM((cb, wblk), F32)] * 3,
        compiler_params=_params(("arbitrary", "arbitrary")),
        name="hgrn_bwd" if rev else "hgrn_fwd",
    )(*args)


ATT_ROW_HEADS = (0, 2, 1, 3)


def _t5_buckets(rel):
    half = N_BUCKETS // 2
    ret = np.where(rel > 0, half, 0)
    n = np.abs(rel)
    max_exact = half // 2
    large = max_exact + (np.log(np.maximum(n, 1) / max_exact)
                         / np.log(REL_MAX_DIST / max_exact) * (half - max_exact)).astype(np.int32)
    large = np.minimum(large, half - 1)
    return (ret + np.where(n < max_exact, n, large)).astype(np.int32)


def _attn_tables(rel_bias, att_sink):
    c = ATT_BLOCK
    rel = np.arange(3 * c)[None, :] - c - np.arange(c)[:, None]
    bias = rel_bias.astype(F32)[_t5_buckets(rel)].transpose(2, 0, 1)
    bias = jnp.where(jnp.asarray(np.abs(rel) <= WINDOW)[None], bias, NEG_BIG)
    col = np.arange(3 * c)
    cases = []
    for case in range(4):
        valid = np.ones(3 * c, bool)
        if case & 1:
            valid &= col >= c
        if case & 2:
            valid &= col < 2 * c
        cases.append(jnp.where(jnp.asarray(valid)[None, None, :], bias, NEG_BIG))
    tab = jnp.stack(cases)
    order = np.array([[ATT_GROUP * g + h for h in ATT_ROW_HEADS] for g in range(ATT_KV)])
    tab = tab[:, order].reshape(4, ATT_KV, ATT_GROUP * c, 3 * c)
    sink = att_sink.astype(F32)[:, order]
    sink = jnp.broadcast_to(sink[..., None, None], sink.shape + (c, 128))
    return tab, sink.reshape(att_sink.shape[0], ATT_KV, ATT_GROUP * c, 128)


def _attn_kernel(q_ref, kp_ref, kc_ref, kn_ref, vp_ref, vc_ref, vn_ref, bias_ref, sink_ref, o_ref):
    n = pl.program_id(1)
    nsteps = pl.num_programs(1)
    c = ATT_BLOCK
    nsub = q_ref.shape[0] // c
    pair = 2 * ATT_HD
    kwin = jnp.concatenate([kp_ref[...], kc_ref[...], kn_ref[...]], axis=0)
    vwin = jnp.concatenate([vp_ref[...], vc_ref[...], vn_ref[...]], axis=0)
    kroll = pltpu.roll(kwin, ATT_HD, axis=1)
    vroll = pltpu.roll(vwin, ATT_HD, axis=1)
    lo = lax.broadcasted_iota(jnp.int32, kwin.shape, 1) < ATT_HD
    k_lo = [jnp.where(lo, kwin, 0.0).astype(BF16), jnp.where(lo, kroll, 0.0).astype(BF16)]
    k_hi = [jnp.where(lo, 0.0, kroll).astype(BF16), jnp.where(lo, 0.0, kwin).astype(BF16)]
    v_lo = [jnp.where(lo, vwin, 0.0).astype(BF16), jnp.where(lo, vroll, 0.0).astype(BF16)]
    v_hi = [jnp.where(lo, 0.0, vroll).astype(BF16), jnp.where(lo, 0.0, vwin).astype(BF16)]
    ones = jnp.ones((3 * c, pair), BF16)
    for j in range(nsub):
        case = jnp.int32(0)
        if j == 0:
            case = case + (n == 0).astype(jnp.int32)
        if j == nsub - 1:
            case = case + 2 * (n == nsteps - 1).astype(jnp.int32)
        band = slice(j * c, (j + 3) * c)
        qrows = slice(j * c, (j + 1) * c)
        for g in range(ATT_KV):
            qg = q_ref[qrows, g * 2 * pair:(g + 1) * 2 * pair] * (ATT_HD ** -0.5)
            lhs = jnp.concatenate([qg[:, :pair], qg[:, pair:]], axis=0).astype(BF16)
            rhs = jnp.concatenate([k_lo[g][band], k_hi[g][band]], axis=0)
            lg = _dot_nt(lhs, rhs)
            logits = jnp.concatenate([lg[:, :3 * c], lg[:, 3 * c:]], axis=0) + bias_ref[case, g]
            s = sink_ref[g][:, 0:1]
            m = jnp.maximum(jnp.max(logits, axis=-1, keepdims=True), s)
            p = jnp.exp(logits - m).astype(BF16)
            den = _dot(p, ones) + jnp.exp(s - m)
            o = (_dot(p[:2 * c], v_lo[g][band]) / den[:2 * c]
                 + _dot(p[2 * c:], v_hi[g][band]) / den[2 * c:])
            o_ref[qrows, g * 2 * pair:g * 2 * pair + pair] = o[:c].astype(o_ref.dtype)
            o_ref[qrows, g * 2 * pair + pair:(g + 1) * 2 * pair] = o[c:].astype(o_ref.dtype)


def _attn(proj3, bias, sink):
    bsz, seq, _ = proj3.shape
    c = ATT_BLOCK
    nsub = 2 if seq % (2 * c) == 0 else 1
    qb = nsub * c
    nsteps = seq // qb
    nb = seq // c
    kcol = COL_AK // KV_WIDTH
    vcol = COL_AV // KV_WIDTH

    def band(col):
        return [
            pl.BlockSpec((None, c, KV_WIDTH), lambda b, n: (b, jnp.maximum(n * nsub - 1, 0), col)),
            pl.BlockSpec((None, qb, KV_WIDTH), lambda b, n: (b, n, col)),
            pl.BlockSpec((None, c, KV_WIDTH), lambda b, n: (b, jnp.minimum((n + 1) * nsub, nb - 1), col)),
        ]

    return pl.pallas_call(
        _attn_kernel,
        grid=(bsz, nsteps),
        in_specs=[pl.BlockSpec((None, qb, ATT_WIDTH), lambda b, n: (b, n, COL_AQ // ATT_WIDTH))]
        + band(kcol) + band(vcol)
        + [pl.BlockSpec(bias.shape, lambda b, n: (0, 0, 0, 0)),
           pl.BlockSpec(sink.shape, lambda b, n: (0, 0, 0))],
        out_specs=pl.BlockSpec((None, qb, ATT_WIDTH), lambda b, n: (b, n, 0)),
        out_shape=jax.ShapeDtypeStruct((bsz, seq, ATT_WIDTH), BF16),
        compiler_params=_params(("arbitrary", "arbitrary")),
        name="window_attn",
    )(proj3, proj3, proj3, proj3, proj3, proj3, proj3, bias, sink)


def _first_argmax(vals):
    best, idx = vals[0], jnp.zeros(vals[0].shape, jnp.int32)
    for j in range(1, len(vals)):
        upd = vals[j] > best
        idx = jnp.where(upd, j, idx)
        best = jnp.where(upd, vals[j], best)
    return best, idx


def _select(vals, idx):
    out = vals[0]
    for j in range(1, len(vals)):
        out = jnp.where(idx == j, vals[j], out)
    return out


def _route(logits_t, rbias):
    m = jnp.max(logits_t, axis=0, keepdims=True)
    e = jnp.exp(logits_t - m)
    scores = e / jnp.sum(e, axis=0, keepdims=True)
    sel = scores + rbias
    srow = [scores[i:i + 1, :] for i in range(N_EXPERTS)]
    lrow = [sel[i:i + 1, :] for i in range(N_EXPERTS)]
    gscore = []
    for g in range(N_GROUPS):
        a, b, c, d = lrow[4 * g:4 * g + 4]
        hi1, lo1 = jnp.maximum(a, b), jnp.minimum(a, b)
        hi2, lo2 = jnp.maximum(c, d), jnp.minimum(c, d)
        gscore.append(jnp.maximum(hi1, hi2) + jnp.maximum(jnp.minimum(hi1, hi2), jnp.maximum(lo1, lo2)))
    _, gi = _first_argmax(gscore)
    ing = [_select([lrow[4 * g + j] for g in range(N_GROUPS)], gi) for j in range(EXPERTS_PER_GROUP)]
    sg = [_select([srow[4 * g + j] for g in range(N_GROUPS)], gi) for j in range(EXPERTS_PER_GROUP)]
    _, i1 = _first_argmax(ing)
    rest = [jnp.where(i1 == j, -jnp.inf, ing[j]) for j in range(EXPERTS_PER_GROUP)]
    _, i2 = _first_argmax(rest)
    s1, s2 = _select(sg, i1), _select(sg, i2)
    tot = s1 + s2
    w1, w2 = s1 / tot, s2 / tot
    idx1 = gi * EXPERTS_PER_GROUP + i1
    idx2 = gi * EXPERTS_PER_GROUP + i2
    erow = lax.broadcasted_iota(jnp.int32, logits_t.shape, 0)
    return jnp.where(erow == idx1, w1, 0.0) + jnp.where(erow == idx2, w2, 0.0)


def _merge_kernel(oh_ref, oa_ref, gh_ref, ga_ref, x_ref, mod_ref, nf_ref, wbh_ref, wba_ref, wo_ref,
                  wrt_ref, rb_ref, x1_ref, h2_ref, comb_ref):
    mh = _dot(oh_ref[...], wbh_ref[...])
    ma = _dot(oa_ref[...], wba_ref[...])
    merged = _sigmoid(gh_ref[...]) * mh + _sigmoid(ga_ref[...]) * ma
    out = _dot(merged.astype(BF16), wo_ref[...])
    x1 = x_ref[...] + mod_ref[2:3, :] * out
    x1_ref[...] = x1
    h2 = _rms(x1, nf_ref[...]) * (1.0 + mod_ref[4:5, :]) + mod_ref[3:4, :]
    h2_ref[...] = h2.astype(BF16)
    logits_t = lax.dot_general(wrt_ref[...], h2, (((1,), (1,)), ((), ())), precision=HIGHEST,
                               preferred_element_type=F32)
    comb_ref[...] = _route(logits_t, rb_ref[:, 0:1])


def _merge(o_h, o_a, proj, x2d, mod, nf, wbh, wba, wo, wrt, rb, seq):
    n = x2d.shape[0]
    tm = min(512, seq)
    per_seq = seq // tm
    const = lambda i: (0, 0)
    return pl.pallas_call(
        _merge_kernel,
        grid=(n // tm,),
        in_specs=[
            pl.BlockSpec((tm, HG_WIDTH), lambda i: (i, 0)),
            pl.BlockSpec((tm, ATT_WIDTH), lambda i: (i, 0)),
            pl.BlockSpec((tm, D_MODEL), lambda i: (i, COL_GATE_H // D_MODEL)),
            pl.BlockSpec((tm, D_MODEL), lambda i: (i, COL_GATE_A // D_MODEL)),
            pl.BlockSpec((tm, D_MODEL), lambda i: (i, 0)),
            pl.BlockSpec((None, N_MOD, D_MODEL), lambda i: (i // per_seq, 0, 0)),
            pl.BlockSpec((1, D_MODEL), const),
            pl.BlockSpec(wbh.shape, const),
            pl.BlockSpec(wba.shape, const),
            pl.BlockSpec(wo.shape, const),
            pl.BlockSpec(wrt.shape, const),
            pl.BlockSpec(rb.shape, const),
        ],
        out_specs=[
            pl.BlockSpec((tm, D_MODEL), lambda i: (i, 0)),
            pl.BlockSpec((tm, D_MODEL), lambda i: (i, 0)),
            pl.BlockSpec((N_EXPERTS, tm), lambda i: (0, i)),
        ],
        out_shape=[
            jax.ShapeDtypeStruct((n, D_MODEL), F32),
            jax.ShapeDtypeStruct((n, D_MODEL), BF16),
            jax.ShapeDtypeStruct((N_EXPERTS, n), F32),
        ],
        compiler_params=_params(("arbitrary",)),
        name="merge_router",
    )(o_h, o_a, proj, proj, x2d, mod, nf, wbh, wba, wo, wrt, rb)


def _moe_kernel(h_ref, comb_ref, wg_ref, wu_ref, wd_ref, x1_ref, mod_ref, nfin_ref, o_ref, acc_ref, *, last):
    e = pl.program_id(1)

    @pl.when(e == 0)
    def _():
        acc_ref[...] = jnp.zeros_like(acc_ref)

    h = h_ref[...]
    he = _silu(_dot(h, wg_ref[...])) * _dot(h, wu_ref[...])
    comb = comb_ref[...]
    lane = lax.broadcasted_iota(jnp.int32, comb.shape, 1)
    ce = jnp.sum(jnp.where(lane == e, comb, 0.0), axis=-1, keepdims=True)
    acc_ref[...] += _dot((he * ce).astype(BF16), wd_ref[...])

    @pl.when(e == pl.num_programs(1) - 1)
    def _():
        x2 = x1_ref[...] + mod_ref[5:6, :] * acc_ref[...]
        if last:
            x2 = _rms(x2, nfin_ref[...])
        o_ref[...] = x2


def _moe(h2, comb, wg, wu, wd, x1, mod, nfin, seq, last):
    n = h2.shape[0]
    tm = min(1024, seq)
    per_seq = seq // tm
    return pl.pallas_call(
        functools.partial(_moe_kernel, last=last),
        grid=(n // tm, N_EXPERTS),
        in_specs=[
            pl.BlockSpec((tm, D_MODEL), lambda i, e: (i, 0)),
            pl.BlockSpec((tm, N_EXPERTS), lambda i, e: (i, 0)),
            pl.BlockSpec((None, D_MODEL, D_EXPERT), lambda i, e: (e, 0, 0)),
            pl.BlockSpec((None, D_MODEL, D_EXPERT), lambda i, e: (e, 0, 0)),
            pl.BlockSpec((None, D_EXPERT, D_MODEL), lambda i, e: (e, 0, 0)),
            pl.BlockSpec((tm, D_MODEL), lambda i, e: (i, 0)),
            pl.BlockSpec((None, N_MOD, D_MODEL), lambda i, e: (i // per_seq, 0, 0)),
            pl.BlockSpec((1, D_MODEL), lambda i, e: (0, 0)),
        ],
        out_specs=pl.BlockSpec((tm, D_MODEL), lambda i, e: (i, 0)),
        out_shape=jax.ShapeDtypeStruct((n, D_MODEL), F32),
        scratch_shapes=[pltpu.VMEM((tm, D_MODEL), F32)],
        compiler_params=_params(("arbitrary", "arbitrary")),
        name="moe_dense",
    )(h2, comb, wg, wu, wd, x1, mod, nfin)


def _permute_w_in(w):
    hg5, att, gates = w[..., :2560], w[..., 2560:3328], w[..., 3328:]
    return jnp.concatenate([gates, hg5, att], axis=-1).astype(BF16)


def _trunk(x, mod, wts):
    bsz, seq, _ = x.shape
    n = bsz * seq
    depth = wts["w_in"].shape[0]
    x2d = x.reshape(n, D_MODEL)
    for l in range(depth):
        mod_l = mod[l]
        proj = _inproj(x2d, mod_l, wts["norm_mix"][l:l + 1], wts["w_in"][l], seq)
        proj3 = proj.reshape(bsz, seq, IN_COLS)
        o_f = _hgrn(proj3, wts["hg_lb_fwd"], l, False)
        o_h = _hgrn(proj3, wts["hg_lb_bwd"], l, True, o_f, wts["hg_norm"][l:l + 1])
        o_a = _attn(proj3, wts["bias"], wts["sink"][l])
        x1, h2, comb_t = _merge(o_h.reshape(n, HG_WIDTH), o_a.reshape(n, ATT_WIDTH), proj, x2d, mod_l,
                                wts["norm_ffn"][l:l + 1], wts["w_br_hgrn"][l], wts["w_br_att"][l],
                                wts["w_out"][l], wts["w_router_t"], wts["router_bias"], seq)
        x2d = _moe(h2, comb_t.T, wts["w_gate"][l], wts["w_up"][l], wts["w_down"][l], x1, mod_l,
                   wts["norm_final"], seq, l == depth - 1)
    return x2d.reshape(bsz, seq, D_MODEL)


def kernel(x_prompt, x_sample, c_prompt, c_sample, w_ada, b_ada, norm_mix, norm_ffn, norm_final, w_in, hg_lb_fwd, hg_lb_bwd, hg_norm, att_sink, rel_bias, w_br_hgrn, w_br_att, w_out, w_router, router_bias, w_gate, w_up, w_down):
    depth = w_in.shape[0]
    bp, bs = c_prompt.shape[0], c_sample.shape[0]
    rows = -(-(bp + bs) // 8) * 8
    c_all = jnp.concatenate([c_prompt, c_sample, jnp.zeros((rows - bp - bs, D_MODEL), F32)], axis=0)
    mod = _ada(c_all, w_ada, b_ada).reshape(depth, rows, N_MOD, D_MODEL)
    bias, sink = _attn_tables(rel_bias, att_sink)
    wts = {
        "norm_mix": norm_mix, "norm_ffn": norm_ffn, "norm_final": norm_final.reshape(1, D_MODEL),
        "w_in": _permute_w_in(w_in),
        "hg_lb_fwd": hg_lb_fwd, "hg_lb_bwd": hg_lb_bwd, "hg_norm": hg_norm,
        "sink": sink, "bias": bias,
        "w_br_hgrn": w_br_hgrn.astype(BF16), "w_br_att": w_br_att.astype(BF16), "w_out": w_out.astype(BF16),
        "w_router_t": w_router.T,
        "router_bias": jnp.broadcast_to(router_bias[:, None], (N_EXPERTS, 128)),
        "w_gate": w_gate.astype(BF16), "w_up": w_up.astype(BF16), "w_down": w_down.astype(BF16),
    }
    y_prompt = _trunk(x_prompt, mod[:, :bp], wts)
    y_sample = _trunk(x_sample, mod[:, bp:bp + bs], wts)
    return (y_prompt, y_sample)
```

```python
import functools

import numpy as np
import jax
import jax.numpy as jnp
from jax import lax
from jax.experimental import pallas as pl
from jax.experimental.pallas import tpu as pltpu
from jax.experimental.pallas import tpu_sc as plsc

D_MODEL = 1024
HG_DK = 128
HG_WIDTH = 512
HG_HEADS = 4
HG_SUB = 64
HG_LEVELS = 6
HG_GROUP = 32
HG_SAFE_SPAN = 80.0
ATT_HD = 64
ATT_HEADS = 8
ATT_KV = 2
ATT_GROUP = 4
ATT_WIDTH = 512
KV_WIDTH = 128
WINDOW = 128
ATT_BLOCK = 128
N_BUCKETS = 32
REL_MAX_DIST = 128
N_EXPERTS = 16
N_GROUPS = 4
EXPERTS_PER_GROUP = 4
D_EXPERT = 512
N_MOD = 6
IN_COLS = 5376
EPS = 1e-6
NEG_BIG = -1e30
TINY = 1e-30

COL_GATE_H = 0
COL_GATE_A = 1024
COL_HQ = 2048
COL_HF_FWD = 2560
COL_HF_BWD = 3072
COL_HI = 3584
COL_HG = 4096
COL_AQ = 4608
COL_AK = 5120
COL_AV = 5248

V7X_VMEM_LIMIT = 56 * 1024 * 1024

F32 = jnp.float32
BF16 = jnp.bfloat16
HIGHEST = lax.Precision.HIGHEST


def _params(sem):
    return pltpu.CompilerParams(dimension_semantics=sem, vmem_limit_bytes=V7X_VMEM_LIMIT)


def _dot(a, b):
    return jnp.dot(a, b, preferred_element_type=F32)


def _dot_nt(a, b):
    return lax.dot_general(a, b, (((1,), (1,)), ((), ())), preferred_element_type=F32)


def _dot_tn(a, b):
    return lax.dot_general(a, b, (((0,), (0,)), ((), ())), preferred_element_type=F32)


def _sigmoid(x):
    return 1.0 / (1.0 + jnp.exp(-x))


def _silu(x):
    return x * _sigmoid(x)


def _rms(x, g):
    return x * lax.rsqrt(jnp.mean(x * x, axis=-1, keepdims=True) + EPS) * g


def _ada_kernel(c_ref, w_ref, b_ref, o_ref):
    c = c_ref[...]
    o_ref[...] = jnp.dot(_silu(c), w_ref[...], precision=HIGHEST, preferred_element_type=F32) + b_ref[...]


def _ada(c_all, w_ada, b_ada):
    depth = w_ada.shape[0]
    rows = c_all.shape[0]
    ncol = w_ada.shape[2]
    tn = 1024
    return pl.pallas_call(
        _ada_kernel,
        grid=(depth, ncol // tn),
        in_specs=[
            pl.BlockSpec((rows, D_MODEL), lambda l, j: (0, 0)),
            pl.BlockSpec((None, D_MODEL, tn), lambda l, j: (l, 0, j)),
            pl.BlockSpec((None, 1, tn), lambda l, j: (l, 0, j)),
        ],
        out_specs=pl.BlockSpec((None, rows, tn), lambda l, j: (l, 0, j)),
        out_shape=jax.ShapeDtypeStruct((depth, rows, ncol), F32),
        compiler_params=_params(("arbitrary", "arbitrary")),
        name="ada_mod",
    )(c_all, w_ada, b_ada.reshape(depth, 1, ncol))


def _inproj_kernel(x_ref, mod_ref, g_ref, w_ref, o_ref):
    x = x_ref[...]
    h = _rms(x, g_ref[...]) * (1.0 + mod_ref[1:2, :]) + mod_ref[0:1, :]
    o_ref[...] = _dot(h.astype(BF16), w_ref[...])


def _inproj(x2d, mod, g, w, seq):
    n = x2d.shape[0]
    tm = min(512, seq)
    tn = IN_COLS // 2
    per_seq = seq // tm
    return pl.pallas_call(
        _inproj_kernel,
        grid=(IN_COLS // tn, n // tm),
        in_specs=[
            pl.BlockSpec((tm, D_MODEL), lambda j, i: (i, 0)),
            pl.BlockSpec((None, N_MOD, D_MODEL), lambda j, i: (i // per_seq, 0, 0)),
            pl.BlockSpec((1, D_MODEL), lambda j, i: (0, 0)),
            pl.BlockSpec((D_MODEL, tn), lambda j, i: (0, j)),
        ],
        out_specs=pl.BlockSpec((tm, tn), lambda j, i: (i, j)),
        out_shape=jax.ShapeDtypeStruct((n, IN_COLS), F32),
        compiler_params=_params(("arbitrary", "arbitrary")),
        name="inproj",
    )(x2d, mod, g, w)


def _lower_bound_row(gam_ref, layer):
    rows = [gam_ref[d:d + 1, :] for d in range(gam_ref.shape[0])]
    m = functools.reduce(jnp.maximum, rows)
    es = [jnp.exp(r - m) for r in rows]
    tot = functools.reduce(lambda a, b: a + b, es)
    ps = [e / tot for e in es]
    cum = ps[0]
    for d in range(1, layer + 1):
        cum = cum + ps[d]
    return jnp.clip(cum - ps[0], 0.0, 1.0)


def _hgrn_level_tables(rev):
    c = HG_SUB
    r = lax.broadcasted_iota(jnp.int32, (c, c), 0)
    s = lax.broadcasted_iota(jnp.int32, (c, c), 1)
    row = lax.broadcasted_iota(jnp.int32, (c, HG_WIDTH), 0)
    sels, qsides, pairs = [], [], []
    for lev in range(HG_LEVELS):
        half = 1 << lev
        blk = 2 * half
        r_up = (r & (blk - 1)) >= half
        s_up = (s & (blk - 1)) >= half
        base = r - (r & (blk - 1))
        mrow = base + (half if rev else half - 1)
        sels.append(jnp.where(s == mrow, 1.0, 0.0).astype(F32))
        row_up = (row & (blk - 1)) >= half
        qsides.append(~row_up if rev else row_up)
        same = (r >> (lev + 1)) == (s >> (lev + 1))
        pairs.append(same & ((~r_up & s_up) if rev else (r_up & ~s_up)))
    return jnp.concatenate(sels, axis=0), qsides, pairs, r == s


def _hgrn_kernel(*refs, layer, rev):
    if rev:
        q_ref, f_ref, v_ref, gam_ref, hg_ref, of_ref, nrm_ref, o_ref, st_ref, qs_ref, ks_ref, bs_ref = refs
    else:
        q_ref, f_ref, v_ref, gam_ref, o_ref, st_ref, qs_ref, ks_ref, bs_ref = refs

    @pl.when(pl.program_id(1) == 0)
    def _():
        st_ref[...] = jnp.zeros_like(st_ref)

    c = HG_SUB
    w = HG_WIDTH
    nsub = q_ref.shape[0] // c
    lb = _lower_bound_row(gam_ref, layer)
    r_i = lax.broadcasted_iota(jnp.int32, (c, c), 0)
    s_i = lax.broadcasted_iota(jnp.int32, (c, c), 1)
    causal = (s_i >= r_i) if rev else (s_i <= r_i)
    tri = jnp.where(causal, 1.0, 0.0).astype(BF16)
    row = lax.broadcasted_iota(jnp.int32, (c, w), 0)
    far = (row < HG_GROUP) if rev else (row >= HG_GROUP)
    ref_row = HG_GROUP if rev else HG_GROUP - 1
    last_row = 0 if rev else c - 1
    heads = [slice(h * HG_DK, (h + 1) * HG_DK) for h in range(HG_HEADS)]

    def rows_of(i):
        ci = (nsub - 1 - i) if rev else i
        return pl.ds(pl.multiple_of(ci * c, c), c)

    def prep(i, worst):
        sl = rows_of(i)
        zq = q_ref[sl, :]
        zf = f_ref[sl, :]
        f = lb + (1.0 - lb) * _sigmoid(zf)
        g = jnp.log(jnp.maximum(f, TINY))
        qs_ref[sl, :] = _silu(zq) * (HG_DK ** -0.5)
        ks_ref[sl, :] = (1.0 - lb) * _sigmoid(-zf)
        g1 = g.astype(BF16)
        r1 = g - g1.astype(F32)
        g2 = r1.astype(BF16)
        g3 = (r1 - g2.astype(F32)).astype(BF16)
        bb = _dot(tri, jnp.concatenate([g1, g2, g3], axis=1))
        b = (bb[:, :w] + bb[:, w:2 * w]) + bb[:, 2 * w:]
        bs_ref[sl, :] = b
        r = b[ref_row:ref_row + 1, :]
        bl = b[last_row:last_row + 1, :]
        return jnp.maximum(worst, jnp.maximum(-r, r - bl))

    worst = lax.fori_loop(0, nsub, prep, jnp.zeros((1, w), F32))
    safe = jnp.max(worst) <= HG_SAFE_SPAN

    def finish(sl, a, q_in, k_dec, dec, vb):
        outs = []
        for h, hs in enumerate(heads):
            st = st_ref[h]
            outs.append(_dot_nt(q_in[:, hs], st.astype(BF16)) + _dot(a[h].astype(BF16), vb[:, hs]))
            st_ref[h] = st * dec[:, hs] + _dot_tn(vb[:, hs], k_dec[:, hs])
        o_all = jnp.concatenate(outs, axis=1)
        if rev:
            tot = of_ref[sl, :] + o_all
            nrm = nrm_ref[...]
            ys = [_rms(tot[:, hs], nrm) for hs in heads]
            o_ref[sl, :] = (jnp.concatenate(ys, axis=1) * _silu(hg_ref[sl, :])).astype(o_ref.dtype)
        else:
            o_ref[sl, :] = o_all

    def factored(i, carry):
        sl = rows_of(i)
        q, k, b = qs_ref[sl, :], ks_ref[sl, :], bs_ref[sl, :]
        vb = v_ref[sl, :].astype(BF16)
        r = b[ref_row:ref_row + 1, :]
        bl = b[last_row:last_row + 1, :]
        rg = jnp.where(far, r, 0.0)
        qt = q * jnp.exp(b - rg)
        kt = k * jnp.exp(rg - b)
        er = jnp.exp(r)
        qn = jnp.where(far, 0.0, qt).astype(BF16)
        qf = jnp.where(far, qt, 0.0).astype(BF16)
        kc = jnp.where(far, kt, kt * er).astype(BF16)
        ktb = kt.astype(BF16)
        q_in = jnp.where(far, qt * er, qt).astype(BF16)
        k_dec = (kt * jnp.where(far, jnp.exp(bl - r), jnp.exp(bl))).astype(BF16)
        a = []
        for hs in heads:
            lhs = jnp.concatenate([qn[:, hs], qf[:, hs]], axis=1)
            rhs = jnp.concatenate([ktb[:, hs], kc[:, hs]], axis=1)
            a.append(jnp.where(causal, _dot_nt(lhs, rhs), 0.0))
        finish(sl, a, q_in, k_dec, jnp.exp(bl), vb)
        return carry

    def levels(i, carry):
        sl = rows_of(i)
        q, k, b = qs_ref[sl, :], ks_ref[sl, :], bs_ref[sl, :]
        vb = v_ref[sl, :].astype(BF16)
        sel_all, qsides, pairs, eye = _hgrn_level_tables(rev)
        bl = b[last_row:last_row + 1, :]
        q_in = (q * jnp.exp(b)).astype(BF16)
        k_dec = (k * jnp.exp(bl - b)).astype(BF16)
        bref_all = jnp.dot(sel_all, b, precision=HIGHEST, preferred_element_type=F32)
        qb = q.astype(BF16)
        kb = k.astype(BF16)
        a = [jnp.where(eye, _dot_nt(qb[:, hs], kb[:, hs]), 0.0) for hs in heads]
        for lev in range(HG_LEVELS):
            bref = bref_all[lev * c:(lev + 1) * c, :]
            qs = qsides[lev]
            x = jnp.exp(jnp.where(qs, b - bref, bref - b))
            ql = jnp.where(qs, q * x, 0.0).astype(BF16)
            kl = jnp.where(qs, 0.0, k * x).astype(BF16)
            for h, hs in enumerate(heads):
                a[h] = a[h] + jnp.where(pairs[lev], _dot_nt(ql[:, hs], kl[:, hs]), 0.0)
        finish(sl, a, q_in, k_dec, jnp.exp(bl), vb)
        return carry

    @pl.when(safe)
    def _():
        lax.fori_loop(0, nsub, factored, 0)

    @pl.when(jnp.logical_not(safe))
    def _():
        lax.fori_loop(0, nsub, levels, 0)


def _hgrn(proj3, gamma, layer, rev, o_fwd=None, nrm=None):
    bsz, seq, _ = proj3.shape
    cb = min(512, seq)
    nc = seq // cb
    wblk = HG_WIDTH

    def cmap(col):
        if rev:
            return lambda b, c: (b, nc - 1 - c, col // wblk)
        return lambda b, c: (b, c, col // wblk)

    in_specs = [
        pl.BlockSpec((None, cb, wblk), cmap(COL_HQ)),
        pl.BlockSpec((None, cb, wblk), cmap(COL_HF_BWD if rev else COL_HF_FWD)),
        pl.BlockSpec((None, cb, wblk), cmap(COL_HI)),
        pl.BlockSpec(gamma.shape, lambda b, c: (0, 0)),
    ]
    args = [proj3, proj3, proj3, gamma]
    if rev:
        in_specs += [
            pl.BlockSpec((None, cb, wblk), cmap(COL_HG)),
            pl.BlockSpec((None, cb, wblk), cmap(0)),
            pl.BlockSpec((1, HG_DK), lambda b, c: (0, 0)),
        ]
        args += [proj3, o_fwd, nrm]
    out_dtype = BF16 if rev else F32
    return pl.pallas_call(
        functools.partial(_hgrn_kernel, layer=layer, rev=rev),
        grid=(bsz, nc),
        in_specs=in_specs,
        out_specs=pl.BlockSpec((None, cb, wblk), cmap(0)),
        out_shape=jax.ShapeDtypeStruct((bsz, seq, HG_WIDTH), out_dtype),
        scratch_shapes=[pltpu.VMEM((HG_HEADS, HG_DK, HG_DK), F32)] + [pltpu.VMEM((cb, wblk), F32)] * 3,
        compiler_params=_params(("arbitrary", "arbitrary")),
        name="hgrn_bwd" if rev else "hgrn_fwd",
    )(*args)


ATT_ROW_HEADS = (0, 2, 1, 3)


def _t5_buckets(rel):
    half = N_BUCKETS // 2
    ret = np.where(rel > 0, half, 0)
    n = np.abs(rel)
    max_exact = half // 2
    large = max_exact + (np.log(np.maximum(n, 1) / max_exact)
                         / np.log(REL_MAX_DIST / max_exact) * (half - max_exact)).astype(np.int32)
    large = np.minimum(large, half - 1)
    return (ret + np.where(n < max_exact, n, large)).astype(np.int32)


def _attn_tables(rel_bias, att_sink):
    c = ATT_BLOCK
    rel = np.arange(3 * c)[None, :] - c - np.arange(c)[:, None]
    bias = rel_bias.astype(F32)[_t5_buckets(rel)].transpose(2, 0, 1)
    bias = jnp.where(jnp.asarray(np.abs(rel) <= WINDOW)[None], bias, NEG_BIG)
    col = np.arange(3 * c)
    cases = []
    for case in range(4):
        valid = np.ones(3 * c, bool)
        if case & 1:
            valid &= col >= c
        if case & 2:
            valid &= col < 2 * c
        cases.append(jnp.where(jnp.asarray(valid)[None, None, :], bias, NEG_BIG))
    tab = jnp.stack(cases)
    order = np.array([[ATT_GROUP * g + h for h in ATT_ROW_HEADS] for g in range(ATT_KV)])
    tab = tab[:, order].reshape(4, ATT_KV, ATT_GROUP * c, 3 * c)
    sink = att_sink.astype(F32)[:, order]
    sink = jnp.broadcast_to(sink[..., None, None], sink.shape + (c, 128))
    return tab, sink.reshape(att_sink.shape[0], ATT_KV, ATT_GROUP * c, 128)


def _attn_kernel(q_ref, kp_ref, kc_ref, kn_ref, vp_ref, vc_ref, vn_ref, bias_ref, sink_ref, o_ref):
    n = pl.program_id(1)
    nsteps = pl.num_programs(1)
    c = ATT_BLOCK
    nsub = q_ref.shape[0] // c
    pair = 2 * ATT_HD
    kwin = jnp.concatenate([kp_ref[...], kc_ref[...], kn_ref[...]], axis=0)
    vwin = jnp.concatenate([vp_ref[...], vc_ref[...], vn_ref[...]], axis=0)
    kroll = pltpu.roll(kwin, ATT_HD, axis=1)
    vroll = pltpu.roll(vwin, ATT_HD, axis=1)
    lo = lax.broadcasted_iota(jnp.int32, kwin.shape, 1) < ATT_HD
    k_lo = [jnp.where(lo, kwin, 0.0).astype(BF16), jnp.where(lo, kroll, 0.0).astype(BF16)]
    k_hi = [jnp.where(lo, 0.0, kroll).astype(BF16), jnp.where(lo, 0.0, kwin).astype(BF16)]
    v_lo = [jnp.where(lo, vwin, 0.0).astype(BF16), jnp.where(lo, vroll, 0.0).astype(BF16)]
    v_hi = [jnp.where(lo, 0.0, vroll).astype(BF16), jnp.where(lo, 0.0, vwin).astype(BF16)]
    ones = jnp.ones((3 * c, pair), BF16)
    for j in range(nsub):
        case = jnp.int32(0)
        if j == 0:
            case = case + (n == 0).astype(jnp.int32)
        if j == nsub - 1:
            case = case + 2 * (n == nsteps - 1).astype(jnp.int32)
        band = slice(j * c, (j + 3) * c)
        qrows = slice(j * c, (j + 1) * c)
        for g in range(ATT_KV):
            qg = q_ref[qrows, g * 2 * pair:(g + 1) * 2 * pair] * (ATT_HD ** -0.5)
            lhs = jnp.concatenate([qg[:, :pair], qg[:, pair:]], axis=0).astype(BF16)
            rhs = jnp.concatenate([k_lo[g][band], k_hi[g][band]], axis=0)
            lg = _dot_nt(lhs, rhs)
            logits = jnp.concatenate([lg[:, :3 * c], lg[:, 3 * c:]], axis=0) + bias_ref[case, g]
            s = sink_ref[g][:, 0:1]
            m = jnp.maximum(jnp.max(logits, axis=-1, keepdims=True), s)
            p = jnp.exp(logits - m).astype(BF16)
            den = _dot(p, ones) + jnp.exp(s - m)
            o = (_dot(p[:2 * c], v_lo[g][band]) / den[:2 * c]
                 + _dot(p[2 * c:], v_hi[g][band]) / den[2 * c:])
            o_ref[qrows, g * 2 * pair:g * 2 * pair + pair] = o[:c].astype(o_ref.dtype)
            o_ref[qrows, g * 2 * pair + pair:(g + 1) * 2 * pair] = o[c:].astype(o_ref.dtype)


def _attn(proj3, bias, sink):
    bsz, seq, _ = proj3.shape
    c = ATT_BLOCK
    nsub = 2 if seq % (2 * c) == 0 else 1
    qb = nsub * c
    nsteps = seq // qb
    nb = seq // c
    kcol = COL_AK // KV_WIDTH
    vcol = COL_AV // KV_WIDTH

    def band(col):
        return [
            pl.BlockSpec((None, c, KV_WIDTH), lambda b, n: (b, jnp.maximum(n * nsub - 1, 0), col)),
            pl.BlockSpec((None, qb, KV_WIDTH), lambda b, n: (b, n, col)),
            pl.BlockSpec((None, c, KV_WIDTH), lambda b, n: (b, jnp.minimum((n + 1) * nsub, nb - 1), col)),
        ]

    return pl.pallas_call(
        _attn_kernel,
        grid=(bsz, nsteps),
        in_specs=[pl.BlockSpec((None, qb, ATT_WIDTH), lambda b, n: (b, n, COL_AQ // ATT_WIDTH))]
        + band(kcol) + band(vcol)
        + [pl.BlockSpec(bias.shape, lambda b, n: (0, 0, 0, 0)),
           pl.BlockSpec(sink.shape, lambda b, n: (0, 0, 0))],
        out_specs=pl.BlockSpec((None, qb, ATT_WIDTH), lambda b, n: (b, n, 0)),
        out_shape=jax.ShapeDtypeStruct((bsz, seq, ATT_WIDTH), BF16),
        compiler_params=_params(("arbitrary", "arbitrary")),
        name="window_attn",
    )(proj3, proj3, proj3, proj3, proj3, proj3, proj3, bias, sink)


def _first_argmax(vals):
    best, idx = vals[0], jnp.zeros(vals[0].shape, jnp.int32)
    for j in range(1, len(vals)):
        upd = vals[j] > best
        idx = jnp.where(upd, j, idx)
        best = jnp.where(upd, vals[j], best)
    return best, idx


def _select(vals, idx):
    out = vals[0]
    for j in range(1, len(vals)):
        out = jnp.where(idx == j, vals[j], out)
    return out


def _route(logits_t, rbias):
    m = jnp.max(logits_t, axis=0, keepdims=True)
    e = jnp.exp(logits_t - m)
    scores = e / jnp.sum(e, axis=0, keepdims=True)
    sel = scores + rbias
    srow = [scores[i:i + 1, :] for i in range(N_EXPERTS)]
    lrow = [sel[i:i + 1, :] for i in range(N_EXPERTS)]
    gscore = []
    for g in range(N_GROUPS):
        a, b, c, d = lrow[4 * g:4 * g + 4]
        hi1, lo1 = jnp.maximum(a, b), jnp.minimum(a, b)
        hi2, lo2 = jnp.maximum(c, d), jnp.minimum(c, d)
        gscore.append(jnp.maximum(hi1, hi2) + jnp.maximum(jnp.minimum(hi1, hi2), jnp.maximum(lo1, lo2)))
    _, gi = _first_argmax(gscore)
    ing = [_select([lrow[4 * g + j] for g in range(N_GROUPS)], gi) for j in range(EXPERTS_PER_GROUP)]
    sg = [_select([srow[4 * g + j] for g in range(N_GROUPS)], gi) for j in range(EXPERTS_PER_GROUP)]
    _, i1 = _first_argmax(ing)
    rest = [jnp.where(i1 == j, -jnp.inf, ing[j]) for j in range(EXPERTS_PER_GROUP)]
    _, i2 = _first_argmax(rest)
    s1, s2 = _select(sg, i1), _select(sg, i2)
    tot = s1 + s2
    w1, w2 = s1 / tot, s2 / tot
    idx1 = gi * EXPERTS_PER_GROUP + i1
    idx2 = gi * EXPERTS_PER_GROUP + i2
    return idx1, idx2, w1, w2


def _merge_kernel(oh_ref, oa_ref, gh_ref, ga_ref, x_ref, mod_ref, nf_ref, wbh_ref, wba_ref, wo_ref,
                  wrt_ref, rb_ref, x1_ref, h2_ref, ri_ref, rw_ref, cnt_ref, carry_ref):
    @pl.when(pl.program_id(0) == 0)
    def _():
        carry_ref[...] = jnp.zeros_like(carry_ref)

    mh = _dot(oh_ref[...], wbh_ref[...])
    ma = _dot(oa_ref[...], wba_ref[...])
    merged = _sigmoid(gh_ref[...]) * mh + _sigmoid(ga_ref[...]) * ma
    out = _dot(merged.astype(BF16), wo_ref[...])
    x1 = x_ref[...] + mod_ref[2:3, :] * out
    x1_ref[...] = x1
    h2 = _rms(x1, nf_ref[...]) * (1.0 + mod_ref[4:5, :]) + mod_ref[3:4, :]
    h2_ref[...] = h2
    logits_t = lax.dot_general(wrt_ref[...], h2, (((1,), (1,)), ((), ())), precision=HIGHEST,
                               preferred_element_type=F32)
    idx1, idx2, w1, w2 = _route(logits_t, rb_ref[:, 0:1])
    tm = h2.shape[0]
    erow = lax.broadcasted_iota(jnp.int32, logits_t.shape, 0)
    oh1 = erow == idx1
    oh2 = erow == idx2
    oh = jnp.where(oh1 | oh2, 1.0, 0.0)
    t_src = lax.broadcasted_iota(jnp.int32, (tm, tm), 0)
    t_dst = lax.broadcasted_iota(jnp.int32, (tm, tm), 1)
    before = jnp.where(t_src < t_dst, 1.0, 0.0).astype(BF16)
    pref = _dot(oh.astype(BF16), before) + carry_ref[:, 0:1]
    rank1 = jnp.sum(jnp.where(oh1, pref, 0.0), axis=0, keepdims=True)
    rank2 = jnp.sum(jnp.where(oh2, pref, 0.0), axis=0, keepdims=True)
    carry = carry_ref[...] + jnp.sum(oh, axis=1, keepdims=True)
    carry_ref[...] = carry
    cnt_ref[...] = carry
    ri_ref[...] = jnp.concatenate([idx1, idx2, rank1.astype(jnp.int32), rank2.astype(jnp.int32)], axis=0)
    rw_ref[...] = jnp.concatenate([w1, w2], axis=0)


def _merge(o_h, o_a, proj, x2d, mod, nf, wbh, wba, wo, wrt, rb, seq):
    n = x2d.shape[0]
    tm = min(512, seq)
    per_seq = seq // tm
    const = lambda i: (0, 0)
    return pl.pallas_call(
        _merge_kernel,
        grid=(n // tm,),
        in_specs=[
            pl.BlockSpec((tm, HG_WIDTH), lambda i: (i, 0)),
            pl.BlockSpec((tm, ATT_WIDTH), lambda i: (i, 0)),
            pl.BlockSpec((tm, D_MODEL), lambda i: (i, COL_GATE_H // D_MODEL)),
            pl.BlockSpec((tm, D_MODEL), lambda i: (i, COL_GATE_A // D_MODEL)),
            pl.BlockSpec((tm, D_MODEL), lambda i: (i, 0)),
            pl.BlockSpec((None, N_MOD, D_MODEL), lambda i: (i // per_seq, 0, 0)),
            pl.BlockSpec((1, D_MODEL), const),
            pl.BlockSpec(wbh.shape, const),
            pl.BlockSpec(wba.shape, const),
            pl.BlockSpec(wo.shape, const),
            pl.BlockSpec(wrt.shape, const),
            pl.BlockSpec(rb.shape, const),
        ],
        out_specs=[
            pl.BlockSpec((tm, D_MODEL), lambda i: (i, 0)),
            pl.BlockSpec((tm, D_MODEL), lambda i: (i, 0)),
            pl.BlockSpec((4, tm), lambda i: (0, i)),
            pl.BlockSpec((2, tm), lambda i: (0, i)),
            pl.BlockSpec((N_EXPERTS, 128), const),
        ],
        out_shape=[
            jax.ShapeDtypeStruct((n, D_MODEL), F32),
            jax.ShapeDtypeStruct((n, D_MODEL), F32),
            jax.ShapeDtypeStruct((4, n), jnp.int32),
            jax.ShapeDtypeStruct((2, n), F32),
            jax.ShapeDtypeStruct((N_EXPERTS, 128), F32),
        ],
        scratch_shapes=[pltpu.VMEM((N_EXPERTS, 128), F32)],
        compiler_params=_params(("arbitrary",)),
        name="merge_router",
    )(o_h, o_a, proj, proj, x2d, mod, nf, wbh, wba, wo, wrt, rb)


MOE_TILE = 512
SC_WINDOW = 32


def _sc_mesh():
    return plsc.VectorSubcoreMesh(core_axis_name="c", subcore_axis_name="s")


def _sc_dispatch(h, pos, p):
    n, d = h.shape
    win = SC_WINDOW
    info = plsc.get_sparse_core_info()
    workers = info.num_cores * info.num_subcores
    wpt = n // (win * workers)
    pos_w = pos.reshape(2, n // win, win).transpose(1, 0, 2)

    @functools.partial(
        pl.kernel, out_type=jax.ShapeDtypeStruct((p, d), h.dtype), mesh=_sc_mesh(),
        scratch_types=[pltpu.VMEM((wpt, 2, win), jnp.int32), pltpu.VMEM((2, win, d), h.dtype),
                       pltpu.SemaphoreType.DMA((2,))],
        name="moe_dispatch")
    def dispatch(h_hbm, pos_hbm, o_hbm, idx_v, rows_v, load_sem):
        wid = lax.axis_index("c") * info.num_subcores + lax.axis_index("s")
        first = wid * wpt
        pltpu.sync_copy(pos_hbm.at[pl.ds(first, wpt)], idx_v)

        def load(j, slot):
            return pltpu.make_async_copy(h_hbm.at[pl.ds((first + j) * win, win)], rows_v.at[slot],
                                         load_sem.at[slot])

        load(0, 0).start()

        @pl.loop(0, wpt, step=2)
        def _(j):
            for slot in range(2):
                jj = j + slot
                load(jj, slot).wait()

                @pl.when(jj + 1 < wpt)
                def _():
                    load(jj + 1, 1 - slot).start()

                pltpu.sync_copy(rows_v.at[slot], o_hbm.at[idx_v.at[jj, 0]])
                pltpu.sync_copy(rows_v.at[slot], o_hbm.at[idx_v.at[jj, 1]])

    return dispatch(h, pos_w)


def _sc_combine(ys, pos):
    n = pos.shape[1]
    d = ys.shape[1]
    win = SC_WINDOW // 2
    info = plsc.get_sparse_core_info()
    workers = info.num_cores * info.num_subcores
    wpt = n // (win * workers)
    pos_w = pos.reshape(2, n // win, win).transpose(1, 0, 2)
    out = jax.ShapeDtypeStruct((n, d), ys.dtype)

    @functools.partial(
        pl.kernel, out_type=(out, out), mesh=_sc_mesh(),
        scratch_types=[pltpu.VMEM((wpt, 2, win), jnp.int32), pltpu.VMEM((2, 2, win, d), ys.dtype),
                       pltpu.SemaphoreType.DMA((2, 2)), pltpu.SemaphoreType.DMA((2,))],
        name="moe_combine")
    def combine(ys_hbm, pos_hbm, a_hbm, b_hbm, idx_v, rows_v, gather_sem, store_sem):
        wid = lax.axis_index("c") * info.num_subcores + lax.axis_index("s")
        first = wid * wpt
        pltpu.sync_copy(pos_hbm.at[pl.ds(first, wpt)], idx_v)
        outs = (a_hbm, b_hbm)

        def gather(j, slot, k):
            return pltpu.make_async_copy(ys_hbm.at[idx_v.at[j, k]], rows_v.at[slot, k], gather_sem.at[slot, k])

        def store(j, slot, k):
            return pltpu.make_async_copy(rows_v.at[slot, k], outs[k].at[pl.ds((first + j) * win, win)],
                                         store_sem.at[k])

        gather(0, 0, 0).start()
        gather(0, 0, 1).start()

        @pl.loop(0, wpt, step=2)
        def _(j):
            for slot in range(2):
                jj = j + slot
                gather(jj, slot, 0).wait()
                gather(jj, slot, 1).wait()

                @pl.when(jj + 1 < wpt)
                def _():
                    gather(jj + 1, 1 - slot, 0).start()
                    gather(jj + 1, 1 - slot, 1).start()

                store(jj, slot, 0).start()
                store(jj, slot, 1).start()
                store(jj, slot, 0).wait()
                store(jj, slot, 1).wait()

    return combine(ys, pos_w)


def _expert_kernel(te_ref, nu_ref, x_ref, wg_ref, wu_ref, wd_ref, o_ref):
    del te_ref
    used = pl.program_id(0) < nu_ref[0]

    @pl.when(used)
    def _():
        x = x_ref[...].astype(BF16)
        he = _silu(_dot(x, wg_ref[...])) * _dot(x, wu_ref[...])
        o_ref[...] = _dot(he.astype(BF16), wd_ref[...])

    @pl.when(jnp.logical_not(used))
    def _():
        o_ref[...] = jnp.zeros_like(o_ref)


def _experts(xs, tile_expert, n_used, wg, wu, wd):
    p = xs.shape[0]
    tm = MOE_TILE
    grid_spec = pltpu.PrefetchScalarGridSpec(
        num_scalar_prefetch=2,
        grid=(p // tm,),
        in_specs=[
            pl.BlockSpec((tm, D_MODEL), lambda i, te, nu: (i, 0)),
            pl.BlockSpec((None, D_MODEL, D_EXPERT), lambda i, te, nu: (te[i], 0, 0)),
            pl.BlockSpec((None, D_MODEL, D_EXPERT), lambda i, te, nu: (te[i], 0, 0)),
            pl.BlockSpec((None, D_EXPERT, D_MODEL), lambda i, te, nu: (te[i], 0, 0)),
        ],
        out_specs=pl.BlockSpec((tm, D_MODEL), lambda i, te, nu: (i, 0)),
    )
    return pl.pallas_call(
        _expert_kernel,
        grid_spec=grid_spec,
        out_shape=jax.ShapeDtypeStruct((p, D_MODEL), F32),
        compiler_params=_params(("arbitrary",)),
        name="moe_experts",
    )(tile_expert, n_used, xs, wg, wu, wd)


def _residual_kernel(ya_ref, yb_ref, w_ref, x1_ref, mod_ref, nfin_ref, o_ref, *, last):
    w = w_ref[...]
    y = w[:, 0:1] * ya_ref[...] + w[:, 1:2] * yb_ref[...]
    x2 = x1_ref[...] + mod_ref[5:6, :] * y
    if last:
        x2 = _rms(x2, nfin_ref[...])
    o_ref[...] = x2


def _residual(ya, yb, w, x1, mod, nfin, seq, last):
    n = x1.shape[0]
    tm = min(512, seq)
    per_seq = seq // tm
    row = pl.BlockSpec((tm, D_MODEL), lambda i: (i, 0))
    return pl.pallas_call(
        functools.partial(_residual_kernel, last=last),
        grid=(n // tm,),
        in_specs=[row, row, pl.BlockSpec((tm, 2), lambda i: (i, 0)), row,
                  pl.BlockSpec((None, N_MOD, D_MODEL), lambda i: (i // per_seq, 0, 0)),
                  pl.BlockSpec((1, D_MODEL), lambda i: (0, 0))],
        out_specs=row,
        out_shape=jax.ShapeDtypeStruct((n, D_MODEL), F32),
        compiler_params=_params(("arbitrary",)),
        name="moe_residual",
    )(ya, yb, w, x1, mod, nfin)


def _moe(h2, ri, rw, cnt, wg, wu, wd, x1, mod, nfin, seq, last):
    n = h2.shape[0]
    tm = MOE_TILE
    p = 2 * n + N_EXPERTS * tm
    counts = cnt[:, 0].astype(jnp.int32)
    padded = (counts + tm - 1) // tm * tm
    ends = jnp.cumsum(padded)
    pos = jnp.take(ends - padded, ri[0:2], axis=0) + ri[2:4]
    tile_start = jnp.arange(p // tm, dtype=jnp.int32) * tm
    tile_expert = jnp.minimum(jnp.sum(tile_start[:, None] >= ends[None, :], axis=1), N_EXPERTS - 1)
    n_used = (ends[-1:] // tm).astype(jnp.int32)
    xs = _sc_dispatch(h2, pos, p)
    ys = _experts(xs, tile_expert.astype(jnp.int32), n_used, wg, wu, wd)
    ya, yb = _sc_combine(ys, pos)
    return _residual(ya, yb, rw.T, x1, mod, nfin, seq, last)


def _permute_w_in(w):
    hg5, att, gates = w[..., :2560], w[..., 2560:3328], w[..., 3328:]
    return jnp.concatenate([gates, hg5, att], axis=-1).astype(BF16)


def _trunk(x, mod, wts):
    bsz, seq, _ = x.shape
    n = bsz * seq
    depth = wts["w_in"].shape[0]
    x2d = x.reshape(n, D_MODEL)
    for l in range(depth):
        mod_l = mod[l]
        proj = _inproj(x2d, mod_l, wts["norm_mix"][l:l + 1], wts["w_in"][l], seq)
        proj3 = proj.reshape(bsz, seq, IN_COLS)
        o_f = _hgrn(proj3, wts["hg_lb_fwd"], l, False)
        o_h = _hgrn(proj3, wts["hg_lb_bwd"], l, True, o_f, wts["hg_norm"][l:l + 1])
        o_a = _attn(proj3, wts["bias"], wts["sink"][l])
        x1, h2, ri, rw, cnt = _merge(o_h.reshape(n, HG_WIDTH), o_a.reshape(n, ATT_WIDTH), proj, x2d, mod_l,
                                     wts["norm_ffn"][l:l + 1], wts["w_br_hgrn"][l], wts["w_br_att"][l],
                                     wts["w_out"][l], wts["w_router_t"], wts["router_bias"], seq)
        x2d = _moe(h2, ri, rw, cnt, wts["w_gate"][l], wts["w_up"][l], wts["w_down"][l], x1, mod_l,
                   wts["norm_final"], seq, l == depth - 1)
    return x2d.reshape(bsz, seq, D_MODEL)


def kernel(x_prompt, x_sample, c_prompt, c_sample, w_ada, b_ada, norm_mix, norm_ffn, norm_final, w_in, hg_lb_fwd, hg_lb_bwd, hg_norm, att_sink, rel_bias, w_br_hgrn, w_br_att, w_out, w_router, router_bias, w_gate, w_up, w_down):
    depth = w_in.shape[0]
    bp, bs = c_prompt.shape[0], c_sample.shape[0]
    rows = -(-(bp + bs) // 8) * 8
    c_all = jnp.concatenate([c_prompt, c_sample, jnp.zeros((rows - bp - bs, D_MODEL), F32)], axis=0)
    mod = _ada(c_all, w_ada, b_ada).reshape(depth, rows, N_MOD, D_MODEL)
    bias, sink = _attn_tables(rel_bias, att_sink)
    wts = {
        "norm_mix": norm_mix, "norm_ffn": norm_ffn, "norm_final": norm_final.reshape(1, D_MODEL),
        "w_in": _permute_w_in(w_in),
        "hg_lb_fwd": hg_lb_fwd, "hg_lb_bwd": hg_lb_bwd, "hg_norm": hg_norm,
        "sink": sink, "bias": bias,
        "w_br_hgrn": w_br_hgrn.astype(BF16), "w_br_att": w_br_att.astype(BF16), "w_out": w_out.astype(BF16),
        "w_router_t": w_router.T,
        "router_bias": jnp.broadcast_to(router_bias[:, None], (N_EXPERTS, 128)),
        "w_gate": w_gate.astype(BF16), "w_up": w_up.astype(BF16), "w_down": w_down.astype(BF16),
    }
    y_prompt = _trunk(x_prompt, mod[:, :bp], wts)
    y_sample = _trunk(x_sample, mod[:, bp:bp + bs], wts)
    return (y_prompt, y_sample)
```

```python
import functools

import numpy as np
import jax
import jax.numpy as jnp
from jax import lax
from jax.experimental import pallas as pl
from jax.experimental.pallas import tpu as pltpu
from jax.experimental.pallas import tpu_sc as plsc

D_MODEL = 1024
HG_DK = 128
HG_WIDTH = 512
HG_HEADS = 4
HG_SUB = 64
HG_LEVELS = 6
HG_GROUP = 32
HG_SAFE_SPAN = 80.0
ATT_HD = 64
ATT_HEADS = 8
ATT_KV = 2
ATT_GROUP = 4
ATT_WIDTH = 512
KV_WIDTH = 128
WINDOW = 128
ATT_BLOCK = 128
N_BUCKETS = 32
REL_MAX_DIST = 128
N_EXPERTS = 16
N_GROUPS = 4
EXPERTS_PER_GROUP = 4
D_EXPERT = 512
N_MOD = 6
IN_COLS = 5376
EPS = 1e-6
NEG_BIG = -1e30
TINY = 1e-30

A_COLS = 2048
A_HQ = 0
A_HF_FWD = 512
A_HF_BWD = 1024
A_HG = 1536
B_COLS = 3328
B_GATE_H = 0
B_GATE_A = 1024
B_HI = 2048
B_AQ = 2560
B_AK = 3072
B_AV = 3200

V7X_VMEM_LIMIT = 56 * 1024 * 1024

F32 = jnp.float32
BF16 = jnp.bfloat16
HIGHEST = lax.Precision.HIGHEST


def _params(sem):
    return pltpu.CompilerParams(dimension_semantics=sem, vmem_limit_bytes=V7X_VMEM_LIMIT)


def _dot(a, b):
    return jnp.dot(a, b, preferred_element_type=F32)


def _dot_nt(a, b):
    return lax.dot_general(a, b, (((1,), (1,)), ((), ())), preferred_element_type=F32)


def _dot_tn(a, b):
    return lax.dot_general(a, b, (((0,), (0,)), ((), ())), preferred_element_type=F32)


def _sigmoid(x):
    return 1.0 / (1.0 + jnp.exp(-x))


def _silu(x):
    return x * _sigmoid(x)


def _rms(x, g):
    return x * lax.rsqrt(jnp.mean(x * x, axis=-1, keepdims=True) + EPS) * g


def _ada_kernel(c_ref, w_ref, b_ref, o_ref):
    c = c_ref[...]
    o_ref[...] = jnp.dot(_silu(c), w_ref[...], precision=HIGHEST, preferred_element_type=F32) + b_ref[...]


def _ada(c_all, w_ada, b_ada):
    depth = w_ada.shape[0]
    rows = c_all.shape[0]
    ncol = w_ada.shape[2]
    tn = 1024
    return pl.pallas_call(
        _ada_kernel,
        grid=(depth, ncol // tn),
        in_specs=[
            pl.BlockSpec((rows, D_MODEL), lambda l, j: (0, 0)),
            pl.BlockSpec((None, D_MODEL, tn), lambda l, j: (l, 0, j)),
            pl.BlockSpec((None, 1, tn), lambda l, j: (l, 0, j)),
        ],
        out_specs=pl.BlockSpec((None, rows, tn), lambda l, j: (l, 0, j)),
        out_shape=jax.ShapeDtypeStruct((depth, rows, ncol), F32),
        compiler_params=_params(("arbitrary", "arbitrary")),
        name="ada_mod",
    )(c_all, w_ada, b_ada.reshape(depth, 1, ncol))


def _inproj_kernel(x_ref, mod_ref, g_ref, wa_ref, wb_ref, oa_ref, ob_ref):
    x = x_ref[...]
    h = (_rms(x, g_ref[...]) * (1.0 + mod_ref[1:2, :]) + mod_ref[0:1, :]).astype(BF16)
    oa_ref[...] = _dot(h, wa_ref[...])
    ob_ref[...] = _dot(h, wb_ref[...]).astype(BF16)


def _inproj(x2d, mod, g, wa, wb, seq):
    n = x2d.shape[0]
    tm = min(256, seq)
    per_seq = seq // tm
    const = lambda i: (0, 0)
    return pl.pallas_call(
        _inproj_kernel,
        grid=(n // tm,),
        in_specs=[
            pl.BlockSpec((tm, D_MODEL), lambda i: (i, 0)),
            pl.BlockSpec((None, N_MOD, D_MODEL), lambda i: (i // per_seq, 0, 0)),
            pl.BlockSpec((1, D_MODEL), const),
            pl.BlockSpec((D_MODEL, A_COLS), const),
            pl.BlockSpec((D_MODEL, B_COLS), const),
        ],
        out_specs=[pl.BlockSpec((tm, A_COLS), lambda i: (i, 0)),
                   pl.BlockSpec((tm, B_COLS), lambda i: (i, 0))],
        out_shape=[jax.ShapeDtypeStruct((n, A_COLS), F32), jax.ShapeDtypeStruct((n, B_COLS), BF16)],
        compiler_params=_params(("arbitrary",)),
        name="inproj",
    )(x2d, mod, g, wa, wb)


def _lower_bound_row(gam_ref, layer):
    rows = [gam_ref[d:d + 1, :] for d in range(gam_ref.shape[0])]
    m = functools.reduce(jnp.maximum, rows)
    es = [jnp.exp(r - m) for r in rows]
    tot = functools.reduce(lambda a, b: a + b, es)
    ps = [e / tot for e in es]
    cum = ps[0]
    for d in range(1, layer + 1):
        cum = cum + ps[d]
    return jnp.clip(cum - ps[0], 0.0, 1.0)


def _hgrn_level_tables(rev):
    c = HG_SUB
    r = lax.broadcasted_iota(jnp.int32, (c, c), 0)
    s = lax.broadcasted_iota(jnp.int32, (c, c), 1)
    row = lax.broadcasted_iota(jnp.int32, (c, HG_WIDTH), 0)
    sels, qsides, pairs = [], [], []
    for lev in range(HG_LEVELS):
        half = 1 << lev
        blk = 2 * half
        r_up = (r & (blk - 1)) >= half
        s_up = (s & (blk - 1)) >= half
        base = r - (r & (blk - 1))
        mrow = base + (half if rev else half - 1)
        sels.append(jnp.where(s == mrow, 1.0, 0.0).astype(F32))
        row_up = (row & (blk - 1)) >= half
        qsides.append(~row_up if rev else row_up)
        same = (r >> (lev + 1)) == (s >> (lev + 1))
        pairs.append(same & ((~r_up & s_up) if rev else (r_up & ~s_up)))
    return jnp.concatenate(sels, axis=0), qsides, pairs, r == s


def _hgrn_kernel(*refs, layer, rev):
    if rev:
        q_ref, f_ref, v_ref, gam_ref, hg_ref, of_ref, nrm_ref, o_ref, st_ref, qs_ref, ks_ref, bs_ref = refs
    else:
        q_ref, f_ref, v_ref, gam_ref, o_ref, st_ref, qs_ref, ks_ref, bs_ref = refs

    @pl.when(pl.program_id(1) == 0)
    def _():
        st_ref[...] = jnp.zeros_like(st_ref)

    c = HG_SUB
    w = HG_WIDTH
    nsub = q_ref.shape[0] // c
    lb = _lower_bound_row(gam_ref, layer)
    r_i = lax.broadcasted_iota(jnp.int32, (c, c), 0)
    s_i = lax.broadcasted_iota(jnp.int32, (c, c), 1)
    causal = (s_i >= r_i) if rev else (s_i <= r_i)
    tri = jnp.where(causal, 1.0, 0.0).astype(BF16)
    row = lax.broadcasted_iota(jnp.int32, (c, w), 0)
    far = (row < HG_GROUP) if rev else (row >= HG_GROUP)
    ref_row = HG_GROUP if rev else HG_GROUP - 1
    last_row = 0 if rev else c - 1
    heads = [slice(h * HG_DK, (h + 1) * HG_DK) for h in range(HG_HEADS)]

    def rows_of(i):
        ci = (nsub - 1 - i) if rev else i
        return pl.ds(pl.multiple_of(ci * c, c), c)

    def prep(i, worst):
        sl = rows_of(i)
        zq = q_ref[sl, :]
        zf = f_ref[sl, :]
        f = lb + (1.0 - lb) * _sigmoid(zf)
        g = jnp.log(jnp.maximum(f, TINY))
        qs_ref[sl, :] = _silu(zq) * (HG_DK ** -0.5)
        ks_ref[sl, :] = 1.0 - f
        g1 = g.astype(BF16)
        r1 = g - g1.astype(F32)
        g2 = r1.astype(BF16)
        g3 = (r1 - g2.astype(F32)).astype(BF16)
        bb = _dot(tri, jnp.concatenate([g1, g2, g3], axis=1))
        b = (bb[:, :w] + bb[:, w:2 * w]) + bb[:, 2 * w:]
        bs_ref[sl, :] = b
        r = b[ref_row:ref_row + 1, :]
        bl = b[last_row:last_row + 1, :]
        return jnp.maximum(worst, jnp.maximum(-r, r - bl))

    worst = lax.fori_loop(0, nsub, prep, jnp.zeros((1, w), F32), unroll=4)
    safe = jnp.max(worst) <= HG_SAFE_SPAN

    def finish(sl, a, q_in, k_dec, dec, vb):
        outs = []
        for h, hs in enumerate(heads):
            st = st_ref[h]
            outs.append(_dot_nt(q_in[:, hs], st.astype(BF16)) + _dot(a[h].astype(BF16), vb[:, hs]))
            st_ref[h] = st * dec[:, hs] + _dot_tn(vb[:, hs], k_dec[:, hs])
        o_all = jnp.concatenate(outs, axis=1)
        if rev:
            tot = of_ref[sl, :] + o_all
            nrm = nrm_ref[...]
            ys = [_rms(tot[:, hs], nrm) for hs in heads]
            o_ref[sl, :] = (jnp.concatenate(ys, axis=1) * _silu(hg_ref[sl, :])).astype(o_ref.dtype)
        else:
            o_ref[sl, :] = o_all

    def factored(i, carry):
        sl = rows_of(i)
        q, k, b = qs_ref[sl, :], ks_ref[sl, :], bs_ref[sl, :]
        vb = v_ref[sl, :].astype(BF16)
        r = b[ref_row:ref_row + 1, :]
        bl = b[last_row:last_row + 1, :]
        rg = jnp.where(far, r, 0.0)
        qt = q * jnp.exp(b - rg)
        kt = k * jnp.exp(rg - b)
        er = jnp.exp(r)
        qn = jnp.where(far, 0.0, qt).astype(BF16)
        qf = jnp.where(far, qt, 0.0).astype(BF16)
        kc = jnp.where(far, kt, kt * er).astype(BF16)
        ktb = kt.astype(BF16)
        q_in = jnp.where(far, qt * er, qt).astype(BF16)
        k_dec = (kt * jnp.where(far, jnp.exp(bl - r), jnp.exp(bl))).astype(BF16)
        a = []
        for hs in heads:
            lhs = jnp.concatenate([qn[:, hs], qf[:, hs]], axis=1)
            rhs = jnp.concatenate([ktb[:, hs], kc[:, hs]], axis=1)
            a.append(jnp.where(causal, _dot_nt(lhs, rhs), 0.0))
        finish(sl, a, q_in, k_dec, jnp.exp(bl), vb)
        return carry

    def levels(i, carry):
        sl = rows_of(i)
        q, k, b = qs_ref[sl, :], ks_ref[sl, :], bs_ref[sl, :]
        vb = v_ref[sl, :].astype(BF16)
        sel_all, qsides, pairs, eye = _hgrn_level_tables(rev)
        bl = b[last_row:last_row + 1, :]
        q_in = (q * jnp.exp(b)).astype(BF16)
        k_dec = (k * jnp.exp(bl - b)).astype(BF16)
        bref_all = jnp.dot(sel_all, b, precision=HIGHEST, preferred_element_type=F32)
        qb = q.astype(BF16)
        kb = k.astype(BF16)
        a = [jnp.where(eye, _dot_nt(qb[:, hs], kb[:, hs]), 0.0) for hs in heads]
        for lev in range(HG_LEVELS):
            bref = bref_all[lev * c:(lev + 1) * c, :]
            qs = qsides[lev]
            x = jnp.exp(jnp.where(qs, b - bref, bref - b))
            ql = jnp.where(qs, q * x, 0.0).astype(BF16)
            kl = jnp.where(qs, 0.0, k * x).astype(BF16)
            for h, hs in enumerate(heads):
                a[h] = a[h] + jnp.where(pairs[lev], _dot_nt(ql[:, hs], kl[:, hs]), 0.0)
        finish(sl, a, q_in, k_dec, jnp.exp(bl), vb)
        return carry

    @pl.when(safe)
    def _():
        lax.fori_loop(0, nsub, factored, 0, unroll=4)

    @pl.when(jnp.logical_not(safe))
    def _():
        lax.fori_loop(0, nsub, levels, 0)


def _hgrn(pa3, pb3, gamma, layer, rev, o_fwd=None, nrm=None):
    bsz, seq, _ = pa3.shape
    cb = min(512, seq)
    nc = seq // cb
    wblk = HG_WIDTH

    def cmap(col):
        if rev:
            return lambda b, c: (b, nc - 1 - c, col // wblk)
        return lambda b, c: (b, c, col // wblk)

    in_specs = [
        pl.BlockSpec((None, cb, wblk), cmap(A_HQ)),
        pl.BlockSpec((None, cb, wblk), cmap(A_HF_BWD if rev else A_HF_FWD)),
        pl.BlockSpec((None, cb, wblk), cmap(B_HI)),
        pl.BlockSpec(gamma.shape, lambda b, c: (0, 0)),
    ]
    args = [pa3, pa3, pb3, gamma]
    if rev:
        in_specs += [
            pl.BlockSpec((None, cb, wblk), cmap(A_HG)),
            pl.BlockSpec((None, cb, wblk), cmap(0)),
            pl.BlockSpec((1, HG_DK), lambda b, c: (0, 0)),
        ]
        args += [pa3, o_fwd, nrm]
    out_dtype = BF16 if rev else F32
    return pl.pallas_call(
        functools.partial(_hgrn_kernel, layer=layer, rev=rev),
        grid=(bsz, nc),
        in_specs=in_specs,
        out_specs=pl.BlockSpec((None, cb, wblk), cmap(0)),
        out_shape=jax.ShapeDtypeStruct((bsz, seq, HG_WIDTH), out_dtype),
        scratch_shapes=[pltpu.VMEM((HG_HEADS, HG_DK, HG_DK), F32)] + [pltpu.VMEM((cb, wblk), F32)] * 3,
        compiler_params=_params(("arbitrary", "arbitrary")),
        name="hgrn_bwd" if rev else "hgrn_fwd",
    )(*args)


ATT_ROW_HEADS = (0, 2, 1, 3)


def _t5_buckets(rel):
    half = N_BUCKETS // 2
    ret = np.where(rel > 0, half, 0)
    n = np.abs(rel)
    max_exact = half // 2
    large = max_exact + (np.log(np.maximum(n, 1) / max_exact)
                         / np.log(REL_MAX_DIST / max_exact) * (half - max_exact)).astype(np.int32)
    large = np.minimum(large, half - 1)
    return (ret + np.where(n < max_exact, n, large)).astype(np.int32)


def _attn_tables(rel_bias, att_sink):
    c = ATT_BLOCK
    rel = np.arange(3 * c)[None, :] - c - np.arange(c)[:, None]
    onehot = np.equal(_t5_buckets(rel).reshape(-1, 1), np.arange(N_BUCKETS)).astype(np.float32)
    bias = jnp.dot(jnp.asarray(onehot, BF16), rel_bias.astype(F32), precision=HIGHEST)
    bias = bias.reshape(c, 3 * c, ATT_HEADS).transpose(2, 0, 1)
    bias = jnp.where(jnp.asarray(np.abs(rel) <= WINDOW)[None], bias, NEG_BIG)
    col = np.arange(3 * c)
    cases = []
    for case in range(4):
        valid = np.ones(3 * c, bool)
        if case & 1:
            valid &= col >= c
        if case & 2:
            valid &= col < 2 * c
        cases.append(jnp.where(jnp.asarray(valid)[None, None, :], bias, NEG_BIG))
    tab = jnp.stack(cases)
    order = np.array([[ATT_GROUP * g + h for h in ATT_ROW_HEADS] for g in range(ATT_KV)])
    tab = tab[:, order].reshape(4, ATT_KV, ATT_GROUP * c, 3 * c)
    sink = att_sink.astype(F32)[:, order]
    sink = jnp.broadcast_to(sink[..., None, None], sink.shape + (c, 128))
    return tab, sink.reshape(att_sink.shape[0], ATT_KV, ATT_GROUP * c, 128)


def _attn_kernel(q_ref, kp_ref, kc_ref, kn_ref, vp_ref, vc_ref, vn_ref, bias_ref, sink_ref, o_ref):
    n = pl.program_id(1)
    nsteps = pl.num_programs(1)
    c = ATT_BLOCK
    nsub = q_ref.shape[0] // c
    pair = 2 * ATT_HD
    kwin = jnp.concatenate([kp_ref[...], kc_ref[...], kn_ref[...]], axis=0).astype(F32)
    vwin = jnp.concatenate([vp_ref[...], vc_ref[...], vn_ref[...]], axis=0).astype(F32)
    kroll = pltpu.roll(kwin, ATT_HD, axis=1)
    vroll = pltpu.roll(vwin, ATT_HD, axis=1)
    lo = lax.broadcasted_iota(jnp.int32, kwin.shape, 1) < ATT_HD
    k_lo = [jnp.where(lo, kwin, 0.0).astype(BF16), jnp.where(lo, kroll, 0.0).astype(BF16)]
    k_hi = [jnp.where(lo, 0.0, kroll).astype(BF16), jnp.where(lo, 0.0, kwin).astype(BF16)]
    v_lo = [jnp.where(lo, vwin, 0.0).astype(BF16), jnp.where(lo, vroll, 0.0).astype(BF16)]
    v_hi = [jnp.where(lo, 0.0, vroll).astype(BF16), jnp.where(lo, 0.0, vwin).astype(BF16)]
    ones = jnp.ones((3 * c, pair), BF16)
    for j in range(nsub):
        case = jnp.int32(0)
        if j == 0:
            case = case + (n == 0).astype(jnp.int32)
        if j == nsub - 1:
            case = case + 2 * (n == nsteps - 1).astype(jnp.int32)
        band = slice(j * c, (j + 3) * c)
        qrows = slice(j * c, (j + 1) * c)
        for g in range(ATT_KV):
            qg = q_ref[qrows, g * 2 * pair:(g + 1) * 2 * pair].astype(F32) * (ATT_HD ** -0.5)
            lhs = jnp.concatenate([qg[:, :pair], qg[:, pair:]], axis=0).astype(BF16)
            rhs = jnp.concatenate([k_lo[g][band], k_hi[g][band]], axis=0)
            lg = _dot_nt(lhs, rhs)
            logits = jnp.concatenate([lg[:, :3 * c], lg[:, 3 * c:]], axis=0) + bias_ref[case, g]
            s = sink_ref[g][:, 0:1]
            m = jnp.maximum(jnp.max(logits, axis=-1, keepdims=True), s)
            p = jnp.exp(logits - m).astype(BF16)
            den = _dot(p, ones) + jnp.exp(s - m)
            o = (_dot(p[:2 * c], v_lo[g][band]) / den[:2 * c]
                 + _dot(p[2 * c:], v_hi[g][band]) / den[2 * c:])
            o_ref[qrows, g * 2 * pair:g * 2 * pair + pair] = o[:c].astype(o_ref.dtype)
            o_ref[qrows, g * 2 * pair + pair:(g + 1) * 2 * pair] = o[c:].astype(o_ref.dtype)


def _attn(proj3, bias, sink):
    bsz, seq, _ = proj3.shape
    c = ATT_BLOCK
    nsub = 2 if seq % (2 * c) == 0 else 1
    qb = nsub * c
    nsteps = seq // qb
    nb = seq // c
    kcol = B_AK // KV_WIDTH
    vcol = B_AV // KV_WIDTH

    def band(col):
        return [
            pl.BlockSpec((None, c, KV_WIDTH), lambda b, n: (b, jnp.maximum(n * nsub - 1, 0), col)),
            pl.BlockSpec((None, qb, KV_WIDTH), lambda b, n: (b, n, col)),
            pl.BlockSpec((None, c, KV_WIDTH), lambda b, n: (b, jnp.minimum((n + 1) * nsub, nb - 1), col)),
        ]

    return pl.pallas_call(
        _attn_kernel,
        grid=(bsz, nsteps),
        in_specs=[pl.BlockSpec((None, qb, ATT_WIDTH), lambda b, n: (b, n, B_AQ // ATT_WIDTH))]
        + band(kcol) + band(vcol)
        + [pl.BlockSpec(bias.shape, lambda b, n: (0, 0, 0, 0)),
           pl.BlockSpec(sink.shape, lambda b, n: (0, 0, 0))],
        out_specs=pl.BlockSpec((None, qb, ATT_WIDTH), lambda b, n: (b, n, 0)),
        out_shape=jax.ShapeDtypeStruct((bsz, seq, ATT_WIDTH), BF16),
        compiler_params=_params(("arbitrary", "arbitrary")),
        name="window_attn",
    )(proj3, proj3, proj3, proj3, proj3, proj3, proj3, bias, sink)


def _first_argmax(vals):
    best, idx = vals[0], jnp.zeros(vals[0].shape, jnp.int32)
    for j in range(1, len(vals)):
        upd = vals[j] > best
        idx = jnp.where(upd, j, idx)
        best = jnp.where(upd, vals[j], best)
    return best, idx


def _select(vals, idx):
    out = vals[0]
    for j in range(1, len(vals)):
        out = jnp.where(idx == j, vals[j], out)
    return out


def _route(logits_t, rbias):
    m = jnp.max(logits_t, axis=0, keepdims=True)
    e = jnp.exp(logits_t - m)
    scores = e / jnp.sum(e, axis=0, keepdims=True)
    sel = scores + rbias
    srow = [scores[i:i + 1, :] for i in range(N_EXPERTS)]
    lrow = [sel[i:i + 1, :] for i in range(N_EXPERTS)]
    gscore = []
    for g in range(N_GROUPS):
        a, b, c, d = lrow[4 * g:4 * g + 4]
        hi1, lo1 = jnp.maximum(a, b), jnp.minimum(a, b)
        hi2, lo2 = jnp.maximum(c, d), jnp.minimum(c, d)
        gscore.append(jnp.maximum(hi1, hi2) + jnp.maximum(jnp.minimum(hi1, hi2), jnp.maximum(lo1, lo2)))
    _, gi = _first_argmax(gscore)
    ing = [_select([lrow[4 * g + j] for g in range(N_GROUPS)], gi) for j in range(EXPERTS_PER_GROUP)]
    sg = [_select([srow[4 * g + j] for g in range(N_GROUPS)], gi) for j in range(EXPERTS_PER_GROUP)]
    _, i1 = _first_argmax(ing)
    rest = [jnp.where(i1 == j, -jnp.inf, ing[j]) for j in range(EXPERTS_PER_GROUP)]
    _, i2 = _first_argmax(rest)
    s1, s2 = _select(sg, i1), _select(sg, i2)
    tot = s1 + s2
    w1, w2 = s1 / tot, s2 / tot
    idx1 = gi * EXPERTS_PER_GROUP + i1
    idx2 = gi * EXPERTS_PER_GROUP + i2
    return idx1, idx2, w1, w2


def _merge_kernel(oh_ref, oa_ref, gh_ref, ga_ref, x_ref, mod_ref, nf_ref, wbh_ref, wba_ref, wo_ref,
                  wrt_ref, rb_ref, x1_ref, h2_ref, ri_ref, rw_ref, cnt_ref, carry_ref):
    @pl.when(pl.program_id(0) == 0)
    def _():
        carry_ref[...] = jnp.zeros_like(carry_ref)

    mh = _dot(oh_ref[...], wbh_ref[...])
    ma = _dot(oa_ref[...], wba_ref[...])
    merged = _sigmoid(gh_ref[...].astype(F32)) * mh + _sigmoid(ga_ref[...].astype(F32)) * ma
    out = _dot(merged.astype(BF16), wo_ref[...])
    x1 = x_ref[...] + mod_ref[2:3, :] * out
    x1_ref[...] = x1
    h2 = _rms(x1, nf_ref[...]) * (1.0 + mod_ref[4:5, :]) + mod_ref[3:4, :]
    h2_ref[...] = h2
    logits_t = lax.dot_general(wrt_ref[...], h2, (((1,), (1,)), ((), ())), precision=HIGHEST,
                               preferred_element_type=F32)
    idx1, idx2, w1, w2 = _route(logits_t, rb_ref[:, 0:1])
    tm = h2.shape[0]
    erow = lax.broadcasted_iota(jnp.int32, logits_t.shape, 0)
    oh1 = erow == idx1
    oh2 = erow == idx2
    oh = jnp.where(oh1 | oh2, 1.0, 0.0)
    t_src = lax.broadcasted_iota(jnp.int32, (tm, tm), 0)
    t_dst = lax.broadcasted_iota(jnp.int32, (tm, tm), 1)
    before = jnp.where(t_src < t_dst, 1.0, 0.0).astype(BF16)
    pref = _dot(oh.astype(BF16), before) + carry_ref[:, 0:1]
    rank1 = jnp.sum(jnp.where(oh1, pref, 0.0), axis=0, keepdims=True)
    rank2 = jnp.sum(jnp.where(oh2, pref, 0.0), axis=0, keepdims=True)
    carry = carry_ref[...] + jnp.sum(oh, axis=1, keepdims=True)
    carry_ref[...] = carry
    cnt_ref[...] = carry
    ri_ref[...] = jnp.concatenate([idx1, idx2, rank1.astype(jnp.int32), rank2.astype(jnp.int32)], axis=0)
    rw_ref[...] = jnp.concatenate([w1, w2], axis=0)


def _merge(o_h, o_a, proj, x2d, mod, nf, wbh, wba, wo, wrt, rb, seq):
    n = x2d.shape[0]
    tm = min(512, seq)
    per_seq = seq // tm
    const = lambda i: (0, 0)
    return pl.pallas_call(
        _merge_kernel,
        grid=(n // tm,),
        in_specs=[
            pl.BlockSpec((tm, HG_WIDTH), lambda i: (i, 0)),
            pl.BlockSpec((tm, ATT_WIDTH), lambda i: (i, 0)),
            pl.BlockSpec((tm, D_MODEL), lambda i: (i, B_GATE_H // D_MODEL)),
            pl.BlockSpec((tm, D_MODEL), lambda i: (i, B_GATE_A // D_MODEL)),
            pl.BlockSpec((tm, D_MODEL), lambda i: (i, 0)),
            pl.BlockSpec((None, N_MOD, D_MODEL), lambda i: (i // per_seq, 0, 0)),
            pl.BlockSpec((1, D_MODEL), const),
            pl.BlockSpec(wbh.shape, const),
            pl.BlockSpec(wba.shape, const),
            pl.BlockSpec(wo.shape, const),
            pl.BlockSpec(wrt.shape, const),
            pl.BlockSpec(rb.shape, const),
        ],
        out_specs=[
            pl.BlockSpec((tm, D_MODEL), lambda i: (i, 0)),
            pl.BlockSpec((tm, D_MODEL), lambda i: (i, 0)),
            pl.BlockSpec((4, tm), lambda i: (0, i)),
            pl.BlockSpec((2, tm), lambda i: (0, i)),
            pl.BlockSpec((N_EXPERTS, 128), const),
        ],
        out_shape=[
            jax.ShapeDtypeStruct((n, D_MODEL), F32),
            jax.ShapeDtypeStruct((n, D_MODEL), F32),
            jax.ShapeDtypeStruct((4, n), jnp.int32),
            jax.ShapeDtypeStruct((2, n), F32),
            jax.ShapeDtypeStruct((N_EXPERTS, 128), F32),
        ],
        scratch_shapes=[pltpu.VMEM((N_EXPERTS, 128), F32)],
        compiler_params=_params(("arbitrary",)),
        name="merge_router",
    )(o_h, o_a, proj, proj, x2d, mod, nf, wbh, wba, wo, wrt, rb)


MOE_TILE = 512
SC_WINDOW = 32


def _sc_mesh():
    return plsc.VectorSubcoreMesh(core_axis_name="c", subcore_axis_name="s")


def _sc_dispatch(h, pos, p):
    n, d = h.shape
    win = SC_WINDOW
    info = plsc.get_sparse_core_info()
    workers = info.num_cores * info.num_subcores
    wpt = n // (win * workers)
    pos_w = pos.reshape(2, n // win, win).transpose(1, 0, 2)

    @functools.partial(
        pl.kernel, out_type=jax.ShapeDtypeStruct((p, d), h.dtype), mesh=_sc_mesh(),
        scratch_types=[pltpu.VMEM((wpt, 2, win), jnp.int32), pltpu.VMEM((2, win, d), h.dtype),
                       pltpu.SemaphoreType.DMA((2,))],
        name="moe_dispatch")
    def dispatch(h_hbm, pos_hbm, o_hbm, idx_v, rows_v, load_sem):
        wid = lax.axis_index("c") * info.num_subcores + lax.axis_index("s")
        first = wid * wpt
        pltpu.sync_copy(pos_hbm.at[pl.ds(first, wpt)], idx_v)

        def load(j, slot):
            return pltpu.make_async_copy(h_hbm.at[pl.ds((first + j) * win, win)], rows_v.at[slot],
                                         load_sem.at[slot])

        load(0, 0).start()

        @pl.loop(0, wpt, step=2)
        def _(j):
            for slot in range(2):
                jj = j + slot
                load(jj, slot).wait()

                @pl.when(jj + 1 < wpt)
                def _():
                    load(jj + 1, 1 - slot).start()

                pltpu.sync_copy(rows_v.at[slot], o_hbm.at[idx_v.at[jj, 0]])
                pltpu.sync_copy(rows_v.at[slot], o_hbm.at[idx_v.at[jj, 1]])

    return dispatch(h, pos_w)


def _sc_combine(ys, pos):
    n = pos.shape[1]
    d = ys.shape[1]
    win = SC_WINDOW // 2
    info = plsc.get_sparse_core_info()
    workers = info.num_cores * info.num_subcores
    wpt = n // (win * workers)
    pos_w = pos.reshape(2, n // win, win).transpose(1, 0, 2)
    out = jax.ShapeDtypeStruct((n, d), ys.dtype)

    @functools.partial(
        pl.kernel, out_type=(out, out), mesh=_sc_mesh(),
        scratch_types=[pltpu.VMEM((wpt, 2, win), jnp.int32), pltpu.VMEM((2, 2, win, d), ys.dtype),
                       pltpu.SemaphoreType.DMA((2, 2)), pltpu.SemaphoreType.DMA((2,))],
        name="moe_combine")
    def combine(ys_hbm, pos_hbm, a_hbm, b_hbm, idx_v, rows_v, gather_sem, store_sem):
        wid = lax.axis_index("c") * info.num_subcores + lax.axis_index("s")
        first = wid * wpt
        pltpu.sync_copy(pos_hbm.at[pl.ds(first, wpt)], idx_v)
        outs = (a_hbm, b_hbm)

        def gather(j, slot, k):
            return pltpu.make_async_copy(ys_hbm.at[idx_v.at[j, k]], rows_v.at[slot, k], gather_sem.at[slot, k])

        def store(j, slot, k):
            return pltpu.make_async_copy(rows_v.at[slot, k], outs[k].at[pl.ds((first + j) * win, win)],
                                         store_sem.at[k])

        gather(0, 0, 0).start()
        gather(0, 0, 1).start()

        @pl.loop(0, wpt, step=2)
        def _(j):
            for slot in range(2):
                jj = j + slot
                gather(jj, slot, 0).wait()
                gather(jj, slot, 1).wait()

                @pl.when(jj + 1 < wpt)
                def _():
                    gather(jj + 1, 1 - slot, 0).start()
                    gather(jj + 1, 1 - slot, 1).start()

                store(jj, slot, 0).start()
                store(jj, slot, 1).start()
                store(jj, slot, 0).wait()
                store(jj, slot, 1).wait()

    return combine(ys, pos_w)


def _expert_kernel(te_ref, nu_ref, x_ref, wg_ref, wu_ref, wd_ref, o_ref):
    del te_ref
    used = pl.program_id(0) < nu_ref[0]

    @pl.when(used)
    def _():
        x = x_ref[...].astype(BF16)
        he = _silu(_dot(x, wg_ref[...])) * _dot(x, wu_ref[...])
        o_ref[...] = _dot(he.astype(BF16), wd_ref[...])

    @pl.when(jnp.logical_not(used))
    def _():
        o_ref[...] = jnp.zeros_like(o_ref)


def _experts(xs, tile_expert, n_used, wg, wu, wd):
    p = xs.shape[0]
    tm = MOE_TILE
    grid_spec = pltpu.PrefetchScalarGridSpec(
        num_scalar_prefetch=2,
        grid=(p // tm,),
        in_specs=[
            pl.BlockSpec((tm, D_MODEL), lambda i, te, nu: (i, 0)),
            pl.BlockSpec((None, D_MODEL, D_EXPERT), lambda i, te, nu: (te[i], 0, 0)),
            pl.BlockSpec((None, D_MODEL, D_EXPERT), lambda i, te, nu: (te[i], 0, 0)),
            pl.BlockSpec((None, D_EXPERT, D_MODEL), lambda i, te, nu: (te[i], 0, 0)),
        ],
        out_specs=pl.BlockSpec((tm, D_MODEL), lambda i, te, nu: (i, 0)),
    )
    return pl.pallas_call(
        _expert_kernel,
        grid_spec=grid_spec,
        out_shape=jax.ShapeDtypeStruct((p, D_MODEL), F32),
        compiler_params=_params(("arbitrary",)),
        name="moe_experts",
    )(tile_expert, n_used, xs, wg, wu, wd)


def _residual_kernel(ya_ref, yb_ref, w_ref, x1_ref, mod_ref, nfin_ref, o_ref, *, last):
    w = w_ref[...]
    y = w[:, 0:1] * ya_ref[...] + w[:, 1:2] * yb_ref[...]
    x2 = x1_ref[...] + mod_ref[5:6, :] * y
    if last:
        x2 = _rms(x2, nfin_ref[...])
    o_ref[...] = x2


def _residual(ya, yb, w, x1, mod, nfin, seq, last):
    n = x1.shape[0]
    tm = min(512, seq)
    per_seq = seq // tm
    row = pl.BlockSpec((tm, D_MODEL), lambda i: (i, 0))
    return pl.pallas_call(
        functools.partial(_residual_kernel, last=last),
        grid=(n // tm,),
        in_specs=[row, row, pl.BlockSpec((tm, 2), lambda i: (i, 0)), row,
                  pl.BlockSpec((None, N_MOD, D_MODEL), lambda i: (i // per_seq, 0, 0)),
                  pl.BlockSpec((1, D_MODEL), lambda i: (0, 0))],
        out_specs=row,
        out_shape=jax.ShapeDtypeStruct((n, D_MODEL), F32),
        compiler_params=_params(("arbitrary",)),
        name="moe_residual",
    )(ya, yb, w, x1, mod, nfin)


def _moe(h2, ri, rw, cnt, wg, wu, wd, x1, mod, nfin, seq, last):
    n = h2.shape[0]
    tm = MOE_TILE
    p = 2 * n + N_EXPERTS * tm
    counts = cnt[:, 0].astype(jnp.int32)
    padded = (counts + tm - 1) // tm * tm
    ends = jnp.cumsum(padded)
    starts = ends - padded
    base = jnp.zeros_like(ri[0:2])
    for e in range(1, N_EXPERTS):
        base = jnp.where(ri[0:2] == e, starts[e], base)
    pos = base + ri[2:4]
    tile_start = jnp.arange(p // tm, dtype=jnp.int32) * tm
    tile_expert = jnp.minimum(jnp.sum(tile_start[:, None] >= ends[None, :], axis=1), N_EXPERTS - 1)
    n_used = (ends[-1:] // tm).astype(jnp.int32)
    xs = _sc_dispatch(h2, pos, p)
    ys = _experts(xs, tile_expert.astype(jnp.int32), n_used, wg, wu, wd)
    ya, yb = _sc_combine(ys, pos)
    return _residual(ya, yb, rw.T, x1, mod, nfin, seq, last)


def _split_w_in(w):
    hq_hf, hi, hg, att, gates = w[..., :1536], w[..., 1536:2048], w[..., 2048:2560], w[..., 2560:3328], w[..., 3328:]
    wa = jnp.concatenate([hq_hf, hg], axis=-1).astype(BF16)
    wb = jnp.concatenate([gates, hi, att], axis=-1).astype(BF16)
    return wa, wb


def _trunk(x, mod, wts):
    bsz, seq, _ = x.shape
    n = bsz * seq
    depth = wts["w_in_a"].shape[0]
    x2d = x.reshape(n, D_MODEL)
    for l in range(depth):
        mod_l = mod[l]
        pa, pb = _inproj(x2d, mod_l, wts["norm_mix"][l:l + 1], wts["w_in_a"][l], wts["w_in_b"][l], seq)
        pa3 = pa.reshape(bsz, seq, A_COLS)
        pb3 = pb.reshape(bsz, seq, B_COLS)
        o_f = _hgrn(pa3, pb3, wts["hg_lb_fwd"], l, False)
        o_h = _hgrn(pa3, pb3, wts["hg_lb_bwd"], l, True, o_f, wts["hg_norm"][l:l + 1])
        o_a = _attn(pb3, wts["bias"], wts["sink"][l])
        x1, h2, ri, rw, cnt = _merge(o_h.reshape(n, HG_WIDTH), o_a.reshape(n, ATT_WIDTH), pb, x2d, mod_l,
                                     wts["norm_ffn"][l:l + 1], wts["w_br_hgrn"][l], wts["w_br_att"][l],
                                     wts["w_out"][l], wts["w_router_t"], wts["router_bias"], seq)
        x2d = _moe(h2, ri, rw, cnt, wts["w_gate"][l], wts["w_up"][l], wts["w_down"][l], x1, mod_l,
                   wts["norm_final"], seq, l == depth - 1)
    return x2d.reshape(bsz, seq, D_MODEL)


def kernel(x_prompt, x_sample, c_prompt, c_sample, w_ada, b_ada, norm_mix, norm_ffn, norm_final, w_in, hg_lb_fwd, hg_lb_bwd, hg_norm, att_sink, rel_bias, w_br_hgrn, w_br_att, w_out, w_router, router_bias, w_gate, w_up, w_down):
    depth = w_in.shape[0]
    bp, bs = c_prompt.shape[0], c_sample.shape[0]
    rows = -(-(bp + bs) // 8) * 8
    c_all = jnp.concatenate([c_prompt, c_sample, jnp.zeros((rows - bp - bs, D_MODEL), F32)], axis=0)
    mod = _ada(c_all, w_ada, b_ada).reshape(depth, rows, N_MOD, D_MODEL)
    bias, sink = _attn_tables(rel_bias, att_sink)
    w_in_a, w_in_b = _split_w_in(w_in)
    wts = {
        "norm_mix": norm_mix, "norm_ffn": norm_ffn, "norm_final": norm_final.reshape(1, D_MODEL),
        "w_in_a": w_in_a, "w_in_b": w_in_b,
        "hg_lb_fwd": hg_lb_fwd, "hg_lb_bwd": hg_lb_bwd, "hg_norm": hg_norm,
        "sink": sink, "bias": bias,
        "w_br_hgrn": w_br_hgrn.astype(BF16), "w_br_att": w_br_att.astype(BF16), "w_out": w_out.astype(BF16),
        "w_router_t": w_router.T,
        "router_bias": jnp.broadcast_to(router_bias[:, None], (N_EXPERTS, 128)),
        "w_gate": w_gate.astype(BF16), "w_up": w_up.astype(BF16), "w_down": w_down.astype(BF16),
    }
    y_prompt = _trunk(x_prompt, mod[:, :bp], wts)
    y_sample = _trunk(x_sample, mod[:, bp:bp + bs], wts)
    return (y_prompt, y_sample)
```

```python
import functools

import numpy as np
import jax
import jax.numpy as jnp
from jax import lax
from jax.experimental import pallas as pl
from jax.experimental.pallas import tpu as pltpu
from jax.experimental.pallas import tpu_sc as plsc

D_MODEL = 1024
HG_DK = 128
HG_WIDTH = 512
HG_HEADS = 4
HG_SUB = 64
HG_LEVELS = 6
HG_GROUP = 32
HG_SAFE_SPAN = 80.0
ATT_HD = 64
ATT_HEADS = 8
ATT_KV = 2
ATT_GROUP = 4
ATT_WIDTH = 512
KV_WIDTH = 128
WINDOW = 128
ATT_BLOCK = 128
N_BUCKETS = 32
REL_MAX_DIST = 128
N_EXPERTS = 16
N_GROUPS = 4
EXPERTS_PER_GROUP = 4
D_EXPERT = 512
N_MOD = 6
IN_COLS = 5376
EPS = 1e-6
NEG_BIG = -1e30
TINY = 1e-30

A_COLS = 2048
A_HQ = 0
A_HF_FWD = 512
A_HF_BWD = 1024
A_HG = 1536
B_COLS = 3328
B_GATE_H = 0
B_GATE_A = 1024
B_HI = 2048
B_AQ = 2560
B_AK = 3072
B_AV = 3200

V7X_VMEM_LIMIT = 56 * 1024 * 1024

F32 = jnp.float32
BF16 = jnp.bfloat16
HIGHEST = lax.Precision.HIGHEST


def _params(sem):
    return pltpu.CompilerParams(dimension_semantics=sem, vmem_limit_bytes=V7X_VMEM_LIMIT)


def _dot(a, b):
    return jnp.dot(a, b, preferred_element_type=F32)


def _dot_nt(a, b):
    return lax.dot_general(a, b, (((1,), (1,)), ((), ())), preferred_element_type=F32)


def _dot_tn(a, b):
    return lax.dot_general(a, b, (((0,), (0,)), ((), ())), preferred_element_type=F32)


def _sigmoid(x):
    return 0.5 * jnp.tanh(0.5 * x) + 0.5


def _pack_bf16_pairs(x):
    c = x.shape[1] // 2
    bits = lax.bitcast_convert_type(x.astype(BF16).astype(F32), jnp.uint32)
    return (bits[:, :c] >> 16) | (bits[:, c:] & jnp.uint32(0xFFFF0000))


def _unpack_bf16_pairs(u):
    lo = lax.bitcast_convert_type(u << 16, F32)
    hi = lax.bitcast_convert_type(u & jnp.uint32(0xFFFF0000), F32)
    return jnp.concatenate([lo, hi], axis=1)


def _silu(x):
    return x * _sigmoid(x)


def _rms(x, g):
    return x * lax.rsqrt(jnp.mean(x * x, axis=-1, keepdims=True) + EPS) * g


def _ada_kernel(c_ref, w_ref, b_ref, o_ref):
    c = c_ref[...]
    o_ref[...] = jnp.dot(_silu(c), w_ref[...], precision=HIGHEST, preferred_element_type=F32) + b_ref[...]


def _ada(c_all, w_ada, b_ada):
    depth = w_ada.shape[0]
    rows = c_all.shape[0]
    ncol = w_ada.shape[2]
    tn = 1024
    return pl.pallas_call(
        _ada_kernel,
        grid=(depth, ncol // tn),
        in_specs=[
            pl.BlockSpec((rows, D_MODEL), lambda l, j: (0, 0)),
            pl.BlockSpec((None, D_MODEL, tn), lambda l, j: (l, 0, j)),
            pl.BlockSpec((None, 1, tn), lambda l, j: (l, 0, j)),
        ],
        out_specs=pl.BlockSpec((None, rows, tn), lambda l, j: (l, 0, j)),
        out_shape=jax.ShapeDtypeStruct((depth, rows, ncol), F32),
        compiler_params=_params(("arbitrary", "arbitrary")),
        name="ada_mod",
    )(c_all, w_ada, b_ada.reshape(depth, 1, ncol))


def _inproj_kernel(x_ref, mod_ref, g_ref, wa_ref, wb_ref, oa_ref, ob_ref):
    x = x_ref[...]
    h = (_rms(x, g_ref[...]) * (1.0 + mod_ref[1:2, :]) + mod_ref[0:1, :]).astype(BF16)
    oa_ref[...] = _dot(h, wa_ref[...])
    ob_ref[...] = _dot(h, wb_ref[...]).astype(BF16)


def _inproj(x2d, mod, g, wa, wb, seq):
    n = x2d.shape[0]
    tm = min(256, seq)
    per_seq = seq // tm
    const = lambda i: (0, 0)
    return pl.pallas_call(
        _inproj_kernel,
        grid=(n // tm,),
        in_specs=[
            pl.BlockSpec((tm, D_MODEL), lambda i: (i, 0)),
            pl.BlockSpec((None, N_MOD, D_MODEL), lambda i: (i // per_seq, 0, 0)),
            pl.BlockSpec((1, D_MODEL), const),
            pl.BlockSpec((D_MODEL, A_COLS), const),
            pl.BlockSpec((D_MODEL, B_COLS), const),
        ],
        out_specs=[pl.BlockSpec((tm, A_COLS), lambda i: (i, 0)),
                   pl.BlockSpec((tm, B_COLS), lambda i: (i, 0))],
        out_shape=[jax.ShapeDtypeStruct((n, A_COLS), F32), jax.ShapeDtypeStruct((n, B_COLS), BF16)],
        compiler_params=_params(("arbitrary",)),
        name="inproj",
    )(x2d, mod, g, wa, wb)


def _lower_bound_row(gam_ref, layer):
    rows = [gam_ref[d:d + 1, :] for d in range(gam_ref.shape[0])]
    m = functools.reduce(jnp.maximum, rows)
    es = [jnp.exp(r - m) for r in rows]
    tot = functools.reduce(lambda a, b: a + b, es)
    ps = [e / tot for e in es]
    cum = ps[0]
    for d in range(1, layer + 1):
        cum = cum + ps[d]
    return jnp.clip(cum - ps[0], 0.0, 1.0)


def _hgrn_level_tables(rev):
    c = HG_SUB
    r = lax.broadcasted_iota(jnp.int32, (c, c), 0)
    s = lax.broadcasted_iota(jnp.int32, (c, c), 1)
    row = lax.broadcasted_iota(jnp.int32, (c, HG_WIDTH), 0)
    sels, qsides, pairs = [], [], []
    for lev in range(HG_LEVELS):
        half = 1 << lev
        blk = 2 * half
        r_up = (r & (blk - 1)) >= half
        s_up = (s & (blk - 1)) >= half
        base = r - (r & (blk - 1))
        mrow = base + (half if rev else half - 1)
        sels.append(jnp.where(s == mrow, 1.0, 0.0).astype(F32))
        row_up = (row & (blk - 1)) >= half
        qsides.append(~row_up if rev else row_up)
        same = (r >> (lev + 1)) == (s >> (lev + 1))
        pairs.append(same & ((~r_up & s_up) if rev else (r_up & ~s_up)))
    return jnp.concatenate(sels, axis=0), qsides, pairs, r == s


def _hgrn_kernel(*refs, layer, rev):
    if rev:
        q_ref, f_ref, v_ref, gam_ref, hg_ref, of_ref, nrm_ref, o_ref, st_ref, qs_ref, ks_ref, bs_ref = refs
    else:
        q_ref, f_ref, v_ref, gam_ref, o_ref, st_ref, qs_ref, ks_ref, bs_ref = refs

    @pl.when(pl.program_id(1) == 0)
    def _():
        st_ref[...] = jnp.zeros_like(st_ref)

    c = HG_SUB
    w = HG_WIDTH
    nsub = q_ref.shape[0] // c
    lb = _lower_bound_row(gam_ref, layer)
    r_i = lax.broadcasted_iota(jnp.int32, (c, c), 0)
    s_i = lax.broadcasted_iota(jnp.int32, (c, c), 1)
    causal = (s_i >= r_i) if rev else (s_i <= r_i)
    tri = jnp.where(causal, 1.0, 0.0).astype(BF16)
    row = lax.broadcasted_iota(jnp.int32, (c, w), 0)
    far = (row < HG_GROUP) if rev else (row >= HG_GROUP)
    ref_row = HG_GROUP if rev else HG_GROUP - 1
    last_row = 0 if rev else c - 1
    heads = [slice(h * HG_DK, (h + 1) * HG_DK) for h in range(HG_HEADS)]

    def rows_of(i):
        ci = (nsub - 1 - i) if rev else i
        return pl.ds(pl.multiple_of(ci * c, c), c)

    def prep(i, worst):
        sl = rows_of(i)
        zq = q_ref[sl, :]
        zf = f_ref[sl, :]
        f = lb + (1.0 - lb) * _sigmoid(zf)
        g = jnp.log(jnp.maximum(f, TINY))
        qs_ref[sl, :] = _silu(zq) * (HG_DK ** -0.5)
        ks_ref[sl, :] = 1.0 - f
        g1 = g.astype(BF16)
        r1 = g - g1.astype(F32)
        g2 = r1.astype(BF16)
        g3 = (r1 - g2.astype(F32)).astype(BF16)
        bb = _dot(tri, jnp.concatenate([g1, g2, g3], axis=1))
        b = (bb[:, :w] + bb[:, w:2 * w]) + bb[:, 2 * w:]
        bs_ref[sl, :] = b
        r = b[ref_row:ref_row + 1, :]
        bl = b[last_row:last_row + 1, :]
        return jnp.maximum(worst, jnp.maximum(-r, r - bl))

    worst = lax.fori_loop(0, nsub, prep, jnp.zeros((1, w), F32), unroll=4)
    safe = jnp.max(worst) <= HG_SAFE_SPAN

    def finish(sl, a, q_in, k_dec, dec, vb):
        outs = []
        for h, hs in enumerate(heads):
            st = st_ref[h]
            outs.append(_dot_nt(q_in[:, hs], st.astype(BF16)) + _dot(a[h].astype(BF16), vb[:, hs]))
            st_ref[h] = st * dec[:, hs] + _dot_tn(vb[:, hs], k_dec[:, hs])
        o_all = jnp.concatenate(outs, axis=1)
        if rev:
            tot = of_ref[sl, :] + o_all
            nrm = nrm_ref[...]
            ys = [_rms(tot[:, hs], nrm) for hs in heads]
            o_ref[sl, :] = (jnp.concatenate(ys, axis=1) * _silu(hg_ref[sl, :])).astype(o_ref.dtype)
        else:
            o_ref[sl, :] = o_all

    def factored(i, carry):
        sl = rows_of(i)
        q, k, b = qs_ref[sl, :], ks_ref[sl, :], bs_ref[sl, :]
        vb = v_ref[sl, :].astype(BF16)
        r = b[ref_row:ref_row + 1, :]
        bl = b[last_row:last_row + 1, :]
        rg = jnp.where(far, r, 0.0)
        qt = q * jnp.exp(b - rg)
        kt = k * jnp.exp(rg - b)
        er = jnp.exp(r)
        qn = jnp.where(far, 0.0, qt).astype(BF16)
        qf = jnp.where(far, qt, 0.0).astype(BF16)
        kc = jnp.where(far, kt, kt * er).astype(BF16)
        ktb = kt.astype(BF16)
        q_in = jnp.where(far, qt * er, qt).astype(BF16)
        k_dec = (kt * jnp.where(far, jnp.exp(bl - r), jnp.exp(bl))).astype(BF16)
        a = []
        for hs in heads:
            lhs = jnp.concatenate([qn[:, hs], qf[:, hs]], axis=1)
            rhs = jnp.concatenate([ktb[:, hs], kc[:, hs]], axis=1)
            a.append(jnp.where(causal, _dot_nt(lhs, rhs), 0.0))
        finish(sl, a, q_in, k_dec, jnp.exp(bl), vb)
        return carry

    def levels(i, carry):
        sl = rows_of(i)
        q, k, b = qs_ref[sl, :], ks_ref[sl, :], bs_ref[sl, :]
        vb = v_ref[sl, :].astype(BF16)
        sel_all, qsides, pairs, eye = _hgrn_level_tables(rev)
        bl = b[last_row:last_row + 1, :]
        q_in = (q * jnp.exp(b)).astype(BF16)
        k_dec = (k * jnp.exp(bl - b)).astype(BF16)
        bref_all = jnp.dot(sel_all, b, precision=HIGHEST, preferred_element_type=F32)
        qb = q.astype(BF16)
        kb = k.astype(BF16)
        a = [jnp.where(eye, _dot_nt(qb[:, hs], kb[:, hs]), 0.0) for hs in heads]
        for lev in range(HG_LEVELS):
            bref = bref_all[lev * c:(lev + 1) * c, :]
            qs = qsides[lev]
            x = jnp.exp(jnp.where(qs, b - bref, bref - b))
            ql = jnp.where(qs, q * x, 0.0).astype(BF16)
            kl = jnp.where(qs, 0.0, k * x).astype(BF16)
            for h, hs in enumerate(heads):
                a[h] = a[h] + jnp.where(pairs[lev], _dot_nt(ql[:, hs], kl[:, hs]), 0.0)
        finish(sl, a, q_in, k_dec, jnp.exp(bl), vb)
        return carry

    @pl.when(safe)
    def _():
        lax.fori_loop(0, nsub, factored, 0, unroll=4)

    @pl.when(jnp.logical_not(safe))
    def _():
        lax.fori_loop(0, nsub, levels, 0)


def _hgrn(pa3, pb3, gamma, layer, rev, o_fwd=None, nrm=None):
    bsz, seq, _ = pa3.shape
    cb = min(512, seq)
    nc = seq // cb
    wblk = HG_WIDTH

    def cmap(col):
        if rev:
            return lambda b, c: (b, nc - 1 - c, col // wblk)
        return lambda b, c: (b, c, col // wblk)

    in_specs = [
        pl.BlockSpec((None, cb, wblk), cmap(A_HQ)),
        pl.BlockSpec((None, cb, wblk), cmap(A_HF_BWD if rev else A_HF_FWD)),
        pl.BlockSpec((None, cb, wblk), cmap(B_HI)),
        pl.BlockSpec(gamma.shape, lambda b, c: (0, 0)),
    ]
    args = [pa3, pa3, pb3, gamma]
    if rev:
        in_specs += [
            pl.BlockSpec((None, cb, wblk), cmap(A_HG)),
            pl.BlockSpec((None, cb, wblk), cmap(0)),
            pl.BlockSpec((1, HG_DK), lambda b, c: (0, 0)),
        ]
        args += [pa3, o_fwd, nrm]
    out_dtype = BF16 if rev else F32
    return pl.pallas_call(
        functools.partial(_hgrn_kernel, layer=layer, rev=rev),
        grid=(bsz, nc),
        in_specs=in_specs,
        out_specs=pl.BlockSpec((None, cb, wblk), cmap(0)),
        out_shape=jax.ShapeDtypeStruct((bsz, seq, HG_WIDTH), out_dtype),
        scratch_shapes=[pltpu.VMEM((HG_HEADS, HG_DK, HG_DK), F32)] + [pltpu.VMEM((cb, wblk), F32)] * 3,
        compiler_params=_params(("arbitrary", "arbitrary")),
        name="hgrn_bwd" if rev else "hgrn_fwd",
    )(*args)


ATT_ROW_HEADS = (0, 2, 1, 3)


def _t5_buckets(rel):
    half = N_BUCKETS // 2
    ret = np.where(rel > 0, half, 0)
    n = np.abs(rel)
    max_exact = half // 2
    large = max_exact + (np.log(np.maximum(n, 1) / max_exact)
                         / np.log(REL_MAX_DIST / max_exact) * (half - max_exact)).astype(np.int32)
    large = np.minimum(large, half - 1)
    return (ret + np.where(n < max_exact, n, large)).astype(np.int32)


def _attn_tables(rel_bias, att_sink):
    c = ATT_BLOCK
    rel = np.arange(3 * c)[None, :] - c - np.arange(c)[:, None]
    onehot = np.equal(_t5_buckets(rel).reshape(-1, 1), np.arange(N_BUCKETS)).astype(np.float32)
    bias = jnp.dot(jnp.asarray(onehot, BF16), rel_bias.astype(F32), precision=HIGHEST)
    bias = bias.reshape(c, 3 * c, ATT_HEADS).transpose(2, 0, 1)
    bias = jnp.where(jnp.asarray(np.abs(rel) <= WINDOW)[None], bias, NEG_BIG)
    col = np.arange(3 * c)
    cases = []
    for case in range(4):
        valid = np.ones(3 * c, bool)
        if case & 1:
            valid &= col >= c
        if case & 2:
            valid &= col < 2 * c
        cases.append(jnp.where(jnp.asarray(valid)[None, None, :], bias, NEG_BIG))
    tab = jnp.stack(cases)
    order = np.array([[ATT_GROUP * g + h for h in ATT_ROW_HEADS] for g in range(ATT_KV)])
    tab = tab[:, order].reshape(4, ATT_KV, ATT_GROUP * c, 3 * c)
    sink = att_sink.astype(F32)[:, order]
    sink = jnp.broadcast_to(sink[..., None, None], sink.shape + (c, 128))
    return tab, sink.reshape(att_sink.shape[0], ATT_KV, ATT_GROUP * c, 128)


def _attn_kernel(q_ref, kp_ref, kc_ref, kn_ref, vp_ref, vc_ref, vn_ref, bias_ref, sink_ref, o_ref):
    n = pl.program_id(1)
    nsteps = pl.num_programs(1)
    c = ATT_BLOCK
    nsub = q_ref.shape[0] // c
    pair = 2 * ATT_HD
    kwin = jnp.concatenate([kp_ref[...], kc_ref[...], kn_ref[...]], axis=0).astype(F32)
    vwin = jnp.concatenate([vp_ref[...], vc_ref[...], vn_ref[...]], axis=0).astype(F32)
    kroll = pltpu.roll(kwin, ATT_HD, axis=1)
    vroll = pltpu.roll(vwin, ATT_HD, axis=1)
    lo = lax.broadcasted_iota(jnp.int32, kwin.shape, 1) < ATT_HD
    k_lo = [jnp.where(lo, kwin, 0.0).astype(BF16), jnp.where(lo, kroll, 0.0).astype(BF16)]
    k_hi = [jnp.where(lo, 0.0, kroll).astype(BF16), jnp.where(lo, 0.0, kwin).astype(BF16)]
    v_lo = [jnp.where(lo, vwin, 1.0).astype(BF16), jnp.where(lo, vroll, 1.0).astype(BF16)]
    v_hi = [jnp.where(lo, 1.0, vroll).astype(BF16), jnp.where(lo, 1.0, vwin).astype(BF16)]
    lo_out = lax.broadcasted_iota(jnp.int32, (2 * c, pair), 1) < ATT_HD
    for j in range(nsub):
        case = jnp.int32(0)
        if j == 0:
            case = case + (n == 0).astype(jnp.int32)
        if j == nsub - 1:
            case = case + 2 * (n == nsteps - 1).astype(jnp.int32)
        band = slice(j * c, (j + 3) * c)
        qrows = slice(j * c, (j + 1) * c)
        for g in range(ATT_KV):
            qg = q_ref[qrows, g * 2 * pair:(g + 1) * 2 * pair].astype(F32) * (ATT_HD ** -0.5)
            lhs = jnp.concatenate([qg[:, :pair], qg[:, pair:]], axis=0).astype(BF16)
            rhs = jnp.concatenate([k_lo[g][band], k_hi[g][band]], axis=0)
            lg = _dot_nt(lhs, rhs)
            logits = jnp.concatenate([lg[:, :3 * c], lg[:, 3 * c:]], axis=0) + bias_ref[case, g]
            s = sink_ref[g][:, 0:1]
            m = jnp.maximum(jnp.max(logits, axis=-1, keepdims=True), s)
            p = jnp.exp(logits - m).astype(BF16)
            es = jnp.exp(s - m)
            o_even = _dot(p[:2 * c], v_lo[g][band])
            o_odd = _dot(p[2 * c:], v_hi[g][band])
            o = jnp.where(lo_out,
                          o_even / (pltpu.roll(o_even, ATT_HD, axis=1) + es[:2 * c]),
                          o_odd / (pltpu.roll(o_odd, ATT_HD, axis=1) + es[2 * c:]))
            o_ref[qrows, g * 2 * pair:g * 2 * pair + pair] = o[:c].astype(o_ref.dtype)
            o_ref[qrows, g * 2 * pair + pair:(g + 1) * 2 * pair] = o[c:].astype(o_ref.dtype)


def _attn(proj3, bias, sink):
    bsz, seq, _ = proj3.shape
    c = ATT_BLOCK
    nsub = 2 if seq % (2 * c) == 0 else 1
    qb = nsub * c
    nsteps = seq // qb
    nb = seq // c
    kcol = B_AK // KV_WIDTH
    vcol = B_AV // KV_WIDTH

    def band(col):
        return [
            pl.BlockSpec((None, c, KV_WIDTH), lambda b, n: (b, jnp.maximum(n * nsub - 1, 0), col)),
            pl.BlockSpec((None, qb, KV_WIDTH), lambda b, n: (b, n, col)),
            pl.BlockSpec((None, c, KV_WIDTH), lambda b, n: (b, jnp.minimum((n + 1) * nsub, nb - 1), col)),
        ]

    return pl.pallas_call(
        _attn_kernel,
        grid=(bsz, nsteps),
        in_specs=[pl.BlockSpec((None, qb, ATT_WIDTH), lambda b, n: (b, n, B_AQ // ATT_WIDTH))]
        + band(kcol) + band(vcol)
        + [pl.BlockSpec(bias.shape, lambda b, n: (0, 0, 0, 0)),
           pl.BlockSpec(sink.shape, lambda b, n: (0, 0, 0))],
        out_specs=pl.BlockSpec((None, qb, ATT_WIDTH), lambda b, n: (b, n, 0)),
        out_shape=jax.ShapeDtypeStruct((bsz, seq, ATT_WIDTH), BF16),
        compiler_params=_params(("arbitrary", "arbitrary")),
        name="window_attn",
    )(proj3, proj3, proj3, proj3, proj3, proj3, proj3, bias, sink)


def _first_argmax(vals):
    best, idx = vals[0], jnp.zeros(vals[0].shape, jnp.int32)
    for j in range(1, len(vals)):
        upd = vals[j] > best
        idx = jnp.where(upd, j, idx)
        best = jnp.where(upd, vals[j], best)
    return best, idx


def _select(vals, idx):
    out = vals[0]
    for j in range(1, len(vals)):
        out = jnp.where(idx == j, vals[j], out)
    return out


def _route(logits_t, rbias):
    m = jnp.max(logits_t, axis=0, keepdims=True)
    e = jnp.exp(logits_t - m)
    scores = e / jnp.sum(e, axis=0, keepdims=True)
    sel = scores + rbias
    srow = [scores[i:i + 1, :] for i in range(N_EXPERTS)]
    lrow = [sel[i:i + 1, :] for i in range(N_EXPERTS)]
    gscore = []
    for g in range(N_GROUPS):
        a, b, c, d = lrow[4 * g:4 * g + 4]
        hi1, lo1 = jnp.maximum(a, b), jnp.minimum(a, b)
        hi2, lo2 = jnp.maximum(c, d), jnp.minimum(c, d)
        gscore.append(jnp.maximum(hi1, hi2) + jnp.maximum(jnp.minimum(hi1, hi2), jnp.maximum(lo1, lo2)))
    _, gi = _first_argmax(gscore)
    ing = [_select([lrow[4 * g + j] for g in range(N_GROUPS)], gi) for j in range(EXPERTS_PER_GROUP)]
    sg = [_select([srow[4 * g + j] for g in range(N_GROUPS)], gi) for j in range(EXPERTS_PER_GROUP)]
    _, i1 = _first_argmax(ing)
    rest = [jnp.where(i1 == j, -jnp.inf, ing[j]) for j in range(EXPERTS_PER_GROUP)]
    _, i2 = _first_argmax(rest)
    s1, s2 = _select(sg, i1), _select(sg, i2)
    tot = s1 + s2
    w1, w2 = s1 / tot, s2 / tot
    idx1 = gi * EXPERTS_PER_GROUP + i1
    idx2 = gi * EXPERTS_PER_GROUP + i2
    return idx1, idx2, w1, w2


def _merge_kernel(oh_ref, oa_ref, gh_ref, ga_ref, x_ref, mod_ref, nf_ref, wbh_ref, wba_ref, wo_ref,
                  wrt_ref, rb_ref, x1_ref, h2_ref, ri_ref, rw_ref, cnt_ref, carry_ref):
    @pl.when(pl.program_id(0) == 0)
    def _():
        carry_ref[...] = jnp.zeros_like(carry_ref)

    mh = _dot(oh_ref[...], wbh_ref[...])
    ma = _dot(oa_ref[...], wba_ref[...])
    merged = _sigmoid(gh_ref[...].astype(F32)) * mh + _sigmoid(ga_ref[...].astype(F32)) * ma
    out = _dot(merged.astype(BF16), wo_ref[...])
    x1 = x_ref[...] + mod_ref[2:3, :] * out
    x1_ref[...] = x1
    h2 = _rms(x1, nf_ref[...]) * (1.0 + mod_ref[4:5, :]) + mod_ref[3:4, :]
    h2_ref[...] = _pack_bf16_pairs(h2)
    h_hi = h2.astype(BF16)
    h_lo = (h2 - h_hi.astype(F32)).astype(BF16)
    wr = wrt_ref[...]
    w_hi = wr.astype(BF16)
    w_lo = (wr - w_hi.astype(F32)).astype(BF16)
    logits_t = (_dot_nt(w_hi, h_hi) + _dot_nt(w_hi, h_lo)) + _dot_nt(w_lo, h_hi)
    idx1, idx2, w1, w2 = _route(logits_t, rb_ref[:, 0:1])
    tm = h2.shape[0]
    erow = lax.broadcasted_iota(jnp.int32, logits_t.shape, 0)
    oh1 = erow == idx1
    oh2 = erow == idx2
    oh = jnp.where(oh1 | oh2, 1.0, 0.0)
    t_src = lax.broadcasted_iota(jnp.int32, (tm, tm), 0)
    t_dst = lax.broadcasted_iota(jnp.int32, (tm, tm), 1)
    before = jnp.where(t_src < t_dst, 1.0, 0.0).astype(BF16)
    pref = _dot(oh.astype(BF16), before) + carry_ref[:, 0:1]
    rank1 = jnp.sum(jnp.where(oh1, pref, 0.0), axis=0, keepdims=True)
    rank2 = jnp.sum(jnp.where(oh2, pref, 0.0), axis=0, keepdims=True)
    carry = carry_ref[...] + jnp.sum(oh, axis=1, keepdims=True)
    carry_ref[...] = carry
    cnt_ref[...] = carry
    ri_ref[...] = jnp.concatenate([idx1, idx2, rank1.astype(jnp.int32), rank2.astype(jnp.int32)], axis=0)
    rw_ref[...] = jnp.concatenate([w1, w2], axis=0)


def _merge(o_h, o_a, proj, x2d, mod, nf, wbh, wba, wo, wrt, rb, seq):
    n = x2d.shape[0]
    tm = min(512, seq)
    per_seq = seq // tm
    const = lambda i: (0, 0)
    return pl.pallas_call(
        _merge_kernel,
        grid=(n // tm,),
        in_specs=[
            pl.BlockSpec((tm, HG_WIDTH), lambda i: (i, 0)),
            pl.BlockSpec((tm, ATT_WIDTH), lambda i: (i, 0)),
            pl.BlockSpec((tm, D_MODEL), lambda i: (i, B_GATE_H // D_MODEL)),
            pl.BlockSpec((tm, D_MODEL), lambda i: (i, B_GATE_A // D_MODEL)),
            pl.BlockSpec((tm, D_MODEL), lambda i: (i, 0)),
            pl.BlockSpec((None, N_MOD, D_MODEL), lambda i: (i // per_seq, 0, 0)),
            pl.BlockSpec((1, D_MODEL), const),
            pl.BlockSpec(wbh.shape, const),
            pl.BlockSpec(wba.shape, const),
            pl.BlockSpec(wo.shape, const),
            pl.BlockSpec(wrt.shape, const),
            pl.BlockSpec(rb.shape, const),
        ],
        out_specs=[
            pl.BlockSpec((tm, D_MODEL), lambda i: (i, 0)),
            pl.BlockSpec((tm, D_MODEL // 2), lambda i: (i, 0)),
            pl.BlockSpec((4, tm), lambda i: (0, i)),
            pl.BlockSpec((2, tm), lambda i: (0, i)),
            pl.BlockSpec((N_EXPERTS, 128), const),
        ],
        out_shape=[
            jax.ShapeDtypeStruct((n, D_MODEL), F32),
            jax.ShapeDtypeStruct((n, D_MODEL // 2), jnp.uint32),
            jax.ShapeDtypeStruct((4, n), jnp.int32),
            jax.ShapeDtypeStruct((2, n), F32),
            jax.ShapeDtypeStruct((N_EXPERTS, 128), F32),
        ],
        scratch_shapes=[pltpu.VMEM((N_EXPERTS, 128), F32)],
        compiler_params=_params(("arbitrary",)),
        name="merge_router",
    )(o_h, o_a, proj, proj, x2d, mod, nf, wbh, wba, wo, wrt, rb)


MOE_TILE = 512
SC_WINDOW = 64


def _sc_mesh():
    return plsc.VectorSubcoreMesh(core_axis_name="c", subcore_axis_name="s")


def _sc_dispatch(h, pos, p):
    n, d = h.shape
    win = SC_WINDOW
    info = plsc.get_sparse_core_info()
    workers = info.num_cores * info.num_subcores
    wpt = n // (win * workers)
    pos_w = pos.reshape(2, n // win, win).transpose(1, 0, 2)

    @functools.partial(
        pl.kernel, out_type=jax.ShapeDtypeStruct((p, d), h.dtype), mesh=_sc_mesh(),
        scratch_types=[pltpu.VMEM((wpt, 2, win), jnp.int32), pltpu.VMEM((2, win, d), h.dtype),
                       pltpu.SemaphoreType.DMA((2,))],
        name="moe_dispatch")
    def dispatch(h_hbm, pos_hbm, o_hbm, idx_v, rows_v, load_sem):
        wid = lax.axis_index("c") * info.num_subcores + lax.axis_index("s")
        first = wid * wpt
        pltpu.sync_copy(pos_hbm.at[pl.ds(first, wpt)], idx_v)

        def load(j, slot):
            return pltpu.make_async_copy(h_hbm.at[pl.ds((first + j) * win, win)], rows_v.at[slot],
                                         load_sem.at[slot])

        load(0, 0).start()

        @pl.loop(0, wpt, step=2)
        def _(j):
            for slot in range(2):
                jj = j + slot
                load(jj, slot).wait()

                @pl.when(jj + 1 < wpt)
                def _():
                    load(jj + 1, 1 - slot).start()

                pltpu.sync_copy(rows_v.at[slot], o_hbm.at[idx_v.at[jj, 0]])
                pltpu.sync_copy(rows_v.at[slot], o_hbm.at[idx_v.at[jj, 1]])

    return dispatch(h, pos_w)


def _sc_combine(ys, pos):
    n = pos.shape[1]
    d = ys.shape[1]
    win = SC_WINDOW // 2
    info = plsc.get_sparse_core_info()
    workers = info.num_cores * info.num_subcores
    wpt = n // (win * workers)
    pos_w = pos.reshape(2, n // win, win).transpose(1, 0, 2)
    out = jax.ShapeDtypeStruct((n, d), ys.dtype)

    @functools.partial(
        pl.kernel, out_type=(out, out), mesh=_sc_mesh(),
        scratch_types=[pltpu.VMEM((wpt, 2, win), jnp.int32), pltpu.VMEM((2, 2, win, d), ys.dtype),
                       pltpu.SemaphoreType.DMA((2, 2)), pltpu.SemaphoreType.DMA((2,))],
        name="moe_combine")
    def combine(ys_hbm, pos_hbm, a_hbm, b_hbm, idx_v, rows_v, gather_sem, store_sem):
        wid = lax.axis_index("c") * info.num_subcores + lax.axis_index("s")
        first = wid * wpt
        pltpu.sync_copy(pos_hbm.at[pl.ds(first, wpt)], idx_v)
        outs = (a_hbm, b_hbm)

        def gather(j, slot, k):
            return pltpu.make_async_copy(ys_hbm.at[idx_v.at[j, k]], rows_v.at[slot, k], gather_sem.at[slot, k])

        def store(j, slot, k):
            return pltpu.make_async_copy(rows_v.at[slot, k], outs[k].at[pl.ds((first + j) * win, win)],
                                         store_sem.at[k])

        gather(0, 0, 0).start()
        gather(0, 0, 1).start()

        @pl.loop(0, wpt, step=2)
        def _(j):
            for slot in range(2):
                jj = j + slot
                gather(jj, slot, 0).wait()
                gather(jj, slot, 1).wait()

                @pl.when(jj + 1 < wpt)
                def _():
                    gather(jj + 1, 1 - slot, 0).start()
                    gather(jj + 1, 1 - slot, 1).start()

                store(jj, slot, 0).start()
                store(jj, slot, 1).start()
                store(jj, slot, 0).wait()
                store(jj, slot, 1).wait()

    return combine(ys, pos_w)


def _expert_kernel(te_ref, nu_ref, x_ref, wg_ref, wu_ref, wd_ref, o_ref):
    del te_ref
    used = pl.program_id(0) < nu_ref[0]

    @pl.when(used)
    def _():
        x = _unpack_bf16_pairs(x_ref[...]).astype(BF16)
        he = _silu(_dot(x, wg_ref[...])) * _dot(x, wu_ref[...])
        o_ref[...] = _pack_bf16_pairs(_dot(he.astype(BF16), wd_ref[...]))

    @pl.when(jnp.logical_not(used))
    def _():
        o_ref[...] = jnp.zeros_like(o_ref)


def _experts(xs, tile_expert, n_used, wg, wu, wd):
    p = xs.shape[0]
    tm = MOE_TILE
    grid_spec = pltpu.PrefetchScalarGridSpec(
        num_scalar_prefetch=2,
        grid=(p // tm,),
        in_specs=[
            pl.BlockSpec((tm, D_MODEL // 2), lambda i, te, nu: (i, 0)),
            pl.BlockSpec((None, D_MODEL, D_EXPERT), lambda i, te, nu: (te[i], 0, 0)),
            pl.BlockSpec((None, D_MODEL, D_EXPERT), lambda i, te, nu: (te[i], 0, 0)),
            pl.BlockSpec((None, D_EXPERT, D_MODEL), lambda i, te, nu: (te[i], 0, 0)),
        ],
        out_specs=pl.BlockSpec((tm, D_MODEL // 2), lambda i, te, nu: (i, 0)),
    )
    return pl.pallas_call(
        _expert_kernel,
        grid_spec=grid_spec,
        out_shape=jax.ShapeDtypeStruct((p, D_MODEL // 2), jnp.uint32),
        compiler_params=_params(("arbitrary",)),
        name="moe_experts",
    )(tile_expert, n_used, xs, wg, wu, wd)


def _residual_kernel(ya_ref, yb_ref, w_ref, x1_ref, mod_ref, nfin_ref, o_ref, *, last):
    w = w_ref[...]
    y = w[:, 0:1] * _unpack_bf16_pairs(ya_ref[...]) + w[:, 1:2] * _unpack_bf16_pairs(yb_ref[...])
    x2 = x1_ref[...] + mod_ref[5:6, :] * y
    if last:
        x2 = _rms(x2, nfin_ref[...])
    o_ref[...] = x2


def _residual(ya, yb, w, x1, mod, nfin, seq, last):
    n = x1.shape[0]
    tm = min(512, seq)
    per_seq = seq // tm
    row = pl.BlockSpec((tm, D_MODEL), lambda i: (i, 0))
    packed = pl.BlockSpec((tm, D_MODEL // 2), lambda i: (i, 0))
    return pl.pallas_call(
        functools.partial(_residual_kernel, last=last),
        grid=(n // tm,),
        in_specs=[packed, packed, pl.BlockSpec((tm, 2), lambda i: (i, 0)), row,
                  pl.BlockSpec((None, N_MOD, D_MODEL), lambda i: (i // per_seq, 0, 0)),
                  pl.BlockSpec((1, D_MODEL), lambda i: (0, 0))],
        out_specs=row,
        out_shape=jax.ShapeDtypeStruct((n, D_MODEL), F32),
        compiler_params=_params(("arbitrary",)),
        name="moe_residual",
    )(ya, yb, w, x1, mod, nfin)


def _moe(h2, ri, rw, cnt, wg, wu, wd, x1, mod, nfin, seq, last):
    n = h2.shape[0]
    tm = MOE_TILE
    p = 2 * n + N_EXPERTS * tm
    counts = cnt[:, 0].astype(jnp.int32)
    padded = (counts + tm - 1) // tm * tm
    ends = jnp.cumsum(padded)
    starts = ends - padded
    base = jnp.zeros_like(ri[0:2])
    for e in range(1, N_EXPERTS):
        base = jnp.where(ri[0:2] == e, starts[e], base)
    pos = base + ri[2:4]
    tile_start = jnp.arange(p // tm, dtype=jnp.int32) * tm
    tile_expert = jnp.minimum(jnp.sum(tile_start[:, None] >= ends[None, :], axis=1), N_EXPERTS - 1)
    n_used = (ends[-1:] // tm).astype(jnp.int32)
    xs = _sc_dispatch(h2, pos, p)
    ys = _experts(xs, tile_expert.astype(jnp.int32), n_used, wg, wu, wd)
    ya, yb = _sc_combine(ys, pos)
    return _residual(ya, yb, rw.T, x1, mod, nfin, seq, last)


def _split_w_in(w):
    hq_hf, hi, hg, att, gates = w[..., :1536], w[..., 1536:2048], w[..., 2048:2560], w[..., 2560:3328], w[..., 3328:]
    wa = jnp.concatenate([hq_hf, hg], axis=-1).astype(BF16)
    wb = jnp.concatenate([gates, hi, att], axis=-1).astype(BF16)
    return wa, wb


def _trunk(x, mod, wts):
    bsz, seq, _ = x.shape
    n = bsz * seq
    depth = wts["w_in_a"].shape[0]
    x2d = x.reshape(n, D_MODEL)
    for l in range(depth):
        mod_l = mod[l]
        pa, pb = _inproj(x2d, mod_l, wts["norm_mix"][l:l + 1], wts["w_in_a"][l], wts["w_in_b"][l], seq)
        pa3 = pa.reshape(bsz, seq, A_COLS)
        pb3 = pb.reshape(bsz, seq, B_COLS)
        o_f = _hgrn(pa3, pb3, wts["hg_lb_fwd"], l, False)
        o_h = _hgrn(pa3, pb3, wts["hg_lb_bwd"], l, True, o_f, wts["hg_norm"][l:l + 1])
        o_a = _attn(pb3, wts["bias"], wts["sink"][l])
        x1, h2, ri, rw, cnt = _merge(o_h.reshape(n, HG_WIDTH), o_a.reshape(n, ATT_WIDTH), pb, x2d, mod_l,
                                     wts["norm_ffn"][l:l + 1], wts["w_br_hgrn"][l], wts["w_br_att"][l],
                                     wts["w_out"][l], wts["w_router_t"], wts["router_bias"], seq)
        x2d = _moe(h2, ri, rw, cnt, wts["w_gate"][l], wts["w_up"][l], wts["w_down"][l], x1, mod_l,
                   wts["norm_final"], seq, l == depth - 1)
    return x2d.reshape(bsz, seq, D_MODEL)


def kernel(x_prompt, x_sample, c_prompt, c_sample, w_ada, b_ada, norm_mix, norm_ffn, norm_final, w_in, hg_lb_fwd, hg_lb_bwd, hg_norm, att_sink, rel_bias, w_br_hgrn, w_br_att, w_out, w_router, router_bias, w_gate, w_up, w_down):
    depth = w_in.shape[0]
    bp, bs = c_prompt.shape[0], c_sample.shape[0]
    rows = -(-(bp + bs) // 8) * 8
    c_all = jnp.concatenate([c_prompt, c_sample, jnp.zeros((rows - bp - bs, D_MODEL), F32)], axis=0)
    mod = _ada(c_all, w_ada, b_ada).reshape(depth, rows, N_MOD, D_MODEL)
    bias, sink = _attn_tables(rel_bias, att_sink)
    w_in_a, w_in_b = _split_w_in(w_in)
    wts = {
        "norm_mix": norm_mix, "norm_ffn": norm_ffn, "norm_final": norm_final.reshape(1, D_MODEL),
        "w_in_a": w_in_a, "w_in_b": w_in_b,
        "hg_lb_fwd": hg_lb_fwd, "hg_lb_bwd": hg_lb_bwd, "hg_norm": hg_norm,
        "sink": sink, "bias": bias,
        "w_br_hgrn": w_br_hgrn.astype(BF16), "w_br_att": w_br_att.astype(BF16), "w_out": w_out.astype(BF16),
        "w_router_t": w_router.T,
        "router_bias": jnp.broadcast_to(router_bias[:, None], (N_EXPERTS, 128)),
        "w_gate": w_gate.astype(BF16), "w_up": w_up.astype(BF16), "w_down": w_down.astype(BF16),
    }
    y_prompt = _trunk(x_prompt, mod[:, :bp], wts)
    y_sample = _trunk(x_sample, mod[:, bp:bp + bs], wts)
    return (y_prompt, y_sample)
```

```python
import functools

import numpy as np
import jax
import jax.numpy as jnp
from jax import lax
from jax.experimental import pallas as pl
from jax.experimental.pallas import tpu as pltpu
from jax.experimental.pallas import tpu_sc as plsc

D_MODEL = 1024
HG_DK = 128
HG_WIDTH = 512
HG_HEADS = 4
HG_SUB = 64
HG_LEVELS = 6
HG_GROUP = 32
HG_SAFE_SPAN = 80.0
HG_BLOCK = 512
ATT_HD = 64
ATT_HEADS = 8
ATT_KV = 2
ATT_GROUP = 4
ATT_WIDTH = 512
KV_WIDTH = 128
WINDOW = 128
ATT_BLOCK = 128
N_BUCKETS = 32
REL_MAX_DIST = 128
N_EXPERTS = 16
N_GROUPS = 4
EXPERTS_PER_GROUP = 4
D_EXPERT = 512
N_MOD = 6
IN_COLS = 5376
EPS = 1e-6
NEG_BIG = -1e30
TINY = 1e-30

A_COLS = 2048
A_HQ = 0
A_HF_FWD = 512
A_HF_BWD = 1024
A_HG = 1536
B_COLS = 3328
B_GATE_H = 0
B_GATE_A = 1024
B_HI = 2048
B_AQ = 2560
B_AK = 3072
B_AV = 3200

V7X_VMEM_LIMIT = 56 * 1024 * 1024

F32 = jnp.float32
BF16 = jnp.bfloat16
HIGHEST = lax.Precision.HIGHEST


def _params(sem):
    return pltpu.CompilerParams(dimension_semantics=sem, vmem_limit_bytes=V7X_VMEM_LIMIT)


def _dot(a, b):
    return jnp.dot(a, b, preferred_element_type=F32)


def _dot_nt(a, b):
    return lax.dot_general(a, b, (((1,), (1,)), ((), ())), preferred_element_type=F32)


def _dot_tn(a, b):
    return lax.dot_general(a, b, (((0,), (0,)), ((), ())), preferred_element_type=F32)


def _sigmoid(x):
    return 0.5 * jnp.tanh(0.5 * x) + 0.5


def _pack_bf16_pairs(x):
    c = x.shape[1] // 2
    bits = lax.bitcast_convert_type(x.astype(BF16).astype(F32), jnp.uint32)
    return (bits[:, :c] >> 16) | (bits[:, c:] & jnp.uint32(0xFFFF0000))


def _unpack_bf16_pairs(u):
    lo = lax.bitcast_convert_type(u << 16, F32)
    hi = lax.bitcast_convert_type(u & jnp.uint32(0xFFFF0000), F32)
    return jnp.concatenate([lo, hi], axis=1)


def _silu(x):
    return x * _sigmoid(x)


def _rms(x, g):
    return x * lax.rsqrt(jnp.mean(x * x, axis=-1, keepdims=True) + EPS) * g


def _ada_kernel(c_ref, w_ref, b_ref, o_ref):
    c = c_ref[...]
    o_ref[...] = jnp.dot(_silu(c), w_ref[...], precision=HIGHEST, preferred_element_type=F32) + b_ref[...]


def _ada(c_all, w_ada, b_ada):
    depth = w_ada.shape[0]
    rows = c_all.shape[0]
    ncol = w_ada.shape[2]
    tn = 1024
    return pl.pallas_call(
        _ada_kernel,
        grid=(depth, ncol // tn),
        in_specs=[
            pl.BlockSpec((rows, D_MODEL), lambda l, j: (0, 0)),
            pl.BlockSpec((None, D_MODEL, tn), lambda l, j: (l, 0, j)),
            pl.BlockSpec((None, 1, tn), lambda l, j: (l, 0, j)),
        ],
        out_specs=pl.BlockSpec((None, rows, tn), lambda l, j: (l, 0, j)),
        out_shape=jax.ShapeDtypeStruct((depth, rows, ncol), F32),
        compiler_params=_params(("arbitrary", "arbitrary")),
        name="ada_mod",
    )(c_all, w_ada, b_ada.reshape(depth, 1, ncol))


def _lower_bound_row(gam_ref, layer):
    rows = [gam_ref[d:d + 1, :] for d in range(gam_ref.shape[0])]
    m = functools.reduce(jnp.maximum, rows)
    es = [jnp.exp(r - m) for r in rows]
    tot = functools.reduce(lambda a, b: a + b, es)
    ps = [e / tot for e in es]
    cum = ps[0]
    for d in range(1, layer + 1):
        cum = cum + ps[d]
    return jnp.clip(cum - ps[0], 0.0, 1.0)


def _forget(z, lb):
    return lb + (1.0 - lb) * _sigmoid(z)


def _inproj_kernel(x_ref, mod_ref, g_ref, wa_ref, wb_ref, gf_ref, gb_ref, oa_ref, ob_ref, span_ref, *, layer):
    x = x_ref[...]
    h = (_rms(x, g_ref[...]) * (1.0 + mod_ref[1:2, :]) + mod_ref[0:1, :]).astype(BF16)
    oa = _dot(h, wa_ref[...])
    oa_ref[...] = oa
    ob_ref[...] = _dot(h, wb_ref[...]).astype(BF16)
    spans = []
    for gam_ref, col in ((gf_ref, A_HF_FWD), (gb_ref, A_HF_BWD)):
        f = _forget(oa[:, col:col + HG_WIDTH], _lower_bound_row(gam_ref, layer))
        g = jnp.log(jnp.maximum(f, TINY))
        gsum = jnp.sum(g.reshape(g.shape[0] // HG_GROUP, HG_GROUP, HG_WIDTH), axis=1)
        spans.append(jnp.max(jnp.max(-gsum, axis=0, keepdims=True), axis=1, keepdims=True))
    half = lax.broadcasted_iota(jnp.int32, span_ref.shape, 0) < span_ref.shape[0] // 2
    span_ref[...] = jnp.where(half, spans[0], spans[1])


def _inproj(x2d, mod, g, wa, wb, gam_f, gam_b, seq, layer):
    n = x2d.shape[0]
    tm = min(HG_BLOCK, seq)
    per_seq = seq // tm
    const = lambda i: (0, 0)
    return pl.pallas_call(
        functools.partial(_inproj_kernel, layer=layer),
        grid=(n // tm,),
        in_specs=[
            pl.BlockSpec((tm, D_MODEL), lambda i: (i, 0)),
            pl.BlockSpec((None, N_MOD, D_MODEL), lambda i: (i // per_seq, 0, 0)),
            pl.BlockSpec((1, D_MODEL), const),
            pl.BlockSpec((D_MODEL, A_COLS), const, pipeline_mode=pl.Buffered(1)),
            pl.BlockSpec((D_MODEL, B_COLS), const, pipeline_mode=pl.Buffered(1)),
            pl.BlockSpec(gam_f.shape, const),
            pl.BlockSpec(gam_b.shape, const),
        ],
        out_specs=[pl.BlockSpec((tm, A_COLS), lambda i: (i, 0)),
                   pl.BlockSpec((tm, B_COLS), lambda i: (i, 0)),
                   pl.BlockSpec((None, 8, 128), lambda i: (i, 0, 0))],
        out_shape=[jax.ShapeDtypeStruct((n, A_COLS), F32), jax.ShapeDtypeStruct((n, B_COLS), BF16),
                   jax.ShapeDtypeStruct((n // tm, 8, 128), F32)],
        compiler_params=_params(("arbitrary",)),
        name="inproj",
    )(x2d, mod, g, wa, wb, gam_f, gam_b)


def _hgrn_level_tables(rev):
    c = HG_SUB
    r = lax.broadcasted_iota(jnp.int32, (c, c), 0)
    s = lax.broadcasted_iota(jnp.int32, (c, c), 1)
    row = lax.broadcasted_iota(jnp.int32, (c, HG_WIDTH), 0)
    sels, qsides, pairs = [], [], []
    for lev in range(HG_LEVELS):
        half = 1 << lev
        blk = 2 * half
        r_up = (r & (blk - 1)) >= half
        s_up = (s & (blk - 1)) >= half
        base = r - (r & (blk - 1))
        mrow = base + (half if rev else half - 1)
        sels.append(jnp.where(s == mrow, 1.0, 0.0).astype(F32))
        row_up = (row & (blk - 1)) >= half
        qsides.append(~row_up if rev else row_up)
        same = (r >> (lev + 1)) == (s >> (lev + 1))
        pairs.append(same & ((~r_up & s_up) if rev else (r_up & ~s_up)))
    return jnp.concatenate(sels, axis=0), qsides, pairs, r == s


def _hgrn_kernel(safe_ref, *refs, layer, rev):
    if rev:
        q_ref, f_ref, v_ref, gam_ref, hg_ref, of_ref, nrm_ref, o_ref = refs[:8]
    else:
        q_ref, f_ref, v_ref, gam_ref, o_ref = refs[:5]
    st_ref, qs_ref, ks_ref, bs_ref, a_ref, qi_ref, dec_ref, up_ref, sb_ref = refs[-9:]

    @pl.when(pl.program_id(1) == 0)
    def _():
        st_ref[...] = jnp.zeros_like(st_ref)

    nc = pl.num_programs(1)
    chunk = (nc - 1 - pl.program_id(1)) if rev else pl.program_id(1)
    safe = safe_ref[pl.program_id(0) * nc + chunk] != 0
    c = HG_SUB
    w = HG_WIDTH
    nsub = q_ref.shape[0] // c
    lb = _lower_bound_row(gam_ref, layer)
    r_i = lax.broadcasted_iota(jnp.int32, (c, c), 0)
    s_i = lax.broadcasted_iota(jnp.int32, (c, c), 1)
    causal = (s_i >= r_i) if rev else (s_i <= r_i)
    tri = jnp.where(causal, 1.0, 0.0).astype(BF16)
    row = lax.broadcasted_iota(jnp.int32, (c, w), 0)
    far = (row < HG_GROUP) if rev else (row >= HG_GROUP)
    ref_row = HG_GROUP if rev else HG_GROUP - 1
    last_row = 0 if rev else c - 1
    heads = [slice(h * HG_DK, (h + 1) * HG_DK) for h in range(HG_HEADS)]

    def rows_of(i):
        ci = (nsub - 1 - i) if rev else i
        return pl.ds(pl.multiple_of(ci * c, c), c)

    def gates_pass(i, carry):
        sl = rows_of(i)
        zq = q_ref[sl, :]
        f = _forget(f_ref[sl, :], lb)
        g = jnp.log(jnp.maximum(f, TINY))
        qs_ref[sl, :] = _silu(zq) * (HG_DK ** -0.5)
        ks_ref[sl, :] = 1.0 - f
        g1 = g.astype(BF16)
        r1 = g - g1.astype(F32)
        g2 = r1.astype(BF16)
        g3 = (r1 - g2.astype(F32)).astype(BF16)
        bb = _dot(tri, jnp.concatenate([g1, g2, g3], axis=1))
        bs_ref[sl, :] = (bb[:, :w] + bb[:, w:2 * w]) + bb[:, 2 * w:]
        return carry

    lax.fori_loop(0, nsub, gates_pass, 0, unroll=4)

    def gates(i):
        sl = rows_of(i)
        return sl, qs_ref[sl, :], ks_ref[sl, :], bs_ref[sl, :]

    def stage(i, sl, a, q_in, k_dec, dec):
        vb = v_ref[sl, :].astype(BF16)
        for h, hs in enumerate(heads):
            a_ref[i, h] = a[h].astype(BF16)
            up_ref[i, h] = _dot_tn(vb[:, hs], k_dec[:, hs])
        qi_ref[sl, :] = q_in
        dec_ref[pl.ds(pl.multiple_of(i * 8, 8), 8), :] = jnp.broadcast_to(dec, (8, w))

    def scan_states():
        for h, hs in enumerate(heads):
            st = st_ref[h]
            for i in range(nsub):
                sb_ref[i, h] = st.astype(BF16)
                st = st * dec_ref[i * 8:i * 8 + 1, hs] + up_ref[i, h]
            st_ref[h] = st

    def finish_pass(i, carry):
        sl = rows_of(i)
        vb = v_ref[sl, :].astype(BF16)
        q_in = qi_ref[sl, :]
        outs = [_dot_nt(q_in[:, hs], sb_ref[i, h]) + _dot(a_ref[i, h], vb[:, hs]) for h, hs in enumerate(heads)]
        o_all = jnp.concatenate(outs, axis=1)
        if rev:
            tot = of_ref[sl, :] + o_all
            nrm = nrm_ref[...]
            ys = [_rms(tot[:, hs], nrm) for hs in heads]
            o_ref[sl, :] = (jnp.concatenate(ys, axis=1) * _silu(hg_ref[sl, :])).astype(o_ref.dtype)
        else:
            o_ref[sl, :] = o_all
        return carry

    def factored(i, carry):
        sl, q, k, b = gates(i)
        r = b[ref_row:ref_row + 1, :]
        bl = b[last_row:last_row + 1, :]
        rg = jnp.where(far, r, 0.0)
        qt = q * jnp.exp(b - rg)
        kt = k * jnp.exp(rg - b)
        er = jnp.exp(r)
        qn = jnp.where(far, 0.0, qt).astype(BF16)
        qf = jnp.where(far, qt, 0.0).astype(BF16)
        kc = jnp.where(far, kt, kt * er).astype(BF16)
        ktb = kt.astype(BF16)
        q_in = jnp.where(far, qt * er, qt).astype(BF16)
        k_dec = (kt * jnp.where(far, jnp.exp(bl - r), jnp.exp(bl))).astype(BF16)
        a = []
        for hs in heads:
            lhs = jnp.concatenate([qn[:, hs], qf[:, hs]], axis=1)
            rhs = jnp.concatenate([ktb[:, hs], kc[:, hs]], axis=1)
            a.append(jnp.where(causal, _dot_nt(lhs, rhs), 0.0))
        stage(i, sl, a, q_in, k_dec, jnp.exp(bl))
        return carry

    def levels(i, carry):
        sl, q, k, b = gates(i)
        sel_all, qsides, pairs, eye = _hgrn_level_tables(rev)
        bl = b[last_row:last_row + 1, :]
        q_in = (q * jnp.exp(b)).astype(BF16)
        k_dec = (k * jnp.exp(bl - b)).astype(BF16)
        bref_all = jnp.dot(sel_all, b, precision=HIGHEST, preferred_element_type=F32)
        qb = q.astype(BF16)
        kb = k.astype(BF16)
        a = [jnp.where(eye, _dot_nt(qb[:, hs], kb[:, hs]), 0.0) for hs in heads]
        for lev in range(HG_LEVELS):
            bref = bref_all[lev * c:(lev + 1) * c, :]
            qs = qsides[lev]
            x = jnp.exp(jnp.where(qs, b - bref, bref - b))
            ql = jnp.where(qs, q * x, 0.0).astype(BF16)
            kl = jnp.where(qs, 0.0, k * x).astype(BF16)
            for h, hs in enumerate(heads):
                a[h] = a[h] + jnp.where(pairs[lev], _dot_nt(ql[:, hs], kl[:, hs]), 0.0)
        stage(i, sl, a, q_in, k_dec, jnp.exp(bl))
        return carry

    @pl.when(safe)
    def _():
        lax.fori_loop(0, nsub, factored, 0, unroll=4)

    @pl.when(jnp.logical_not(safe))
    def _():
        lax.fori_loop(0, nsub, levels, 0)

    scan_states()
    lax.fori_loop(0, nsub, finish_pass, 0, unroll=4)


def _hgrn(safe, pa3, pb3, gamma, layer, rev, o_fwd=None, nrm=None):
    bsz, seq, _ = pa3.shape
    cb = min(HG_BLOCK, seq)
    nc = seq // cb
    wblk = HG_WIDTH

    def cmap(col):
        if rev:
            return lambda b, c, safe_ref: (b, nc - 1 - c, col // wblk)
        return lambda b, c, safe_ref: (b, c, col // wblk)

    const = lambda b, c, safe_ref: (0, 0)
    in_specs = [
        pl.BlockSpec((None, cb, wblk), cmap(A_HQ)),
        pl.BlockSpec((None, cb, wblk), cmap(A_HF_BWD if rev else A_HF_FWD)),
        pl.BlockSpec((None, cb, wblk), cmap(B_HI)),
        pl.BlockSpec(gamma.shape, const),
    ]
    args = [pa3, pa3, pb3, gamma]
    if rev:
        in_specs += [
            pl.BlockSpec((None, cb, wblk), cmap(A_HG)),
            pl.BlockSpec((None, cb, wblk), cmap(0)),
            pl.BlockSpec((1, HG_DK), const),
        ]
        args += [pa3, o_fwd, nrm]
    out_dtype = BF16 if rev else F32
    grid_spec = pltpu.PrefetchScalarGridSpec(
        num_scalar_prefetch=1,
        grid=(bsz, nc),
        in_specs=in_specs,
        out_specs=pl.BlockSpec((None, cb, wblk), cmap(0)),
        scratch_shapes=[pltpu.VMEM((HG_HEADS, HG_DK, HG_DK), F32)] + [pltpu.VMEM((cb, wblk), F32)] * 3
        + [pltpu.VMEM((cb // HG_SUB, HG_HEADS, HG_SUB, HG_SUB), BF16), pltpu.VMEM((cb, wblk), BF16),
           pltpu.VMEM((cb // HG_SUB * 8, wblk), F32),
           pltpu.VMEM((cb // HG_SUB, HG_HEADS, HG_DK, HG_DK), F32),
           pltpu.VMEM((cb // HG_SUB, HG_HEADS, HG_DK, HG_DK), BF16)],
    )
    return pl.pallas_call(
        functools.partial(_hgrn_kernel, layer=layer, rev=rev),
        grid_spec=grid_spec,
        out_shape=jax.ShapeDtypeStruct((bsz, seq, HG_WIDTH), out_dtype),
        compiler_params=_params(("arbitrary", "arbitrary")),
        name="hgrn_bwd" if rev else "hgrn_fwd",
    )(safe, *args)


ATT_ROW_HEADS = (0, 2, 1, 3)


def _t5_buckets(rel):
    half = N_BUCKETS // 2
    ret = np.where(rel > 0, half, 0)
    n = np.abs(rel)
    max_exact = half // 2
    large = max_exact + (np.log(np.maximum(n, 1) / max_exact)
                         / np.log(REL_MAX_DIST / max_exact) * (half - max_exact)).astype(np.int32)
    large = np.minimum(large, half - 1)
    return (ret + np.where(n < max_exact, n, large)).astype(np.int32)


def _attn_tables(rel_bias, att_sink):
    c = ATT_BLOCK
    rel = np.arange(3 * c)[None, :] - c - np.arange(c)[:, None]
    onehot = np.equal(_t5_buckets(rel).reshape(-1, 1), np.arange(N_BUCKETS)).astype(np.float32)
    bias = jnp.dot(jnp.asarray(onehot, BF16), rel_bias.astype(F32), precision=HIGHEST)
    bias = bias.reshape(c, 3 * c, ATT_HEADS).transpose(2, 0, 1)
    bias = jnp.where(jnp.asarray(np.abs(rel) <= WINDOW)[None], bias, NEG_BIG)
    col = np.arange(3 * c)
    cases = []
    for case in range(4):
        valid = np.ones(3 * c, bool)
        if case & 1:
            valid &= col >= c
        if case & 2:
            valid &= col < 2 * c
        cases.append(jnp.where(jnp.asarray(valid)[None, None, :], bias, NEG_BIG))
    tab = jnp.stack(cases)
    order = np.array([[ATT_GROUP * g + h for h in ATT_ROW_HEADS] for g in range(ATT_KV)])
    tab = tab[:, order].reshape(4, ATT_KV, ATT_GROUP * c, 3 * c)
    sink = att_sink.astype(F32)[:, order]
    sink = jnp.broadcast_to(sink[..., None, None], sink.shape + (c, 128))
    return tab, sink.reshape(att_sink.shape[0], ATT_KV, ATT_GROUP * c, 128)


def _attn_kernel(q_ref, kp_ref, kc_ref, kn_ref, vp_ref, vc_ref, vn_ref, bias_ref, sink_ref, o_ref):
    n = pl.program_id(1)
    nsteps = pl.num_programs(1)
    c = ATT_BLOCK
    nsub = q_ref.shape[0] // c
    pair = 2 * ATT_HD
    kwin = jnp.concatenate([kp_ref[...], kc_ref[...], kn_ref[...]], axis=0).astype(F32)
    vwin = jnp.concatenate([vp_ref[...], vc_ref[...], vn_ref[...]], axis=0).astype(F32)
    kroll = pltpu.roll(kwin, ATT_HD, axis=1)
    vroll = pltpu.roll(vwin, ATT_HD, axis=1)
    lo = lax.broadcasted_iota(jnp.int32, kwin.shape, 1) < ATT_HD
    k_lo = [jnp.where(lo, kwin, 0.0).astype(BF16), jnp.where(lo, kroll, 0.0).astype(BF16)]
    k_hi = [jnp.where(lo, 0.0, kroll).astype(BF16), jnp.where(lo, 0.0, kwin).astype(BF16)]
    v_lo = [jnp.where(lo, vwin, 1.0).astype(BF16), jnp.where(lo, vroll, 1.0).astype(BF16)]
    v_hi = [jnp.where(lo, 1.0, vroll).astype(BF16), jnp.where(lo, 1.0, vwin).astype(BF16)]
    lo_out = lax.broadcasted_iota(jnp.int32, (2 * c, pair), 1) < ATT_HD
    for j in range(nsub):
        case = jnp.int32(0)
        if j == 0:
            case = case + (n == 0).astype(jnp.int32)
        if j == nsub - 1:
            case = case + 2 * (n == nsteps - 1).astype(jnp.int32)
        band = slice(j * c, (j + 3) * c)
        qrows = slice(j * c, (j + 1) * c)
        for g in range(ATT_KV):
            qg = q_ref[qrows, g * 2 * pair:(g + 1) * 2 * pair].astype(F32) * (ATT_HD ** -0.5)
            lhs = jnp.concatenate([qg[:, :pair], qg[:, pair:]], axis=0).astype(BF16)
            rhs = jnp.concatenate([k_lo[g][band], k_hi[g][band]], axis=0)
            lg = _dot_nt(lhs, rhs)
            logits = jnp.concatenate([lg[:, :3 * c], lg[:, 3 * c:]], axis=0) + bias_ref[case, g]
            s = sink_ref[g][:, 0:1]
            m = jnp.maximum(jnp.max(logits, axis=-1, keepdims=True), s)
            p = jnp.exp(logits - m).astype(BF16)
            es = jnp.exp(s - m)
            o_even = _dot(p[:2 * c], v_lo[g][band])
            o_odd = _dot(p[2 * c:], v_hi[g][band])
            o = jnp.where(lo_out,
                          o_even / (pltpu.roll(o_even, ATT_HD, axis=1) + es[:2 * c]),
                          o_odd / (pltpu.roll(o_odd, ATT_HD, axis=1) + es[2 * c:]))
            o_ref[qrows, g * 2 * pair:g * 2 * pair + pair] = o[:c].astype(o_ref.dtype)
            o_ref[qrows, g * 2 * pair + pair:(g + 1) * 2 * pair] = o[c:].astype(o_ref.dtype)


def _attn(proj3, bias, sink):
    bsz, seq, _ = proj3.shape
    c = ATT_BLOCK
    nsub = max(k for k in (4, 2, 1) if seq % (k * c) == 0)
    qb = nsub * c
    nsteps = seq // qb
    nb = seq // c
    kcol = B_AK // KV_WIDTH
    vcol = B_AV // KV_WIDTH

    def band(col):
        return [
            pl.BlockSpec((None, c, KV_WIDTH), lambda b, n: (b, jnp.maximum(n * nsub - 1, 0), col)),
            pl.BlockSpec((None, qb, KV_WIDTH), lambda b, n: (b, n, col)),
            pl.BlockSpec((None, c, KV_WIDTH), lambda b, n: (b, jnp.minimum((n + 1) * nsub, nb - 1), col)),
        ]

    return pl.pallas_call(
        _attn_kernel,
        grid=(bsz, nsteps),
        in_specs=[pl.BlockSpec((None, qb, ATT_WIDTH), lambda b, n: (b, n, B_AQ // ATT_WIDTH))]
        + band(kcol) + band(vcol)
        + [pl.BlockSpec(bias.shape, lambda b, n: (0, 0, 0, 0)),
           pl.BlockSpec(sink.shape, lambda b, n: (0, 0, 0))],
        out_specs=pl.BlockSpec((None, qb, ATT_WIDTH), lambda b, n: (b, n, 0)),
        out_shape=jax.ShapeDtypeStruct((bsz, seq, ATT_WIDTH), BF16),
        compiler_params=_params(("arbitrary", "arbitrary")),
        name="window_attn",
    )(proj3, proj3, proj3, proj3, proj3, proj3, proj3, bias, sink)


def _first_argmax(vals):
    best, idx = vals[0], jnp.zeros(vals[0].shape, jnp.int32)
    for j in range(1, len(vals)):
        upd = vals[j] > best
        idx = jnp.where(upd, j, idx)
        best = jnp.where(upd, vals[j], best)
    return best, idx


def _select(vals, idx):
    out = vals[0]
    for j in range(1, len(vals)):
        out = jnp.where(idx == j, vals[j], out)
    return out


def _route(logits_t, rbias):
    m = jnp.max(logits_t, axis=0, keepdims=True)
    e = jnp.exp(logits_t - m)
    scores = e / jnp.sum(e, axis=0, keepdims=True)
    sel = scores + rbias
    srow = [scores[i:i + 1, :] for i in range(N_EXPERTS)]
    lrow = [sel[i:i + 1, :] for i in range(N_EXPERTS)]
    gscore = []
    for g in range(N_GROUPS):
        a, b, c, d = lrow[4 * g:4 * g + 4]
        hi1, lo1 = jnp.maximum(a, b), jnp.minimum(a, b)
        hi2, lo2 = jnp.maximum(c, d), jnp.minimum(c, d)
        gscore.append(jnp.maximum(hi1, hi2) + jnp.maximum(jnp.minimum(hi1, hi2), jnp.maximum(lo1, lo2)))
    _, gi = _first_argmax(gscore)
    ing = [_select([lrow[4 * g + j] for g in range(N_GROUPS)], gi) for j in range(EXPERTS_PER_GROUP)]
    sg = [_select([srow[4 * g + j] for g in range(N_GROUPS)], gi) for j in range(EXPERTS_PER_GROUP)]
    _, i1 = _first_argmax(ing)
    rest = [jnp.where(i1 == j, -jnp.inf, ing[j]) for j in range(EXPERTS_PER_GROUP)]
    _, i2 = _first_argmax(rest)
    s1, s2 = _select(sg, i1), _select(sg, i2)
    tot = s1 + s2
    w1, w2 = s1 / tot, s2 / tot
    idx1 = gi * EXPERTS_PER_GROUP + i1
    idx2 = gi * EXPERTS_PER_GROUP + i2
    return idx1, idx2, w1, w2


MERGE_SPLIT = 1


def _merge_kernel(oh_ref, oa_ref, gh_ref, ga_ref, x_ref, mod_ref, nf_ref, wbh_ref, wba_ref, wo_ref,
                  wrt_ref, rb_ref, x1_ref, h2_ref, ri_ref, rw_ref, cnt_ref, carry_ref):
    @pl.when(pl.program_id(0) == 0)
    def _():
        carry_ref[...] = jnp.zeros_like(carry_ref)

    wr = wrt_ref[...]
    w_hi = wr.astype(BF16)
    w_lo = (wr - w_hi.astype(F32)).astype(BF16)
    tm = x_ref.shape[0]
    th = tm // MERGE_SPLIT
    t_src = lax.broadcasted_iota(jnp.int32, (th, th), 0)
    t_dst = lax.broadcasted_iota(jnp.int32, (th, th), 1)
    before = jnp.where(t_src < t_dst, 1.0, 0.0).astype(BF16)
    erow = lax.broadcasted_iota(jnp.int32, (N_EXPERTS, th), 0)
    carry = carry_ref[...]
    for r in range(MERGE_SPLIT):
        rows = slice(r * th, (r + 1) * th)
        mh = _dot(oh_ref[rows, :], wbh_ref[...])
        ma = _dot(oa_ref[rows, :], wba_ref[...])
        merged = _sigmoid(gh_ref[rows, :].astype(F32)) * mh + _sigmoid(ga_ref[rows, :].astype(F32)) * ma
        out = _dot(merged.astype(BF16), wo_ref[...])
        x1 = x_ref[rows, :] + mod_ref[2:3, :] * out
        x1_ref[rows, :] = x1
        h2 = _rms(x1, nf_ref[...]) * (1.0 + mod_ref[4:5, :]) + mod_ref[3:4, :]
        h2_ref[rows, :] = _pack_bf16_pairs(h2)
        h_hi = h2.astype(BF16)
        h_lo = (h2 - h_hi.astype(F32)).astype(BF16)
        logits_t = (_dot_nt(w_hi, h_hi) + _dot_nt(w_hi, h_lo)) + _dot_nt(w_lo, h_hi)
        idx1, idx2, w1, w2 = _route(logits_t, rb_ref[:, 0:1])
        oh1 = erow == idx1
        oh2 = erow == idx2
        oh = jnp.where(oh1 | oh2, 1.0, 0.0)
        pref = _dot(oh.astype(BF16), before) + carry[:, 0:1]
        rank1 = jnp.sum(jnp.where(oh1, pref, 0.0), axis=0, keepdims=True)
        rank2 = jnp.sum(jnp.where(oh2, pref, 0.0), axis=0, keepdims=True)
        carry = carry + jnp.sum(oh, axis=1, keepdims=True)
        ri_ref[:, rows] = jnp.concatenate([idx1, idx2, rank1.astype(jnp.int32), rank2.astype(jnp.int32)], axis=0)
        rw_ref[:, rows] = jnp.concatenate([w1, w2], axis=0)
    carry_ref[...] = carry
    cnt_ref[...] = carry


def _merge(o_h, o_a, proj, x2d, mod, nf, wbh, wba, wo, wrt, rb, seq):
    n = x2d.shape[0]
    tm = min(512, seq)
    per_seq = seq // tm
    const = lambda i: (0, 0)
    return pl.pallas_call(
        _merge_kernel,
        grid=(n // tm,),
        in_specs=[
            pl.BlockSpec((tm, HG_WIDTH), lambda i: (i, 0)),
            pl.BlockSpec((tm, ATT_WIDTH), lambda i: (i, 0)),
            pl.BlockSpec((tm, D_MODEL), lambda i: (i, B_GATE_H // D_MODEL)),
            pl.BlockSpec((tm, D_MODEL), lambda i: (i, B_GATE_A // D_MODEL)),
            pl.BlockSpec((tm, D_MODEL), lambda i: (i, 0)),
            pl.BlockSpec((None, N_MOD, D_MODEL), lambda i: (i // per_seq, 0, 0)),
            pl.BlockSpec((1, D_MODEL), const),
            pl.BlockSpec(wbh.shape, const),
            pl.BlockSpec(wba.shape, const),
            pl.BlockSpec(wo.shape, const),
            pl.BlockSpec(wrt.shape, const),
            pl.BlockSpec(rb.shape, const),
        ],
        out_specs=[
            pl.BlockSpec((tm, D_MODEL), lambda i: (i, 0)),
            pl.BlockSpec((tm, D_MODEL // 2), lambda i: (i, 0)),
            pl.BlockSpec((4, tm), lambda i: (0, i)),
            pl.BlockSpec((2, tm), lambda i: (0, i)),
            pl.BlockSpec((N_EXPERTS, 128), const),
        ],
        out_shape=[
            jax.ShapeDtypeStruct((n, D_MODEL), F32),
            jax.ShapeDtypeStruct((n, D_MODEL // 2), jnp.uint32),
            jax.ShapeDtypeStruct((4, n), jnp.int32),
            jax.ShapeDtypeStruct((2, n), F32),
            jax.ShapeDtypeStruct((N_EXPERTS, 128), F32),
        ],
        scratch_shapes=[pltpu.VMEM((N_EXPERTS, 128), F32)],
        compiler_params=_params(("arbitrary",)),
        name="merge_router",
    )(o_h, o_a, proj, proj, x2d, mod, nf, wbh, wba, wo, wrt, rb)


MOE_TILE = 512
SC_WINDOW = 64


def _sc_mesh():
    return plsc.VectorSubcoreMesh(core_axis_name="c", subcore_axis_name="s")


def _sc_dispatch(h, pos, p):
    n, d = h.shape
    win = SC_WINDOW
    info = plsc.get_sparse_core_info()
    workers = info.num_cores * info.num_subcores
    wpt = n // (win * workers)
    pos_w = pos.reshape(2, n // win, win).transpose(1, 0, 2)

    @functools.partial(
        pl.kernel, out_type=jax.ShapeDtypeStruct((p, d), h.dtype), mesh=_sc_mesh(),
        scratch_types=[pltpu.VMEM((wpt, 2, win), jnp.int32), pltpu.VMEM((2, win, d), h.dtype),
                       pltpu.SemaphoreType.DMA((2,))],
        name="moe_dispatch")
    def dispatch(h_hbm, pos_hbm, o_hbm, idx_v, rows_v, load_sem):
        wid = lax.axis_index("c") * info.num_subcores + lax.axis_index("s")
        first = wid * wpt
        pltpu.sync_copy(pos_hbm.at[pl.ds(first, wpt)], idx_v)

        def load(j, slot):
            return pltpu.make_async_copy(h_hbm.at[pl.ds((first + j) * win, win)], rows_v.at[slot],
                                         load_sem.at[slot])

        load(0, 0).start()

        @pl.loop(0, wpt, step=2)
        def _(j):
            for slot in range(2):
                jj = j + slot
                load(jj, slot).wait()

                @pl.when(jj + 1 < wpt)
                def _():
                    load(jj + 1, 1 - slot).start()

                pltpu.sync_copy(rows_v.at[slot], o_hbm.at[idx_v.at[jj, 0]])
                pltpu.sync_copy(rows_v.at[slot], o_hbm.at[idx_v.at[jj, 1]])

    return dispatch(h, pos_w)


def _sc_combine(ys, pos):
    n = pos.shape[1]
    d = ys.shape[1]
    win = SC_WINDOW // 2
    info = plsc.get_sparse_core_info()
    workers = info.num_cores * info.num_subcores
    wpt = n // (win * workers)
    pos_w = pos.reshape(2, n // win, win).transpose(1, 0, 2)
    out = jax.ShapeDtypeStruct((n, d), ys.dtype)

    @functools.partial(
        pl.kernel, out_type=(out, out), mesh=_sc_mesh(),
        scratch_types=[pltpu.VMEM((wpt, 2, win), jnp.int32), pltpu.VMEM((2, 2, win, d), ys.dtype),
                       pltpu.SemaphoreType.DMA((2, 2)), pltpu.SemaphoreType.DMA((2,))],
        name="moe_combine")
    def combine(ys_hbm, pos_hbm, a_hbm, b_hbm, idx_v, rows_v, gather_sem, store_sem):
        wid = lax.axis_index("c") * info.num_subcores + lax.axis_index("s")
        first = wid * wpt
        pltpu.sync_copy(pos_hbm.at[pl.ds(first, wpt)], idx_v)
        outs = (a_hbm, b_hbm)

        def gather(j, slot, k):
            return pltpu.make_async_copy(ys_hbm.at[idx_v.at[j, k]], rows_v.at[slot, k], gather_sem.at[slot, k])

        def store(j, slot, k):
            return pltpu.make_async_copy(rows_v.at[slot, k], outs[k].at[pl.ds((first + j) * win, win)],
                                         store_sem.at[k])

        gather(0, 0, 0).start()
        gather(0, 0, 1).start()

        @pl.loop(0, wpt, step=2)
        def _(j):
            for slot in range(2):
                jj = j + slot
                gather(jj, slot, 0).wait()
                gather(jj, slot, 1).wait()

                @pl.when(jj + 1 < wpt)
                def _():
                    gather(jj + 1, 1 - slot, 0).start()
                    gather(jj + 1, 1 - slot, 1).start()

                store(jj, slot, 0).start()
                store(jj, slot, 1).start()
                store(jj, slot, 0).wait()
                store(jj, slot, 1).wait()

    return combine(ys, pos_w)


def _expert_kernel(te_ref, nu_ref, x_ref, wg_ref, wu_ref, wd_ref, o_ref, wg_c, wu_c, wd_c):
    i = pl.program_id(0)
    used = i < nu_ref[0]
    new_expert = (i == 0) | (te_ref[i] != te_ref[jnp.maximum(i - 1, 0)])

    @pl.when(used & new_expert)
    def _():
        wg_c[...] = wg_ref[...].astype(BF16)
        wu_c[...] = wu_ref[...].astype(BF16)
        wd_c[...] = wd_ref[...].astype(BF16)

    @pl.when(used)
    def _():
        x = _unpack_bf16_pairs(x_ref[...]).astype(BF16)
        he = _silu(_dot(x, wg_c[...])) * _dot(x, wu_c[...])
        o_ref[...] = _pack_bf16_pairs(_dot(he.astype(BF16), wd_c[...]))

    @pl.when(jnp.logical_not(used))
    def _():
        o_ref[...] = jnp.zeros_like(o_ref)


def _experts(xs, tile_expert, n_used, wg, wu, wd):
    p = xs.shape[0]
    tm = MOE_TILE
    grid_spec = pltpu.PrefetchScalarGridSpec(
        num_scalar_prefetch=2,
        grid=(p // tm,),
        in_specs=[
            pl.BlockSpec((tm, D_MODEL // 2), lambda i, te, nu: (i, 0)),
            pl.BlockSpec((None, D_MODEL, D_EXPERT), lambda i, te, nu: (te[i], 0, 0)),
            pl.BlockSpec((None, D_MODEL, D_EXPERT), lambda i, te, nu: (te[i], 0, 0)),
            pl.BlockSpec((None, D_EXPERT, D_MODEL), lambda i, te, nu: (te[i], 0, 0)),
        ],
        out_specs=pl.BlockSpec((tm, D_MODEL // 2), lambda i, te, nu: (i, 0)),
        scratch_shapes=[pltpu.VMEM((D_MODEL, D_EXPERT), BF16), pltpu.VMEM((D_MODEL, D_EXPERT), BF16),
                        pltpu.VMEM((D_EXPERT, D_MODEL), BF16)],
    )
    return pl.pallas_call(
        _expert_kernel,
        grid_spec=grid_spec,
        out_shape=jax.ShapeDtypeStruct((p, D_MODEL // 2), jnp.uint32),
        compiler_params=_params(("arbitrary",)),
        name="moe_experts",
    )(tile_expert, n_used, xs, wg, wu, wd)


def _residual_kernel(ya_ref, yb_ref, w_ref, x1_ref, mod_ref, nfin_ref, o_ref, *, last):
    w = w_ref[...]
    y = w[:, 0:1] * _unpack_bf16_pairs(ya_ref[...]) + w[:, 1:2] * _unpack_bf16_pairs(yb_ref[...])
    x2 = x1_ref[...] + mod_ref[5:6, :] * y
    if last:
        x2 = _rms(x2, nfin_ref[...])
    o_ref[...] = x2


def _residual(ya, yb, w, x1, mod, nfin, seq, last):
    n = x1.shape[0]
    tm = min(512, seq)
    per_seq = seq // tm
    row = pl.BlockSpec((tm, D_MODEL), lambda i: (i, 0))
    packed = pl.BlockSpec((tm, D_MODEL // 2), lambda i: (i, 0))
    return pl.pallas_call(
        functools.partial(_residual_kernel, last=last),
        grid=(n // tm,),
        in_specs=[packed, packed, pl.BlockSpec((tm, 2), lambda i: (i, 0)), row,
                  pl.BlockSpec((None, N_MOD, D_MODEL), lambda i: (i // per_seq, 0, 0)),
                  pl.BlockSpec((1, D_MODEL), lambda i: (0, 0))],
        out_specs=row,
        out_shape=jax.ShapeDtypeStruct((n, D_MODEL), F32),
        compiler_params=_params(("arbitrary",)),
        name="moe_residual",
    )(ya, yb, w, x1, mod, nfin)


def _moe(h2, ri, rw, cnt, wg, wu, wd, x1, mod, nfin, seq, last):
    n = h2.shape[0]
    tm = MOE_TILE
    p = 2 * n + N_EXPERTS * tm
    counts = cnt[:, 0].astype(jnp.int32)
    padded = (counts + tm - 1) // tm * tm
    ends = jnp.cumsum(padded)
    starts = ends - padded
    base = jnp.zeros_like(ri[0:2])
    for e in range(1, N_EXPERTS):
        base = jnp.where(ri[0:2] == e, starts[e], base)
    pos = base + ri[2:4]
    tile_start = jnp.arange(p // tm, dtype=jnp.int32) * tm
    tile_expert = jnp.minimum(jnp.sum(tile_start[:, None] >= ends[None, :], axis=1), N_EXPERTS - 1)
    n_used = (ends[-1:] // tm).astype(jnp.int32)
    xs = _sc_dispatch(h2, pos, p)
    ys = _experts(xs, tile_expert.astype(jnp.int32), n_used, wg, wu, wd)
    ya, yb = _sc_combine(ys, pos)
    return _residual(ya, yb, rw.T, x1, mod, nfin, seq, last)


def _split_w_in(w):
    hq_hf, hi, hg, att, gates = w[..., :1536], w[..., 1536:2048], w[..., 2048:2560], w[..., 2560:3328], w[..., 3328:]
    wa = jnp.concatenate([hq_hf, hg], axis=-1).astype(BF16)
    wb = jnp.concatenate([gates, hi, att], axis=-1).astype(BF16)
    return wa, wb


def _trunk(x, mod, wts):
    bsz, seq, _ = x.shape
    n = bsz * seq
    depth = wts["w_in_a"].shape[0]
    x2d = x.reshape(n, D_MODEL)
    for l in range(depth):
        mod_l = mod[l]
        pa, pb, span = _inproj(x2d, mod_l, wts["norm_mix"][l:l + 1], wts["w_in_a"][l], wts["w_in_b"][l],
                               wts["hg_lb_fwd"], wts["hg_lb_bwd"], seq, l)
        pa3 = pa.reshape(bsz, seq, A_COLS)
        pb3 = pb.reshape(bsz, seq, B_COLS)
        safe_f = (span[:, 0, 0] <= HG_SAFE_SPAN).astype(jnp.int32)
        safe_b = (span[:, 4, 0] <= HG_SAFE_SPAN).astype(jnp.int32)
        o_f = _hgrn(safe_f, pa3, pb3, wts["hg_lb_fwd"], l, False)
        o_h = _hgrn(safe_b, pa3, pb3, wts["hg_lb_bwd"], l, True, o_f, wts["hg_norm"][l:l + 1])
        o_a = _attn(pb3, wts["bias"], wts["sink"][l])
        x1, h2, ri, rw, cnt = _merge(o_h.reshape(n, HG_WIDTH), o_a.reshape(n, ATT_WIDTH), pb, x2d, mod_l,
                                     wts["norm_ffn"][l:l + 1], wts["w_br_hgrn"][l], wts["w_br_att"][l],
                                     wts["w_out"][l], wts["w_router_t"], wts["router_bias"], seq)
        x2d = _moe(h2, ri, rw, cnt, wts["w_gate"][l], wts["w_up"][l], wts["w_down"][l], x1, mod_l,
                   wts["norm_final"], seq, l == depth - 1)
    return x2d.reshape(bsz, seq, D_MODEL)


def kernel(x_prompt, x_sample, c_prompt, c_sample, w_ada, b_ada, norm_mix, norm_ffn, norm_final, w_in, hg_lb_fwd, hg_lb_bwd, hg_norm, att_sink, rel_bias, w_br_hgrn, w_br_att, w_out, w_router, router_bias, w_gate, w_up, w_down):
    depth = w_in.shape[0]
    bp, bs = c_prompt.shape[0], c_sample.shape[0]
    rows = -(-(bp + bs) // 8) * 8
    c_all = jnp.concatenate([c_prompt, c_sample, jnp.zeros((rows - bp - bs, D_MODEL), F32)], axis=0)
    mod = _ada(c_all, w_ada, b_ada).reshape(depth, rows, N_MOD, D_MODEL)
    bias, sink = _attn_tables(rel_bias, att_sink)
    w_in_a, w_in_b = _split_w_in(w_in)
    wts = {
        "norm_mix": norm_mix, "norm_ffn": norm_ffn, "norm_final": norm_final.reshape(1, D_MODEL),
        "w_in_a": w_in_a, "w_in_b": w_in_b,
        "hg_lb_fwd": hg_lb_fwd, "hg_lb_bwd": hg_lb_bwd, "hg_norm": hg_norm,
        "sink": sink, "bias": bias,
        "w_br_hgrn": w_br_hgrn.astype(BF16), "w_br_att": w_br_att.astype(BF16), "w_out": w_out.astype(BF16),
        "w_router_t": w_router.T,
        "router_bias": jnp.broadcast_to(router_bias[:, None], (N_EXPERTS, 128)),
        "w_gate": w_gate, "w_up": w_up, "w_down": w_down,
    }
    y_prompt = _trunk(x_prompt, mod[:, :bp], wts)
    y_sample = _trunk(x_sample, mod[:, bp:bp + bs], wts)
    return (y_prompt, y_sample)
```

```python
import functools

import numpy as np
import jax
import jax.numpy as jnp
from jax import lax
from jax.experimental import pallas as pl
from jax.experimental.pallas import tpu as pltpu
from jax.experimental.pallas import tpu_sc as plsc

D_MODEL = 1024
HG_DK = 128
HG_WIDTH = 512
HG_HEADS = 4
HG_SUB = 64
HG_LEVELS = 6
HG_GROUP = 32
HG_SAFE_SPAN = 80.0
HG_BLOCK = 512
ATT_HD = 64
ATT_HEADS = 8
ATT_KV = 2
ATT_GROUP = 4
ATT_WIDTH = 512
KV_WIDTH = 128
WINDOW = 128
ATT_BLOCK = 128
N_BUCKETS = 32
REL_MAX_DIST = 128
N_EXPERTS = 16
N_GROUPS = 4
EXPERTS_PER_GROUP = 4
D_EXPERT = 512
N_MOD = 6
IN_COLS = 5376
EPS = 1e-6
NEG_BIG = -1e30
TINY = 1e-30

A_COLS = 2048
A_HQ = 0
A_HF_FWD = 512
A_HF_BWD = 1024
A_HG = 1536
B_COLS = 3328
B_GATE_H = 0
B_GATE_A = 1024
B_HI = 2048
B_AQ = 2560
B_AK = 3072
B_AV = 3200

V7X_VMEM_LIMIT = 56 * 1024 * 1024

F32 = jnp.float32
BF16 = jnp.bfloat16
HIGHEST = lax.Precision.HIGHEST


def _params(sem):
    return pltpu.CompilerParams(dimension_semantics=sem, vmem_limit_bytes=V7X_VMEM_LIMIT)


def _dot(a, b):
    return jnp.dot(a, b, preferred_element_type=F32)


def _dot_nt(a, b):
    return lax.dot_general(a, b, (((1,), (1,)), ((), ())), preferred_element_type=F32)


def _dot_tn(a, b):
    return lax.dot_general(a, b, (((0,), (0,)), ((), ())), preferred_element_type=F32)


def _sigmoid(x):
    return 0.5 * jnp.tanh(0.5 * x) + 0.5


def _pack_bf16_pairs(x):
    c = x.shape[1] // 2
    bits = lax.bitcast_convert_type(x.astype(BF16).astype(F32), jnp.uint32)
    return (bits[:, :c] >> 16) | (bits[:, c:] & jnp.uint32(0xFFFF0000))


def _unpack_bf16_pairs(u):
    lo = lax.bitcast_convert_type(u << 16, F32)
    hi = lax.bitcast_convert_type(u & jnp.uint32(0xFFFF0000), F32)
    return jnp.concatenate([lo, hi], axis=1)


def _silu(x):
    return x * _sigmoid(x)


def _silu_of_half(xh):
    return xh * (jnp.tanh(xh) + 1.0)


def _rms(x, g):
    return x * lax.rsqrt(jnp.mean(x * x, axis=-1, keepdims=True) + EPS) * g


def _ada_kernel(c_ref, w_ref, b_ref, o_ref):
    c = c_ref[...]
    o_ref[...] = jnp.dot(_silu(c), w_ref[...], precision=HIGHEST, preferred_element_type=F32) + b_ref[...]


def _ada(c_all, w_ada, b_ada):
    depth = w_ada.shape[0]
    rows = c_all.shape[0]
    ncol = w_ada.shape[2]
    tn = 1024
    return pl.pallas_call(
        _ada_kernel,
        grid=(depth, ncol // tn),
        in_specs=[
            pl.BlockSpec((rows, D_MODEL), lambda l, j: (0, 0)),
            pl.BlockSpec((None, D_MODEL, tn), lambda l, j: (l, 0, j)),
            pl.BlockSpec((None, 1, tn), lambda l, j: (l, 0, j)),
        ],
        out_specs=pl.BlockSpec((None, rows, tn), lambda l, j: (l, 0, j)),
        out_shape=jax.ShapeDtypeStruct((depth, rows, ncol), F32),
        compiler_params=_params(("arbitrary", "arbitrary")),
        name="ada_mod",
    )(c_all, w_ada, b_ada.reshape(depth, 1, ncol))


def _lower_bound_row(gam_ref, layer):
    rows = [gam_ref[d:d + 1, :] for d in range(gam_ref.shape[0])]
    m = functools.reduce(jnp.maximum, rows)
    es = [jnp.exp(r - m) for r in rows]
    tot = functools.reduce(lambda a, b: a + b, es)
    ps = [e / tot for e in es]
    cum = ps[0]
    for d in range(1, layer + 1):
        cum = cum + ps[d]
    return jnp.clip(cum - ps[0], 0.0, 1.0)


def _forget(zh, lb):
    return 0.5 * (1.0 + lb) + (0.5 * (1.0 - lb)) * jnp.tanh(zh)


def _inproj_kernel(x_ref, mod_ref, g_ref, wa_ref, wb_ref, gf_ref, gb_ref, oa_ref, ob_ref, span_ref, *, layer):
    x = x_ref[...]
    h = (_rms(x, g_ref[...]) * (1.0 + mod_ref[1:2, :]) + mod_ref[0:1, :]).astype(BF16)
    oa = _dot(h, wa_ref[...])
    oa_ref[...] = oa
    ob_ref[...] = _dot(h, wb_ref[...]).astype(BF16)
    spans = []
    for gam_ref, col in ((gf_ref, A_HF_FWD), (gb_ref, A_HF_BWD)):
        f = _forget(oa[:, col:col + HG_WIDTH], _lower_bound_row(gam_ref, layer))
        g = jnp.log(jnp.maximum(f, TINY))
        gsum = jnp.sum(g.reshape(g.shape[0] // HG_GROUP, HG_GROUP, HG_WIDTH), axis=1)
        spans.append(jnp.max(jnp.max(-gsum, axis=0, keepdims=True), axis=1, keepdims=True))
    half = lax.broadcasted_iota(jnp.int32, span_ref.shape, 0) < span_ref.shape[0] // 2
    span_ref[...] = jnp.where(half, spans[0], spans[1])


def _inproj(x2d, mod, g, wa, wb, gam_f, gam_b, seq, layer):
    n = x2d.shape[0]
    tm = min(HG_BLOCK, seq)
    per_seq = seq // tm
    const = lambda i: (0, 0)
    return pl.pallas_call(
        functools.partial(_inproj_kernel, layer=layer),
        grid=(n // tm,),
        in_specs=[
            pl.BlockSpec((tm, D_MODEL), lambda i: (i, 0)),
            pl.BlockSpec((None, N_MOD, D_MODEL), lambda i: (i // per_seq, 0, 0)),
            pl.BlockSpec((1, D_MODEL), const),
            pl.BlockSpec((D_MODEL, A_COLS), const, pipeline_mode=pl.Buffered(1)),
            pl.BlockSpec((D_MODEL, B_COLS), const, pipeline_mode=pl.Buffered(1)),
            pl.BlockSpec(gam_f.shape, const),
            pl.BlockSpec(gam_b.shape, const),
        ],
        out_specs=[pl.BlockSpec((tm, A_COLS), lambda i: (i, 0)),
                   pl.BlockSpec((tm, B_COLS), lambda i: (i, 0)),
                   pl.BlockSpec((None, 8, 128), lambda i: (i, 0, 0))],
        out_shape=[jax.ShapeDtypeStruct((n, A_COLS), F32), jax.ShapeDtypeStruct((n, B_COLS), BF16),
                   jax.ShapeDtypeStruct((n // tm, 8, 128), F32)],
        compiler_params=_params(("arbitrary",)),
        name="inproj",
    )(x2d, mod, g, wa, wb, gam_f, gam_b)


def _hgrn_level_tables(rev):
    c = HG_SUB
    r = lax.broadcasted_iota(jnp.int32, (c, c), 0)
    s = lax.broadcasted_iota(jnp.int32, (c, c), 1)
    row = lax.broadcasted_iota(jnp.int32, (c, HG_WIDTH), 0)
    sels, qsides, pairs = [], [], []
    for lev in range(HG_LEVELS):
        half = 1 << lev
        blk = 2 * half
        r_up = (r & (blk - 1)) >= half
        s_up = (s & (blk - 1)) >= half
        base = r - (r & (blk - 1))
        mrow = base + (half if rev else half - 1)
        sels.append(jnp.where(s == mrow, 1.0, 0.0).astype(F32))
        row_up = (row & (blk - 1)) >= half
        qsides.append(~row_up if rev else row_up)
        same = (r >> (lev + 1)) == (s >> (lev + 1))
        pairs.append(same & ((~r_up & s_up) if rev else (r_up & ~s_up)))
    return jnp.concatenate(sels, axis=0), qsides, pairs, r == s


def _hgrn_kernel(safe_ref, *refs, layer, rev):
    if rev:
        q_ref, f_ref, v_ref, gam_ref, hg_ref, of_ref, nrm_ref, o_ref = refs[:8]
    else:
        q_ref, f_ref, v_ref, gam_ref, o_ref = refs[:5]
    st_ref, qs_ref, ks_ref, bs_ref, a_ref, qi_ref, dec_ref, up_ref, sb_ref = refs[-9:]

    @pl.when(pl.program_id(1) == 0)
    def _():
        st_ref[...] = jnp.zeros_like(st_ref)

    nc = pl.num_programs(1)
    chunk = (nc - 1 - pl.program_id(1)) if rev else pl.program_id(1)
    safe = safe_ref[pl.program_id(0) * nc + chunk] != 0
    c = HG_SUB
    w = HG_WIDTH
    nsub = q_ref.shape[0] // c
    lb = _lower_bound_row(gam_ref, layer)
    r_i = lax.broadcasted_iota(jnp.int32, (c, c), 0)
    s_i = lax.broadcasted_iota(jnp.int32, (c, c), 1)
    causal = (s_i >= r_i) if rev else (s_i <= r_i)
    tri = jnp.where(causal, 1.0, 0.0).astype(BF16)
    row = lax.broadcasted_iota(jnp.int32, (c, w), 0)
    far = (row < HG_GROUP) if rev else (row >= HG_GROUP)
    ref_row = HG_GROUP if rev else HG_GROUP - 1
    last_row = 0 if rev else c - 1
    heads = [slice(h * HG_DK, (h + 1) * HG_DK) for h in range(HG_HEADS)]

    def rows_of(i):
        ci = (nsub - 1 - i) if rev else i
        return pl.ds(pl.multiple_of(ci * c, c), c)

    def gates_pass(i, carry):
        sl = rows_of(i)
        zq = q_ref[sl, :]
        f = _forget(f_ref[sl, :], lb)
        g = jnp.log(jnp.maximum(f, TINY))
        qs_ref[sl, :] = _silu_of_half(zq) * (HG_DK ** -0.5)
        ks_ref[sl, :] = 1.0 - f
        g1 = g.astype(BF16)
        r1 = g - g1.astype(F32)
        g2 = r1.astype(BF16)
        g3 = (r1 - g2.astype(F32)).astype(BF16)
        bb = _dot(tri, jnp.concatenate([g1, g2, g3], axis=1))
        bs_ref[sl, :] = (bb[:, :w] + bb[:, w:2 * w]) + bb[:, 2 * w:]
        return carry

    lax.fori_loop(0, nsub, gates_pass, 0, unroll=4)

    def gates(i):
        sl = rows_of(i)
        return sl, qs_ref[sl, :], ks_ref[sl, :], bs_ref[sl, :]

    def stage(i, sl, a, q_in, k_dec, dec):
        vb = v_ref[sl, :].astype(BF16)
        for h, hs in enumerate(heads):
            a_ref[i, h] = a[h].astype(BF16)
            up_ref[i, h] = _dot_tn(vb[:, hs], k_dec[:, hs])
        qi_ref[sl, :] = q_in
        dec_ref[pl.ds(pl.multiple_of(i * 8, 8), 8), :] = jnp.broadcast_to(dec, (8, w))

    def scan_states():
        for h, hs in enumerate(heads):
            st = st_ref[h]
            for i in range(nsub):
                sb_ref[i, h] = st.astype(BF16)
                st = st * dec_ref[i * 8:i * 8 + 1, hs] + up_ref[i, h]
            st_ref[h] = st

    def finish_pass(i, carry):
        sl = rows_of(i)
        vb = v_ref[sl, :].astype(BF16)
        q_in = qi_ref[sl, :]
        outs = [_dot_nt(q_in[:, hs], sb_ref[i, h]) + _dot(a_ref[i, h], vb[:, hs]) for h, hs in enumerate(heads)]
        o_all = jnp.concatenate(outs, axis=1)
        if rev:
            tot = of_ref[sl, :] + o_all
            nrm = nrm_ref[...]
            ys = [_rms(tot[:, hs], nrm) for hs in heads]
            o_ref[sl, :] = (jnp.concatenate(ys, axis=1) * _silu_of_half(hg_ref[sl, :])).astype(o_ref.dtype)
        else:
            o_ref[sl, :] = o_all
        return carry

    def factored(i, carry):
        sl, q, k, b = gates(i)
        r = b[ref_row:ref_row + 1, :]
        bl = b[last_row:last_row + 1, :]
        rg = jnp.where(far, r, 0.0)
        qt = q * jnp.exp(b - rg)
        kt = k * jnp.exp(rg - b)
        er = jnp.exp(r)
        qn = jnp.where(far, 0.0, qt).astype(BF16)
        qf = jnp.where(far, qt, 0.0).astype(BF16)
        kc = jnp.where(far, kt, kt * er).astype(BF16)
        ktb = kt.astype(BF16)
        q_in = jnp.where(far, qt * er, qt).astype(BF16)
        k_dec = (kt * jnp.where(far, jnp.exp(bl - r), jnp.exp(bl))).astype(BF16)
        a = []
        for hs in heads:
            lhs = jnp.concatenate([qn[:, hs], qf[:, hs]], axis=1)
            rhs = jnp.concatenate([ktb[:, hs], kc[:, hs]], axis=1)
            a.append(jnp.where(causal, _dot_nt(lhs, rhs), 0.0))
        stage(i, sl, a, q_in, k_dec, jnp.exp(bl))
        return carry

    def levels(i, carry):
        sl, q, k, b = gates(i)
        sel_all, qsides, pairs, eye = _hgrn_level_tables(rev)
        bl = b[last_row:last_row + 1, :]
        q_in = (q * jnp.exp(b)).astype(BF16)
        k_dec = (k * jnp.exp(bl - b)).astype(BF16)
        bref_all = jnp.dot(sel_all, b, precision=HIGHEST, preferred_element_type=F32)
        qb = q.astype(BF16)
        kb = k.astype(BF16)
        a = [jnp.where(eye, _dot_nt(qb[:, hs], kb[:, hs]), 0.0) for hs in heads]
        for lev in range(HG_LEVELS):
            bref = bref_all[lev * c:(lev + 1) * c, :]
            qs = qsides[lev]
            x = jnp.exp(jnp.where(qs, b - bref, bref - b))
            ql = jnp.where(qs, q * x, 0.0).astype(BF16)
            kl = jnp.where(qs, 0.0, k * x).astype(BF16)
            for h, hs in enumerate(heads):
                a[h] = a[h] + jnp.where(pairs[lev], _dot_nt(ql[:, hs], kl[:, hs]), 0.0)
        stage(i, sl, a, q_in, k_dec, jnp.exp(bl))
        return carry

    @pl.when(safe)
    def _():
        lax.fori_loop(0, nsub, factored, 0, unroll=4)

    @pl.when(jnp.logical_not(safe))
    def _():
        lax.fori_loop(0, nsub, levels, 0)

    scan_states()
    lax.fori_loop(0, nsub, finish_pass, 0, unroll=4)


def _hgrn(safe, pa3, pb3, gamma, layer, rev, o_fwd=None, nrm=None):
    bsz, seq, _ = pa3.shape
    cb = min(HG_BLOCK, seq)
    nc = seq // cb
    wblk = HG_WIDTH

    def cmap(col):
        if rev:
            return lambda b, c, safe_ref: (b, nc - 1 - c, col // wblk)
        return lambda b, c, safe_ref: (b, c, col // wblk)

    const = lambda b, c, safe_ref: (0, 0)
    in_specs = [
        pl.BlockSpec((None, cb, wblk), cmap(A_HQ)),
        pl.BlockSpec((None, cb, wblk), cmap(A_HF_BWD if rev else A_HF_FWD)),
        pl.BlockSpec((None, cb, wblk), cmap(B_HI)),
        pl.BlockSpec(gamma.shape, const),
    ]
    args = [pa3, pa3, pb3, gamma]
    if rev:
        in_specs += [
            pl.BlockSpec((None, cb, wblk), cmap(A_HG)),
            pl.BlockSpec((None, cb, wblk), cmap(0)),
            pl.BlockSpec((1, HG_DK), const),
        ]
        args += [pa3, o_fwd, nrm]
    out_dtype = BF16 if rev else F32
    grid_spec = pltpu.PrefetchScalarGridSpec(
        num_scalar_prefetch=1,
        grid=(bsz, nc),
        in_specs=in_specs,
        out_specs=pl.BlockSpec((None, cb, wblk), cmap(0)),
        scratch_shapes=[pltpu.VMEM((HG_HEADS, HG_DK, HG_DK), F32)] + [pltpu.VMEM((cb, wblk), F32)] * 3
        + [pltpu.VMEM((cb // HG_SUB, HG_HEADS, HG_SUB, HG_SUB), BF16), pltpu.VMEM((cb, wblk), BF16),
           pltpu.VMEM((cb // HG_SUB * 8, wblk), F32),
           pltpu.VMEM((cb // HG_SUB, HG_HEADS, HG_DK, HG_DK), F32),
           pltpu.VMEM((cb // HG_SUB, HG_HEADS, HG_DK, HG_DK), BF16)],
    )
    return pl.pallas_call(
        functools.partial(_hgrn_kernel, layer=layer, rev=rev),
        grid_spec=grid_spec,
        out_shape=jax.ShapeDtypeStruct((bsz, seq, HG_WIDTH), out_dtype),
        compiler_params=_params(("arbitrary", "arbitrary")),
        name="hgrn_bwd" if rev else "hgrn_fwd",
    )(safe, *args)


ATT_ROW_HEADS = (0, 2, 1, 3)


def _t5_buckets(rel):
    half = N_BUCKETS // 2
    ret = np.where(rel > 0, half, 0)
    n = np.abs(rel)
    max_exact = half // 2
    large = max_exact + (np.log(np.maximum(n, 1) / max_exact)
                         / np.log(REL_MAX_DIST / max_exact) * (half - max_exact)).astype(np.int32)
    large = np.minimum(large, half - 1)
    return (ret + np.where(n < max_exact, n, large)).astype(np.int32)


def _attn_tables(rel_bias, att_sink):
    c = ATT_BLOCK
    rel = np.arange(3 * c)[None, :] - c - np.arange(c)[:, None]
    onehot = np.equal(_t5_buckets(rel).reshape(-1, 1), np.arange(N_BUCKETS)).astype(np.float32)
    bias = jnp.dot(jnp.asarray(onehot, BF16), rel_bias.astype(F32), precision=HIGHEST)
    bias = bias.reshape(c, 3 * c, ATT_HEADS).transpose(2, 0, 1)
    bias = jnp.where(jnp.asarray(np.abs(rel) <= WINDOW)[None], bias, NEG_BIG)
    col = np.arange(3 * c)
    cases = []
    for case in range(4):
        valid = np.ones(3 * c, bool)
        if case & 1:
            valid &= col >= c
        if case & 2:
            valid &= col < 2 * c
        cases.append(jnp.where(jnp.asarray(valid)[None, None, :], bias, NEG_BIG))
    tab = jnp.stack(cases)
    order = np.array([[ATT_GROUP * g + h for h in ATT_ROW_HEADS] for g in range(ATT_KV)])
    tab = tab[:, order].reshape(4, ATT_KV, ATT_GROUP * c, 3 * c)
    sink = att_sink.astype(F32)[:, order]
    sink = jnp.broadcast_to(sink[..., None, None], sink.shape + (c, 128))
    return tab, sink.reshape(att_sink.shape[0], ATT_KV, ATT_GROUP * c, 128)


def _attn_kernel(q_ref, kp_ref, kc_ref, kn_ref, vp_ref, vc_ref, vn_ref, bias_ref, sink_ref, o_ref):
    n = pl.program_id(1)
    nsteps = pl.num_programs(1)
    c = ATT_BLOCK
    nsub = q_ref.shape[0] // c
    pair = 2 * ATT_HD
    kwin = jnp.concatenate([kp_ref[...], kc_ref[...], kn_ref[...]], axis=0).astype(F32)
    vwin = jnp.concatenate([vp_ref[...], vc_ref[...], vn_ref[...]], axis=0).astype(F32)
    kroll = pltpu.roll(kwin, ATT_HD, axis=1)
    vroll = pltpu.roll(vwin, ATT_HD, axis=1)
    lo = lax.broadcasted_iota(jnp.int32, kwin.shape, 1) < ATT_HD
    k_lo = [jnp.where(lo, kwin, 0.0).astype(BF16), jnp.where(lo, kroll, 0.0).astype(BF16)]
    k_hi = [jnp.where(lo, 0.0, kroll).astype(BF16), jnp.where(lo, 0.0, kwin).astype(BF16)]
    v_lo = [jnp.where(lo, vwin, 1.0).astype(BF16), jnp.where(lo, vroll, 1.0).astype(BF16)]
    v_hi = [jnp.where(lo, 1.0, vroll).astype(BF16), jnp.where(lo, 1.0, vwin).astype(BF16)]
    lo_out = lax.broadcasted_iota(jnp.int32, (2 * c, pair), 1) < ATT_HD
    for j in range(nsub):
        case = jnp.int32(0)
        if j == 0:
            case = case + (n == 0).astype(jnp.int32)
        if j == nsub - 1:
            case = case + 2 * (n == nsteps - 1).astype(jnp.int32)
        band = slice(j * c, (j + 3) * c)
        qrows = slice(j * c, (j + 1) * c)
        for g in range(ATT_KV):
            qg = q_ref[qrows, g * 2 * pair:(g + 1) * 2 * pair].astype(F32) * (ATT_HD ** -0.5)
            lhs = jnp.concatenate([qg[:, :pair], qg[:, pair:]], axis=0).astype(BF16)
            rhs = jnp.concatenate([k_lo[g][band], k_hi[g][band]], axis=0)
            lg = _dot_nt(lhs, rhs)
            logits = jnp.concatenate([lg[:, :3 * c], lg[:, 3 * c:]], axis=0) + bias_ref[case, g]
            s = sink_ref[g][:, 0:1]
            m = jnp.maximum(jnp.max(logits, axis=-1, keepdims=True), s)
            p = jnp.exp(logits - m).astype(BF16)
            es = jnp.exp(s - m)
            o_even = _dot(p[:2 * c], v_lo[g][band])
            o_odd = _dot(p[2 * c:], v_hi[g][band])
            o = jnp.where(lo_out,
                          o_even / (pltpu.roll(o_even, ATT_HD, axis=1) + es[:2 * c]),
                          o_odd / (pltpu.roll(o_odd, ATT_HD, axis=1) + es[2 * c:]))
            o_ref[qrows, g * 2 * pair:g * 2 * pair + pair] = o[:c].astype(o_ref.dtype)
            o_ref[qrows, g * 2 * pair + pair:(g + 1) * 2 * pair] = o[c:].astype(o_ref.dtype)


def _attn(proj3, bias, sink):
    bsz, seq, _ = proj3.shape
    c = ATT_BLOCK
    nsub = max(k for k in (4, 2, 1) if seq % (k * c) == 0)
    qb = nsub * c
    nsteps = seq // qb
    nb = seq // c
    kcol = B_AK // KV_WIDTH
    vcol = B_AV // KV_WIDTH

    def band(col):
        return [
            pl.BlockSpec((None, c, KV_WIDTH), lambda b, n: (b, jnp.maximum(n * nsub - 1, 0), col)),
            pl.BlockSpec((None, qb, KV_WIDTH), lambda b, n: (b, n, col)),
            pl.BlockSpec((None, c, KV_WIDTH), lambda b, n: (b, jnp.minimum((n + 1) * nsub, nb - 1), col)),
        ]

    return pl.pallas_call(
        _attn_kernel,
        grid=(bsz, nsteps),
        in_specs=[pl.BlockSpec((None, qb, ATT_WIDTH), lambda b, n: (b, n, B_AQ // ATT_WIDTH))]
        + band(kcol) + band(vcol)
        + [pl.BlockSpec(bias.shape, lambda b, n: (0, 0, 0, 0)),
           pl.BlockSpec(sink.shape, lambda b, n: (0, 0, 0))],
        out_specs=pl.BlockSpec((None, qb, ATT_WIDTH), lambda b, n: (b, n, 0)),
        out_shape=jax.ShapeDtypeStruct((bsz, seq, ATT_WIDTH), BF16),
        compiler_params=_params(("arbitrary", "arbitrary")),
        name="window_attn",
    )(proj3, proj3, proj3, proj3, proj3, proj3, proj3, bias, sink)


def _first_argmax(vals):
    best, idx = vals[0], jnp.zeros(vals[0].shape, jnp.int32)
    for j in range(1, len(vals)):
        upd = vals[j] > best
        idx = jnp.where(upd, j, idx)
        best = jnp.where(upd, vals[j], best)
    return best, idx


def _select(vals, idx):
    out = vals[0]
    for j in range(1, len(vals)):
        out = jnp.where(idx == j, vals[j], out)
    return out


def _route(logits_t, rbias):
    m = jnp.max(logits_t, axis=0, keepdims=True)
    e = jnp.exp(logits_t - m)
    scores = e / jnp.sum(e, axis=0, keepdims=True)
    sel = scores + rbias
    srow = [scores[i:i + 1, :] for i in range(N_EXPERTS)]
    lrow = [sel[i:i + 1, :] for i in range(N_EXPERTS)]
    gscore = []
    for g in range(N_GROUPS):
        a, b, c, d = lrow[4 * g:4 * g + 4]
        hi1, lo1 = jnp.maximum(a, b), jnp.minimum(a, b)
        hi2, lo2 = jnp.maximum(c, d), jnp.minimum(c, d)
        gscore.append(jnp.maximum(hi1, hi2) + jnp.maximum(jnp.minimum(hi1, hi2), jnp.maximum(lo1, lo2)))
    _, gi = _first_argmax(gscore)
    ing = [_select([lrow[4 * g + j] for g in range(N_GROUPS)], gi) for j in range(EXPERTS_PER_GROUP)]
    sg = [_select([srow[4 * g + j] for g in range(N_GROUPS)], gi) for j in range(EXPERTS_PER_GROUP)]
    _, i1 = _first_argmax(ing)
    rest = [jnp.where(i1 == j, -jnp.inf, ing[j]) for j in range(EXPERTS_PER_GROUP)]
    _, i2 = _first_argmax(rest)
    s1, s2 = _select(sg, i1), _select(sg, i2)
    tot = s1 + s2
    w1, w2 = s1 / tot, s2 / tot
    idx1 = gi * EXPERTS_PER_GROUP + i1
    idx2 = gi * EXPERTS_PER_GROUP + i2
    return idx1, idx2, w1, w2


MERGE_SPLIT = 1


def _merge_kernel(oh_ref, oa_ref, gh_ref, ga_ref, x_ref, mod_ref, nf_ref, wbh_ref, wba_ref, wo_ref,
                  wrt_ref, rb_ref, x1_ref, h2_ref, ri_ref, rw_ref, cnt_ref, carry_ref):
    @pl.when(pl.program_id(0) == 0)
    def _():
        carry_ref[...] = jnp.zeros_like(carry_ref)

    wr = wrt_ref[...]
    w_hi = wr.astype(BF16)
    w_lo = (wr - w_hi.astype(F32)).astype(BF16)
    tm = x_ref.shape[0]
    th = tm // MERGE_SPLIT
    t_src = lax.broadcasted_iota(jnp.int32, (th, th), 0)
    t_dst = lax.broadcasted_iota(jnp.int32, (th, th), 1)
    before = jnp.where(t_src < t_dst, 1.0, 0.0).astype(BF16)
    erow = lax.broadcasted_iota(jnp.int32, (N_EXPERTS, th), 0)
    carry = carry_ref[...]
    for r in range(MERGE_SPLIT):
        rows = slice(r * th, (r + 1) * th)
        mh = _dot(oh_ref[rows, :], wbh_ref[...])
        ma = _dot(oa_ref[rows, :], wba_ref[...])
        merged = ((jnp.tanh(gh_ref[rows, :].astype(F32)) + 1.0) * mh
                  + (jnp.tanh(ga_ref[rows, :].astype(F32)) + 1.0) * ma)
        out = _dot(merged.astype(BF16), wo_ref[...])
        x1 = x_ref[rows, :] + mod_ref[2:3, :] * out
        x1_ref[rows, :] = x1
        h2 = _rms(x1, nf_ref[...]) * (1.0 + mod_ref[4:5, :]) + mod_ref[3:4, :]
        h2_ref[rows, :] = _pack_bf16_pairs(h2)
        h_hi = h2.astype(BF16)
        h_lo = (h2 - h_hi.astype(F32)).astype(BF16)
        logits_t = (_dot_nt(w_hi, h_hi) + _dot_nt(w_hi, h_lo)) + _dot_nt(w_lo, h_hi)
        idx1, idx2, w1, w2 = _route(logits_t, rb_ref[:, 0:1])
        oh1 = erow == idx1
        oh2 = erow == idx2
        oh = jnp.where(oh1 | oh2, 1.0, 0.0)
        pref = _dot(oh.astype(BF16), before) + carry[:, 0:1]
        rank1 = jnp.sum(jnp.where(oh1, pref, 0.0), axis=0, keepdims=True)
        rank2 = jnp.sum(jnp.where(oh2, pref, 0.0), axis=0, keepdims=True)
        carry = carry + jnp.sum(oh, axis=1, keepdims=True)
        ri_ref[:, rows] = jnp.concatenate([idx1, idx2, rank1.astype(jnp.int32), rank2.astype(jnp.int32)], axis=0)
        rw_ref[:, rows] = jnp.concatenate([w1, w2], axis=0)
    carry_ref[...] = carry
    cnt_ref[...] = carry


def _merge(o_h, o_a, proj, x2d, mod, nf, wbh, wba, wo, wrt, rb, seq):
    n = x2d.shape[0]
    tm = min(512, seq)
    per_seq = seq // tm
    const = lambda i: (0, 0)
    return pl.pallas_call(
        _merge_kernel,
        grid=(n // tm,),
        in_specs=[
            pl.BlockSpec((tm, HG_WIDTH), lambda i: (i, 0)),
            pl.BlockSpec((tm, ATT_WIDTH), lambda i: (i, 0)),
            pl.BlockSpec((tm, D_MODEL), lambda i: (i, B_GATE_H // D_MODEL)),
            pl.BlockSpec((tm, D_MODEL), lambda i: (i, B_GATE_A // D_MODEL)),
            pl.BlockSpec((tm, D_MODEL), lambda i: (i, 0)),
            pl.BlockSpec((None, N_MOD, D_MODEL), lambda i: (i // per_seq, 0, 0)),
            pl.BlockSpec((1, D_MODEL), const),
            pl.BlockSpec(wbh.shape, const),
            pl.BlockSpec(wba.shape, const),
            pl.BlockSpec(wo.shape, const),
            pl.BlockSpec(wrt.shape, const),
            pl.BlockSpec(rb.shape, const),
        ],
        out_specs=[
            pl.BlockSpec((tm, D_MODEL), lambda i: (i, 0)),
            pl.BlockSpec((tm, D_MODEL // 2), lambda i: (i, 0)),
            pl.BlockSpec((4, tm), lambda i: (0, i)),
            pl.BlockSpec((2, tm), lambda i: (0, i)),
            pl.BlockSpec((N_EXPERTS, 128), const),
        ],
        out_shape=[
            jax.ShapeDtypeStruct((n, D_MODEL), F32),
            jax.ShapeDtypeStruct((n, D_MODEL // 2), jnp.uint32),
            jax.ShapeDtypeStruct((4, n), jnp.int32),
            jax.ShapeDtypeStruct((2, n), F32),
            jax.ShapeDtypeStruct((N_EXPERTS, 128), F32),
        ],
        scratch_shapes=[pltpu.VMEM((N_EXPERTS, 128), F32)],
        compiler_params=_params(("arbitrary",)),
        name="merge_router",
    )(o_h, o_a, proj, proj, x2d, mod, nf, wbh, wba, wo, wrt, rb)


MOE_TILE = 512
SC_WINDOW = 64


def _sc_mesh():
    return plsc.VectorSubcoreMesh(core_axis_name="c", subcore_axis_name="s")


def _sc_dispatch(h, pos, p):
    n, d = h.shape
    win = SC_WINDOW
    info = plsc.get_sparse_core_info()
    workers = info.num_cores * info.num_subcores
    wpt = n // (win * workers)
    pos_w = pos.reshape(2, n // win, win).transpose(1, 0, 2)

    @functools.partial(
        pl.kernel, out_type=jax.ShapeDtypeStruct((p, d), h.dtype), mesh=_sc_mesh(),
        scratch_types=[pltpu.VMEM((wpt, 2, win), jnp.int32), pltpu.VMEM((2, win, d), h.dtype),
                       pltpu.SemaphoreType.DMA((2,))],
        name="moe_dispatch")
    def dispatch(h_hbm, pos_hbm, o_hbm, idx_v, rows_v, load_sem):
        wid = lax.axis_index("c") * info.num_subcores + lax.axis_index("s")
        first = wid * wpt
        pltpu.sync_copy(pos_hbm.at[pl.ds(first, wpt)], idx_v)

        def load(j, slot):
            return pltpu.make_async_copy(h_hbm.at[pl.ds((first + j) * win, win)], rows_v.at[slot],
                                         load_sem.at[slot])

        load(0, 0).start()

        @pl.loop(0, wpt, step=2)
        def _(j):
            for slot in range(2):
                jj = j + slot
                load(jj, slot).wait()

                @pl.when(jj + 1 < wpt)
                def _():
                    load(jj + 1, 1 - slot).start()

                pltpu.sync_copy(rows_v.at[slot], o_hbm.at[idx_v.at[jj, 0]])
                pltpu.sync_copy(rows_v.at[slot], o_hbm.at[idx_v.at[jj, 1]])

    return dispatch(h, pos_w)


def _sc_combine(ys, pos):
    n = pos.shape[1]
    d = ys.shape[1]
    win = SC_WINDOW // 2
    info = plsc.get_sparse_core_info()
    workers = info.num_cores * info.num_subcores
    wpt = n // (win * workers)
    pos_w = pos.reshape(2, n // win, win).transpose(1, 0, 2)
    out = jax.ShapeDtypeStruct((n, d), ys.dtype)

    @functools.partial(
        pl.kernel, out_type=(out, out), mesh=_sc_mesh(),
        scratch_types=[pltpu.VMEM((wpt, 2, win), jnp.int32), pltpu.VMEM((2, 2, win, d), ys.dtype),
                       pltpu.SemaphoreType.DMA((2, 2)), pltpu.SemaphoreType.DMA((2,))],
        name="moe_combine")
    def combine(ys_hbm, pos_hbm, a_hbm, b_hbm, idx_v, rows_v, gather_sem, store_sem):
        wid = lax.axis_index("c") * info.num_subcores + lax.axis_index("s")
        first = wid * wpt
        pltpu.sync_copy(pos_hbm.at[pl.ds(first, wpt)], idx_v)
        outs = (a_hbm, b_hbm)

        def gather(j, slot, k):
            return pltpu.make_async_copy(ys_hbm.at[idx_v.at[j, k]], rows_v.at[slot, k], gather_sem.at[slot, k])

        def store(j, slot, k):
            return pltpu.make_async_copy(rows_v.at[slot, k], outs[k].at[pl.ds((first + j) * win, win)],
                                         store_sem.at[k])

        gather(0, 0, 0).start()
        gather(0, 0, 1).start()

        @pl.loop(0, wpt, step=2)
        def _(j):
            for slot in range(2):
                jj = j + slot
                gather(jj, slot, 0).wait()
                gather(jj, slot, 1).wait()

                @pl.when(jj + 1 < wpt)
                def _():
                    gather(jj + 1, 1 - slot, 0).start()
                    gather(jj + 1, 1 - slot, 1).start()

                store(jj, slot, 0).start()
                store(jj, slot, 1).start()
                store(jj, slot, 0).wait()
                store(jj, slot, 1).wait()

    return combine(ys, pos_w)


def _expert_kernel(te_ref, nu_ref, x_ref, wg_ref, wu_ref, wd_ref, o_ref):
    del te_ref
    used = pl.program_id(0) < nu_ref[0]

    @pl.when(used)
    def _():
        x = _unpack_bf16_pairs(x_ref[...]).astype(BF16)
        he = _silu_of_half(_dot(x, wg_ref[...])) * _dot(x, wu_ref[...])
        o_ref[...] = _pack_bf16_pairs(_dot(he.astype(BF16), wd_ref[...]))

    @pl.when(jnp.logical_not(used))
    def _():
        o_ref[...] = jnp.zeros_like(o_ref)


def _experts(xs, tile_expert, n_used, wg, wu, wd, layer):
    p = xs.shape[0]
    tm = MOE_TILE
    grid_spec = pltpu.PrefetchScalarGridSpec(
        num_scalar_prefetch=2,
        grid=(p // tm,),
        in_specs=[
            pl.BlockSpec((tm, D_MODEL // 2), lambda i, te, nu: (i, 0)),
            pl.BlockSpec((None, None, D_MODEL, D_EXPERT), lambda i, te, nu: (layer, te[i], 0, 0)),
            pl.BlockSpec((None, None, D_MODEL, D_EXPERT), lambda i, te, nu: (layer, te[i], 0, 0)),
            pl.BlockSpec((None, None, D_EXPERT, D_MODEL), lambda i, te, nu: (layer, te[i], 0, 0)),
        ],
        out_specs=pl.BlockSpec((tm, D_MODEL // 2), lambda i, te, nu: (i, 0)),
    )
    return pl.pallas_call(
        _expert_kernel,
        grid_spec=grid_spec,
        out_shape=jax.ShapeDtypeStruct((p, D_MODEL // 2), jnp.uint32),
        compiler_params=_params(("arbitrary",)),
        name="moe_experts",
    )(tile_expert, n_used, xs, wg, wu, wd)


def _residual_kernel(ya_ref, yb_ref, w_ref, x1_ref, mod_ref, nfin_ref, o_ref, *, last):
    w = w_ref[...]
    y = w[:, 0:1] * _unpack_bf16_pairs(ya_ref[...]) + w[:, 1:2] * _unpack_bf16_pairs(yb_ref[...])
    x2 = x1_ref[...] + mod_ref[5:6, :] * y
    if last:
        x2 = _rms(x2, nfin_ref[...])
    o_ref[...] = x2


def _residual(ya, yb, w, x1, mod, nfin, seq, last):
    n = x1.shape[0]
    tm = min(512, seq)
    per_seq = seq // tm
    row = pl.BlockSpec((tm, D_MODEL), lambda i: (i, 0))
    packed = pl.BlockSpec((tm, D_MODEL // 2), lambda i: (i, 0))
    return pl.pallas_call(
        functools.partial(_residual_kernel, last=last),
        grid=(n // tm,),
        in_specs=[packed, packed, pl.BlockSpec((tm, 2), lambda i: (i, 0)), row,
                  pl.BlockSpec((None, N_MOD, D_MODEL), lambda i: (i // per_seq, 0, 0)),
                  pl.BlockSpec((1, D_MODEL), lambda i: (0, 0))],
        out_specs=row,
        out_shape=jax.ShapeDtypeStruct((n, D_MODEL), F32),
        compiler_params=_params(("arbitrary",)),
        name="moe_residual",
    )(ya, yb, w, x1, mod, nfin)


def _moe(h2, ri, rw, cnt, wg, wu, wd, layer, x1, mod, nfin, seq, last):
    n = h2.shape[0]
    tm = MOE_TILE
    p = 2 * n + N_EXPERTS * tm
    counts = cnt[:, 0].astype(jnp.int32)
    padded = (counts + tm - 1) // tm * tm
    ends = jnp.cumsum(padded)
    starts = ends - padded
    base = jnp.zeros_like(ri[0:2])
    for e in range(1, N_EXPERTS):
        base = jnp.where(ri[0:2] == e, starts[e], base)
    pos = base + ri[2:4]
    tile_start = jnp.arange(p // tm, dtype=jnp.int32) * tm
    tile_expert = jnp.minimum(jnp.sum(tile_start[:, None] >= ends[None, :], axis=1), N_EXPERTS - 1)
    n_used = (ends[-1:] // tm).astype(jnp.int32)
    xs = _sc_dispatch(h2, pos, p)
    ys = _experts(xs, tile_expert.astype(jnp.int32), n_used, wg, wu, wd, layer)
    ya, yb = _sc_combine(ys, pos)
    return _residual(ya, yb, rw.T, x1, mod, nfin, seq, last)


def _split_w_in(w):
    hq_hf, hi, hg, att, gates = w[..., :1536], w[..., 1536:2048], w[..., 2048:2560], w[..., 2560:3328], w[..., 3328:]
    wa = (0.5 * jnp.concatenate([hq_hf, hg], axis=-1)).astype(BF16)
    wb = jnp.concatenate([0.5 * gates, hi, att], axis=-1).astype(BF16)
    return wa, wb


def _trunk(x, mod, wts):
    bsz, seq, _ = x.shape
    n = bsz * seq
    depth = wts["w_in_a"].shape[0]
    x2d = x.reshape(n, D_MODEL)
    for l in range(depth):
        mod_l = mod[l]
        pa, pb, span = _inproj(x2d, mod_l, wts["norm_mix"][l:l + 1], wts["w_in_a"][l], wts["w_in_b"][l],
                               wts["hg_lb_fwd"], wts["hg_lb_bwd"], seq, l)
        pa3 = pa.reshape(bsz, seq, A_COLS)
        pb3 = pb.reshape(bsz, seq, B_COLS)
        safe_f = (span[:, 0, 0] <= HG_SAFE_SPAN).astype(jnp.int32)
        safe_b = (span[:, 4, 0] <= HG_SAFE_SPAN).astype(jnp.int32)
        o_f = _hgrn(safe_f, pa3, pb3, wts["hg_lb_fwd"], l, False)
        o_h = _hgrn(safe_b, pa3, pb3, wts["hg_lb_bwd"], l, True, o_f, wts["hg_norm"][l:l + 1])
        o_a = _attn(pb3, wts["bias"], wts["sink"][l])
        x1, h2, ri, rw, cnt = _merge(o_h.reshape(n, HG_WIDTH), o_a.reshape(n, ATT_WIDTH), pb, x2d, mod_l,
                                     wts["norm_ffn"][l:l + 1], wts["w_br_hgrn"][l], wts["w_br_att"][l],
                                     wts["w_out"][l], wts["w_router_t"], wts["router_bias"], seq)
        x2d = _moe(h2, ri, rw, cnt, wts["w_gate"], wts["w_up"], wts["w_down"], l, x1, mod_l,
                   wts["norm_final"], seq, l == depth - 1)
    return x2d.reshape(bsz, seq, D_MODEL)


def kernel(x_prompt, x_sample, c_prompt, c_sample, w_ada, b_ada, norm_mix, norm_ffn, norm_final, w_in, hg_lb_fwd, hg_lb_bwd, hg_norm, att_sink, rel_bias, w_br_hgrn, w_br_att, w_out, w_router, router_bias, w_gate, w_up, w_down):
    depth = w_in.shape[0]
    bp, bs = c_prompt.shape[0], c_sample.shape[0]
    rows = -(-(bp + bs) // 8) * 8
    c_all = jnp.concatenate([c_prompt, c_sample, jnp.zeros((rows - bp - bs, D_MODEL), F32)], axis=0)
    mod = _ada(c_all, w_ada, b_ada).reshape(depth, rows, N_MOD, D_MODEL)
    bias, sink = _attn_tables(rel_bias, att_sink)
    w_in_a, w_in_b = _split_w_in(w_in)
    wts = {
        "norm_mix": norm_mix, "norm_ffn": norm_ffn, "norm_final": norm_final.reshape(1, D_MODEL),
        "w_in_a": w_in_a, "w_in_b": w_in_b,
        "hg_lb_fwd": hg_lb_fwd, "hg_lb_bwd": hg_lb_bwd, "hg_norm": hg_norm,
        "sink": sink, "bias": bias,
        "w_br_hgrn": w_br_hgrn.astype(BF16), "w_br_att": w_br_att.astype(BF16), "w_out": (0.5 * w_out).astype(BF16),
        "w_router_t": w_router.T,
        "router_bias": jnp.broadcast_to(router_bias[:, None], (N_EXPERTS, 128)),
        "w_gate": (0.5 * w_gate).astype(BF16), "w_up": w_up.astype(BF16), "w_down": w_down.astype(BF16),
    }
    y_prompt = _trunk(x_prompt, mod[:, :bp], wts)
    y_sample = _trunk(x_sample, mod[:, bp:bp + bs], wts)
    return (y_prompt, y_sample)
```

```python
import functools

import numpy as np
import jax
import jax.numpy as jnp
from jax import lax
from jax.experimental import pallas as pl
from jax.experimental.pallas import tpu as pltpu
from jax.experimental.pallas import tpu_sc as plsc

D_MODEL = 1024
HG_DK = 128
HG_WIDTH = 512
HG_HEADS = 4
HG_SUB = 64
HG_LEVELS = 6
HG_GROUP = 32
HG_SAFE_SPAN = 80.0
HG_BLOCK = 512
ATT_HD = 64
ATT_HEADS = 8
ATT_KV = 2
ATT_GROUP = 4
ATT_WIDTH = 512
KV_WIDTH = 128
WINDOW = 128
ATT_BLOCK = 128
N_BUCKETS = 32
REL_MAX_DIST = 128
N_EXPERTS = 16
N_GROUPS = 4
EXPERTS_PER_GROUP = 4
D_EXPERT = 512
N_MOD = 6
IN_COLS = 5376
EPS = 1e-6
NEG_BIG = -1e30
TINY = 1e-30

A_COLS = 2048
A_HQ = 0
A_HF_FWD = 512
A_HF_BWD = 1024
A_HG = 1536
B_COLS = 3328
B_GATE_H = 0
B_GATE_A = 1024
B_HI = 2048
B_AQ = 2560
B_AK = 3072
B_AV = 3200

V7X_VMEM_LIMIT = 56 * 1024 * 1024

F32 = jnp.float32
BF16 = jnp.bfloat16
HIGHEST = lax.Precision.HIGHEST


def _params(sem, **kw):
    return pltpu.CompilerParams(dimension_semantics=sem, vmem_limit_bytes=V7X_VMEM_LIMIT, **kw)


def _dot(a, b):
    return jnp.dot(a, b, preferred_element_type=F32)


def _dot_nt(a, b):
    return lax.dot_general(a, b, (((1,), (1,)), ((), ())), preferred_element_type=F32)


def _dot_tn(a, b):
    return lax.dot_general(a, b, (((0,), (0,)), ((), ())), preferred_element_type=F32)


def _sigmoid(x):
    return 0.5 * jnp.tanh(0.5 * x) + 0.5


def _pack_bf16_pairs(x):
    c = x.shape[1] // 2
    bits = lax.bitcast_convert_type(x.astype(BF16).astype(F32), jnp.uint32)
    return (bits[:, :c] >> 16) | (bits[:, c:] & jnp.uint32(0xFFFF0000))


def _unpack_bf16_pairs(u):
    lo = lax.bitcast_convert_type(u << 16, F32)
    hi = lax.bitcast_convert_type(u & jnp.uint32(0xFFFF0000), F32)
    return jnp.concatenate([lo, hi], axis=1)


def _silu(x):
    return x * _sigmoid(x)


def _silu_of_half(xh):
    return xh * (jnp.tanh(xh) + 1.0)


def _rms(x, g):
    return x * lax.rsqrt(jnp.mean(x * x, axis=-1, keepdims=True) + EPS) * g


def _ada_kernel(c_ref, w_ref, b_ref, o_ref):
    c = c_ref[...]
    o_ref[...] = jnp.dot(_silu(c), w_ref[...], precision=HIGHEST, preferred_element_type=F32) + b_ref[...]


def _ada(c_all, w_ada, b_ada):
    depth = w_ada.shape[0]
    rows = c_all.shape[0]
    ncol = w_ada.shape[2]
    tn = 1024
    return pl.pallas_call(
        _ada_kernel,
        grid=(depth, ncol // tn),
        in_specs=[
            pl.BlockSpec((rows, D_MODEL), lambda l, j: (0, 0)),
            pl.BlockSpec((None, D_MODEL, tn), lambda l, j: (l, 0, j)),
            pl.BlockSpec((None, 1, tn), lambda l, j: (l, 0, j)),
        ],
        out_specs=pl.BlockSpec((None, rows, tn), lambda l, j: (l, 0, j)),
        out_shape=jax.ShapeDtypeStruct((depth, rows, ncol), F32),
        compiler_params=_params(("arbitrary", "arbitrary")),
        name="ada_mod",
    )(c_all, w_ada, b_ada.reshape(depth, 1, ncol))


def _lower_bound_row(gam_ref, layer):
    rows = [gam_ref[d:d + 1, :] for d in range(gam_ref.shape[0])]
    m = functools.reduce(jnp.maximum, rows)
    es = [jnp.exp(r - m) for r in rows]
    tot = functools.reduce(lambda a, b: a + b, es)
    ps = [e / tot for e in es]
    cum = ps[0]
    for d in range(1, layer + 1):
        cum = cum + ps[d]
    return jnp.clip(cum - ps[0], 0.0, 1.0)


def _forget(zh, lb):
    return 0.5 * (1.0 + lb) + (0.5 * (1.0 - lb)) * jnp.tanh(zh)


def _inproj_kernel(x_ref, mod_ref, g_ref, wa_ref, wb_ref, gf_ref, gb_ref, oa_ref, ob_ref, span_ref, *, layer):
    x = x_ref[...]
    h = (_rms(x, g_ref[...]) * (1.0 + mod_ref[1:2, :]) + mod_ref[0:1, :]).astype(BF16)
    oa = _dot(h, wa_ref[...])
    oa_ref[...] = oa
    ob_ref[...] = _dot(h, wb_ref[...]).astype(BF16)
    spans = []
    for gam_ref, col in ((gf_ref, A_HF_FWD), (gb_ref, A_HF_BWD)):
        f = _forget(oa[:, col:col + HG_WIDTH], _lower_bound_row(gam_ref, layer))
        g = jnp.log(jnp.maximum(f, TINY))
        gsum = jnp.sum(g.reshape(g.shape[0] // HG_GROUP, HG_GROUP, HG_WIDTH), axis=1)
        spans.append(jnp.max(jnp.max(-gsum, axis=0, keepdims=True), axis=1, keepdims=True))
    half = lax.broadcasted_iota(jnp.int32, span_ref.shape, 0) < span_ref.shape[0] // 2
    span_ref[...] = jnp.where(half, spans[0], spans[1])


def _inproj(x2d, mod, g, wa, wb, gam_f, gam_b, seq, layer):
    n = x2d.shape[0]
    tm = min(HG_BLOCK, seq)
    per_seq = seq // tm
    const = lambda i: (0, 0)
    return pl.pallas_call(
        functools.partial(_inproj_kernel, layer=layer),
        grid=(n // tm,),
        in_specs=[
            pl.BlockSpec((tm, D_MODEL), lambda i: (i, 0)),
            pl.BlockSpec((None, N_MOD, D_MODEL), lambda i: (i // per_seq, 0, 0)),
            pl.BlockSpec((1, D_MODEL), const),
            pl.BlockSpec((D_MODEL, A_COLS), const, pipeline_mode=pl.Buffered(1)),
            pl.BlockSpec((D_MODEL, B_COLS), const, pipeline_mode=pl.Buffered(1)),
            pl.BlockSpec(gam_f.shape, const),
            pl.BlockSpec(gam_b.shape, const),
        ],
        out_specs=[pl.BlockSpec((tm, A_COLS), lambda i: (i, 0)),
                   pl.BlockSpec((tm, B_COLS), lambda i: (i, 0)),
                   pl.BlockSpec((None, 8, 128), lambda i: (i, 0, 0))],
        out_shape=[jax.ShapeDtypeStruct((n, A_COLS), F32), jax.ShapeDtypeStruct((n, B_COLS), BF16),
                   jax.ShapeDtypeStruct((n // tm, 8, 128), F32)],
        compiler_params=_params(("arbitrary",)),
        name="inproj",
    )(x2d, mod, g, wa, wb, gam_f, gam_b)


def _hgrn_level_tables(rev):
    c = HG_SUB
    r = lax.broadcasted_iota(jnp.int32, (c, c), 0)
    s = lax.broadcasted_iota(jnp.int32, (c, c), 1)
    row = lax.broadcasted_iota(jnp.int32, (c, HG_WIDTH), 0)
    sels, qsides, pairs = [], [], []
    for lev in range(HG_LEVELS):
        half = 1 << lev
        blk = 2 * half
        r_up = (r & (blk - 1)) >= half
        s_up = (s & (blk - 1)) >= half
        base = r - (r & (blk - 1))
        mrow = base + (half if rev else half - 1)
        sels.append(jnp.where(s == mrow, 1.0, 0.0).astype(F32))
        row_up = (row & (blk - 1)) >= half
        qsides.append(~row_up if rev else row_up)
        same = (r >> (lev + 1)) == (s >> (lev + 1))
        pairs.append(same & ((~r_up & s_up) if rev else (r_up & ~s_up)))
    return jnp.concatenate(sels, axis=0), qsides, pairs, r == s


def _hgrn_kernel(safe_ref, *refs, layer, rev):
    if rev:
        q_ref, f_ref, v_ref, gam_ref, hg_ref, of_ref, nrm_ref, o_ref = refs[:8]
    else:
        q_ref, f_ref, v_ref, gam_ref, o_ref = refs[:5]
    st_ref, qs_ref, ks_ref, bs_ref, a_ref, qi_ref, dec_ref, up_ref, sb_ref = refs[-9:]

    @pl.when(pl.program_id(1) == 0)
    def _():
        st_ref[...] = jnp.zeros_like(st_ref)

    nc = pl.num_programs(1)
    chunk = (nc - 1 - pl.program_id(1)) if rev else pl.program_id(1)
    safe = safe_ref[pl.program_id(0) * nc + chunk] != 0
    c = HG_SUB
    w = HG_WIDTH
    nsub = q_ref.shape[0] // c
    lb = _lower_bound_row(gam_ref, layer)
    r_i = lax.broadcasted_iota(jnp.int32, (c, c), 0)
    s_i = lax.broadcasted_iota(jnp.int32, (c, c), 1)
    causal = (s_i >= r_i) if rev else (s_i <= r_i)
    tri = jnp.where(causal, 1.0, 0.0).astype(BF16)
    row = lax.broadcasted_iota(jnp.int32, (c, w), 0)
    far = (row < HG_GROUP) if rev else (row >= HG_GROUP)
    ref_row = HG_GROUP if rev else HG_GROUP - 1
    last_row = 0 if rev else c - 1
    heads = [slice(h * HG_DK, (h + 1) * HG_DK) for h in range(HG_HEADS)]

    def rows_of(i):
        ci = (nsub - 1 - i) if rev else i
        return pl.ds(pl.multiple_of(ci * c, c), c)

    def gates_pass(i, carry):
        sl = rows_of(i)
        zq = q_ref[sl, :]
        f = _forget(f_ref[sl, :], lb)
        g = jnp.log(jnp.maximum(f, TINY))
        qs_ref[sl, :] = _silu_of_half(zq) * (HG_DK ** -0.5)
        ks_ref[sl, :] = 1.0 - f
        g1 = g.astype(BF16)
        r1 = g - g1.astype(F32)
        g2 = r1.astype(BF16)
        g3 = (r1 - g2.astype(F32)).astype(BF16)
        bb = _dot(tri, jnp.concatenate([g1, g2, g3], axis=1))
        bs_ref[sl, :] = (bb[:, :w] + bb[:, w:2 * w]) + bb[:, 2 * w:]
        return carry

    lax.fori_loop(0, nsub, gates_pass, 0, unroll=4)

    def gates(i):
        sl = rows_of(i)
        return sl, qs_ref[sl, :], ks_ref[sl, :], bs_ref[sl, :]

    def stage(i, sl, a, q_in, k_dec, dec):
        vb = v_ref[sl, :].astype(BF16)
        for h, hs in enumerate(heads):
            a_ref[i, h] = a[h].astype(BF16)
            up_ref[i, h] = _dot_tn(vb[:, hs], k_dec[:, hs])
        qi_ref[sl, :] = q_in
        dec_ref[pl.ds(pl.multiple_of(i * 8, 8), 8), :] = jnp.broadcast_to(dec, (8, w))

    def scan_states():
        for h, hs in enumerate(heads):
            st = st_ref[h]
            for i in range(nsub):
                sb_ref[i, h] = st.astype(BF16)
                st = st * dec_ref[i * 8:i * 8 + 1, hs] + up_ref[i, h]
            st_ref[h] = st

    def finish_pass(i, carry):
        sl = rows_of(i)
        vb = v_ref[sl, :].astype(BF16)
        q_in = qi_ref[sl, :]
        outs = [_dot_nt(q_in[:, hs], sb_ref[i, h]) + _dot(a_ref[i, h], vb[:, hs]) for h, hs in enumerate(heads)]
        o_all = jnp.concatenate(outs, axis=1)
        if rev:
            tot = of_ref[sl, :] + o_all
            nrm = nrm_ref[...]
            ys = [_rms(tot[:, hs], nrm) for hs in heads]
            o_ref[sl, :] = (jnp.concatenate(ys, axis=1) * _silu_of_half(hg_ref[sl, :])).astype(o_ref.dtype)
        else:
            o_ref[sl, :] = o_all
        return carry

    def factored(i, carry):
        sl, q, k, b = gates(i)
        r = b[ref_row:ref_row + 1, :]
        bl = b[last_row:last_row + 1, :]
        rg = jnp.where(far, r, 0.0)
        qt = q * jnp.exp(b - rg)
        kt = k * jnp.exp(rg - b)
        er = jnp.exp(r)
        qn = jnp.where(far, 0.0, qt).astype(BF16)
        qf = jnp.where(far, qt, 0.0).astype(BF16)
        kc = jnp.where(far, kt, kt * er).astype(BF16)
        ktb = kt.astype(BF16)
        q_in = jnp.where(far, qt * er, qt).astype(BF16)
        k_dec = (kt * jnp.where(far, jnp.exp(bl - r), jnp.exp(bl))).astype(BF16)
        a = []
        for hs in heads:
            lhs = jnp.concatenate([qn[:, hs], qf[:, hs]], axis=1)
            rhs = jnp.concatenate([ktb[:, hs], kc[:, hs]], axis=1)
            a.append(jnp.where(causal, _dot_nt(lhs, rhs), 0.0))
        stage(i, sl, a, q_in, k_dec, jnp.exp(bl))
        return carry

    def levels(i, carry):
        sl, q, k, b = gates(i)
        sel_all, qsides, pairs, eye = _hgrn_level_tables(rev)
        bl = b[last_row:last_row + 1, :]
        q_in = (q * jnp.exp(b)).astype(BF16)
        k_dec = (k * jnp.exp(bl - b)).astype(BF16)
        bref_all = jnp.dot(sel_all, b, precision=HIGHEST, preferred_element_type=F32)
        qb = q.astype(BF16)
        kb = k.astype(BF16)
        a = [jnp.where(eye, _dot_nt(qb[:, hs], kb[:, hs]), 0.0) for hs in heads]
        for lev in range(HG_LEVELS):
            bref = bref_all[lev * c:(lev + 1) * c, :]
            qs = qsides[lev]
            x = jnp.exp(jnp.where(qs, b - bref, bref - b))
            ql = jnp.where(qs, q * x, 0.0).astype(BF16)
            kl = jnp.where(qs, 0.0, k * x).astype(BF16)
            for h, hs in enumerate(heads):
                a[h] = a[h] + jnp.where(pairs[lev], _dot_nt(ql[:, hs], kl[:, hs]), 0.0)
        stage(i, sl, a, q_in, k_dec, jnp.exp(bl))
        return carry

    @pl.when(safe)
    def _():
        lax.fori_loop(0, nsub, factored, 0, unroll=4)

    @pl.when(jnp.logical_not(safe))
    def _():
        lax.fori_loop(0, nsub, levels, 0)

    scan_states()
    lax.fori_loop(0, nsub, finish_pass, 0, unroll=4)


def _hgrn(safe, pa3, pb3, gamma, layer, rev, o_fwd=None, nrm=None):
    bsz, seq, _ = pa3.shape
    cb = min(HG_BLOCK, seq)
    nc = seq // cb
    wblk = HG_WIDTH

    def cmap(col):
        if rev:
            return lambda b, c, safe_ref: (b, nc - 1 - c, col // wblk)
        return lambda b, c, safe_ref: (b, c, col // wblk)

    const = lambda b, c, safe_ref: (0, 0)
    in_specs = [
        pl.BlockSpec((None, cb, wblk), cmap(A_HQ)),
        pl.BlockSpec((None, cb, wblk), cmap(A_HF_BWD if rev else A_HF_FWD)),
        pl.BlockSpec((None, cb, wblk), cmap(B_HI)),
        pl.BlockSpec(gamma.shape, const),
    ]
    args = [pa3, pa3, pb3, gamma]
    if rev:
        in_specs += [
            pl.BlockSpec((None, cb, wblk), cmap(A_HG)),
            pl.BlockSpec((None, cb, wblk), cmap(0)),
            pl.BlockSpec((1, HG_DK), const),
        ]
        args += [pa3, o_fwd, nrm]
    out_dtype = BF16 if rev else F32
    grid_spec = pltpu.PrefetchScalarGridSpec(
        num_scalar_prefetch=1,
        grid=(bsz, nc),
        in_specs=in_specs,
        out_specs=pl.BlockSpec((None, cb, wblk), cmap(0)),
        scratch_shapes=[pltpu.VMEM((HG_HEADS, HG_DK, HG_DK), F32)] + [pltpu.VMEM((cb, wblk), F32)] * 3
        + [pltpu.VMEM((cb // HG_SUB, HG_HEADS, HG_SUB, HG_SUB), BF16), pltpu.VMEM((cb, wblk), BF16),
           pltpu.VMEM((cb // HG_SUB * 8, wblk), F32),
           pltpu.VMEM((cb // HG_SUB, HG_HEADS, HG_DK, HG_DK), F32),
           pltpu.VMEM((cb // HG_SUB, HG_HEADS, HG_DK, HG_DK), BF16)],
    )
    return pl.pallas_call(
        functools.partial(_hgrn_kernel, layer=layer, rev=rev),
        grid_spec=grid_spec,
        out_shape=jax.ShapeDtypeStruct((bsz, seq, HG_WIDTH), out_dtype),
        compiler_params=_params(("arbitrary", "arbitrary")),
        name="hgrn_bwd" if rev else "hgrn_fwd",
    )(safe, *args)


ATT_ROW_HEADS = (0, 2, 1, 3)


def _t5_buckets(rel):
    half = N_BUCKETS // 2
    ret = np.where(rel > 0, half, 0)
    n = np.abs(rel)
    max_exact = half // 2
    large = max_exact + (np.log(np.maximum(n, 1) / max_exact)
                         / np.log(REL_MAX_DIST / max_exact) * (half - max_exact)).astype(np.int32)
    large = np.minimum(large, half - 1)
    return (ret + np.where(n < max_exact, n, large)).astype(np.int32)


def _attn_tables(rel_bias, att_sink):
    c = ATT_BLOCK
    rel = np.arange(3 * c)[None, :] - c - np.arange(c)[:, None]
    onehot = np.equal(_t5_buckets(rel).reshape(-1, 1), np.arange(N_BUCKETS)).astype(np.float32)
    bias = jnp.dot(jnp.asarray(onehot, BF16), rel_bias.astype(F32), precision=HIGHEST)
    bias = bias.reshape(c, 3 * c, ATT_HEADS).transpose(2, 0, 1)
    bias = jnp.where(jnp.asarray(np.abs(rel) <= WINDOW)[None], bias, NEG_BIG)
    col = np.arange(3 * c)
    cases = []
    for case in range(4):
        valid = np.ones(3 * c, bool)
        if case & 1:
            valid &= col >= c
        if case & 2:
            valid &= col < 2 * c
        cases.append(jnp.where(jnp.asarray(valid)[None, None, :], bias, NEG_BIG))
    tab = jnp.stack(cases)
    order = np.array([[ATT_GROUP * g + h for h in ATT_ROW_HEADS] for g in range(ATT_KV)])
    tab = tab[:, order].reshape(4, ATT_KV, ATT_GROUP * c, 3 * c)
    sink = att_sink.astype(F32)[:, order]
    sink = jnp.broadcast_to(sink[..., None, None], sink.shape + (c, 128))
    return tab, sink.reshape(att_sink.shape[0], ATT_KV, ATT_GROUP * c, 128)


def _attn_kernel(q_ref, kp_ref, kc_ref, kn_ref, vp_ref, vc_ref, vn_ref, bias_ref, sink_ref, o_ref):
    n = pl.program_id(1)
    nsteps = pl.num_programs(1)
    c = ATT_BLOCK
    nsub = q_ref.shape[0] // c
    pair = 2 * ATT_HD
    kwin = jnp.concatenate([kp_ref[...], kc_ref[...], kn_ref[...]], axis=0).astype(F32)
    vwin = jnp.concatenate([vp_ref[...], vc_ref[...], vn_ref[...]], axis=0).astype(F32)
    kroll = pltpu.roll(kwin, ATT_HD, axis=1)
    vroll = pltpu.roll(vwin, ATT_HD, axis=1)
    lo = lax.broadcasted_iota(jnp.int32, kwin.shape, 1) < ATT_HD
    k_lo = [jnp.where(lo, kwin, 0.0).astype(BF16), jnp.where(lo, kroll, 0.0).astype(BF16)]
    k_hi = [jnp.where(lo, 0.0, kroll).astype(BF16), jnp.where(lo, 0.0, kwin).astype(BF16)]
    v_lo = [jnp.where(lo, vwin, 1.0).astype(BF16), jnp.where(lo, vroll, 1.0).astype(BF16)]
    v_hi = [jnp.where(lo, 1.0, vroll).astype(BF16), jnp.where(lo, 1.0, vwin).astype(BF16)]
    lo_out = lax.broadcasted_iota(jnp.int32, (2 * c, pair), 1) < ATT_HD
    for j in range(nsub):
        case = jnp.int32(0)
        if j == 0:
            case = case + (n == 0).astype(jnp.int32)
        if j == nsub - 1:
            case = case + 2 * (n == nsteps - 1).astype(jnp.int32)
        band = slice(j * c, (j + 3) * c)
        qrows = slice(j * c, (j + 1) * c)
        for g in range(ATT_KV):
            qg = q_ref[qrows, g * 2 * pair:(g + 1) * 2 * pair].astype(F32) * (ATT_HD ** -0.5)
            lhs = jnp.concatenate([qg[:, :pair], qg[:, pair:]], axis=0).astype(BF16)
            rhs = jnp.concatenate([k_lo[g][band], k_hi[g][band]], axis=0)
            lg = _dot_nt(lhs, rhs)
            logits = jnp.concatenate([lg[:, :3 * c], lg[:, 3 * c:]], axis=0) + bias_ref[case, g]
            s = sink_ref[g][:, 0:1]
            m = jnp.maximum(jnp.max(logits, axis=-1, keepdims=True), s)
            p = jnp.exp(logits - m).astype(BF16)
            es = jnp.exp(s - m)
            o_even = _dot(p[:2 * c], v_lo[g][band])
            o_odd = _dot(p[2 * c:], v_hi[g][band])
            o = jnp.where(lo_out,
                          o_even / (pltpu.roll(o_even, ATT_HD, axis=1) + es[:2 * c]),
                          o_odd / (pltpu.roll(o_odd, ATT_HD, axis=1) + es[2 * c:]))
            o_ref[qrows, g * 2 * pair:g * 2 * pair + pair] = o[:c].astype(o_ref.dtype)
            o_ref[qrows, g * 2 * pair + pair:(g + 1) * 2 * pair] = o[c:].astype(o_ref.dtype)


def _attn(proj3, bias, sink):
    bsz, seq, _ = proj3.shape
    c = ATT_BLOCK
    nsub = max(k for k in (4, 2, 1) if seq % (k * c) == 0)
    qb = nsub * c
    nsteps = seq // qb
    nb = seq // c
    kcol = B_AK // KV_WIDTH
    vcol = B_AV // KV_WIDTH

    def band(col):
        return [
            pl.BlockSpec((None, c, KV_WIDTH), lambda b, n: (b, jnp.maximum(n * nsub - 1, 0), col)),
            pl.BlockSpec((None, qb, KV_WIDTH), lambda b, n: (b, n, col)),
            pl.BlockSpec((None, c, KV_WIDTH), lambda b, n: (b, jnp.minimum((n + 1) * nsub, nb - 1), col)),
        ]

    return pl.pallas_call(
        _attn_kernel,
        grid=(bsz, nsteps),
        in_specs=[pl.BlockSpec((None, qb, ATT_WIDTH), lambda b, n: (b, n, B_AQ // ATT_WIDTH))]
        + band(kcol) + band(vcol)
        + [pl.BlockSpec(bias.shape, lambda b, n: (0, 0, 0, 0)),
           pl.BlockSpec(sink.shape, lambda b, n: (0, 0, 0))],
        out_specs=pl.BlockSpec((None, qb, ATT_WIDTH), lambda b, n: (b, n, 0)),
        out_shape=jax.ShapeDtypeStruct((bsz, seq, ATT_WIDTH), BF16),
        compiler_params=_params(("arbitrary", "arbitrary")),
        name="window_attn",
    )(proj3, proj3, proj3, proj3, proj3, proj3, proj3, bias, sink)


def _first_argmax(vals):
    best, idx = vals[0], jnp.zeros(vals[0].shape, jnp.int32)
    for j in range(1, len(vals)):
        upd = vals[j] > best
        idx = jnp.where(upd, j, idx)
        best = jnp.where(upd, vals[j], best)
    return best, idx


def _select(vals, idx):
    out = vals[0]
    for j in range(1, len(vals)):
        out = jnp.where(idx == j, vals[j], out)
    return out


def _route(logits_t, rbias):
    m = jnp.max(logits_t, axis=0, keepdims=True)
    e = jnp.exp(logits_t - m)
    scores = e / jnp.sum(e, axis=0, keepdims=True)
    sel = scores + rbias
    srow = [scores[i:i + 1, :] for i in range(N_EXPERTS)]
    lrow = [sel[i:i + 1, :] for i in range(N_EXPERTS)]
    gscore = []
    for g in range(N_GROUPS):
        a, b, c, d = lrow[4 * g:4 * g + 4]
        hi1, lo1 = jnp.maximum(a, b), jnp.minimum(a, b)
        hi2, lo2 = jnp.maximum(c, d), jnp.minimum(c, d)
        gscore.append(jnp.maximum(hi1, hi2) + jnp.maximum(jnp.minimum(hi1, hi2), jnp.maximum(lo1, lo2)))
    _, gi = _first_argmax(gscore)
    ing = [_select([lrow[4 * g + j] for g in range(N_GROUPS)], gi) for j in range(EXPERTS_PER_GROUP)]
    sg = [_select([srow[4 * g + j] for g in range(N_GROUPS)], gi) for j in range(EXPERTS_PER_GROUP)]
    _, i1 = _first_argmax(ing)
    rest = [jnp.where(i1 == j, -jnp.inf, ing[j]) for j in range(EXPERTS_PER_GROUP)]
    _, i2 = _first_argmax(rest)
    s1, s2 = _select(sg, i1), _select(sg, i2)
    tot = s1 + s2
    w1, w2 = s1 / tot, s2 / tot
    idx1 = gi * EXPERTS_PER_GROUP + i1
    idx2 = gi * EXPERTS_PER_GROUP + i2
    return idx1, idx2, w1, w2


MERGE_SPLIT = 1


def _merge_kernel(oh_ref, oa_ref, gh_ref, ga_ref, x_ref, mod_ref, nf_ref, wbh_ref, wba_ref, wo_ref,
                  wrt_ref, rb_ref, before_ref, x1_ref, h2_ref, ri_ref, rw_ref, cnt_ref, carry_ref):
    @pl.when(pl.program_id(0) == 0)
    def _():
        carry_ref[...] = jnp.zeros_like(carry_ref)

    wr = wrt_ref[...]
    w_hi = wr.astype(BF16)
    w_lo = (wr - w_hi.astype(F32)).astype(BF16)
    tm = x_ref.shape[0]
    th = tm // MERGE_SPLIT
    before = before_ref[...]
    erow = lax.broadcasted_iota(jnp.int32, (N_EXPERTS, th), 0)
    carry = carry_ref[...]
    for r in range(MERGE_SPLIT):
        rows = slice(r * th, (r + 1) * th)
        mh = _dot(oh_ref[rows, :], wbh_ref[...])
        ma = _dot(oa_ref[rows, :], wba_ref[...])
        merged = ((jnp.tanh(gh_ref[rows, :].astype(F32)) + 1.0) * mh
                  + (jnp.tanh(ga_ref[rows, :].astype(F32)) + 1.0) * ma)
        out = _dot(merged.astype(BF16), wo_ref[...])
        x1 = x_ref[rows, :] + mod_ref[2:3, :] * out
        x1_ref[rows, :] = x1
        h2 = _rms(x1, nf_ref[...]) * (1.0 + mod_ref[4:5, :]) + mod_ref[3:4, :]
        h2_ref[rows, :] = _pack_bf16_pairs(h2)
        h_hi = h2.astype(BF16)
        h_lo = (h2 - h_hi.astype(F32)).astype(BF16)
        logits_t = (_dot_nt(w_hi, h_hi) + _dot_nt(w_hi, h_lo)) + _dot_nt(w_lo, h_hi)
        idx1, idx2, w1, w2 = _route(logits_t, rb_ref[:, 0:1])
        oh1 = erow == idx1
        oh2 = erow == idx2
        oh = jnp.where(oh1 | oh2, 1.0, 0.0)
        pref = _dot(oh.astype(BF16), before) + carry[:, 0:1]
        rank1 = jnp.sum(jnp.where(oh1, pref, 0.0), axis=0, keepdims=True)
        rank2 = jnp.sum(jnp.where(oh2, pref, 0.0), axis=0, keepdims=True)
        carry = carry + jnp.sum(oh, axis=1, keepdims=True)
        ri_ref[:, rows] = jnp.concatenate([idx1, idx2, rank1.astype(jnp.int32), rank2.astype(jnp.int32)], axis=0)
        rw_ref[:, rows] = jnp.concatenate([w1, w2], axis=0)
    carry_ref[...] = carry
    cnt_ref[...] = carry


def _merge(o_h, o_a, proj, x2d, mod, nf, wbh, wba, wo, wrt, rb, seq):
    n = x2d.shape[0]
    tm = min(1024, seq)
    per_seq = seq // tm
    const = lambda i: (0, 0)
    th = tm // MERGE_SPLIT
    before = jnp.asarray(np.arange(th)[:, None] < np.arange(th)[None, :], BF16)
    return pl.pallas_call(
        _merge_kernel,
        grid=(n // tm,),
        in_specs=[
            pl.BlockSpec((tm, HG_WIDTH), lambda i: (i, 0)),
            pl.BlockSpec((tm, ATT_WIDTH), lambda i: (i, 0)),
            pl.BlockSpec((tm, D_MODEL), lambda i: (i, B_GATE_H // D_MODEL)),
            pl.BlockSpec((tm, D_MODEL), lambda i: (i, B_GATE_A // D_MODEL)),
            pl.BlockSpec((tm, D_MODEL), lambda i: (i, 0)),
            pl.BlockSpec((None, N_MOD, D_MODEL), lambda i: (i // per_seq, 0, 0)),
            pl.BlockSpec((1, D_MODEL), const),
            pl.BlockSpec(wbh.shape, const),
            pl.BlockSpec(wba.shape, const),
            pl.BlockSpec(wo.shape, const),
            pl.BlockSpec(wrt.shape, const),
            pl.BlockSpec(rb.shape, const),
            pl.BlockSpec(before.shape, const),
        ],
        out_specs=[
            pl.BlockSpec((tm, D_MODEL), lambda i: (i, 0)),
            pl.BlockSpec((tm, D_MODEL // 2), lambda i: (i, 0)),
            pl.BlockSpec((4, tm), lambda i: (0, i)),
            pl.BlockSpec((2, tm), lambda i: (0, i)),
            pl.BlockSpec((N_EXPERTS, 128), const),
        ],
        out_shape=[
            jax.ShapeDtypeStruct((n, D_MODEL), F32),
            jax.ShapeDtypeStruct((n, D_MODEL // 2), jnp.uint32),
            jax.ShapeDtypeStruct((4, n), jnp.int32),
            jax.ShapeDtypeStruct((2, n), F32),
            jax.ShapeDtypeStruct((N_EXPERTS, 128), F32),
        ],
        scratch_shapes=[pltpu.VMEM((N_EXPERTS, 128), F32)],
        compiler_params=_params(("arbitrary",)),
        name="merge_router",
    )(o_h, o_a, proj, proj, x2d, mod, nf, wbh, wba, wo, wrt, rb, before)


MOE_TILE = 512
SC_WINDOW = 64


def _sc_mesh():
    return plsc.VectorSubcoreMesh(core_axis_name="c", subcore_axis_name="s")


def _sc_dispatch(h, pos, p):
    n, d = h.shape
    win = SC_WINDOW
    info = plsc.get_sparse_core_info()
    workers = info.num_cores * info.num_subcores
    wpt = n // (win * workers)
    pos_w = pos.reshape(2, n // win, win).transpose(1, 0, 2)

    @functools.partial(
        pl.kernel, out_type=jax.ShapeDtypeStruct((p, d), h.dtype), mesh=_sc_mesh(),
        scratch_types=[pltpu.VMEM((wpt, 2, win), jnp.int32), pltpu.VMEM((2, win, d), h.dtype),
                       pltpu.SemaphoreType.DMA((2,))],
        name="moe_dispatch")
    def dispatch(h_hbm, pos_hbm, o_hbm, idx_v, rows_v, load_sem):
        wid = lax.axis_index("c") * info.num_subcores + lax.axis_index("s")
        first = wid * wpt
        pltpu.sync_copy(pos_hbm.at[pl.ds(first, wpt)], idx_v)

        def load(j, slot):
            return pltpu.make_async_copy(h_hbm.at[pl.ds((first + j) * win, win)], rows_v.at[slot],
                                         load_sem.at[slot])

        load(0, 0).start()

        @pl.loop(0, wpt, step=2)
        def _(j):
            for slot in range(2):
                jj = j + slot
                load(jj, slot).wait()

                @pl.when(jj + 1 < wpt)
                def _():
                    load(jj + 1, 1 - slot).start()

                pltpu.sync_copy(rows_v.at[slot], o_hbm.at[idx_v.at[jj, 0]])
                pltpu.sync_copy(rows_v.at[slot], o_hbm.at[idx_v.at[jj, 1]])

    return dispatch(h, pos_w)


def _sc_combine(ys, pos):
    n = pos.shape[1]
    d = ys.shape[1]
    win = SC_WINDOW // 2
    info = plsc.get_sparse_core_info()
    workers = info.num_cores * info.num_subcores
    wpt = n // (win * workers)
    pos_w = pos.reshape(2, n // win, win).transpose(1, 0, 2)
    out = jax.ShapeDtypeStruct((n, d), ys.dtype)

    @functools.partial(
        pl.kernel, out_type=(out, out), mesh=_sc_mesh(),
        scratch_types=[pltpu.VMEM((wpt, 2, win), jnp.int32), pltpu.VMEM((2, 2, win, d), ys.dtype),
                       pltpu.SemaphoreType.DMA((2, 2)), pltpu.SemaphoreType.DMA((2,))],
        name="moe_combine")
    def combine(ys_hbm, pos_hbm, a_hbm, b_hbm, idx_v, rows_v, gather_sem, store_sem):
        wid = lax.axis_index("c") * info.num_subcores + lax.axis_index("s")
        first = wid * wpt
        pltpu.sync_copy(pos_hbm.at[pl.ds(first, wpt)], idx_v)
        outs = (a_hbm, b_hbm)

        def gather(j, slot, k):
            return pltpu.make_async_copy(ys_hbm.at[idx_v.at[j, k]], rows_v.at[slot, k], gather_sem.at[slot, k])

        def store(j, slot, k):
            return pltpu.make_async_copy(rows_v.at[slot, k], outs[k].at[pl.ds((first + j) * win, win)],
                                         store_sem.at[k])

        gather(0, 0, 0).start()
        gather(0, 0, 1).start()

        @pl.loop(0, wpt, step=2)
        def _(j):
            for slot in range(2):
                jj = j + slot
                gather(jj, slot, 0).wait()
                gather(jj, slot, 1).wait()

                @pl.when(jj + 1 < wpt)
                def _():
                    gather(jj + 1, 1 - slot, 0).start()
                    gather(jj + 1, 1 - slot, 1).start()

                store(jj, slot, 0).start()
                store(jj, slot, 1).start()
                store(jj, slot, 0).wait()
                store(jj, slot, 1).wait()

    return combine(ys, pos_w)


def _expert_kernel(te_ref, nu_ref, x_ref, wg_ref, wu_ref, wd_ref, o_ref):
    del te_ref
    used = pl.program_id(0) < nu_ref[0]

    @pl.when(used)
    def _():
        x = _unpack_bf16_pairs(x_ref[...]).astype(BF16)
        he = _silu_of_half(_dot(x, wg_ref[...])) * _dot(x, wu_ref[...])
        o_ref[...] = _pack_bf16_pairs(_dot(he.astype(BF16), wd_ref[...]))

    @pl.when(jnp.logical_not(used))
    def _():
        o_ref[...] = jnp.zeros_like(o_ref)


def _experts(xs, tile_expert, n_used, wg, wu, wd, layer):
    p = xs.shape[0]
    tm = MOE_TILE
    grid_spec = pltpu.PrefetchScalarGridSpec(
        num_scalar_prefetch=2,
        grid=(p // tm,),
        in_specs=[
            pl.BlockSpec((tm, D_MODEL // 2), lambda i, te, nu: (i, 0)),
            pl.BlockSpec((None, None, D_MODEL, D_EXPERT), lambda i, te, nu: (layer, te[i], 0, 0)),
            pl.BlockSpec((None, None, D_MODEL, D_EXPERT), lambda i, te, nu: (layer, te[i], 0, 0)),
            pl.BlockSpec((None, None, D_EXPERT, D_MODEL), lambda i, te, nu: (layer, te[i], 0, 0)),
        ],
        out_specs=pl.BlockSpec((tm, D_MODEL // 2), lambda i, te, nu: (i, 0)),
    )
    return pl.pallas_call(
        _expert_kernel,
        grid_spec=grid_spec,
        out_shape=jax.ShapeDtypeStruct((p, D_MODEL // 2), jnp.uint32),
        compiler_params=_params(("arbitrary",)),
        name="moe_experts",
    )(tile_expert, n_used, xs, wg, wu, wd)


def _residual_kernel(ya_ref, yb_ref, w_ref, x1_ref, mod_ref, nfin_ref, o_ref, *, last):
    w = w_ref[...]
    y = w[:, 0:1] * _unpack_bf16_pairs(ya_ref[...]) + w[:, 1:2] * _unpack_bf16_pairs(yb_ref[...])
    x2 = x1_ref[...] + mod_ref[5:6, :] * y
    if last:
        x2 = _rms(x2, nfin_ref[...])
    o_ref[...] = x2


def _residual(ya, yb, w, x1, mod, nfin, seq, last):
    n = x1.shape[0]
    tm = min(1024, seq)
    per_seq = seq // tm
    row = pl.BlockSpec((tm, D_MODEL), lambda i: (i, 0))
    packed = pl.BlockSpec((tm, D_MODEL // 2), lambda i: (i, 0))
    return pl.pallas_call(
        functools.partial(_residual_kernel, last=last),
        grid=(n // tm,),
        in_specs=[packed, packed, pl.BlockSpec((tm, 2), lambda i: (i, 0)), row,
                  pl.BlockSpec((None, N_MOD, D_MODEL), lambda i: (i // per_seq, 0, 0)),
                  pl.BlockSpec((1, D_MODEL), lambda i: (0, 0))],
        out_specs=row,
        out_shape=jax.ShapeDtypeStruct((n, D_MODEL), F32),
        compiler_params=_params(("arbitrary",)),
        name="moe_residual",
    )(ya, yb, w, x1, mod, nfin)


def _moe(h2, ri, rw, cnt, wg, wu, wd, layer, x1, mod, nfin, seq, last):
    n = h2.shape[0]
    tm = MOE_TILE
    p = 2 * n + N_EXPERTS * tm
    counts = cnt[:, 0].astype(jnp.int32)
    padded = (counts + tm - 1) // tm * tm
    ends = jnp.cumsum(padded)
    starts = ends - padded
    base = jnp.zeros_like(ri[0:2])
    for e in range(1, N_EXPERTS):
        base = jnp.where(ri[0:2] == e, starts[e], base)
    pos = base + ri[2:4]
    tile_start = jnp.arange(p // tm, dtype=jnp.int32) * tm
    tile_expert = jnp.minimum(jnp.sum(tile_start[:, None] >= ends[None, :], axis=1), N_EXPERTS - 1)
    n_used = (ends[-1:] // tm).astype(jnp.int32)
    xs = _sc_dispatch(h2, pos, p)
    ys = _experts(xs, tile_expert.astype(jnp.int32), n_used, wg, wu, wd, layer)
    ya, yb = _sc_combine(ys, pos)
    return _residual(ya, yb, rw.T, x1, mod, nfin, seq, last)


def _split_w_in(w):
    hq_hf, hi, hg, att, gates = w[..., :1536], w[..., 1536:2048], w[..., 2048:2560], w[..., 2560:3328], w[..., 3328:]
    wa = (0.5 * jnp.concatenate([hq_hf, hg], axis=-1)).astype(BF16)
    wb = jnp.concatenate([0.5 * gates, hi, att], axis=-1).astype(BF16)
    return wa, wb


def _trunk(x, mod, wts):
    bsz, seq, _ = x.shape
    n = bsz * seq
    depth = wts["w_in_a"].shape[0]
    x2d = x.reshape(n, D_MODEL)
    for l in range(depth):
        mod_l = mod[l]
        pa, pb, span = _inproj(x2d, mod_l, wts["norm_mix"][l:l + 1], wts["w_in_a"][l], wts["w_in_b"][l],
                               wts["hg_lb_fwd"], wts["hg_lb_bwd"], seq, l)
        pa3 = pa.reshape(bsz, seq, A_COLS)
        pb3 = pb.reshape(bsz, seq, B_COLS)
        safe_f = (span[:, 0, 0] <= HG_SAFE_SPAN).astype(jnp.int32)
        safe_b = (span[:, 4, 0] <= HG_SAFE_SPAN).astype(jnp.int32)
        o_f = _hgrn(safe_f, pa3, pb3, wts["hg_lb_fwd"], l, False)
        o_h = _hgrn(safe_b, pa3, pb3, wts["hg_lb_bwd"], l, True, o_f, wts["hg_norm"][l:l + 1])
        o_a = _attn(pb3, wts["bias"], wts["sink"][l])
        x1, h2, ri, rw, cnt = _merge(o_h.reshape(n, HG_WIDTH), o_a.reshape(n, ATT_WIDTH), pb, x2d, mod_l,
                                     wts["norm_ffn"][l:l + 1], wts["w_br_hgrn"][l], wts["w_br_att"][l],
                                     wts["w_out"][l], wts["w_router_t"], wts["router_bias"], seq)
        x2d = _moe(h2, ri, rw, cnt, wts["w_gate"], wts["w_up"], wts["w_down"], l, x1, mod_l,
                   wts["norm_final"], seq, l == depth - 1)
    return x2d.reshape(bsz, seq, D_MODEL)


def kernel(x_prompt, x_sample, c_prompt, c_sample, w_ada, b_ada, norm_mix, norm_ffn, norm_final, w_in, hg_lb_fwd, hg_lb_bwd, hg_norm, att_sink, rel_bias, w_br_hgrn, w_br_att, w_out, w_router, router_bias, w_gate, w_up, w_down):
    depth = w_in.shape[0]
    bp, bs = c_prompt.shape[0], c_sample.shape[0]
    rows = -(-(bp + bs) // 8) * 8
    c_all = jnp.concatenate([c_prompt, c_sample, jnp.zeros((rows - bp - bs, D_MODEL), F32)], axis=0)
    mod = _ada(c_all, w_ada, b_ada).reshape(depth, rows, N_MOD, D_MODEL)
    bias, sink = _attn_tables(rel_bias, att_sink)
    w_in_a, w_in_b = _split_w_in(w_in)
    wts = {
        "norm_mix": norm_mix, "norm_ffn": norm_ffn, "norm_final": norm_final.reshape(1, D_MODEL),
        "w_in_a": w_in_a, "w_in_b": w_in_b,
        "hg_lb_fwd": hg_lb_fwd, "hg_lb_bwd": hg_lb_bwd, "hg_norm": hg_norm,
        "sink": sink, "bias": bias,
        "w_br_hgrn": w_br_hgrn.astype(BF16), "w_br_att": w_br_att.astype(BF16), "w_out": (0.5 * w_out).astype(BF16),
        "w_router_t": w_router.T,
        "router_bias": jnp.broadcast_to(router_bias[:, None], (N_EXPERTS, 128)),
        "w_gate": (0.5 * w_gate).astype(BF16), "w_up": w_up.astype(BF16), "w_down": w_down.astype(BF16),
    }
    y_prompt = _trunk(x_prompt, mod[:, :bp], wts)
    y_sample = _trunk(x_sample, mod[:, bp:bp + bs], wts)
    return (y_prompt, y_sample)
```

```python
import functools

import numpy as np
import jax
import jax.numpy as jnp
from jax import lax
from jax.experimental import pallas as pl
from jax.experimental.pallas import tpu as pltpu
from jax.experimental.pallas import tpu_sc as plsc

D_MODEL = 1024
HG_DK = 128
HG_WIDTH = 512
HG_HEADS = 4
HG_SUB = 64
HG_LEVELS = 6
HG_GROUP = 32
HG_SAFE_SPAN = 80.0
HG_BLOCK = 512
ATT_HD = 64
ATT_HEADS = 8
ATT_KV = 2
ATT_GROUP = 4
ATT_WIDTH = 512
KV_WIDTH = 128
WINDOW = 128
ATT_BLOCK = 128
N_BUCKETS = 32
REL_MAX_DIST = 128
N_EXPERTS = 16
N_GROUPS = 4
EXPERTS_PER_GROUP = 4
D_EXPERT = 512
N_MOD = 6
IN_COLS = 5376
EPS = 1e-6
NEG_BIG = -1e30
TINY = 1e-30

A_COLS = 2048
A_HQ = 0
A_HF_FWD = 512
A_HF_BWD = 1024
A_HG = 1536
B_COLS = 3328
B_GATE_H = 0
B_GATE_A = 1024
B_HI = 2048
B_AQ = 2560
B_AK = 3072
B_AV = 3200

V7X_VMEM_LIMIT = 56 * 1024 * 1024
LANES = 128
SUBLANES = 8

F32 = jnp.float32
BF16 = jnp.bfloat16
HIGHEST = lax.Precision.HIGHEST


def _params(sem):
    return pltpu.CompilerParams(dimension_semantics=sem, vmem_limit_bytes=V7X_VMEM_LIMIT)


def _dot(a, b):
    return jnp.dot(a, b, preferred_element_type=F32)


def _dot_nt(a, b):
    return lax.dot_general(a, b, (((1,), (1,)), ((), ())), preferred_element_type=F32)


def _dot_tn(a, b):
    return lax.dot_general(a, b, (((0,), (0,)), ((), ())), preferred_element_type=F32)


def _sigmoid(x):
    return 0.5 * jnp.tanh(0.5 * x) + 0.5


def _pack_bf16_pairs(x):
    c = x.shape[1] // 2
    bits = lax.bitcast_convert_type(x.astype(BF16).astype(F32), jnp.uint32)
    return (bits[:, :c] >> 16) | (bits[:, c:] & jnp.uint32(0xFFFF0000))


def _unpack_bf16_pairs(u):
    lo = lax.bitcast_convert_type(u << 16, F32)
    hi = lax.bitcast_convert_type(u & jnp.uint32(0xFFFF0000), F32)
    return jnp.concatenate([lo, hi], axis=1)


def _silu(x):
    return x * _sigmoid(x)


def _silu_of_half(xh):
    return xh * (jnp.tanh(xh) + 1.0)


def _rms(x, g):
    return x * lax.rsqrt(jnp.mean(x * x, axis=-1, keepdims=True) + EPS) * g


def _ada_kernel(c_ref, w_ref, b_ref, o_ref):
    c = c_ref[...]
    o_ref[...] = jnp.dot(_silu(c), w_ref[...], precision=HIGHEST, preferred_element_type=F32) + b_ref[...]


def _ada(c_all, w_ada, b_ada):
    depth = w_ada.shape[0]
    rows = c_all.shape[0]
    ncol = w_ada.shape[2]
    tn = 1024
    return pl.pallas_call(
        _ada_kernel,
        grid=(depth, ncol // tn),
        in_specs=[
            pl.BlockSpec((rows, D_MODEL), lambda l, j: (0, 0)),
            pl.BlockSpec((None, D_MODEL, tn), lambda l, j: (l, 0, j)),
            pl.BlockSpec((None, 1, tn), lambda l, j: (l, 0, j)),
        ],
        out_specs=pl.BlockSpec((None, rows, tn), lambda l, j: (l, 0, j)),
        out_shape=jax.ShapeDtypeStruct((depth, rows, ncol), F32),
        compiler_params=_params(("arbitrary", "arbitrary")),
        name="ada_mod",
    )(c_all, w_ada, b_ada.reshape(depth, 1, ncol))


def _lower_bound_row(gam_ref, layer):
    rows = [gam_ref[d:d + 1, :] for d in range(gam_ref.shape[0])]
    m = functools.reduce(jnp.maximum, rows)
    es = [jnp.exp(r - m) for r in rows]
    tot = functools.reduce(lambda a, b: a + b, es)
    ps = [e / tot for e in es]
    cum = ps[0]
    for d in range(1, layer + 1):
        cum = cum + ps[d]
    return jnp.clip(cum - ps[0], 0.0, 1.0)


def _forget(zh, lb):
    return 0.5 * (1.0 + lb) + (0.5 * (1.0 - lb)) * jnp.tanh(zh)


def _inproj_kernel(x_ref, mod_ref, g_ref, wa_ref, wb_ref, gf_ref, gb_ref, oa_ref, ob_ref, span_ref, *, layer):
    x = x_ref[...]
    h = (_rms(x, g_ref[...]) * (1.0 + mod_ref[1:2, :]) + mod_ref[0:1, :]).astype(BF16)
    oa = _dot(h, wa_ref[...])
    oa_ref[...] = oa
    ob_ref[...] = _dot(h, wb_ref[...]).astype(BF16)
    spans = []
    for gam_ref, col in ((gf_ref, A_HF_FWD), (gb_ref, A_HF_BWD)):
        f = _forget(oa[:, col:col + HG_WIDTH], _lower_bound_row(gam_ref, layer))
        g = jnp.log(jnp.maximum(f, TINY))
        gsum = jnp.sum(g.reshape(g.shape[0] // HG_GROUP, HG_GROUP, HG_WIDTH), axis=1)
        spans.append(jnp.max(jnp.max(-gsum, axis=0, keepdims=True), axis=1, keepdims=True))
    half = lax.broadcasted_iota(jnp.int32, span_ref.shape, 0) < span_ref.shape[0] // 2
    span_ref[...] = jnp.where(half, spans[0], spans[1])


def _inproj(x2d, mod, g, wa, wb, gam_f, gam_b, seq, layer):
    n = x2d.shape[0]
    tm = min(HG_BLOCK, seq)
    per_seq = seq // tm
    const = lambda i: (0, 0)
    return pl.pallas_call(
        functools.partial(_inproj_kernel, layer=layer),
        grid=(n // tm,),
        in_specs=[
            pl.BlockSpec((tm, D_MODEL), lambda i: (i, 0)),
            pl.BlockSpec((None, N_MOD, D_MODEL), lambda i: (i // per_seq, 0, 0)),
            pl.BlockSpec((1, D_MODEL), const),
            pl.BlockSpec((D_MODEL, A_COLS), const, pipeline_mode=pl.Buffered(1)),
            pl.BlockSpec((D_MODEL, B_COLS), const, pipeline_mode=pl.Buffered(1)),
            pl.BlockSpec(gam_f.shape, const),
            pl.BlockSpec(gam_b.shape, const),
        ],
        out_specs=[pl.BlockSpec((tm, A_COLS), lambda i: (i, 0)),
                   pl.BlockSpec((tm, B_COLS), lambda i: (i, 0)),
                   pl.BlockSpec((None, SUBLANES, LANES), lambda i: (i, 0, 0))],
        out_shape=[jax.ShapeDtypeStruct((n, A_COLS), F32), jax.ShapeDtypeStruct((n, B_COLS), BF16),
                   jax.ShapeDtypeStruct((n // tm, SUBLANES, LANES), F32)],
        compiler_params=_params(("arbitrary",)),
        name="inproj",
    )(x2d, mod, g, wa, wb, gam_f, gam_b)


def _hgrn_level_tables(rev):
    c = HG_SUB
    r = lax.broadcasted_iota(jnp.int32, (c, c), 0)
    s = lax.broadcasted_iota(jnp.int32, (c, c), 1)
    row = lax.broadcasted_iota(jnp.int32, (c, HG_WIDTH), 0)
    sels, qsides, pairs = [], [], []
    for lev in range(HG_LEVELS):
        half = 1 << lev
        blk = 2 * half
        r_up = (r & (blk - 1)) >= half
        s_up = (s & (blk - 1)) >= half
        base = r - (r & (blk - 1))
        mrow = base + (half if rev else half - 1)
        sels.append(jnp.where(s == mrow, 1.0, 0.0).astype(F32))
        row_up = (row & (blk - 1)) >= half
        qsides.append(~row_up if rev else row_up)
        same = (r >> (lev + 1)) == (s >> (lev + 1))
        pairs.append(same & ((~r_up & s_up) if rev else (r_up & ~s_up)))
    return jnp.concatenate(sels, axis=0), qsides, pairs, r == s


def _hgrn_kernel(safe_ref, *refs, layer, rev):
    if rev:
        q_ref, f_ref, v_ref, gam_ref, hg_ref, of_ref, nrm_ref, o_ref = refs[:8]
    else:
        q_ref, f_ref, v_ref, gam_ref, o_ref = refs[:5]
    st_ref, qs_ref, ks_ref, bs_ref, a_ref, qi_ref, dec_ref, up_ref, sb_ref = refs[-9:]

    @pl.when(pl.program_id(1) == 0)
    def _():
        st_ref[...] = jnp.zeros_like(st_ref)

    nc = pl.num_programs(1)
    chunk = (nc - 1 - pl.program_id(1)) if rev else pl.program_id(1)
    safe = safe_ref[pl.program_id(0) * nc + chunk] != 0
    c = HG_SUB
    w = HG_WIDTH
    nsub = q_ref.shape[0] // c
    lb = _lower_bound_row(gam_ref, layer)
    r_i = lax.broadcasted_iota(jnp.int32, (c, c), 0)
    s_i = lax.broadcasted_iota(jnp.int32, (c, c), 1)
    causal = (s_i >= r_i) if rev else (s_i <= r_i)
    tri = jnp.where(causal, 1.0, 0.0).astype(BF16)
    row = lax.broadcasted_iota(jnp.int32, (c, w), 0)
    far = (row < HG_GROUP) if rev else (row >= HG_GROUP)
    ref_row = HG_GROUP if rev else HG_GROUP - 1
    last_row = 0 if rev else c - 1
    heads = [slice(h * HG_DK, (h + 1) * HG_DK) for h in range(HG_HEADS)]

    def rows_of(i):
        ci = (nsub - 1 - i) if rev else i
        return pl.ds(pl.multiple_of(ci * c, c), c)

    def gates_pass(i, carry):
        sl = rows_of(i)
        zq = q_ref[sl, :]
        f = _forget(f_ref[sl, :], lb)
        g = jnp.log(jnp.maximum(f, TINY))
        qs_ref[sl, :] = _silu_of_half(zq) * (HG_DK ** -0.5)
        ks_ref[sl, :] = 1.0 - f
        g1 = g.astype(BF16)
        r1 = g - g1.astype(F32)
        g2 = r1.astype(BF16)
        g3 = (r1 - g2.astype(F32)).astype(BF16)
        bb = _dot(tri, jnp.concatenate([g1, g2, g3], axis=1))
        bs_ref[sl, :] = (bb[:, :w] + bb[:, w:2 * w]) + bb[:, 2 * w:]
        return carry

    lax.fori_loop(0, nsub, gates_pass, 0, unroll=4)

    def gates(i):
        sl = rows_of(i)
        return sl, qs_ref[sl, :], ks_ref[sl, :], bs_ref[sl, :]

    def stage(i, sl, a, q_in, k_dec, dec):
        vb = v_ref[sl, :].astype(BF16)
        for h, hs in enumerate(heads):
            a_ref[i, h] = a[h].astype(BF16)
            up_ref[i, h] = _dot_tn(vb[:, hs], k_dec[:, hs])
        qi_ref[sl, :] = q_in
        dec_ref[pl.ds(pl.multiple_of(i * SUBLANES, SUBLANES), SUBLANES), :] = jnp.broadcast_to(dec, (SUBLANES, w))

    def scan_states():
        for h, hs in enumerate(heads):
            st = st_ref[h]
            for i in range(nsub):
                sb_ref[i, h] = st.astype(BF16)
                st = st * dec_ref[i * SUBLANES:i * SUBLANES + 1, hs] + up_ref[i, h]
            st_ref[h] = st

    def finish_pass(i, carry):
        sl = rows_of(i)
        vb = v_ref[sl, :].astype(BF16)
        q_in = qi_ref[sl, :]
        outs = [_dot_nt(q_in[:, hs], sb_ref[i, h]) + _dot(a_ref[i, h], vb[:, hs]) for h, hs in enumerate(heads)]
        o_all = jnp.concatenate(outs, axis=1)
        if rev:
            tot = of_ref[sl, :] + o_all
            nrm = nrm_ref[...]
            ys = [_rms(tot[:, hs], nrm) for hs in heads]
            o_ref[sl, :] = (jnp.concatenate(ys, axis=1) * _silu_of_half(hg_ref[sl, :])).astype(o_ref.dtype)
        else:
            o_ref[sl, :] = o_all
        return carry

    def factored(i, carry):
        sl, q, k, b = gates(i)
        r = b[ref_row:ref_row + 1, :]
        bl = b[last_row:last_row + 1, :]
        rg = jnp.where(far, r, 0.0)
        qt = q * jnp.exp(b - rg)
        kt = k * jnp.exp(rg - b)
        er = jnp.exp(r)
        qn = jnp.where(far, 0.0, qt).astype(BF16)
        qf = jnp.where(far, qt, 0.0).astype(BF16)
        kc = jnp.where(far, kt, kt * er).astype(BF16)
        ktb = kt.astype(BF16)
        q_in = jnp.where(far, qt * er, qt).astype(BF16)
        k_dec = (kt * jnp.where(far, jnp.exp(bl - r), jnp.exp(bl))).astype(BF16)
        a = []
        for hs in heads:
            lhs = jnp.concatenate([qn[:, hs], qf[:, hs]], axis=1)
            rhs = jnp.concatenate([ktb[:, hs], kc[:, hs]], axis=1)
            a.append(jnp.where(causal, _dot_nt(lhs, rhs), 0.0))
        stage(i, sl, a, q_in, k_dec, jnp.exp(bl))
        return carry

    def levels(i, carry):
        sl, q, k, b = gates(i)
        sel_all, qsides, pairs, eye = _hgrn_level_tables(rev)
        bl = b[last_row:last_row + 1, :]
        q_in = (q * jnp.exp(b)).astype(BF16)
        k_dec = (k * jnp.exp(bl - b)).astype(BF16)
        bref_all = jnp.dot(sel_all, b, precision=HIGHEST, preferred_element_type=F32)
        qb = q.astype(BF16)
        kb = k.astype(BF16)
        a = [jnp.where(eye, _dot_nt(qb[:, hs], kb[:, hs]), 0.0) for hs in heads]
        for lev in range(HG_LEVELS):
            bref = bref_all[lev * c:(lev + 1) * c, :]
            qs = qsides[lev]
            x = jnp.exp(jnp.where(qs, b - bref, bref - b))
            ql = jnp.where(qs, q * x, 0.0).astype(BF16)
            kl = jnp.where(qs, 0.0, k * x).astype(BF16)
            for h, hs in enumerate(heads):
                a[h] = a[h] + jnp.where(pairs[lev], _dot_nt(ql[:, hs], kl[:, hs]), 0.0)
        stage(i, sl, a, q_in, k_dec, jnp.exp(bl))
        return carry

    @pl.when(safe)
    def _():
        lax.fori_loop(0, nsub, factored, 0, unroll=4)

    @pl.when(jnp.logical_not(safe))
    def _():
        lax.fori_loop(0, nsub, levels, 0)

    scan_states()
    lax.fori_loop(0, nsub, finish_pass, 0, unroll=4)


def _hgrn(safe, pa3, pb3, gamma, layer, rev, o_fwd=None, nrm=None):
    bsz, seq, _ = pa3.shape
    cb = min(HG_BLOCK, seq)
    assert seq % cb == 0 and cb % HG_SUB == 0
    nc = seq // cb
    wblk = HG_WIDTH

    def cmap(col):
        if rev:
            return lambda b, c, safe_ref: (b, nc - 1 - c, col // wblk)
        return lambda b, c, safe_ref: (b, c, col // wblk)

    const = lambda b, c, safe_ref: (0, 0)
    in_specs = [
        pl.BlockSpec((None, cb, wblk), cmap(A_HQ)),
        pl.BlockSpec((None, cb, wblk), cmap(A_HF_BWD if rev else A_HF_FWD)),
        pl.BlockSpec((None, cb, wblk), cmap(B_HI)),
        pl.BlockSpec(gamma.shape, const),
    ]
    args = [pa3, pa3, pb3, gamma]
    if rev:
        in_specs += [
            pl.BlockSpec((None, cb, wblk), cmap(A_HG)),
            pl.BlockSpec((None, cb, wblk), cmap(0)),
            pl.BlockSpec((1, HG_DK), const),
        ]
        args += [pa3, o_fwd, nrm]
    out_dtype = BF16 if rev else F32
    grid_spec = pltpu.PrefetchScalarGridSpec(
        num_scalar_prefetch=1,
        grid=(bsz, nc),
        in_specs=in_specs,
        out_specs=pl.BlockSpec((None, cb, wblk), cmap(0)),
        scratch_shapes=[pltpu.VMEM((HG_HEADS, HG_DK, HG_DK), F32)] + [pltpu.VMEM((cb, wblk), F32)] * 3
        + [pltpu.VMEM((cb // HG_SUB, HG_HEADS, HG_SUB, HG_SUB), BF16), pltpu.VMEM((cb, wblk), BF16),
           pltpu.VMEM((cb // HG_SUB * SUBLANES, wblk), F32),
           pltpu.VMEM((cb // HG_SUB, HG_HEADS, HG_DK, HG_DK), F32),
           pltpu.VMEM((cb // HG_SUB, HG_HEADS, HG_DK, HG_DK), BF16)],
    )
    return pl.pallas_call(
        functools.partial(_hgrn_kernel, layer=layer, rev=rev),
        grid_spec=grid_spec,
        out_shape=jax.ShapeDtypeStruct((bsz, seq, HG_WIDTH), out_dtype),
        compiler_params=_params(("arbitrary", "arbitrary")),
        name="hgrn_bwd" if rev else "hgrn_fwd",
    )(safe, *args)


ATT_ROW_HEADS = (0, 2, 1, 3)


def _t5_buckets(rel):
    half = N_BUCKETS // 2
    ret = np.where(rel > 0, half, 0)
    n = np.abs(rel)
    max_exact = half // 2
    large = max_exact + (np.log(np.maximum(n, 1) / max_exact)
                         / np.log(REL_MAX_DIST / max_exact) * (half - max_exact)).astype(np.int32)
    large = np.minimum(large, half - 1)
    return (ret + np.where(n < max_exact, n, large)).astype(np.int32)


def _attn_tables(rel_bias, att_sink):
    c = ATT_BLOCK
    rel = np.arange(3 * c)[None, :] - c - np.arange(c)[:, None]
    onehot = np.equal(_t5_buckets(rel).reshape(-1, 1), np.arange(N_BUCKETS)).astype(np.float32)
    bias = jnp.dot(jnp.asarray(onehot, BF16), rel_bias.astype(F32), precision=HIGHEST)
    bias = bias.reshape(c, 3 * c, ATT_HEADS).transpose(2, 0, 1)
    bias = jnp.where(jnp.asarray(np.abs(rel) <= WINDOW)[None], bias, NEG_BIG)
    col = np.arange(3 * c)
    cases = []
    for case in range(4):
        valid = np.ones(3 * c, bool)
        if case & 1:
            valid &= col >= c
        if case & 2:
            valid &= col < 2 * c
        cases.append(jnp.where(jnp.asarray(valid)[None, None, :], bias, NEG_BIG))
    tab = jnp.stack(cases)
    order = np.array([[ATT_GROUP * g + h for h in ATT_ROW_HEADS] for g in range(ATT_KV)])
    tab = tab[:, order].reshape(4, ATT_KV, ATT_GROUP * c, 3 * c)
    sink = att_sink.astype(F32)[:, order]
    sink = jnp.broadcast_to(sink[..., None, None], sink.shape + (c, LANES))
    return tab, sink.reshape(att_sink.shape[0], ATT_KV, ATT_GROUP * c, LANES)


def _attn_kernel(q_ref, kp_ref, kc_ref, kn_ref, vp_ref, vc_ref, vn_ref, bias_ref, sink_ref, o_ref):
    n = pl.program_id(1)
    nsteps = pl.num_programs(1)
    c = ATT_BLOCK
    nsub = q_ref.shape[0] // c
    pair = 2 * ATT_HD
    kwin = jnp.concatenate([kp_ref[...], kc_ref[...], kn_ref[...]], axis=0).astype(F32)
    vwin = jnp.concatenate([vp_ref[...], vc_ref[...], vn_ref[...]], axis=0).astype(F32)
    kroll = pltpu.roll(kwin, ATT_HD, axis=1)
    vroll = pltpu.roll(vwin, ATT_HD, axis=1)
    lo = lax.broadcasted_iota(jnp.int32, kwin.shape, 1) < ATT_HD
    k_lo = [jnp.where(lo, kwin, 0.0).astype(BF16), jnp.where(lo, kroll, 0.0).astype(BF16)]
    k_hi = [jnp.where(lo, 0.0, kroll).astype(BF16), jnp.where(lo, 0.0, kwin).astype(BF16)]
    v_lo = [jnp.where(lo, vwin, 1.0).astype(BF16), jnp.where(lo, vroll, 1.0).astype(BF16)]
    v_hi = [jnp.where(lo, 1.0, vroll).astype(BF16), jnp.where(lo, 1.0, vwin).astype(BF16)]
    lo_out = lax.broadcasted_iota(jnp.int32, (2 * c, pair), 1) < ATT_HD
    for j in range(nsub):
        case = jnp.int32(0)
        if j == 0:
            case = case + (n == 0).astype(jnp.int32)
        if j == nsub - 1:
            case = case + 2 * (n == nsteps - 1).astype(jnp.int32)
        band = slice(j * c, (j + 3) * c)
        qrows = slice(j * c, (j + 1) * c)
        for g in range(ATT_KV):
            qg = q_ref[qrows, g * 2 * pair:(g + 1) * 2 * pair].astype(F32) * (ATT_HD ** -0.5)
            lhs = jnp.concatenate([qg[:, :pair], qg[:, pair:]], axis=0).astype(BF16)
            rhs = jnp.concatenate([k_lo[g][band], k_hi[g][band]], axis=0)
            lg = _dot_nt(lhs, rhs)
            logits = jnp.concatenate([lg[:, :3 * c], lg[:, 3 * c:]], axis=0) + bias_ref[case, g]
            s = sink_ref[g][:, 0:1]
            m = jnp.maximum(jnp.max(logits, axis=-1, keepdims=True), s)
            p = jnp.exp(logits - m).astype(BF16)
            es = jnp.exp(s - m)
            o_even = _dot(p[:2 * c], v_lo[g][band])
            o_odd = _dot(p[2 * c:], v_hi[g][band])
            o = jnp.where(lo_out,
                          o_even / (pltpu.roll(o_even, ATT_HD, axis=1) + es[:2 * c]),
                          o_odd / (pltpu.roll(o_odd, ATT_HD, axis=1) + es[2 * c:]))
            o_ref[qrows, g * 2 * pair:g * 2 * pair + pair] = o[:c].astype(o_ref.dtype)
            o_ref[qrows, g * 2 * pair + pair:(g + 1) * 2 * pair] = o[c:].astype(o_ref.dtype)


def _attn(proj3, bias, sink):
    bsz, seq, _ = proj3.shape
    c = ATT_BLOCK
    nsub = max(k for k in (4, 2, 1) if seq % (k * c) == 0)
    qb = nsub * c
    nsteps = seq // qb
    assert seq % c == 0
    nb = seq // c
    kcol = B_AK // KV_WIDTH
    vcol = B_AV // KV_WIDTH

    def band(col):
        return [
            pl.BlockSpec((None, c, KV_WIDTH), lambda b, n: (b, jnp.maximum(n * nsub - 1, 0), col)),
            pl.BlockSpec((None, qb, KV_WIDTH), lambda b, n: (b, n, col)),
            pl.BlockSpec((None, c, KV_WIDTH), lambda b, n: (b, jnp.minimum((n + 1) * nsub, nb - 1), col)),
        ]

    return pl.pallas_call(
        _attn_kernel,
        grid=(bsz, nsteps),
        in_specs=[pl.BlockSpec((None, qb, ATT_WIDTH), lambda b, n: (b, n, B_AQ // ATT_WIDTH))]
        + band(kcol) + band(vcol)
        + [pl.BlockSpec(bias.shape, lambda b, n: (0, 0, 0, 0)),
           pl.BlockSpec(sink.shape, lambda b, n: (0, 0, 0))],
        out_specs=pl.BlockSpec((None, qb, ATT_WIDTH), lambda b, n: (b, n, 0)),
        out_shape=jax.ShapeDtypeStruct((bsz, seq, ATT_WIDTH), BF16),
        compiler_params=_params(("arbitrary", "arbitrary")),
        name="window_attn",
    )(proj3, proj3, proj3, proj3, proj3, proj3, proj3, bias, sink)


def _first_argmax(vals):
    best, idx = vals[0], jnp.zeros(vals[0].shape, jnp.int32)
    for j in range(1, len(vals)):
        upd = vals[j] > best
        idx = jnp.where(upd, j, idx)
        best = jnp.where(upd, vals[j], best)
    return best, idx


def _select(vals, idx):
    out = vals[0]
    for j in range(1, len(vals)):
        out = jnp.where(idx == j, vals[j], out)
    return out


def _route(logits_t, rbias):
    m = jnp.max(logits_t, axis=0, keepdims=True)
    e = jnp.exp(logits_t - m)
    scores = e / jnp.sum(e, axis=0, keepdims=True)
    sel = scores + rbias
    srow = [scores[i:i + 1, :] for i in range(N_EXPERTS)]
    lrow = [sel[i:i + 1, :] for i in range(N_EXPERTS)]
    gscore = []
    for g in range(N_GROUPS):
        a, b, c, d = lrow[4 * g:4 * g + 4]
        hi1, lo1 = jnp.maximum(a, b), jnp.minimum(a, b)
        hi2, lo2 = jnp.maximum(c, d), jnp.minimum(c, d)
        gscore.append(jnp.maximum(hi1, hi2) + jnp.maximum(jnp.minimum(hi1, hi2), jnp.maximum(lo1, lo2)))
    _, gi = _first_argmax(gscore)
    ing = [_select([lrow[4 * g + j] for g in range(N_GROUPS)], gi) for j in range(EXPERTS_PER_GROUP)]
    sg = [_select([srow[4 * g + j] for g in range(N_GROUPS)], gi) for j in range(EXPERTS_PER_GROUP)]
    _, i1 = _first_argmax(ing)
    rest = [jnp.where(i1 == j, -jnp.inf, ing[j]) for j in range(EXPERTS_PER_GROUP)]
    _, i2 = _first_argmax(rest)
    s1, s2 = _select(sg, i1), _select(sg, i2)
    tot = s1 + s2
    w1, w2 = s1 / tot, s2 / tot
    idx1 = gi * EXPERTS_PER_GROUP + i1
    idx2 = gi * EXPERTS_PER_GROUP + i2
    return idx1, idx2, w1, w2


def _merge_kernel(oh_ref, oa_ref, gh_ref, ga_ref, x_ref, mod_ref, nf_ref, wbh_ref, wba_ref, wo_ref,
                  wrt_ref, rb_ref, before_ref, x1_ref, h2_ref, ri_ref, rw_ref, cnt_ref, carry_ref):
    @pl.when(pl.program_id(0) == 0)
    def _():
        carry_ref[...] = jnp.zeros_like(carry_ref)

    mh = _dot(oh_ref[...], wbh_ref[...])
    ma = _dot(oa_ref[...], wba_ref[...])
    merged = ((jnp.tanh(gh_ref[...].astype(F32)) + 1.0) * mh + (jnp.tanh(ga_ref[...].astype(F32)) + 1.0) * ma)
    out = _dot(merged.astype(BF16), wo_ref[...])
    x1 = x_ref[...] + mod_ref[2:3, :] * out
    x1_ref[...] = x1
    h2 = _rms(x1, nf_ref[...]) * (1.0 + mod_ref[4:5, :]) + mod_ref[3:4, :]
    h2_ref[...] = _pack_bf16_pairs(h2)
    wr = wrt_ref[...]
    w_hi = wr.astype(BF16)
    w_lo = (wr - w_hi.astype(F32)).astype(BF16)
    h_hi = h2.astype(BF16)
    h_lo = (h2 - h_hi.astype(F32)).astype(BF16)
    logits_t = (_dot_nt(w_hi, h_hi) + _dot_nt(w_hi, h_lo)) + _dot_nt(w_lo, h_hi)
    idx1, idx2, w1, w2 = _route(logits_t, rb_ref[:, 0:1])
    erow = lax.broadcasted_iota(jnp.int32, logits_t.shape, 0)
    oh1 = erow == idx1
    oh2 = erow == idx2
    oh = jnp.where(oh1 | oh2, 1.0, 0.0)
    carry = carry_ref[...]
    pref = _dot(oh.astype(BF16), before_ref[...]) + carry[:, 0:1]
    rank1 = jnp.sum(jnp.where(oh1, pref, 0.0), axis=0, keepdims=True)
    rank2 = jnp.sum(jnp.where(oh2, pref, 0.0), axis=0, keepdims=True)
    carry = carry + jnp.sum(oh, axis=1, keepdims=True)
    carry_ref[...] = carry
    cnt_ref[...] = carry
    ri_ref[...] = jnp.concatenate([idx1, idx2, rank1.astype(jnp.int32), rank2.astype(jnp.int32)], axis=0)
    rw_ref[...] = jnp.concatenate([w1, w2], axis=0)


def _merge(o_h, o_a, proj, x2d, mod, nf, wbh, wba, wo, wrt, rb, seq):
    n = x2d.shape[0]
    tm = min(1024, seq)
    per_seq = seq // tm
    const = lambda i: (0, 0)
    before = jnp.asarray(np.arange(tm)[:, None] < np.arange(tm)[None, :], BF16)
    return pl.pallas_call(
        _merge_kernel,
        grid=(n // tm,),
        in_specs=[
            pl.BlockSpec((tm, HG_WIDTH), lambda i: (i, 0)),
            pl.BlockSpec((tm, ATT_WIDTH), lambda i: (i, 0)),
            pl.BlockSpec((tm, D_MODEL), lambda i: (i, B_GATE_H // D_MODEL)),
            pl.BlockSpec((tm, D_MODEL), lambda i: (i, B_GATE_A // D_MODEL)),
            pl.BlockSpec((tm, D_MODEL), lambda i: (i, 0)),
            pl.BlockSpec((None, N_MOD, D_MODEL), lambda i: (i // per_seq, 0, 0)),
            pl.BlockSpec((1, D_MODEL), const),
            pl.BlockSpec(wbh.shape, const),
            pl.BlockSpec(wba.shape, const),
            pl.BlockSpec(wo.shape, const),
            pl.BlockSpec(wrt.shape, const),
            pl.BlockSpec(rb.shape, const),
            pl.BlockSpec(before.shape, const),
        ],
        out_specs=[
            pl.BlockSpec((tm, D_MODEL), lambda i: (i, 0)),
            pl.BlockSpec((tm, D_MODEL // 2), lambda i: (i, 0)),
            pl.BlockSpec((4, tm), lambda i: (0, i)),
            pl.BlockSpec((2, tm), lambda i: (0, i)),
            pl.BlockSpec((N_EXPERTS, LANES), const),
        ],
        out_shape=[
            jax.ShapeDtypeStruct((n, D_MODEL), F32),
            jax.ShapeDtypeStruct((n, D_MODEL // 2), jnp.uint32),
            jax.ShapeDtypeStruct((4, n), jnp.int32),
            jax.ShapeDtypeStruct((2, n), F32),
            jax.ShapeDtypeStruct((N_EXPERTS, LANES), F32),
        ],
        scratch_shapes=[pltpu.VMEM((N_EXPERTS, LANES), F32)],
        compiler_params=_params(("arbitrary",)),
        name="merge_router",
    )(o_h, o_a, proj, proj, x2d, mod, nf, wbh, wba, wo, wrt, rb, before)


MOE_TILE = 512
SC_WINDOW = 64


def _sc_mesh():
    return plsc.VectorSubcoreMesh(core_axis_name="c", subcore_axis_name="s")


def _sc_dispatch(h, pos, p):
    n, d = h.shape
    win = SC_WINDOW
    info = plsc.get_sparse_core_info()
    workers = info.num_cores * info.num_subcores
    assert n % (2 * win * workers) == 0, "each vector subcore walks its windows two at a time"
    wpt = n // (win * workers)
    pos_w = pos.reshape(2, n // win, win).transpose(1, 0, 2)

    @functools.partial(
        pl.kernel, out_type=jax.ShapeDtypeStruct((p, d), h.dtype), mesh=_sc_mesh(),
        scratch_types=[pltpu.VMEM((wpt, 2, win), jnp.int32), pltpu.VMEM((2, win, d), h.dtype),
                       pltpu.SemaphoreType.DMA((2,))],
        name="moe_dispatch")
    def dispatch(h_hbm, pos_hbm, o_hbm, idx_v, rows_v, load_sem):
        wid = lax.axis_index("c") * info.num_subcores + lax.axis_index("s")
        first = wid * wpt
        pltpu.sync_copy(pos_hbm.at[pl.ds(first, wpt)], idx_v)

        def load(j, slot):
            return pltpu.make_async_copy(h_hbm.at[pl.ds((first + j) * win, win)], rows_v.at[slot],
                                         load_sem.at[slot])

        load(0, 0).start()

        @pl.loop(0, wpt, step=2)
        def _(j):
            for slot in range(2):
                jj = j + slot
                load(jj, slot).wait()

                @pl.when(jj + 1 < wpt)
                def _():
                    load(jj + 1, 1 - slot).start()

                pltpu.sync_copy(rows_v.at[slot], o_hbm.at[idx_v.at[jj, 0]])
                pltpu.sync_copy(rows_v.at[slot], o_hbm.at[idx_v.at[jj, 1]])

    return dispatch(h, pos_w)


def _sc_combine(ys, pos):
    n = pos.shape[1]
    d = ys.shape[1]
    win = SC_WINDOW // 2
    info = plsc.get_sparse_core_info()
    workers = info.num_cores * info.num_subcores
    assert n % (2 * win * workers) == 0, "each vector subcore walks its windows two at a time"
    wpt = n // (win * workers)
    pos_w = pos.reshape(2, n // win, win).transpose(1, 0, 2)
    out =jax.ShapeDtypeStruct((n, d), ys.dtype)

    @functools.partial(
        pl.kernel, out_type=(out, out), mesh=_sc_mesh(),
        scratch_types=[pltpu.VMEM((wpt, 2, win), jnp.int32), pltpu.VMEM((2, 2, win, d), ys.dtype),
                       pltpu.SemaphoreType.DMA((2, 2)), pltpu.SemaphoreType.DMA((2,))],
        name="moe_combine")
    def combine(ys_hbm, pos_hbm, a_hbm, b_hbm, idx_v, rows_v, gather_sem, store_sem):
        wid = lax.axis_index("c") * info.num_subcores + lax.axis_index("s")
        first = wid * wpt
        pltpu.sync_copy(pos_hbm.at[pl.ds(first, wpt)], idx_v)
        outs = (a_hbm, b_hbm)

        def gather(j, slot, k):
            return pltpu.make_async_copy(ys_hbm.at[idx_v.at[j, k]], rows_v.at[slot, k], gather_sem.at[slot, k])

        def store(j, slot, k):
            return pltpu.make_async_copy(rows_v.at[slot, k], outs[k].at[pl.ds((first + j) * win, win)],
                                         store_sem.at[k])

        gather(0, 0, 0).start()
        gather(0, 0, 1).start()

        @pl.loop(0, wpt, step=2)
        def _(j):
            for slot in range(2):
                jj = j + slot
                gather(jj, slot, 0).wait()
                gather(jj, slot, 1).wait()

                @pl.when(jj + 1 < wpt)
                def _():
                    gather(jj + 1, 1 - slot, 0).start()
                    gather(jj + 1, 1 - slot, 1).start()

                store(jj, slot, 0).start()
                store(jj, slot, 1).start()
                store(jj, slot, 0).wait()
                store(jj, slot, 1).wait()

    return combine(ys, pos_w)


def _expert_kernel(te_ref, nu_ref, x_ref, wg_ref, wu_ref, wd_ref, o_ref):
    del te_ref
    used = pl.program_id(0) < nu_ref[0]

    @pl.when(used)
    def _():
        x = _unpack_bf16_pairs(x_ref[...]).astype(BF16)
        he = _silu_of_half(_dot(x, wg_ref[...])) * _dot(x, wu_ref[...])
        o_ref[...] = _pack_bf16_pairs(_dot(he.astype(BF16), wd_ref[...]))

    @pl.when(jnp.logical_not(used))
    def _():
        o_ref[...] = jnp.zeros_like(o_ref)


def _experts(xs, tile_expert, n_used, wg, wu, wd, layer):
    p = xs.shape[0]
    tm = MOE_TILE
    grid_spec = pltpu.PrefetchScalarGridSpec(
        num_scalar_prefetch=2,
        grid=(p // tm,),
        in_specs=[
            pl.BlockSpec((tm, D_MODEL // 2), lambda i, te, nu: (i, 0)),
            pl.BlockSpec((None, None, D_MODEL, D_EXPERT), lambda i, te, nu: (layer, te[i], 0, 0)),
            pl.BlockSpec((None, None, D_MODEL, D_EXPERT), lambda i, te, nu: (layer, te[i], 0, 0)),
            pl.BlockSpec((None, None, D_EXPERT, D_MODEL), lambda i, te, nu: (layer, te[i], 0, 0)),
        ],
        out_specs=pl.BlockSpec((tm, D_MODEL // 2), lambda i, te, nu: (i, 0)),
    )
    return pl.pallas_call(
        _expert_kernel,
        grid_spec=grid_spec,
        out_shape=jax.ShapeDtypeStruct((p, D_MODEL // 2), jnp.uint32),
        compiler_params=_params(("arbitrary",)),
        name="moe_experts",
    )(tile_expert, n_used, xs, wg, wu, wd)


def _residual_kernel(ya_ref, yb_ref, w_ref, x1_ref, mod_ref, nfin_ref, o_ref, *, last):
    w = w_ref[...]
    y = w[:, 0:1] * _unpack_bf16_pairs(ya_ref[...]) + w[:, 1:2] * _unpack_bf16_pairs(yb_ref[...])
    x2 = x1_ref[...] + mod_ref[5:6, :] * y
    if last:
        x2 = _rms(x2, nfin_ref[...])
    o_ref[...] = x2


def _residual(ya, yb, w, x1, mod, nfin, seq, last):
    n = x1.shape[0]
    tm = min(1024, seq)
    per_seq = seq // tm
    row = pl.BlockSpec((tm, D_MODEL), lambda i: (i, 0))
    packed = pl.BlockSpec((tm, D_MODEL // 2), lambda i: (i, 0))
    return pl.pallas_call(
        functools.partial(_residual_kernel, last=last),
        grid=(n // tm,),
        in_specs=[packed, packed, pl.BlockSpec((tm, 2), lambda i: (i, 0)), row,
                  pl.BlockSpec((None, N_MOD, D_MODEL), lambda i: (i // per_seq, 0, 0)),
                  pl.BlockSpec((1, D_MODEL), lambda i: (0, 0))],
        out_specs=row,
        out_shape=jax.ShapeDtypeStruct((n, D_MODEL), F32),
        compiler_params=_params(("arbitrary",)),
        name="moe_residual",
    )(ya, yb, w, x1, mod, nfin)


def _moe(h2, ri, rw, cnt, wg, wu, wd, layer, x1, mod, nfin, seq, last):
    n = h2.shape[0]
    tm = MOE_TILE
    p = 2 * n + N_EXPERTS * tm
    counts = cnt[:, 0].astype(jnp.int32)
    padded = (counts + tm - 1) // tm * tm
    ends = jnp.cumsum(padded)
    starts = ends - padded
    base = jnp.zeros_like(ri[0:2])
    for e in range(1, N_EXPERTS):
        base = jnp.where(ri[0:2] == e, starts[e], base)
    pos = base + ri[2:4]
    tile_start = jnp.arange(p // tm, dtype=jnp.int32) * tm
    tile_expert = jnp.minimum(jnp.sum(tile_start[:, None] >= ends[None, :], axis=1), N_EXPERTS - 1)
    n_used = (ends[-1:] // tm).astype(jnp.int32)
    xs = _sc_dispatch(h2, pos, p)
    ys = _experts(xs, tile_expert.astype(jnp.int32), n_used, wg, wu, wd, layer)
    ya, yb = _sc_combine(ys, pos)
    return _residual(ya, yb, rw.T, x1, mod, nfin, seq, last)


def _split_w_in(w):
    hq_hf, hi, hg, att, gates = w[..., :1536], w[..., 1536:2048], w[..., 2048:2560], w[..., 2560:3328], w[..., 3328:]
    wa = (0.5 * jnp.concatenate([hq_hf, hg], axis=-1)).astype(BF16)
    wb = jnp.concatenate([0.5 * gates, hi, att], axis=-1).astype(BF16)
    return wa, wb


def _trunk(x, mod, wts):
    bsz, seq, _ = x.shape
    n = bsz * seq
    depth = wts["w_in_a"].shape[0]
    x2d = x.reshape(n, D_MODEL)
    for l in range(depth):
        mod_l = mod[l]
        pa, pb, span = _inproj(x2d, mod_l, wts["norm_mix"][l:l + 1], wts["w_in_a"][l], wts["w_in_b"][l],
                               wts["hg_lb_fwd"], wts["hg_lb_bwd"], seq, l)
        pa3 = pa.reshape(bsz, seq, A_COLS)
        pb3 = pb.reshape(bsz, seq, B_COLS)
        safe_f = (span[:, 0, 0] <= HG_SAFE_SPAN).astype(jnp.int32)
        safe_b = (span[:, 4, 0] <= HG_SAFE_SPAN).astype(jnp.int32)
        o_f = _hgrn(safe_f, pa3, pb3, wts["hg_lb_fwd"], l, False)
        o_h = _hgrn(safe_b, pa3, pb3, wts["hg_lb_bwd"], l, True, o_f, wts["hg_norm"][l:l + 1])
        o_a = _attn(pb3, wts["bias"], wts["sink"][l])
        x1, h2, ri, rw, cnt = _merge(o_h.reshape(n, HG_WIDTH), o_a.reshape(n, ATT_WIDTH), pb, x2d, mod_l,
                                     wts["norm_ffn"][l:l + 1], wts["w_br_hgrn"][l], wts["w_br_att"][l],
                                     wts["w_out"][l], wts["w_router_t"], wts["router_bias"], seq)
        x2d = _moe(h2, ri, rw, cnt, wts["w_gate"], wts["w_up"], wts["w_down"], l, x1, mod_l,
                   wts["norm_final"], seq, l == depth - 1)
    return x2d.reshape(bsz, seq, D_MODEL)


def kernel(x_prompt, x_sample, c_prompt, c_sample, w_ada, b_ada, norm_mix, norm_ffn, norm_final, w_in, hg_lb_fwd, hg_lb_bwd, hg_norm, att_sink, rel_bias, w_br_hgrn, w_br_att, w_out, w_router, router_bias, w_gate, w_up, w_down):
    depth = w_in.shape[0]
    bp, bs = c_prompt.shape[0], c_sample.shape[0]
    rows = -(-(bp + bs) // SUBLANES) * SUBLANES
    c_all = jnp.concatenate([c_prompt, c_sample, jnp.zeros((rows - bp - bs, D_MODEL), F32)], axis=0)
    mod = _ada(c_all, w_ada, b_ada).reshape(depth, rows, N_MOD, D_MODEL)
    bias, sink = _attn_tables(rel_bias, att_sink)
    w_in_a, w_in_b = _split_w_in(w_in)
    wts = {
        "norm_mix": norm_mix, "norm_ffn": norm_ffn, "norm_final": norm_final.reshape(1, D_MODEL),
        "w_in_a": w_in_a, "w_in_b": w_in_b,
        "hg_lb_fwd": hg_lb_fwd, "hg_lb_bwd": hg_lb_bwd, "hg_norm": hg_norm,
        "sink": sink, "bias": bias,
        "w_br_hgrn": w_br_hgrn.astype(BF16), "w_br_att": w_br_att.astype(BF16), "w_out": (0.5 * w_out).astype(BF16),
        "w_router_t": w_router.T,
        "router_bias": jnp.broadcast_to(router_bias[:, None], (N_EXPERTS, LANES)),
        "w_gate": (0.5 * w_gate).astype(BF16), "w_up": w_up.astype(BF16), "w_down": w_down.astype(BF16),
    }
    y_prompt = _trunk(x_prompt, mod[:, :bp], wts)
    y_sample = _trunk(x_sample, mod[:, bp:bp + bs], wts)
    return (y_prompt, y_sample)
```

```python
import functools

import numpy as np
import jax
import jax.numpy as jnp
from jax import lax
from jax.experimental import pallas as pl
from jax.experimental.pallas import tpu as pltpu
from jax.experimental.pallas import tpu_sc as plsc

D_MODEL = 1024
HG_DK = 128
HG_WIDTH = 512
HG_HEADS = 4
HG_SUB = 64
HG_LEVELS = 6
HG_GROUP = 32
HG_SAFE_SPAN = 80.0
HG_BLOCK = 512
ATT_HD = 64
ATT_HEADS = 8
ATT_KV = 2
ATT_GROUP = 4
ATT_WIDTH = 512
KV_WIDTH = 128
WINDOW = 128
ATT_BLOCK = 128
N_BUCKETS = 32
REL_MAX_DIST = 128
N_EXPERTS = 16
N_GROUPS = 4
EXPERTS_PER_GROUP = 4
D_EXPERT = 512
N_MOD = 6
IN_COLS = 5376
EPS = 1e-6
NEG_BIG = -1e30
TINY = 1e-30

A_COLS = 2048
A_HQ = 0
A_HF_FWD = 512
A_HF_BWD = 1024
A_HG = 1536
B_COLS = 3328
B_GATE_H = 0
B_GATE_A = 1024
B_HI = 2048
B_AQ = 2560
B_AK = 3072
B_AV = 3200

V7X_VMEM_LIMIT = 56 * 1024 * 1024
MXU_COLS = 256
LANES = 128
SUBLANES = 8

F32 = jnp.float32
BF16 = jnp.bfloat16
HIGHEST = lax.Precision.HIGHEST


def _params(sem):
    return pltpu.CompilerParams(dimension_semantics=sem, vmem_limit_bytes=V7X_VMEM_LIMIT)


def _dot(a, b):
    return jnp.dot(a, b, preferred_element_type=F32)


def _dot_nt(a, b):
    return lax.dot_general(a, b, (((1,), (1,)), ((), ())), preferred_element_type=F32)


def _dot_tn(a, b):
    return lax.dot_general(a, b, (((0,), (0,)), ((), ())), preferred_element_type=F32)


def _sigmoid(x):
    return 0.5 * jnp.tanh(0.5 * x) + 0.5


def _pack_bf16_pairs(x):
    c = x.shape[1] // 2
    bits = lax.bitcast_convert_type(x.astype(BF16).astype(F32), jnp.uint32)
    return (bits[:, :c] >> 16) | (bits[:, c:] & jnp.uint32(0xFFFF0000))


def _unpack_bf16_pairs(u):
    lo = lax.bitcast_convert_type(u << 16, F32)
    hi = lax.bitcast_convert_type(u & jnp.uint32(0xFFFF0000), F32)
    return jnp.concatenate([lo, hi], axis=1)


def _silu(x):
    return x * _sigmoid(x)


def _silu_of_half(xh):
    return xh * (jnp.tanh(xh) + 1.0)


def _rms(x, g):
    return x * lax.rsqrt(jnp.mean(x * x, axis=-1, keepdims=True) + EPS) * g


def _ada_kernel(c_ref, w_ref, b_ref, o_ref):
    c = c_ref[...]
    o_ref[...] = jnp.dot(_silu(c), w_ref[...], precision=HIGHEST, preferred_element_type=F32) + b_ref[...]


def _ada(c_all, w_ada, b_ada):
    depth = w_ada.shape[0]
    rows = c_all.shape[0]
    ncol = w_ada.shape[2]
    tn = 1024
    return pl.pallas_call(
        _ada_kernel,
        grid=(depth, ncol // tn),
        in_specs=[
            pl.BlockSpec((rows, D_MODEL), lambda l, j: (0, 0)),
            pl.BlockSpec((None, D_MODEL, tn), lambda l, j: (l, 0, j)),
            pl.BlockSpec((None, 1, tn), lambda l, j: (l, 0, j)),
        ],
        out_specs=pl.BlockSpec((None, rows, tn), lambda l, j: (l, 0, j)),
        out_shape=jax.ShapeDtypeStruct((depth, rows, ncol), F32),
        compiler_params=_params(("arbitrary", "arbitrary")),
        name="ada_mod",
    )(c_all, w_ada, b_ada.reshape(depth, 1, ncol))


def _lower_bound_row(gam_ref, layer):
    rows = [gam_ref[d:d + 1, :] for d in range(gam_ref.shape[0])]
    m = functools.reduce(jnp.maximum, rows)
    es = [jnp.exp(r - m) for r in rows]
    tot = functools.reduce(lambda a, b: a + b, es)
    ps = [e / tot for e in es]
    cum = ps[0]
    for d in range(1, layer + 1):
        cum = cum + ps[d]
    return jnp.clip(cum - ps[0], 0.0, 1.0)


def _forget(zh, lb):
    return 0.5 * (1.0 + lb) + (0.5 * (1.0 - lb)) * jnp.tanh(zh)


def _inproj_kernel(x_ref, mod_ref, g_ref, wa_ref, wb_ref, gf_ref, gb_ref, oa_ref, ob_ref, span_ref, *, layer):
    x = x_ref[...]
    h = (_rms(x, g_ref[...]) * (1.0 + mod_ref[1:2, :]) + mod_ref[0:1, :]).astype(BF16)
    oa = _dot(h, wa_ref[...])
    oa_ref[...] = oa
    ob_ref[...] = _dot(h, wb_ref[...]).astype(BF16)
    spans = []
    for gam_ref, col in ((gf_ref, A_HF_FWD), (gb_ref, A_HF_BWD)):
        f = _forget(oa[:, col:col + HG_WIDTH], _lower_bound_row(gam_ref, layer))
        g = jnp.log(jnp.maximum(f, TINY))
        gsum = jnp.sum(g.reshape(g.shape[0] // HG_GROUP, HG_GROUP, HG_WIDTH), axis=1)
        spans.append(jnp.max(jnp.max(-gsum, axis=0, keepdims=True), axis=1, keepdims=True))
    half = lax.broadcasted_iota(jnp.int32, span_ref.shape, 0) < span_ref.shape[0] // 2
    span_ref[...] = jnp.where(half, spans[0], spans[1])


def _inproj(x2d, mod, g, wa, wb, gam_f, gam_b, seq, layer):
    n = x2d.shape[0]
    tm = min(HG_BLOCK, seq)
    per_seq = seq // tm
    const = lambda i: (0, 0)
    return pl.pallas_call(
        functools.partial(_inproj_kernel, layer=layer),
        grid=(n // tm,),
        in_specs=[
            pl.BlockSpec((tm, D_MODEL), lambda i: (i, 0)),
            pl.BlockSpec((None, N_MOD, D_MODEL), lambda i: (i // per_seq, 0, 0)),
            pl.BlockSpec((1, D_MODEL), const),
            pl.BlockSpec((D_MODEL, A_COLS), const, pipeline_mode=pl.Buffered(1)),
            pl.BlockSpec((D_MODEL, B_COLS), const, pipeline_mode=pl.Buffered(1)),
            pl.BlockSpec(gam_f.shape, const),
            pl.BlockSpec(gam_b.shape, const),
        ],
        out_specs=[pl.BlockSpec((tm, A_COLS), lambda i: (i, 0)),
                   pl.BlockSpec((tm, B_COLS), lambda i: (i, 0)),
                   pl.BlockSpec((None, SUBLANES, LANES), lambda i: (i, 0, 0))],
        out_shape=[jax.ShapeDtypeStruct((n, A_COLS), F32), jax.ShapeDtypeStruct((n, B_COLS), BF16),
                   jax.ShapeDtypeStruct((n // tm, SUBLANES, LANES), F32)],
        compiler_params=_params(("arbitrary",)),
        name="inproj",
    )(x2d, mod, g, wa, wb, gam_f, gam_b)


def _hgrn_level_tables(rev):
    c = HG_SUB
    r = lax.broadcasted_iota(jnp.int32, (c, c), 0)
    s = lax.broadcasted_iota(jnp.int32, (c, c), 1)
    row = lax.broadcasted_iota(jnp.int32, (c, HG_WIDTH), 0)
    sels, qsides, pairs = [], [], []
    for lev in range(HG_LEVELS):
        half = 1 << lev
        blk = 2 * half
        r_up = (r & (blk - 1)) >= half
        s_up = (s & (blk - 1)) >= half
        base = r - (r & (blk - 1))
        mrow = base + (half if rev else half - 1)
        sels.append(jnp.where(s == mrow, 1.0, 0.0).astype(F32))
        row_up = (row & (blk - 1)) >= half
        qsides.append(~row_up if rev else row_up)
        same = (r >> (lev + 1)) == (s >> (lev + 1))
        pairs.append(same & ((~r_up & s_up) if rev else (r_up & ~s_up)))
    return jnp.concatenate(sels, axis=0), qsides, pairs, r == s


def _hgrn_kernel(safe_ref, *refs, layer, rev):
    if rev:
        q_ref, f_ref, v_ref, gam_ref, hg_ref, of_ref, nrm_ref, o_ref = refs[:8]
    else:
        q_ref, f_ref, v_ref, gam_ref, o_ref = refs[:5]
    st_ref, qs_ref, ks_ref, bs_ref, a_ref, qi_ref, dec_ref, up_ref, sb_ref = refs[-9:]

    @pl.when(pl.program_id(1) == 0)
    def _():
        st_ref[...] = jnp.zeros_like(st_ref)

    nc = pl.num_programs(1)
    chunk = (nc - 1 - pl.program_id(1)) if rev else pl.program_id(1)
    safe = safe_ref[pl.program_id(0) * nc + chunk] != 0
    c = HG_SUB
    w = HG_WIDTH
    nsub = q_ref.shape[0] // c
    lb = _lower_bound_row(gam_ref, layer)
    r_i = lax.broadcasted_iota(jnp.int32, (c, c), 0)
    s_i = lax.broadcasted_iota(jnp.int32, (c, c), 1)
    causal = (s_i >= r_i) if rev else (s_i <= r_i)
    tri = jnp.where(causal, 1.0, 0.0).astype(BF16)
    row = lax.broadcasted_iota(jnp.int32, (c, w), 0)
    far = (row < HG_GROUP) if rev else (row >= HG_GROUP)
    ref_row = HG_GROUP if rev else HG_GROUP - 1
    last_row = 0 if rev else c - 1
    heads = [slice(h * HG_DK, (h + 1) * HG_DK) for h in range(HG_HEADS)]

    def rows_of(i):
        ci = (nsub - 1 - i) if rev else i
        return pl.ds(pl.multiple_of(ci * c, c), c)

    def gates_pass(i, carry):
        sl = rows_of(i)
        zq = q_ref[sl, :]
        f = _forget(f_ref[sl, :], lb)
        g = jnp.log(jnp.maximum(f, TINY))
        qs_ref[sl, :] = _silu_of_half(zq) * (HG_DK ** -0.5)
        ks_ref[sl, :] = 1.0 - f
        g1 = g.astype(BF16)
        r1 = g - g1.astype(F32)
        g2 = r1.astype(BF16)
        g3 = (r1 - g2.astype(F32)).astype(BF16)
        bb = _dot(tri, jnp.concatenate([g1, g2, g3], axis=1))
        bs_ref[sl, :] = (bb[:, :w] + bb[:, w:2 * w]) + bb[:, 2 * w:]
        return carry

    lax.fori_loop(0, nsub, gates_pass, 0, unroll=4)

    def gates(i):
        sl = rows_of(i)
        return sl, qs_ref[sl, :], ks_ref[sl, :], bs_ref[sl, :]

    def stage(i, sl, a, q_in, k_dec, dec):
        vb = v_ref[sl, :].astype(BF16)
        for h, hs in enumerate(heads):
            a_ref[i, h] = a[h].astype(BF16)
            up_ref[i, h] = _dot_tn(vb[:, hs], k_dec[:, hs])
        qi_ref[sl, :] = q_in
        dec_ref[pl.ds(pl.multiple_of(i * SUBLANES, SUBLANES), SUBLANES), :] = jnp.broadcast_to(dec, (SUBLANES, w))

    def scan_states():
        for h, hs in enumerate(heads):
            st = st_ref[h]
            for i in range(nsub):
                sb_ref[i, h] = st.astype(BF16)
                st = st * dec_ref[i * SUBLANES:i * SUBLANES + 1, hs] + up_ref[i, h]
            st_ref[h] = st

    def finish_pass(i, carry):
        sl = rows_of(i)
        vb = v_ref[sl, :].astype(BF16)
        q_in = qi_ref[sl, :]
        outs = [_dot_nt(q_in[:, hs], sb_ref[i, h]) + _dot(a_ref[i, h], vb[:, hs]) for h, hs in enumerate(heads)]
        o_all = jnp.concatenate(outs, axis=1)
        if rev:
            tot = of_ref[sl, :] + o_all
            nrm = nrm_ref[...]
            ys = [_rms(tot[:, hs], nrm) for hs in heads]
            o_ref[sl, :] = (jnp.concatenate(ys, axis=1) * _silu_of_half(hg_ref[sl, :])).astype(o_ref.dtype)
        else:
            o_ref[sl, :] = o_all
        return carry

    def factored(i, carry):
        sl, q, k, b = gates(i)
        r = b[ref_row:ref_row + 1, :]
        bl = b[last_row:last_row + 1, :]
        rg = jnp.where(far, r, 0.0)
        qt = q * jnp.exp(b - rg)
        kt = k * jnp.exp(rg - b)
        er = jnp.exp(r)
        qn = jnp.where(far, 0.0, qt).astype(BF16)
        qf = jnp.where(far, qt, 0.0).astype(BF16)
        kc = jnp.where(far, kt, kt * er).astype(BF16)
        ktb = kt.astype(BF16)
        q_in = jnp.where(far, qt * er, qt).astype(BF16)
        k_dec = (kt * jnp.where(far, jnp.exp(bl - r), jnp.exp(bl))).astype(BF16)
        a = []
        for hs in heads:
            lhs = jnp.concatenate([qn[:, hs], qf[:, hs]], axis=1)
            rhs = jnp.concatenate([ktb[:, hs], kc[:, hs]], axis=1)
            a.append(jnp.where(causal, _dot_nt(lhs, rhs), 0.0))
        stage(i, sl, a, q_in, k_dec, jnp.exp(bl))
        return carry

    def levels(i, carry):
        sl, q, k, b = gates(i)
        sel_all, qsides, pairs, eye = _hgrn_level_tables(rev)
        bl = b[last_row:last_row + 1, :]
        q_in = (q * jnp.exp(b)).astype(BF16)
        k_dec = (k * jnp.exp(bl - b)).astype(BF16)
        bref_all = jnp.dot(sel_all, b, precision=HIGHEST, preferred_element_type=F32)
        qb = q.astype(BF16)
        kb = k.astype(BF16)
        a = [jnp.where(eye, _dot_nt(qb[:, hs], kb[:, hs]), 0.0) for hs in heads]
        for lev in range(HG_LEVELS):
            bref = bref_all[lev * c:(lev + 1) * c, :]
            qs = qsides[lev]
            x = jnp.exp(jnp.where(qs, b - bref, bref - b))
            ql = jnp.where(qs, q * x, 0.0).astype(BF16)
            kl = jnp.where(qs, 0.0, k * x).astype(BF16)
            for h, hs in enumerate(heads):
                a[h] = a[h] + jnp.where(pairs[lev], _dot_nt(ql[:, hs], kl[:, hs]), 0.0)
        stage(i, sl, a, q_in, k_dec, jnp.exp(bl))
        return carry

    @pl.when(safe)
    def _():
        lax.fori_loop(0, nsub, factored, 0, unroll=4)

    @pl.when(jnp.logical_not(safe))
    def _():
        lax.fori_loop(0, nsub, levels, 0)

    scan_states()
    lax.fori_loop(0, nsub, finish_pass, 0, unroll=4)


def _hgrn(safe, pa3, pb3, gamma, layer, rev, o_fwd=None, nrm=None):
    bsz, seq, _ = pa3.shape
    cb = min(HG_BLOCK, seq)
    assert seq % cb == 0 and cb % HG_SUB == 0
    nc = seq // cb
    wblk = HG_WIDTH

    def cmap(col):
        if rev:
            return lambda b, c, safe_ref: (b, nc - 1 - c, col // wblk)
        return lambda b, c, safe_ref: (b, c, col // wblk)

    const = lambda b, c, safe_ref: (0, 0)
    in_specs = [
        pl.BlockSpec((None, cb, wblk), cmap(A_HQ)),
        pl.BlockSpec((None, cb, wblk), cmap(A_HF_BWD if rev else A_HF_FWD)),
        pl.BlockSpec((None, cb, wblk), cmap(B_HI)),
        pl.BlockSpec(gamma.shape, const),
    ]
    args = [pa3, pa3, pb3, gamma]
    if rev:
        in_specs += [
            pl.BlockSpec((None, cb, wblk), cmap(A_HG)),
            pl.BlockSpec((None, cb, wblk), cmap(0)),
            pl.BlockSpec((1, HG_DK), const),
        ]
        args += [pa3, o_fwd, nrm]
    out_dtype = BF16 if rev else F32
    grid_spec = pltpu.PrefetchScalarGridSpec(
        num_scalar_prefetch=1,
        grid=(bsz, nc),
        in_specs=in_specs,
        out_specs=pl.BlockSpec((None, cb, wblk), cmap(0)),
        scratch_shapes=[pltpu.VMEM((HG_HEADS, HG_DK, HG_DK), F32)] + [pltpu.VMEM((cb, wblk), F32)] * 3
        + [pltpu.VMEM((cb // HG_SUB, HG_HEADS, HG_SUB, HG_SUB), BF16), pltpu.VMEM((cb, wblk), BF16),
           pltpu.VMEM((cb // HG_SUB * SUBLANES, wblk), F32),
           pltpu.VMEM((cb // HG_SUB, HG_HEADS, HG_DK, HG_DK), F32),
           pltpu.VMEM((cb // HG_SUB, HG_HEADS, HG_DK, HG_DK), BF16)],
    )
    return pl.pallas_call(
        functools.partial(_hgrn_kernel, layer=layer, rev=rev),
        grid_spec=grid_spec,
        out_shape=jax.ShapeDtypeStruct((bsz, seq, HG_WIDTH), out_dtype),
        compiler_params=_params(("arbitrary", "arbitrary")),
        name="hgrn_bwd" if rev else "hgrn_fwd",
    )(safe, *args)


ATT_ROW_HEADS = (0, 2, 1, 3)


def _t5_buckets(rel):
    half = N_BUCKETS // 2
    ret = np.where(rel > 0, half, 0)
    n = np.abs(rel)
    max_exact = half // 2
    large = max_exact + (np.log(np.maximum(n, 1) / max_exact)
                         / np.log(REL_MAX_DIST / max_exact) * (half - max_exact)).astype(np.int32)
    large = np.minimum(large, half - 1)
    return (ret + np.where(n < max_exact, n, large)).astype(np.int32)


def _attn_tables(rel_bias, att_sink):
    c = ATT_BLOCK
    rel = np.arange(3 * c)[None, :] - c - np.arange(c)[:, None]
    onehot = np.equal(_t5_buckets(rel).reshape(-1, 1), np.arange(N_BUCKETS)).astype(np.float32)
    bias = jnp.dot(jnp.asarray(onehot, BF16), rel_bias.astype(F32), precision=HIGHEST)
    bias = bias.reshape(c, 3 * c, ATT_HEADS).transpose(2, 0, 1)
    bias = jnp.where(jnp.asarray(np.abs(rel) <= WINDOW)[None], bias, NEG_BIG)
    col = np.arange(3 * c)
    cases = []
    for case in range(4):
        valid = np.ones(3 * c, bool)
        if case & 1:
            valid &= col >= c
        if case & 2:
            valid &= col < 2 * c
        cases.append(jnp.where(jnp.asarray(valid)[None, None, :], bias, NEG_BIG))
    tab = jnp.stack(cases)
    order = np.array([[ATT_GROUP * g + h for h in ATT_ROW_HEADS] for g in range(ATT_KV)])
    tab = tab[:, order].reshape(4, ATT_KV, ATT_GROUP * c, 3 * c)
    sink = att_sink.astype(F32)[:, order]
    sink = jnp.broadcast_to(sink[..., None, None], sink.shape + (c, LANES))
    return tab, sink.reshape(att_sink.shape[0], ATT_KV, ATT_GROUP * c, LANES)


def _attn_kernel(q_ref, kp_ref, kc_ref, kn_ref, vp_ref, vc_ref, vn_ref, bias_ref, sink_ref, o_ref):
    n = pl.program_id(1)
    nsteps = pl.num_programs(1)
    c = ATT_BLOCK
    nsub = q_ref.shape[0] // c
    pair = 2 * ATT_HD
    kwin = jnp.concatenate([kp_ref[...], kc_ref[...], kn_ref[...]], axis=0).astype(F32)
    vwin = jnp.concatenate([vp_ref[...], vc_ref[...], vn_ref[...]], axis=0).astype(F32)
    kroll = pltpu.roll(kwin, ATT_HD, axis=1)
    vroll = pltpu.roll(vwin, ATT_HD, axis=1)
    lo = lax.broadcasted_iota(jnp.int32, kwin.shape, 1) < ATT_HD
    k_lo = [jnp.where(lo, kwin, 0.0).astype(BF16), jnp.where(lo, kroll, 0.0).astype(BF16)]
    k_hi = [jnp.where(lo, 0.0, kroll).astype(BF16), jnp.where(lo, 0.0, kwin).astype(BF16)]
    v_lo = [jnp.where(lo, vwin, 1.0).astype(BF16), jnp.where(lo, vroll, 1.0).astype(BF16)]
    v_hi = [jnp.where(lo, 1.0, vroll).astype(BF16), jnp.where(lo, 1.0, vwin).astype(BF16)]
    lo_out = lax.broadcasted_iota(jnp.int32, (2 * c, pair), 1) < ATT_HD
    for j in range(nsub):
        case = jnp.int32(0)
        if j == 0:
            case = case + (n == 0).astype(jnp.int32)
        if j == nsub - 1:
            case = case + 2 * (n == nsteps - 1).astype(jnp.int32)
        band = slice(j * c, (j + 3) * c)
        qrows = slice(j * c, (j + 1) * c)
        for g in range(ATT_KV):
            qg = q_ref[qrows, g * 2 * pair:(g + 1) * 2 * pair].astype(F32) * (ATT_HD ** -0.5)
            lhs = jnp.concatenate([qg[:, :pair], qg[:, pair:]], axis=0).astype(BF16)
            rhs = jnp.concatenate([k_lo[g][band], k_hi[g][band]], axis=0)
            lg = _dot_nt(lhs, rhs)
            logits = jnp.concatenate([lg[:, :3 * c], lg[:, 3 * c:]], axis=0) + bias_ref[case, g]
            s = sink_ref[g][:, 0:1]
            m = jnp.maximum(jnp.max(logits, axis=-1, keepdims=True), s)
            p = jnp.exp(logits - m).astype(BF16)
            es = jnp.exp(s - m)
            o_even = _dot(p[:2 * c], v_lo[g][band])
            o_odd = _dot(p[2 * c:], v_hi[g][band])
            o = jnp.where(lo_out,
                          o_even / (pltpu.roll(o_even, ATT_HD, axis=1) + es[:2 * c]),
                          o_odd / (pltpu.roll(o_odd, ATT_HD, axis=1) + es[2 * c:]))
            o_ref[qrows, g * 2 * pair:g * 2 * pair + pair] = o[:c].astype(o_ref.dtype)
            o_ref[qrows, g * 2 * pair + pair:(g + 1) * 2 * pair] = o[c:].astype(o_ref.dtype)


def _attn(proj3, bias, sink):
    bsz, seq, _ = proj3.shape
    c = ATT_BLOCK
    nsub = max(k for k in (4, 2, 1) if seq % (k * c) == 0)
    qb = nsub * c
    nsteps = seq // qb
    assert seq % c == 0
    nb = seq // c
    kcol = B_AK // KV_WIDTH
    vcol = B_AV // KV_WIDTH

    def band(col):
        return [
            pl.BlockSpec((None, c, KV_WIDTH), lambda b, n: (b, jnp.maximum(n * nsub - 1, 0), col)),
            pl.BlockSpec((None, qb, KV_WIDTH), lambda b, n: (b, n, col)),
            pl.BlockSpec((None, c, KV_WIDTH), lambda b, n: (b, jnp.minimum((n + 1) * nsub, nb - 1), col)),
        ]

    return pl.pallas_call(
        _attn_kernel,
        grid=(bsz, nsteps),
        in_specs=[pl.BlockSpec((None, qb, ATT_WIDTH), lambda b, n: (b, n, B_AQ // ATT_WIDTH))]
        + band(kcol) + band(vcol)
        + [pl.BlockSpec(bias.shape, lambda b, n: (0, 0, 0, 0)),
           pl.BlockSpec(sink.shape, lambda b, n: (0, 0, 0))],
        out_specs=pl.BlockSpec((None, qb, ATT_WIDTH), lambda b, n: (b, n, 0)),
        out_shape=jax.ShapeDtypeStruct((bsz, seq, ATT_WIDTH), BF16),
        compiler_params=_params(("arbitrary", "arbitrary")),
        name="window_attn",
    )(proj3, proj3, proj3, proj3, proj3, proj3, proj3, bias, sink)


def _first_argmax(vals):
    best, idx = vals[0], jnp.zeros(vals[0].shape, jnp.int32)
    for j in range(1, len(vals)):
        upd = vals[j] > best
        idx = jnp.where(upd, j, idx)
        best = jnp.where(upd, vals[j], best)
    return best, idx


def _select(vals, idx):
    out = vals[0]
    for j in range(1, len(vals)):
        out = jnp.where(idx == j, vals[j], out)
    return out


def _route(logits_t, rbias):
    m = jnp.max(logits_t, axis=0, keepdims=True)
    e = jnp.exp(logits_t - m)
    scores = e / jnp.sum(e, axis=0, keepdims=True)
    sel = scores + rbias
    srow = [scores[i:i + 1, :] for i in range(N_EXPERTS)]
    lrow = [sel[i:i + 1, :] for i in range(N_EXPERTS)]
    gscore = []
    for g in range(N_GROUPS):
        a, b, c, d = lrow[4 * g:4 * g + 4]
        hi1, lo1 = jnp.maximum(a, b), jnp.minimum(a, b)
        hi2, lo2 = jnp.maximum(c, d), jnp.minimum(c, d)
        gscore.append(jnp.maximum(hi1, hi2) + jnp.maximum(jnp.minimum(hi1, hi2), jnp.maximum(lo1, lo2)))
    _, gi = _first_argmax(gscore)
    ing = [_select([lrow[4 * g + j] for g in range(N_GROUPS)], gi) for j in range(EXPERTS_PER_GROUP)]
    sg = [_select([srow[4 * g + j] for g in range(N_GROUPS)], gi) for j in range(EXPERTS_PER_GROUP)]
    _, i1 = _first_argmax(ing)
    rest = [jnp.where(i1 == j, -jnp.inf, ing[j]) for j in range(EXPERTS_PER_GROUP)]
    _, i2 = _first_argmax(rest)
    s1, s2 = _select(sg, i1), _select(sg, i2)
    tot = s1 + s2
    w1, w2 = s1 / tot, s2 / tot
    idx1 = gi * EXPERTS_PER_GROUP + i1
    idx2 = gi * EXPERTS_PER_GROUP + i2
    return idx1, idx2, w1, w2


def _merge_kernel(oh_ref, oa_ref, gh_ref, ga_ref, x_ref, mod_ref, nf_ref, wbh_ref, wba_ref, wo_ref,
                  wrt_ref, rb_ref, before_ref, x1_ref, h2_ref, ri_ref, rw_ref, cnt_ref, carry_ref, mg_ref):
    @pl.when(pl.program_id(0) == 0)
    def _():
        carry_ref[...] = jnp.zeros_like(carry_ref)

    tm = x_ref.shape[0]
    blocks = [slice(j * MXU_COLS, (j + 1) * MXU_COLS) for j in range(D_MODEL // MXU_COLS)]
    for cols in blocks:
        mh = _dot(oh_ref[...], wbh_ref[:, cols])
        ma = _dot(oa_ref[...], wba_ref[:, cols])
        mg_ref[:, cols] = ((jnp.tanh(gh_ref[:, cols].astype(F32)) + 1.0) * mh
                           + (jnp.tanh(ga_ref[:, cols].astype(F32)) + 1.0) * ma).astype(BF16)
    ssq = jnp.zeros((tm, 1), F32)
    for cols in blocks:
        x1 = x_ref[:, cols] + mod_ref[2:3, cols] * _dot(mg_ref[...], wo_ref[:, cols])
        x1_ref[:, cols] = x1
        ssq = ssq + jnp.sum(x1 * x1, axis=-1, keepdims=True)
    inv = lax.rsqrt(ssq * (1.0 / D_MODEL) + EPS)
    wr = wrt_ref[...]
    w_hi = wr.astype(BF16)
    w_lo = (wr - w_hi.astype(F32)).astype(BF16)
    logits_t = jnp.zeros((N_EXPERTS, tm), F32)
    half = len(blocks) // 2
    for j in range(half):
        pair = []
        for cols in (blocks[j], blocks[half + j]):
            h2 = x1_ref[:, cols] * inv * nf_ref[:, cols] * (1.0 + mod_ref[4:5, cols]) + mod_ref[3:4, cols]
            h_hi = h2.astype(BF16)
            h_lo = (h2 - h_hi.astype(F32)).astype(BF16)
            logits_t = logits_t + ((_dot_nt(w_hi[:, cols], h_hi) + _dot_nt(w_hi[:, cols], h_lo))
                                   + _dot_nt(w_lo[:, cols], h_hi))
            pair.append(h2)
        h2_ref[:, blocks[j]] = _pack_bf16_pairs(jnp.concatenate(pair, axis=1))
    idx1, idx2, w1, w2 = _route(logits_t, rb_ref[:, 0:1])
    erow = lax.broadcasted_iota(jnp.int32, logits_t.shape, 0)
    oh1 = erow == idx1
    oh2 = erow == idx2
    oh = jnp.where(oh1 | oh2, 1.0, 0.0)
    carry = carry_ref[...]
    pref = _dot(oh.astype(BF16), before_ref[...]) + carry[:, 0:1]
    rank1 = jnp.sum(jnp.where(oh1, pref, 0.0), axis=0, keepdims=True)
    rank2 = jnp.sum(jnp.where(oh2, pref, 0.0), axis=0, keepdims=True)
    carry = carry + jnp.sum(oh, axis=1, keepdims=True)
    carry_ref[...] = carry
    cnt_ref[...] = carry
    ri_ref[...] = jnp.concatenate([idx1, idx2, rank1.astype(jnp.int32), rank2.astype(jnp.int32)], axis=0)
    rw_ref[...] = jnp.concatenate([w1, w2], axis=0)


def _merge(o_h, o_a, proj, x2d, mod, nf, wbh, wba, wo, wrt, rb, seq):
    n = x2d.shape[0]
    tm = min(1024, seq)
    per_seq = seq // tm
    const = lambda i: (0, 0)
    before = jnp.asarray(np.arange(tm)[:, None] < np.arange(tm)[None, :], BF16)
    return pl.pallas_call(
        _merge_kernel,
        grid=(n // tm,),
        in_specs=[
            pl.BlockSpec((tm, HG_WIDTH), lambda i: (i, 0)),
            pl.BlockSpec((tm, ATT_WIDTH), lambda i: (i, 0)),
            pl.BlockSpec((tm, D_MODEL), lambda i: (i, B_GATE_H // D_MODEL)),
            pl.BlockSpec((tm, D_MODEL), lambda i: (i, B_GATE_A // D_MODEL)),
            pl.BlockSpec((tm, D_MODEL), lambda i: (i, 0)),
            pl.BlockSpec((None, N_MOD, D_MODEL), lambda i: (i // per_seq, 0, 0)),
            pl.BlockSpec((1, D_MODEL), const),
            pl.BlockSpec(wbh.shape, const),
            pl.BlockSpec(wba.shape, const),
            pl.BlockSpec(wo.shape, const),
            pl.BlockSpec(wrt.shape, const),
            pl.BlockSpec(rb.shape, const),
            pl.BlockSpec(before.shape, const),
        ],
        out_specs=[
            pl.BlockSpec((tm, D_MODEL), lambda i: (i, 0)),
            pl.BlockSpec((tm, D_MODEL // 2), lambda i: (i, 0)),
            pl.BlockSpec((4, tm), lambda i: (0, i)),
            pl.BlockSpec((2, tm), lambda i: (0, i)),
            pl.BlockSpec((N_EXPERTS, LANES), const),
        ],
        out_shape=[
            jax.ShapeDtypeStruct((n, D_MODEL), F32),
            jax.ShapeDtypeStruct((n, D_MODEL // 2), jnp.uint32),
            jax.ShapeDtypeStruct((4, n), jnp.int32),
            jax.ShapeDtypeStruct((2, n), F32),
            jax.ShapeDtypeStruct((N_EXPERTS, LANES), F32),
        ],
        scratch_shapes=[pltpu.VMEM((N_EXPERTS, LANES), F32), pltpu.VMEM((tm, D_MODEL), BF16)],
        compiler_params=_params(("arbitrary",)),
        name="merge_router",
    )(o_h, o_a, proj, proj, x2d, mod, nf, wbh, wba, wo, wrt, rb, before)


MOE_TILE = 512
SC_WINDOW = 64


def _sc_mesh():
    return plsc.VectorSubcoreMesh(core_axis_name="c", subcore_axis_name="s")


def _sc_dispatch(h, pos, p):
    n, d = h.shape
    win = SC_WINDOW
    info = plsc.get_sparse_core_info()
    workers = info.num_cores * info.num_subcores
    assert n % (2 * win * workers) == 0, "each vector subcore walks its windows two at a time"
    wpt = n // (win * workers)
    pos_w = pos.reshape(2, n // win, win).transpose(1, 0, 2)

    @functools.partial(
        pl.kernel, out_type=jax.ShapeDtypeStruct((p, d), h.dtype), mesh=_sc_mesh(),
        scratch_types=[pltpu.VMEM((wpt, 2, win), jnp.int32), pltpu.VMEM((2, win, d), h.dtype),
                       pltpu.SemaphoreType.DMA((2,))],
        name="moe_dispatch")
    def dispatch(h_hbm, pos_hbm, o_hbm, idx_v, rows_v, load_sem):
        wid = lax.axis_index("c") * info.num_subcores + lax.axis_index("s")
        first = wid * wpt
        pltpu.sync_copy(pos_hbm.at[pl.ds(first, wpt)], idx_v)

        def load(j, slot):
            return pltpu.make_async_copy(h_hbm.at[pl.ds((first + j) * win, win)], rows_v.at[slot],
                                         load_sem.at[slot])

        load(0, 0).start()

        @pl.loop(0, wpt, step=2)
        def _(j):
            for slot in range(2):
                jj = j + slot
                load(jj, slot).wait()

                @pl.when(jj + 1 < wpt)
                def _():
                    load(jj + 1, 1 - slot).start()

                pltpu.sync_copy(rows_v.at[slot], o_hbm.at[idx_v.at[jj, 0]])
                pltpu.sync_copy(rows_v.at[slot], o_hbm.at[idx_v.at[jj, 1]])

    return dispatch(h, pos_w)


def _sc_combine(ys, pos):
    n = pos.shape[1]
    d = ys.shape[1]
    win = SC_WINDOW // 2
    info = plsc.get_sparse_core_info()
    workers = info.num_cores * info.num_subcores
    assert n % (2 * win * workers) == 0, "each vector subcore walks its windows two at a time"
    wpt = n // (win * workers)
    pos_w = pos.reshape(2, n // win, win).transpose(1, 0, 2)
    out =jax.ShapeDtypeStruct((n, d), ys.dtype)

    @functools.partial(
        pl.kernel, out_type=(out, out), mesh=_sc_mesh(),
        scratch_types=[pltpu.VMEM((wpt, 2, win), jnp.int32), pltpu.VMEM((2, 2, win, d), ys.dtype),
                       pltpu.SemaphoreType.DMA((2, 2)), pltpu.SemaphoreType.DMA((2,))],
        name="moe_combine")
    def combine(ys_hbm, pos_hbm, a_hbm, b_hbm, idx_v, rows_v, gather_sem, store_sem):
        wid = lax.axis_index("c") * info.num_subcores + lax.axis_index("s")
        first = wid * wpt
        pltpu.sync_copy(pos_hbm.at[pl.ds(first, wpt)], idx_v)
        outs = (a_hbm, b_hbm)

        def gather(j, slot, k):
            return pltpu.make_async_copy(ys_hbm.at[idx_v.at[j, k]], rows_v.at[slot, k], gather_sem.at[slot, k])

        def store(j, slot, k):
            return pltpu.make_async_copy(rows_v.at[slot, k], outs[k].at[pl.ds((first + j) * win, win)],
                                         store_sem.at[k])

        gather(0, 0, 0).start()
        gather(0, 0, 1).start()

        @pl.loop(0, wpt, step=2)
        def _(j):
            for slot in range(2):
                jj = j + slot
                gather(jj, slot, 0).wait()
                gather(jj, slot, 1).wait()

                @pl.when(jj + 1 < wpt)
                def _():
                    gather(jj + 1, 1 - slot, 0).start()
                    gather(jj + 1, 1 - slot, 1).start()

                store(jj, slot, 0).start()
                store(jj, slot, 1).start()
                store(jj, slot, 0).wait()
                store(jj, slot, 1).wait()

    return combine(ys, pos_w)


def _expert_kernel(te_ref, nu_ref, x_ref, wg_ref, wu_ref, wd_ref, o_ref):
    del te_ref
    used = pl.program_id(0) < nu_ref[0]

    @pl.when(used)
    def _():
        x = _unpack_bf16_pairs(x_ref[...]).astype(BF16)
        he = _silu_of_half(_dot(x, wg_ref[...])) * _dot(x, wu_ref[...])
        o_ref[...] = _pack_bf16_pairs(_dot(he.astype(BF16), wd_ref[...]))

    @pl.when(jnp.logical_not(used))
    def _():
        o_ref[...] = jnp.zeros_like(o_ref)


def _experts(xs, tile_expert, n_used, wg, wu, wd, layer):
    p = xs.shape[0]
    tm = MOE_TILE
    grid_spec = pltpu.PrefetchScalarGridSpec(
        num_scalar_prefetch=2,
        grid=(p // tm,),
        in_specs=[
            pl.BlockSpec((tm, D_MODEL // 2), lambda i, te, nu: (i, 0)),
            pl.BlockSpec((None, None, D_MODEL, D_EXPERT), lambda i, te, nu: (layer, te[i], 0, 0)),
            pl.BlockSpec((None, None, D_MODEL, D_EXPERT), lambda i, te, nu: (layer, te[i], 0, 0)),
            pl.BlockSpec((None, None, D_EXPERT, D_MODEL), lambda i, te, nu: (layer, te[i], 0, 0)),
        ],
        out_specs=pl.BlockSpec((tm, D_MODEL // 2), lambda i, te, nu: (i, 0)),
    )
    return pl.pallas_call(
        _expert_kernel,
        grid_spec=grid_spec,
        out_shape=jax.ShapeDtypeStruct((p, D_MODEL // 2), jnp.uint32),
        compiler_params=_params(("arbitrary",)),
        name="moe_experts",
    )(tile_expert, n_used, xs, wg, wu, wd)


def _residual_kernel(ya_ref, yb_ref, w_ref, x1_ref, mod_ref, nfin_ref, o_ref, *, last):
    w = w_ref[...]
    y = w[:, 0:1] * _unpack_bf16_pairs(ya_ref[...]) + w[:, 1:2] * _unpack_bf16_pairs(yb_ref[...])
    x2 = x1_ref[...] + mod_ref[5:6, :] * y
    if last:
        x2 = _rms(x2, nfin_ref[...])
    o_ref[...] = x2


def _residual(ya, yb, w, x1, mod, nfin, seq, last):
    n = x1.shape[0]
    tm = min(1024, seq)
    per_seq = seq // tm
    row = pl.BlockSpec((tm, D_MODEL), lambda i: (i, 0))
    packed = pl.BlockSpec((tm, D_MODEL // 2), lambda i: (i, 0))
    return pl.pallas_call(
        functools.partial(_residual_kernel, last=last),
        grid=(n // tm,),
        in_specs=[packed, packed, pl.BlockSpec((tm, 2), lambda i: (i, 0)), row,
                  pl.BlockSpec((None, N_MOD, D_MODEL), lambda i: (i // per_seq, 0, 0)),
                  pl.BlockSpec((1, D_MODEL), lambda i: (0, 0))],
        out_specs=row,
        out_shape=jax.ShapeDtypeStruct((n, D_MODEL), F32),
        compiler_params=_params(("arbitrary",)),
        name="moe_residual",
    )(ya, yb, w, x1, mod, nfin)


def _moe(h2, ri, rw, cnt, wg, wu, wd, layer, x1, mod, nfin, seq, last):
    n = h2.shape[0]
    tm = MOE_TILE
    p = 2 * n + N_EXPERTS * tm
    counts = cnt[:, 0].astype(jnp.int32)
    padded = (counts + tm - 1) // tm * tm
    ends = jnp.cumsum(padded)
    starts = ends - padded
    base = jnp.zeros_like(ri[0:2])
    for e in range(1, N_EXPERTS):
        base = jnp.where(ri[0:2] == e, starts[e], base)
    pos = base + ri[2:4]
    tile_start = jnp.arange(p // tm, dtype=jnp.int32) * tm
    tile_expert = jnp.minimum(jnp.sum(tile_start[:, None] >= ends[None, :], axis=1), N_EXPERTS - 1)
    n_used = (ends[-1:] // tm).astype(jnp.int32)
    xs = _sc_dispatch(h2, pos, p)
    ys = _experts(xs, tile_expert.astype(jnp.int32), n_used, wg, wu, wd, layer)
    ya, yb = _sc_combine(ys, pos)
    return _residual(ya, yb, rw.T, x1, mod, nfin, seq, last)


def _split_w_in(w):
    hq_hf, hi, hg, att, gates = w[..., :1536], w[..., 1536:2048], w[..., 2048:2560], w[..., 2560:3328], w[..., 3328:]
    wa = (0.5 * jnp.concatenate([hq_hf, hg], axis=-1)).astype(BF16)
    wb = jnp.concatenate([0.5 * gates, hi, att], axis=-1).astype(BF16)
    return wa, wb


def _trunk(x, mod, wts):
    bsz, seq, _ = x.shape
    n = bsz * seq
    depth = wts["w_in_a"].shape[0]
    x2d = x.reshape(n, D_MODEL)
    for l in range(depth):
        mod_l = mod[l]
        pa, pb, span = _inproj(x2d, mod_l, wts["norm_mix"][l:l + 1], wts["w_in_a"][l], wts["w_in_b"][l],
                               wts["hg_lb_fwd"], wts["hg_lb_bwd"], seq, l)
        pa3 = pa.reshape(bsz, seq, A_COLS)
        pb3 = pb.reshape(bsz, seq, B_COLS)
        safe_f = (span[:, 0, 0] <= HG_SAFE_SPAN).astype(jnp.int32)
        safe_b = (span[:, 4, 0] <= HG_SAFE_SPAN).astype(jnp.int32)
        o_f = _hgrn(safe_f, pa3, pb3, wts["hg_lb_fwd"], l, False)
        o_h = _hgrn(safe_b, pa3, pb3, wts["hg_lb_bwd"], l, True, o_f, wts["hg_norm"][l:l + 1])
        o_a = _attn(pb3, wts["bias"], wts["sink"][l])
        x1, h2, ri, rw, cnt = _merge(o_h.reshape(n, HG_WIDTH), o_a.reshape(n, ATT_WIDTH), pb, x2d, mod_l,
                                     wts["norm_ffn"][l:l + 1], wts["w_br_hgrn"][l], wts["w_br_att"][l],
                                     wts["w_out"][l], wts["w_router_t"], wts["router_bias"], seq)
        x2d = _moe(h2, ri, rw, cnt, wts["w_gate"], wts["w_up"], wts["w_down"], l, x1, mod_l,
                   wts["norm_final"], seq, l == depth - 1)
    return x2d.reshape(bsz, seq, D_MODEL)


def kernel(x_prompt, x_sample, c_prompt, c_sample, w_ada, b_ada, norm_mix, norm_ffn, norm_final, w_in, hg_lb_fwd, hg_lb_bwd, hg_norm, att_sink, rel_bias, w_br_hgrn, w_br_att, w_out, w_router, router_bias, w_gate, w_up, w_down):
    depth = w_in.shape[0]
    bp, bs = c_prompt.shape[0], c_sample.shape[0]
    rows = -(-(bp + bs) // SUBLANES) * SUBLANES
    c_all = jnp.concatenate([c_prompt, c_sample, jnp.zeros((rows - bp - bs, D_MODEL), F32)], axis=0)
    mod = _ada(c_all, w_ada, b_ada).reshape(depth, rows, N_MOD, D_MODEL)
    bias, sink = _attn_tables(rel_bias, att_sink)
    w_in_a, w_in_b = _split_w_in(w_in)
    wts = {
        "norm_mix": norm_mix, "norm_ffn": norm_ffn, "norm_final": norm_final.reshape(1, D_MODEL),
        "w_in_a": w_in_a, "w_in_b": w_in_b,
        "hg_lb_fwd": hg_lb_fwd, "hg_lb_bwd": hg_lb_bwd, "hg_norm": hg_norm,
        "sink": sink, "bias": bias,
        "w_br_hgrn": w_br_hgrn.astype(BF16), "w_br_att": w_br_att.astype(BF16), "w_out": (0.5 * w_out).astype(BF16),
        "w_router_t": w_router.T,
        "router_bias": jnp.broadcast_to(router_bias[:, None], (N_EXPERTS, LANES)),
        "w_gate": (0.5 * w_gate).astype(BF16), "w_up": w_up.astype(BF16), "w_down": w_down.astype(BF16),
    }
    y_prompt = _trunk(x_prompt, mod[:, :bp], wts)
    y_sample = _trunk(x_sample, mod[:, bp:bp + bs], wts)
    return (y_prompt, y_sample)
```

```python
import functools

import numpy as np
import jax
import jax.numpy as jnp
from jax import lax
from jax.experimental import pallas as pl
from jax.experimental.pallas import tpu as pltpu
from jax.experimental.pallas import tpu_sc as plsc

D_MODEL = 1024
HG_DK = 128
HG_WIDTH = 512
HG_HEADS = 4
HG_SUB = 64
HG_LEVELS = 6
HG_GROUP = 32
HG_SAFE_SPAN = 80.0
HG_BLOCK = 512
ATT_HD = 64
ATT_HEADS = 8
ATT_KV = 2
ATT_GROUP = 4
ATT_WIDTH = 512
KV_WIDTH = 128
WINDOW = 128
ATT_BLOCK = 128
N_BUCKETS = 32
REL_MAX_DIST = 128
N_EXPERTS = 16
N_GROUPS = 4
EXPERTS_PER_GROUP = 4
D_EXPERT = 512
N_MOD = 6
IN_COLS = 5376
EPS = 1e-6
NEG_BIG = -1e30
TINY = 1e-30

A_COLS = 2048
A_HQ = 0
A_HF_FWD = 512
A_HF_BWD = 1024
A_HG = 1536
B_COLS = 3328
B_GATE_H = 0
B_GATE_A = 1024
B_HI = 2048
B_AQ = 2560
B_AK = 3072
B_AV = 3200

V7X_VMEM_LIMIT = 56 * 1024 * 1024
MXU_COLS = 256
LANES = 128
SUBLANES = 8

F32 = jnp.float32
BF16 = jnp.bfloat16
HIGHEST = lax.Precision.HIGHEST


def _params(sem):
    return pltpu.CompilerParams(dimension_semantics=sem, vmem_limit_bytes=V7X_VMEM_LIMIT)


def _dot(a, b):
    return jnp.dot(a, b, preferred_element_type=F32)


def _dot_nt(a, b):
    return lax.dot_general(a, b, (((1,), (1,)), ((), ())), preferred_element_type=F32)


def _dot_tn(a, b):
    return lax.dot_general(a, b, (((0,), (0,)), ((), ())), preferred_element_type=F32)


def _sigmoid(x):
    return 0.5 * jnp.tanh(0.5 * x) + 0.5


def _pack_bf16_pairs(x):
    c = x.shape[1] // 2
    bits = lax.bitcast_convert_type(x.astype(BF16).astype(F32), jnp.uint32)
    return (bits[:, :c] >> 16) | (bits[:, c:] & jnp.uint32(0xFFFF0000))


def _unpack_bf16_pairs(u):
    lo = lax.bitcast_convert_type(u << 16, F32)
    hi = lax.bitcast_convert_type(u & jnp.uint32(0xFFFF0000), F32)
    return jnp.concatenate([lo, hi], axis=1)


def _silu(x):
    return x * _sigmoid(x)


def _silu_of_half(xh):
    return xh * (jnp.tanh(xh) + 1.0)


def _rms(x, g):
    return x * lax.rsqrt(jnp.mean(x * x, axis=-1, keepdims=True) + EPS) * g


def _ada_kernel(c_ref, w_ref, b_ref, o_ref):
    c = c_ref[...]
    o_ref[...] = jnp.dot(_silu(c), w_ref[...], precision=HIGHEST, preferred_element_type=F32) + b_ref[...]


def _ada(c_all, w_ada, b_ada):
    depth = w_ada.shape[0]
    rows = c_all.shape[0]
    ncol = w_ada.shape[2]
    tn = 1024
    return pl.pallas_call(
        _ada_kernel,
        grid=(depth, ncol // tn),
        in_specs=[
            pl.BlockSpec((rows, D_MODEL), lambda l, j: (0, 0)),
            pl.BlockSpec((None, D_MODEL, tn), lambda l, j: (l, 0, j)),
            pl.BlockSpec((None, 1, tn), lambda l, j: (l, 0, j)),
        ],
        out_specs=pl.BlockSpec((None, rows, tn), lambda l, j: (l, 0, j)),
        out_shape=jax.ShapeDtypeStruct((depth, rows, ncol), F32),
        compiler_params=_params(("arbitrary", "arbitrary")),
        name="ada_mod",
    )(c_all, w_ada, b_ada.reshape(depth, 1, ncol))


def _lower_bound_row(gam_ref, layer):
    rows = [gam_ref[d:d + 1, :] for d in range(gam_ref.shape[0])]
    m = functools.reduce(jnp.maximum, rows)
    es = [jnp.exp(r - m) for r in rows]
    tot = functools.reduce(lambda a, b: a + b, es)
    ps = [e / tot for e in es]
    cum = ps[0]
    for d in range(1, layer + 1):
        cum = cum + ps[d]
    return jnp.clip(cum - ps[0], 0.0, 1.0)


def _forget(zh, lb):
    return 0.5 * (1.0 + lb) + (0.5 * (1.0 - lb)) * jnp.tanh(zh)


def _inproj_kernel(x_ref, mod_ref, g_ref, wa_ref, wb_ref, gf_ref, gb_ref, oa_ref, ob_ref, span_ref, *, layer):
    x = x_ref[...]
    h = (_rms(x, g_ref[...]) * (1.0 + mod_ref[1:2, :]) + mod_ref[0:1, :]).astype(BF16)
    oa = _dot(h, wa_ref[...])
    oa_ref[...] = oa
    ob_ref[...] = _dot(h, wb_ref[...]).astype(BF16)
    spans = []
    for gam_ref, col in ((gf_ref, A_HF_FWD), (gb_ref, A_HF_BWD)):
        f = _forget(oa[:, col:col + HG_WIDTH], _lower_bound_row(gam_ref, layer))
        g = jnp.log(jnp.maximum(f, TINY))
        gsum = jnp.sum(g.reshape(g.shape[0] // HG_GROUP, HG_GROUP, HG_WIDTH), axis=1)
        spans.append(jnp.max(jnp.max(-gsum, axis=0, keepdims=True), axis=1, keepdims=True))
    half = lax.broadcasted_iota(jnp.int32, span_ref.shape, 0) < span_ref.shape[0] // 2
    span_ref[...] = jnp.where(half, spans[0], spans[1])


def _inproj(x2d, mod, g, wa, wb, gam_f, gam_b, seq, layer):
    n = x2d.shape[0]
    tm = min(HG_BLOCK, seq)
    per_seq = seq // tm
    const = lambda i: (0, 0)
    return pl.pallas_call(
        functools.partial(_inproj_kernel, layer=layer),
        grid=(n // tm,),
        in_specs=[
            pl.BlockSpec((tm, D_MODEL), lambda i: (i, 0)),
            pl.BlockSpec((None, N_MOD, D_MODEL), lambda i: (i // per_seq, 0, 0)),
            pl.BlockSpec((1, D_MODEL), const),
            pl.BlockSpec((D_MODEL, A_COLS), const, pipeline_mode=pl.Buffered(1)),
            pl.BlockSpec((D_MODEL, B_COLS), const, pipeline_mode=pl.Buffered(1)),
            pl.BlockSpec(gam_f.shape, const),
            pl.BlockSpec(gam_b.shape, const),
        ],
        out_specs=[pl.BlockSpec((tm, A_COLS), lambda i: (i, 0)),
                   pl.BlockSpec((tm, B_COLS), lambda i: (i, 0)),
                   pl.BlockSpec((None, SUBLANES, LANES), lambda i: (i, 0, 0))],
        out_shape=[jax.ShapeDtypeStruct((n, A_COLS), F32), jax.ShapeDtypeStruct((n, B_COLS), BF16),
                   jax.ShapeDtypeStruct((n // tm, SUBLANES, LANES), F32)],
        compiler_params=_params(("arbitrary",)),
        name="inproj",
    )(x2d, mod, g, wa, wb, gam_f, gam_b)


def _hgrn_level_tables(rev):
    c = HG_SUB
    r = lax.broadcasted_iota(jnp.int32, (c, c), 0)
    s = lax.broadcasted_iota(jnp.int32, (c, c), 1)
    row = lax.broadcasted_iota(jnp.int32, (c, HG_WIDTH), 0)
    sels, qsides, pairs = [], [], []
    for lev in range(HG_LEVELS):
        half = 1 << lev
        blk = 2 * half
        r_up = (r & (blk - 1)) >= half
        s_up = (s & (blk - 1)) >= half
        base = r - (r & (blk - 1))
        mrow = base + (half if rev else half - 1)
        sels.append(jnp.where(s == mrow, 1.0, 0.0).astype(F32))
        row_up = (row & (blk - 1)) >= half
        qsides.append(~row_up if rev else row_up)
        same = (r >> (lev + 1)) == (s >> (lev + 1))
        pairs.append(same & ((~r_up & s_up) if rev else (r_up & ~s_up)))
    return jnp.concatenate(sels, axis=0), qsides, pairs, r == s


def _hgrn_kernel(safe_ref, *refs, layer, rev):
    if rev:
        q_ref, f_ref, v_ref, gam_ref, hg_ref, of_ref, nrm_ref, o_ref = refs[:8]
    else:
        q_ref, f_ref, v_ref, gam_ref, o_ref = refs[:5]
    st_ref, qs_ref, ks_ref, bs_ref, a_ref, qi_ref, dec_ref, up_ref, sb_ref = refs[-9:]

    @pl.when(pl.program_id(1) == 0)
    def _():
        st_ref[...] = jnp.zeros_like(st_ref)

    nc = pl.num_programs(1)
    chunk = (nc - 1 - pl.program_id(1)) if rev else pl.program_id(1)
    safe = safe_ref[pl.program_id(0) * nc + chunk] != 0
    c = HG_SUB
    w = HG_WIDTH
    nsub = q_ref.shape[0] // c
    lb = _lower_bound_row(gam_ref, layer)
    r_i = lax.broadcasted_iota(jnp.int32, (c, c), 0)
    s_i = lax.broadcasted_iota(jnp.int32, (c, c), 1)
    causal = (s_i >= r_i) if rev else (s_i <= r_i)
    tri = jnp.where(causal, 1.0, 0.0).astype(BF16)
    row = lax.broadcasted_iota(jnp.int32, (c, w), 0)
    far = (row < HG_GROUP) if rev else (row >= HG_GROUP)
    ref_row = HG_GROUP if rev else HG_GROUP - 1
    last_row = 0 if rev else c - 1
    heads = [slice(h * HG_DK, (h + 1) * HG_DK) for h in range(HG_HEADS)]

    def rows_of(i):
        ci = (nsub - 1 - i) if rev else i
        return pl.ds(pl.multiple_of(ci * c, c), c)

    def gates_pass(i, carry):
        sl = rows_of(i)
        zq = q_ref[sl, :]
        f = _forget(f_ref[sl, :], lb)
        g = jnp.log(jnp.maximum(f, TINY))
        qs_ref[sl, :] = _silu_of_half(zq) * (HG_DK ** -0.5)
        ks_ref[sl, :] = 1.0 - f
        g1 = g.astype(BF16)
        r1 = g - g1.astype(F32)
        g2 = r1.astype(BF16)
        g3 = (r1 - g2.astype(F32)).astype(BF16)
        bb = _dot(tri, jnp.concatenate([g1, g2, g3], axis=1))
        bs_ref[sl, :] = (bb[:, :w] + bb[:, w:2 * w]) + bb[:, 2 * w:]
        return carry

    lax.fori_loop(0, nsub, gates_pass, 0, unroll=4)

    def gates(i):
        sl = rows_of(i)
        return sl, qs_ref[sl, :], ks_ref[sl, :], bs_ref[sl, :]

    def stage(i, sl, a, q_in, k_dec, dec):
        vb = v_ref[sl, :].astype(BF16)
        for h, hs in enumerate(heads):
            a_ref[i, h] = a[h].astype(BF16)
            up_ref[i, h] = _dot_tn(vb[:, hs], k_dec[:, hs])
        qi_ref[sl, :] = q_in
        dec_ref[pl.ds(pl.multiple_of(i * SUBLANES, SUBLANES), SUBLANES), :] = jnp.broadcast_to(dec, (SUBLANES, w))

    def scan_states():
        for h, hs in enumerate(heads):
            st = st_ref[h]
            for i in range(nsub):
                sb_ref[i, h] = st.astype(BF16)
                st = st * dec_ref[i * SUBLANES:i * SUBLANES + 1, hs] + up_ref[i, h]
            st_ref[h] = st

    def finish_pass(i, carry):
        sl = rows_of(i)
        vb = v_ref[sl, :].astype(BF16)
        q_in = qi_ref[sl, :]
        outs = [_dot_nt(q_in[:, hs], sb_ref[i, h]) + _dot(a_ref[i, h], vb[:, hs]) for h, hs in enumerate(heads)]
        o_all = jnp.concatenate(outs, axis=1)
        if rev:
            tot = of_ref[sl, :] + o_all
            nrm = nrm_ref[...]
            ys = [_rms(tot[:, hs], nrm) for hs in heads]
            o_ref[sl, :] = (jnp.concatenate(ys, axis=1) * _silu_of_half(hg_ref[sl, :])).astype(o_ref.dtype)
        else:
            o_ref[sl, :] = o_all
        return carry

    def factored(i, carry):
        sl, q, k, b = gates(i)
        r = b[ref_row:ref_row + 1, :]
        bl = b[last_row:last_row + 1, :]
        rg = jnp.where(far, r, 0.0)
        qt = q * jnp.exp(b - rg)
        kt = k * jnp.exp(rg - b)
        er = jnp.exp(r)
        qn = jnp.where(far, 0.0, qt).astype(BF16)
        qf = jnp.where(far, qt, 0.0).astype(BF16)
        kc = jnp.where(far, kt, kt * er).astype(BF16)
        ktb = kt.astype(BF16)
        q_in = jnp.where(far, qt * er, qt).astype(BF16)
        k_dec = (kt * jnp.where(far, jnp.exp(bl - r), jnp.exp(bl))).astype(BF16)
        a = []
        for hs in heads:
            lhs = jnp.concatenate([qn[:, hs], qf[:, hs]], axis=1)
            rhs = jnp.concatenate([ktb[:, hs], kc[:, hs]], axis=1)
            a.append(jnp.where(causal, _dot_nt(lhs, rhs), 0.0))
        stage(i, sl, a, q_in, k_dec, jnp.exp(bl))
        return carry

    def levels(i, carry):
        sl, q, k, b = gates(i)
        sel_all, qsides, pairs, eye = _hgrn_level_tables(rev)
        bl = b[last_row:last_row + 1, :]
        q_in = (q * jnp.exp(b)).astype(BF16)
        k_dec = (k * jnp.exp(bl - b)).astype(BF16)
        bref_all = jnp.dot(sel_all, b, precision=HIGHEST, preferred_element_type=F32)
        qb = q.astype(BF16)
        kb = k.astype(BF16)
        a = [jnp.where(eye, _dot_nt(qb[:, hs], kb[:, hs]), 0.0) for hs in heads]
        for lev in range(HG_LEVELS):
            bref = bref_all[lev * c:(lev + 1) * c, :]
            qs = qsides[lev]
            x = jnp.exp(jnp.where(qs, b - bref, bref - b))
            ql = jnp.where(qs, q * x, 0.0).astype(BF16)
            kl = jnp.where(qs, 0.0, k * x).astype(BF16)
            for h, hs in enumerate(heads):
                a[h] = a[h] + jnp.where(pairs[lev], _dot_nt(ql[:, hs], kl[:, hs]), 0.0)
        stage(i, sl, a, q_in, k_dec, jnp.exp(bl))
        return carry

    @pl.when(safe)
    def _():
        lax.fori_loop(0, nsub, factored, 0, unroll=4)

    @pl.when(jnp.logical_not(safe))
    def _():
        lax.fori_loop(0, nsub, levels, 0)

    scan_states()
    lax.fori_loop(0, nsub, finish_pass, 0, unroll=4)


def _hgrn(safe, pa3, pb3, gamma, layer, rev, o_fwd=None, nrm=None):
    bsz, seq, _ = pa3.shape
    cb = min(HG_BLOCK, seq)
    assert seq % cb == 0 and cb % HG_SUB == 0
    nc = seq // cb
    wblk = HG_WIDTH

    def cmap(col):
        if rev:
            return lambda b, c, safe_ref: (b, nc - 1 - c, col // wblk)
        return lambda b, c, safe_ref: (b, c, col // wblk)

    const = lambda b, c, safe_ref: (0, 0)
    in_specs = [
        pl.BlockSpec((None, cb, wblk), cmap(A_HQ)),
        pl.BlockSpec((None, cb, wblk), cmap(A_HF_BWD if rev else A_HF_FWD)),
        pl.BlockSpec((None, cb, wblk), cmap(B_HI)),
        pl.BlockSpec(gamma.shape, const),
    ]
    args = [pa3, pa3, pb3, gamma]
    if rev:
        in_specs += [
            pl.BlockSpec((None, cb, wblk), cmap(A_HG)),
            pl.BlockSpec((None, cb, wblk), cmap(0)),
            pl.BlockSpec((1, HG_DK), const),
        ]
        args += [pa3, o_fwd, nrm]
    out_dtype = BF16 if rev else F32
    grid_spec = pltpu.PrefetchScalarGridSpec(
        num_scalar_prefetch=1,
        grid=(bsz, nc),
        in_specs=in_specs,
        out_specs=pl.BlockSpec((None, cb, wblk), cmap(0)),
        scratch_shapes=[pltpu.VMEM((HG_HEADS, HG_DK, HG_DK), F32)] + [pltpu.VMEM((cb, wblk), F32)] * 3
        + [pltpu.VMEM((cb // HG_SUB, HG_HEADS, HG_SUB, HG_SUB), BF16), pltpu.VMEM((cb, wblk), BF16),
           pltpu.VMEM((cb // HG_SUB * SUBLANES, wblk), F32),
           pltpu.VMEM((cb // HG_SUB, HG_HEADS, HG_DK, HG_DK), F32),
           pltpu.VMEM((cb // HG_SUB, HG_HEADS, HG_DK, HG_DK), BF16)],
    )
    return pl.pallas_call(
        functools.partial(_hgrn_kernel, layer=layer, rev=rev),
        grid_spec=grid_spec,
        out_shape=jax.ShapeDtypeStruct((bsz, seq, HG_WIDTH), out_dtype),
        compiler_params=_params(("arbitrary", "arbitrary")),
        name="hgrn_bwd" if rev else "hgrn_fwd",
    )(safe, *args)


ATT_ROW_HEADS = (0, 2, 1, 3)


def _t5_buckets(rel):
    half = N_BUCKETS // 2
    ret = np.where(rel > 0, half, 0)
    n = np.abs(rel)
    max_exact = half // 2
    large = max_exact + (np.log(np.maximum(n, 1) / max_exact)
                         / np.log(REL_MAX_DIST / max_exact) * (half - max_exact)).astype(np.int32)
    large = np.minimum(large, half - 1)
    return (ret + np.where(n < max_exact, n, large)).astype(np.int32)


def _attn_tables(rel_bias, att_sink):
    c = ATT_BLOCK
    rel = np.arange(3 * c)[None, :] - c - np.arange(c)[:, None]
    onehot = np.equal(_t5_buckets(rel).reshape(-1, 1), np.arange(N_BUCKETS)).astype(np.float32)
    bias = jnp.dot(jnp.asarray(onehot, BF16), rel_bias.astype(F32), precision=HIGHEST)
    bias = bias.reshape(c, 3 * c, ATT_HEADS).transpose(2, 0, 1)
    bias = jnp.where(jnp.asarray(np.abs(rel) <= WINDOW)[None], bias, NEG_BIG)
    col = np.arange(3 * c)
    cases = []
    for case in range(4):
        valid = np.ones(3 * c, bool)
        if case & 1:
            valid &= col >= c
        if case & 2:
            valid &= col < 2 * c
        cases.append(jnp.where(jnp.asarray(valid)[None, None, :], bias, NEG_BIG))
    tab = jnp.stack(cases)
    order = np.array([[ATT_GROUP * g + h for h in ATT_ROW_HEADS] for g in range(ATT_KV)])
    tab = tab[:, order].reshape(4, ATT_KV, ATT_GROUP * c, 3 * c)
    sink = att_sink.astype(F32)[:, order]
    sink = jnp.broadcast_to(sink[..., None, None], sink.shape + (c, LANES))
    return tab, sink.reshape(att_sink.shape[0], ATT_KV, ATT_GROUP * c, LANES)


def _attn_kernel(q_ref, kp_ref, kc_ref, kn_ref, vp_ref, vc_ref, vn_ref, bias_ref, sink_ref, o_ref):
    n = pl.program_id(1)
    nsteps = pl.num_programs(1)
    c = ATT_BLOCK
    nsub = q_ref.shape[0] // c
    pair = 2 * ATT_HD
    kwin = jnp.concatenate([kp_ref[...], kc_ref[...], kn_ref[...]], axis=0).astype(F32)
    vwin = jnp.concatenate([vp_ref[...], vc_ref[...], vn_ref[...]], axis=0).astype(F32)
    kroll = pltpu.roll(kwin, ATT_HD, axis=1)
    vroll = pltpu.roll(vwin, ATT_HD, axis=1)
    lo = lax.broadcasted_iota(jnp.int32, kwin.shape, 1) < ATT_HD
    k_lo = [jnp.where(lo, kwin, 0.0).astype(BF16), jnp.where(lo, kroll, 0.0).astype(BF16)]
    k_hi = [jnp.where(lo, 0.0, kroll).astype(BF16), jnp.where(lo, 0.0, kwin).astype(BF16)]
    v_lo = [jnp.where(lo, vwin, 1.0).astype(BF16), jnp.where(lo, vroll, 1.0).astype(BF16)]
    v_hi = [jnp.where(lo, 1.0, vroll).astype(BF16), jnp.where(lo, 1.0, vwin).astype(BF16)]
    lo_out = lax.broadcasted_iota(jnp.int32, (2 * c, pair), 1) < ATT_HD
    for j in range(nsub):
        case = jnp.int32(0)
        if j == 0:
            case = case + (n == 0).astype(jnp.int32)
        if j == nsub - 1:
            case = case + 2 * (n == nsteps - 1).astype(jnp.int32)
        band = slice(j * c, (j + 3) * c)
        qrows = slice(j * c, (j + 1) * c)
        for g in range(ATT_KV):
            qg = q_ref[qrows, g * 2 * pair:(g + 1) * 2 * pair].astype(F32) * (ATT_HD ** -0.5)
            lhs = jnp.concatenate([qg[:, :pair], qg[:, pair:]], axis=0).astype(BF16)
            rhs = jnp.concatenate([k_lo[g][band], k_hi[g][band]], axis=0)
            lg = _dot_nt(lhs, rhs)
            logits = jnp.concatenate([lg[:, :3 * c], lg[:, 3 * c:]], axis=0) + bias_ref[case, g]
            s = sink_ref[g]
            m = jnp.maximum(jnp.max(logits, axis=-1, keepdims=True), s)
            p = jnp.exp(logits - jnp.concatenate([m] * 3, axis=1)).astype(BF16)
            es = jnp.exp(s - m)
            o_even = _dot(p[:2 * c], v_lo[g][band])
            o_odd = _dot(p[2 * c:], v_hi[g][band])
            o = jnp.where(lo_out,
                          o_even / (pltpu.roll(o_even, ATT_HD, axis=1) + es[:2 * c]),
                          o_odd / (pltpu.roll(o_odd, ATT_HD, axis=1) + es[2 * c:]))
            o_ref[qrows, g * 2 * pair:g * 2 * pair + pair] = o[:c].astype(o_ref.dtype)
            o_ref[qrows, g * 2 * pair + pair:(g + 1) * 2 * pair] = o[c:].astype(o_ref.dtype)


def _attn(proj3, bias, sink):
    bsz, seq, _ = proj3.shape
    c = ATT_BLOCK
    nsub = max(k for k in (4, 2, 1) if seq % (k * c) == 0)
    qb = nsub * c
    nsteps = seq // qb
    assert seq % c == 0
    nb = seq // c
    kcol = B_AK // KV_WIDTH
    vcol = B_AV // KV_WIDTH

    def band(col):
        return [
            pl.BlockSpec((None, c, KV_WIDTH), lambda b, n: (b, jnp.maximum(n * nsub - 1, 0), col)),
            pl.BlockSpec((None, qb, KV_WIDTH), lambda b, n: (b, n, col)),
            pl.BlockSpec((None, c, KV_WIDTH), lambda b, n: (b, jnp.minimum((n + 1) * nsub, nb - 1), col)),
        ]

    return pl.pallas_call(
        _attn_kernel,
        grid=(bsz, nsteps),
        in_specs=[pl.BlockSpec((None, qb, ATT_WIDTH), lambda b, n: (b, n, B_AQ // ATT_WIDTH))]
        + band(kcol) + band(vcol)
        + [pl.BlockSpec(bias.shape, lambda b, n: (0, 0, 0, 0)),
           pl.BlockSpec(sink.shape, lambda b, n: (0, 0, 0))],
        out_specs=pl.BlockSpec((None, qb, ATT_WIDTH), lambda b, n: (b, n, 0)),
        out_shape=jax.ShapeDtypeStruct((bsz, seq, ATT_WIDTH), BF16),
        compiler_params=_params(("arbitrary", "arbitrary")),
        name="window_attn",
    )(proj3, proj3, proj3, proj3, proj3, proj3, proj3, bias, sink)


def _first_argmax(vals):
    best, idx = vals[0], jnp.zeros(vals[0].shape, jnp.int32)
    for j in range(1, len(vals)):
        upd = vals[j] > best
        idx = jnp.where(upd, j, idx)
        best = jnp.where(upd, vals[j], best)
    return best, idx


def _select(vals, idx):
    out = vals[0]
    for j in range(1, len(vals)):
        out = jnp.where(idx == j, vals[j], out)
    return out


def _route(logits_t, rbias):
    m = jnp.max(logits_t, axis=0, keepdims=True)
    e = jnp.exp(logits_t - m)
    scores = e / jnp.sum(e, axis=0, keepdims=True)
    sel = scores + rbias
    srow = [scores[i:i + 1, :] for i in range(N_EXPERTS)]
    lrow = [sel[i:i + 1, :] for i in range(N_EXPERTS)]
    gscore = []
    for g in range(N_GROUPS):
        a, b, c, d = lrow[4 * g:4 * g + 4]
        hi1, lo1 = jnp.maximum(a, b), jnp.minimum(a, b)
        hi2, lo2 = jnp.maximum(c, d), jnp.minimum(c, d)
        gscore.append(jnp.maximum(hi1, hi2) + jnp.maximum(jnp.minimum(hi1, hi2), jnp.maximum(lo1, lo2)))
    _, gi = _first_argmax(gscore)
    ing = [_select([lrow[4 * g + j] for g in range(N_GROUPS)], gi) for j in range(EXPERTS_PER_GROUP)]
    sg = [_select([srow[4 * g + j] for g in range(N_GROUPS)], gi) for j in range(EXPERTS_PER_GROUP)]
    _, i1 = _first_argmax(ing)
    rest = [jnp.where(i1 == j, -jnp.inf, ing[j]) for j in range(EXPERTS_PER_GROUP)]
    _, i2 = _first_argmax(rest)
    s1, s2 = _select(sg, i1), _select(sg, i2)
    tot = s1 + s2
    w1, w2 = s1 / tot, s2 / tot
    idx1 = gi * EXPERTS_PER_GROUP + i1
    idx2 = gi * EXPERTS_PER_GROUP + i2
    return idx1, idx2, w1, w2


def _merge_kernel(oh_ref, oa_ref, gh_ref, ga_ref, x_ref, mod_ref, nf_ref, wbh_ref, wba_ref, wo_ref,
                  wrt_ref, rb_ref, before_ref, x1_ref, h2_ref, ri_ref, rw_ref, cnt_ref, carry_ref, mg_ref):
    @pl.when(pl.program_id(0) == 0)
    def _():
        carry_ref[...] = jnp.zeros_like(carry_ref)

    tm = x_ref.shape[0]
    blocks = [slice(j * MXU_COLS, (j + 1) * MXU_COLS) for j in range(D_MODEL // MXU_COLS)]
    for cols in blocks:
        mh = _dot(oh_ref[...], wbh_ref[:, cols])
        ma = _dot(oa_ref[...], wba_ref[:, cols])
        mg_ref[:, cols] = ((jnp.tanh(gh_ref[:, cols].astype(F32)) + 1.0) * mh
                           + (jnp.tanh(ga_ref[:, cols].astype(F32)) + 1.0) * ma).astype(BF16)
    ssq = jnp.zeros((tm, 1), F32)
    for cols in blocks:
        x1 = x_ref[:, cols] + mod_ref[2:3, cols] * _dot(mg_ref[...], wo_ref[:, cols])
        x1_ref[:, cols] = x1
        ssq = ssq + jnp.sum(x1 * x1, axis=-1, keepdims=True)
    inv = lax.rsqrt(ssq * (1.0 / D_MODEL) + EPS)
    wr = wrt_ref[...]
    w_hi = wr.astype(BF16)
    w_lo = (wr - w_hi.astype(F32)).astype(BF16)
    logits_t = jnp.zeros((N_EXPERTS, tm), F32)
    half = len(blocks) // 2
    for j in range(half):
        pair = []
        for cols in (blocks[j], blocks[half + j]):
            h2 = x1_ref[:, cols] * inv * nf_ref[:, cols] * (1.0 + mod_ref[4:5, cols]) + mod_ref[3:4, cols]
            h_hi = h2.astype(BF16)
            h_lo = (h2 - h_hi.astype(F32)).astype(BF16)
            logits_t = logits_t + ((_dot_nt(w_hi[:, cols], h_hi) + _dot_nt(w_hi[:, cols], h_lo))
                                   + _dot_nt(w_lo[:, cols], h_hi))
            pair.append(h2)
        h2_ref[:, blocks[j]] = _pack_bf16_pairs(jnp.concatenate(pair, axis=1))
    idx1, idx2, w1, w2 = _route(logits_t, rb_ref[:, 0:1])
    erow = lax.broadcasted_iota(jnp.int32, logits_t.shape, 0)
    oh1 = erow == idx1
    oh2 = erow == idx2
    oh = jnp.where(oh1 | oh2, 1.0, 0.0)
    carry = carry_ref[...]
    pref = _dot(oh.astype(BF16), before_ref[...]) + carry[:, 0:1]
    rank1 = jnp.sum(jnp.where(oh1, pref, 0.0), axis=0, keepdims=True)
    rank2 = jnp.sum(jnp.where(oh2, pref, 0.0), axis=0, keepdims=True)
    carry = carry + jnp.sum(oh, axis=1, keepdims=True)
    carry_ref[...] = carry
    cnt_ref[...] = carry
    ri_ref[...] = jnp.concatenate([idx1, idx2, rank1.astype(jnp.int32), rank2.astype(jnp.int32)], axis=0)
    rw_ref[...] = jnp.concatenate([w1, w2], axis=0)


def _merge(o_h, o_a, proj, x2d, mod, nf, wbh, wba, wo, wrt, rb, seq):
    n = x2d.shape[0]
    tm = min(1024, seq)
    per_seq = seq // tm
    const = lambda i: (0, 0)
    before = jnp.asarray(np.arange(tm)[:, None] < np.arange(tm)[None, :], BF16)
    return pl.pallas_call(
        _merge_kernel,
        grid=(n // tm,),
        in_specs=[
            pl.BlockSpec((tm, HG_WIDTH), lambda i: (i, 0)),
            pl.BlockSpec((tm, ATT_WIDTH), lambda i: (i, 0)),
            pl.BlockSpec((tm, D_MODEL), lambda i: (i, B_GATE_H // D_MODEL)),
            pl.BlockSpec((tm, D_MODEL), lambda i: (i, B_GATE_A // D_MODEL)),
            pl.BlockSpec((tm, D_MODEL), lambda i: (i, 0)),
            pl.BlockSpec((None, N_MOD, D_MODEL), lambda i: (i // per_seq, 0, 0)),
            pl.BlockSpec((1, D_MODEL), const),
            pl.BlockSpec(wbh.shape, const),
            pl.BlockSpec(wba.shape, const),
            pl.BlockSpec(wo.shape, const),
            pl.BlockSpec(wrt.shape, const),
            pl.BlockSpec(rb.shape, const),
            pl.BlockSpec(before.shape, const),
        ],
        out_specs=[
            pl.BlockSpec((tm, D_MODEL), lambda i: (i, 0)),
            pl.BlockSpec((tm, D_MODEL // 2), lambda i: (i, 0)),
            pl.BlockSpec((4, tm), lambda i: (0, i)),
            pl.BlockSpec((2, tm), lambda i: (0, i)),
            pl.BlockSpec((N_EXPERTS, LANES), const),
        ],
        out_shape=[
            jax.ShapeDtypeStruct((n, D_MODEL), F32),
            jax.ShapeDtypeStruct((n, D_MODEL // 2), jnp.uint32),
            jax.ShapeDtypeStruct((4, n), jnp.int32),
            jax.ShapeDtypeStruct((2, n), F32),
            jax.ShapeDtypeStruct((N_EXPERTS, LANES), F32),
        ],
        scratch_shapes=[pltpu.VMEM((N_EXPERTS, LANES), F32), pltpu.VMEM((tm, D_MODEL), BF16)],
        compiler_params=_params(("arbitrary",)),
        name="merge_router",
    )(o_h, o_a, proj, proj, x2d, mod, nf, wbh, wba, wo, wrt, rb, before)


MOE_TILE = 512
SC_WINDOW = 64


def _sc_mesh():
    return plsc.VectorSubcoreMesh(core_axis_name="c", subcore_axis_name="s")


def _sc_dispatch(h, pos, p):
    n, d = h.shape
    win = SC_WINDOW
    info = plsc.get_sparse_core_info()
    workers = info.num_cores * info.num_subcores
    assert n % (2 * win * workers) == 0, "each vector subcore walks its windows two at a time"
    wpt = n // (win * workers)
    pos_w = pos.reshape(2, n // win, win).transpose(1, 0, 2)

    @functools.partial(
        pl.kernel, out_type=jax.ShapeDtypeStruct((p, d), h.dtype), mesh=_sc_mesh(),
        scratch_types=[pltpu.VMEM((wpt, 2, win), jnp.int32), pltpu.VMEM((2, win, d), h.dtype),
                       pltpu.SemaphoreType.DMA((2,))],
        name="moe_dispatch")
    def dispatch(h_hbm, pos_hbm, o_hbm, idx_v, rows_v, load_sem):
        wid = lax.axis_index("c") * info.num_subcores + lax.axis_index("s")
        first = wid * wpt
        pltpu.sync_copy(pos_hbm.at[pl.ds(first, wpt)], idx_v)

        def load(j, slot):
            return pltpu.make_async_copy(h_hbm.at[pl.ds((first + j) * win, win)], rows_v.at[slot],
                                         load_sem.at[slot])

        load(0, 0).start()

        @pl.loop(0, wpt, step=2)
        def _(j):
            for slot in range(2):
                jj = j + slot
                load(jj, slot).wait()

                @pl.when(jj + 1 < wpt)
                def _():
                    load(jj + 1, 1 - slot).start()

                pltpu.sync_copy(rows_v.at[slot], o_hbm.at[idx_v.at[jj, 0]])
                pltpu.sync_copy(rows_v.at[slot], o_hbm.at[idx_v.at[jj, 1]])

    return dispatch(h, pos_w)


def _sc_combine(ys, pos):
    n = pos.shape[1]
    d = ys.shape[1]
    win = SC_WINDOW // 2
    info = plsc.get_sparse_core_info()
    workers = info.num_cores * info.num_subcores
    assert n % (2 * win * workers) == 0, "each vector subcore walks its windows two at a time"
    wpt = n // (win * workers)
    pos_w = pos.reshape(2, n // win, win).transpose(1, 0, 2)
    out =jax.ShapeDtypeStruct((n, d), ys.dtype)

    @functools.partial(
        pl.kernel, out_type=(out, out), mesh=_sc_mesh(),
        scratch_types=[pltpu.VMEM((wpt, 2, win), jnp.int32), pltpu.VMEM((2, 2, win, d), ys.dtype),
                       pltpu.SemaphoreType.DMA((2, 2)), pltpu.SemaphoreType.DMA((2,))],
        name="moe_combine")
    def combine(ys_hbm, pos_hbm, a_hbm, b_hbm, idx_v, rows_v, gather_sem, store_sem):
        wid = lax.axis_index("c") * info.num_subcores + lax.axis_index("s")
        first = wid * wpt
        pltpu.sync_copy(pos_hbm.at[pl.ds(first, wpt)], idx_v)
        outs = (a_hbm, b_hbm)

        def gather(j, slot, k):
            return pltpu.make_async_copy(ys_hbm.at[idx_v.at[j, k]], rows_v.at[slot, k], gather_sem.at[slot, k])

        def store(j, slot, k):
            return pltpu.make_async_copy(rows_v.at[slot, k], outs[k].at[pl.ds((first + j) * win, win)],
                                         store_sem.at[k])

        gather(0, 0, 0).start()
        gather(0, 0, 1).start()

        @pl.loop(0, wpt, step=2)
        def _(j):
            for slot in range(2):
                jj = j + slot
                gather(jj, slot, 0).wait()
                gather(jj, slot, 1).wait()

                @pl.when(jj + 1 < wpt)
                def _():
                    gather(jj + 1, 1 - slot, 0).start()
                    gather(jj + 1, 1 - slot, 1).start()

                store(jj, slot, 0).start()
                store(jj, slot, 1).start()
                store(jj, slot, 0).wait()
                store(jj, slot, 1).wait()

    return combine(ys, pos_w)


def _expert_kernel(te_ref, nu_ref, x_ref, wg_ref, wu_ref, wd_ref, o_ref):
    del te_ref
    used = pl.program_id(0) < nu_ref[0]

    @pl.when(used)
    def _():
        x = _unpack_bf16_pairs(x_ref[...]).astype(BF16)
        he = _silu_of_half(_dot(x, wg_ref[...])) * _dot(x, wu_ref[...])
        o_ref[...] = _pack_bf16_pairs(_dot(he.astype(BF16), wd_ref[...]))

    @pl.when(jnp.logical_not(used))
    def _():
        o_ref[...] = jnp.zeros_like(o_ref)


def _experts(xs, tile_expert, n_used, wg, wu, wd, layer):
    p = xs.shape[0]
    tm = MOE_TILE
    grid_spec = pltpu.PrefetchScalarGridSpec(
        num_scalar_prefetch=2,
        grid=(p // tm,),
        in_specs=[
            pl.BlockSpec((tm, D_MODEL // 2), lambda i, te, nu: (i, 0)),
            pl.BlockSpec((None, None, D_MODEL, D_EXPERT), lambda i, te, nu: (layer, te[i], 0, 0)),
            pl.BlockSpec((None, None, D_MODEL, D_EXPERT), lambda i, te, nu: (layer, te[i], 0, 0)),
            pl.BlockSpec((None, None, D_EXPERT, D_MODEL), lambda i, te, nu: (layer, te[i], 0, 0)),
        ],
        out_specs=pl.BlockSpec((tm, D_MODEL // 2), lambda i, te, nu: (i, 0)),
    )
    return pl.pallas_call(
        _expert_kernel,
        grid_spec=grid_spec,
        out_shape=jax.ShapeDtypeStruct((p, D_MODEL // 2), jnp.uint32),
        compiler_params=_params(("arbitrary",)),
        name="moe_experts",
    )(tile_expert, n_used, xs, wg, wu, wd)


def _residual_kernel(ya_ref, yb_ref, w_ref, x1_ref, mod_ref, nfin_ref, o_ref, *, last):
    w = w_ref[...]
    y = w[:, 0:1] * _unpack_bf16_pairs(ya_ref[...]) + w[:, 1:2] * _unpack_bf16_pairs(yb_ref[...])
    x2 = x1_ref[...] + mod_ref[5:6, :] * y
    if last:
        x2 = _rms(x2, nfin_ref[...])
    o_ref[...] = x2


def _residual(ya, yb, w, x1, mod, nfin, seq, last):
    n = x1.shape[0]
    tm = min(1024, seq)
    per_seq = seq // tm
    row = pl.BlockSpec((tm, D_MODEL), lambda i: (i, 0))
    packed = pl.BlockSpec((tm, D_MODEL // 2), lambda i: (i, 0))
    return pl.pallas_call(
        functools.partial(_residual_kernel, last=last),
        grid=(n // tm,),
        in_specs=[packed, packed, pl.BlockSpec((tm, 2), lambda i: (i, 0)), row,
                  pl.BlockSpec((None, N_MOD, D_MODEL), lambda i: (i // per_seq, 0, 0)),
                  pl.BlockSpec((1, D_MODEL), lambda i: (0, 0))],
        out_specs=row,
        out_shape=jax.ShapeDtypeStruct((n, D_MODEL), F32),
        compiler_params=_params(("arbitrary",)),
        name="moe_residual",
    )(ya, yb, w, x1, mod, nfin)


def _moe(h2, ri, rw, cnt, wg, wu, wd, layer, x1, mod, nfin, seq, last):
    n = h2.shape[0]
    tm = MOE_TILE
    p = 2 * n + N_EXPERTS * tm
    counts = cnt[:, 0].astype(jnp.int32)
    padded = (counts + tm - 1) // tm * tm
    ends = jnp.cumsum(padded)
    starts = ends - padded
    base = jnp.zeros_like(ri[0:2])
    for e in range(1, N_EXPERTS):
        base = jnp.where(ri[0:2] == e, starts[e], base)
    pos = base + ri[2:4]
    tile_start = jnp.arange(p // tm, dtype=jnp.int32) * tm
    tile_expert = jnp.minimum(jnp.sum(tile_start[:, None] >= ends[None, :], axis=1), N_EXPERTS - 1)
    n_used = (ends[-1:] // tm).astype(jnp.int32)
    xs = _sc_dispatch(h2, pos, p)
    ys = _experts(xs, tile_expert.astype(jnp.int32), n_used, wg, wu, wd, layer)
    ya, yb = _sc_combine(ys, pos)
    return _residual(ya, yb, rw.T, x1, mod, nfin, seq, last)


def _split_w_in(w):
    hq_hf, hi, hg, att, gates = w[..., :1536], w[..., 1536:2048], w[..., 2048:2560], w[..., 2560:3328], w[..., 3328:]
    wa = (0.5 * jnp.concatenate([hq_hf, hg], axis=-1)).astype(BF16)
    wb = jnp.concatenate([0.5 * gates, hi, att], axis=-1).astype(BF16)
    return wa, wb


def _trunk(x, mod, wts):
    bsz, seq, _ = x.shape
    n = bsz * seq
    depth = wts["w_in_a"].shape[0]
    x2d = x.reshape(n, D_MODEL)
    for l in range(depth):
        mod_l = mod[l]
        pa, pb, span = _inproj(x2d, mod_l, wts["norm_mix"][l:l + 1], wts["w_in_a"][l], wts["w_in_b"][l],
                               wts["hg_lb_fwd"], wts["hg_lb_bwd"], seq, l)
        pa3 = pa.reshape(bsz, seq, A_COLS)
        pb3 = pb.reshape(bsz, seq, B_COLS)
        safe_f = (span[:, 0, 0] <= HG_SAFE_SPAN).astype(jnp.int32)
        safe_b = (span[:, 4, 0] <= HG_SAFE_SPAN).astype(jnp.int32)
        o_f = _hgrn(safe_f, pa3, pb3, wts["hg_lb_fwd"], l, False)
        o_h = _hgrn(safe_b, pa3, pb3, wts["hg_lb_bwd"], l, True, o_f, wts["hg_norm"][l:l + 1])
        o_a = _attn(pb3, wts["bias"], wts["sink"][l])
        x1, h2, ri, rw, cnt = _merge(o_h.reshape(n, HG_WIDTH), o_a.reshape(n, ATT_WIDTH), pb, x2d, mod_l,
                                     wts["norm_ffn"][l:l + 1], wts["w_br_hgrn"][l], wts["w_br_att"][l],
                                     wts["w_out"][l], wts["w_router_t"], wts["router_bias"], seq)
        x2d = _moe(h2, ri, rw, cnt, wts["w_gate"], wts["w_up"], wts["w_down"], l, x1, mod_l,
                   wts["norm_final"], seq, l == depth - 1)
    return x2d.reshape(bsz, seq, D_MODEL)


def kernel(x_prompt, x_sample, c_prompt, c_sample, w_ada, b_ada, norm_mix, norm_ffn, norm_final, w_in, hg_lb_fwd, hg_lb_bwd, hg_norm, att_sink, rel_bias, w_br_hgrn, w_br_att, w_out, w_router, router_bias, w_gate, w_up, w_down):
    depth = w_in.shape[0]
    bp, bs = c_prompt.shape[0], c_sample.shape[0]
    rows = -(-(bp + bs) // SUBLANES) * SUBLANES
    c_all = jnp.concatenate([c_prompt, c_sample, jnp.zeros((rows - bp - bs, D_MODEL), F32)], axis=0)
    mod = _ada(c_all, w_ada, b_ada).reshape(depth, rows, N_MOD, D_MODEL)
    bias, sink = _attn_tables(rel_bias, att_sink)
    w_in_a, w_in_b = _split_w_in(w_in)
    wts = {
        "norm_mix": norm_mix, "norm_ffn": norm_ffn, "norm_final": norm_final.reshape(1, D_MODEL),
        "w_in_a": w_in_a, "w_in_b": w_in_b,
        "hg_lb_fwd": hg_lb_fwd, "hg_lb_bwd": hg_lb_bwd, "hg_norm": hg_norm,
        "sink": sink, "bias": bias,
        "w_br_hgrn": w_br_hgrn.astype(BF16), "w_br_att": w_br_att.astype(BF16), "w_out": (0.5 * w_out).astype(BF16),
        "w_router_t": w_router.T,
        "router_bias": jnp.broadcast_to(router_bias[:, None], (N_EXPERTS, LANES)),
        "w_gate": (0.5 * w_gate).astype(BF16), "w_up": w_up.astype(BF16), "w_down": w_down.astype(BF16),
    }
    y_prompt = _trunk(x_prompt, mod[:, :bp], wts)
    y_sample = _trunk(x_sample, mod[:, bp:bp + bs], wts)
    return (y_prompt, y_sample)
```

```python
import functools

import numpy as np
import jax
import jax.numpy as jnp
from jax import lax
from jax.experimental import pallas as pl
from jax.experimental.pallas import tpu as pltpu
from jax.experimental.pallas import tpu_sc as plsc

D_MODEL = 1024
HG_DK = 128
HG_WIDTH = 512
HG_HEADS = 4
HG_SUB = 64
HG_LEVELS = 6
HG_GROUP = 32
HG_SAFE_SPAN = 80.0
HG_BLOCK = 512
ATT_HD = 64
ATT_HEADS = 8
ATT_KV = 2
ATT_GROUP = 4
ATT_WIDTH = 512
KV_WIDTH = 128
WINDOW = 128
ATT_BLOCK = 128
N_BUCKETS = 32
REL_MAX_DIST = 128
N_EXPERTS = 16
N_GROUPS = 4
EXPERTS_PER_GROUP = 4
D_EXPERT = 512
N_MOD = 6
IN_COLS = 5376
EPS = 1e-6
NEG_BIG = -1e30
TINY = 1e-30

A_COLS = 2048
A_HQ = 0
A_HF_FWD = 512
A_HF_BWD = 1024
A_HG = 1536
B_COLS = 3328
B_GATE_H = 0
B_GATE_A = 1024
B_HI = 2048
B_AQ = 2560
B_AK = 3072
B_AV = 3200

V7X_VMEM_LIMIT = 56 * 1024 * 1024
MXU_COLS = 256
LANES = 128
SUBLANES = 8

F32 = jnp.float32
BF16 = jnp.bfloat16
HIGHEST = lax.Precision.HIGHEST


def _params(sem):
    return pltpu.CompilerParams(dimension_semantics=sem, vmem_limit_bytes=V7X_VMEM_LIMIT)


def _dot(a, b):
    return jnp.dot(a, b, preferred_element_type=F32)


def _dot_nt(a, b):
    return lax.dot_general(a, b, (((1,), (1,)), ((), ())), preferred_element_type=F32)


def _dot_tn(a, b):
    return lax.dot_general(a, b, (((0,), (0,)), ((), ())), preferred_element_type=F32)


def _sigmoid(x):
    return 0.5 * jnp.tanh(0.5 * x) + 0.5


def _pack_bf16_pairs(x):
    c = x.shape[1] // 2
    bits = lax.bitcast_convert_type(x.astype(BF16).astype(F32), jnp.uint32)
    return (bits[:, :c] >> 16) | (bits[:, c:] & jnp.uint32(0xFFFF0000))


def _unpack_bf16_pairs(u):
    lo = lax.bitcast_convert_type(u << 16, F32)
    hi = lax.bitcast_convert_type(u & jnp.uint32(0xFFFF0000), F32)
    return jnp.concatenate([lo, hi], axis=1)


def _silu(x):
    return x * _sigmoid(x)


def _silu_of_half(xh):
    return xh * (jnp.tanh(xh) + 1.0)


def _rms(x, g):
    return x * lax.rsqrt(jnp.mean(x * x, axis=-1, keepdims=True) + EPS) * g


def _ada_kernel(c_ref, w_ref, b_ref, o_ref):
    c = c_ref[...]
    o_ref[...] = jnp.dot(_silu(c), w_ref[...], precision=HIGHEST, preferred_element_type=F32) + b_ref[...]


def _ada(c_all, w_ada, b_ada):
    depth = w_ada.shape[0]
    rows = c_all.shape[0]
    ncol = w_ada.shape[2]
    tn = 1024
    return pl.pallas_call(
        _ada_kernel,
        grid=(depth, ncol // tn),
        in_specs=[
            pl.BlockSpec((rows, D_MODEL), lambda l, j: (0, 0)),
            pl.BlockSpec((None, D_MODEL, tn), lambda l, j: (l, 0, j)),
            pl.BlockSpec((None, 1, tn), lambda l, j: (l, 0, j)),
        ],
        out_specs=pl.BlockSpec((None, rows, tn), lambda l, j: (l, 0, j)),
        out_shape=jax.ShapeDtypeStruct((depth, rows, ncol), F32),
        compiler_params=_params(("arbitrary", "arbitrary")),
        name="ada_mod",
    )(c_all, w_ada, b_ada.reshape(depth, 1, ncol))


def _lower_bound_row(gam_ref, layer):
    rows = [gam_ref[d:d + 1, :] for d in range(gam_ref.shape[0])]
    m = functools.reduce(jnp.maximum, rows)
    es = [jnp.exp(r - m) for r in rows]
    tot = functools.reduce(lambda a, b: a + b, es)
    ps = [e / tot for e in es]
    cum = ps[0]
    for d in range(1, layer + 1):
        cum = cum + ps[d]
    return jnp.clip(cum - ps[0], 0.0, 1.0)


def _forget(zh, lb):
    return 0.5 * (1.0 + lb) + (0.5 * (1.0 - lb)) * jnp.tanh(zh)


def _inproj_kernel(x_ref, mod_ref, g_ref, wa_ref, wb_ref, gf_ref, gb_ref, oa_ref, ob_ref, span_ref, *, layer):
    x = x_ref[...]
    h = (_rms(x, g_ref[...]) * (1.0 + mod_ref[1:2, :]) + mod_ref[0:1, :]).astype(BF16)
    oa = _dot(h, wa_ref[...])
    oa_ref[...] = oa
    ob_ref[...] = _dot(h, wb_ref[...]).astype(BF16)
    spans = []
    for gam_ref, col in ((gf_ref, A_HF_FWD), (gb_ref, A_HF_BWD)):
        f = _forget(oa[:, col:col + HG_WIDTH], _lower_bound_row(gam_ref, layer))
        g = jnp.log(jnp.maximum(f, TINY))
        gsum = jnp.sum(g.reshape(g.shape[0] // HG_GROUP, HG_GROUP, HG_WIDTH), axis=1)
        spans.append(jnp.max(jnp.max(-gsum, axis=0, keepdims=True), axis=1, keepdims=True))
    half = lax.broadcasted_iota(jnp.int32, span_ref.shape, 0) < span_ref.shape[0] // 2
    span_ref[...] = jnp.where(half, spans[0], spans[1])


def _inproj(x2d, mod, g, wa, wb, gam_f, gam_b, seq, layer):
    n = x2d.shape[0]
    tm = min(HG_BLOCK, seq)
    per_seq = seq // tm
    const = lambda i: (0, 0)
    return pl.pallas_call(
        functools.partial(_inproj_kernel, layer=layer),
        grid=(n // tm,),
        in_specs=[
            pl.BlockSpec((tm, D_MODEL), lambda i: (i, 0)),
            pl.BlockSpec((None, N_MOD, D_MODEL), lambda i: (i // per_seq, 0, 0)),
            pl.BlockSpec((1, D_MODEL), const),
            pl.BlockSpec((D_MODEL, A_COLS), const, pipeline_mode=pl.Buffered(1)),
            pl.BlockSpec((D_MODEL, B_COLS), const, pipeline_mode=pl.Buffered(1)),
            pl.BlockSpec(gam_f.shape, const),
            pl.BlockSpec(gam_b.shape, const),
        ],
        out_specs=[pl.BlockSpec((tm, A_COLS), lambda i: (i, 0)),
                   pl.BlockSpec((tm, B_COLS), lambda i: (i, 0)),
                   pl.BlockSpec((None, SUBLANES, LANES), lambda i: (i, 0, 0))],
        out_shape=[jax.ShapeDtypeStruct((n, A_COLS), F32), jax.ShapeDtypeStruct((n, B_COLS), BF16),
                   jax.ShapeDtypeStruct((n // tm, SUBLANES, LANES), F32)],
        compiler_params=_params(("arbitrary",)),
        name="inproj",
    )(x2d, mod, g, wa, wb, gam_f, gam_b)


def _hgrn_level_tables(rev):
    c = HG_SUB
    r = lax.broadcasted_iota(jnp.int32, (c, c), 0)
    s = lax.broadcasted_iota(jnp.int32, (c, c), 1)
    row = lax.broadcasted_iota(jnp.int32, (c, HG_WIDTH), 0)
    sels, qsides, pairs = [], [], []
    for lev in range(HG_LEVELS):
        half = 1 << lev
        blk = 2 * half
        r_up = (r & (blk - 1)) >= half
        s_up = (s & (blk - 1)) >= half
        base = r - (r & (blk - 1))
        mrow = base + (half if rev else half - 1)
        sels.append(jnp.where(s == mrow, 1.0, 0.0).astype(F32))
        row_up = (row & (blk - 1)) >= half
        qsides.append(~row_up if rev else row_up)
        same = (r >> (lev + 1)) == (s >> (lev + 1))
        pairs.append(same & ((~r_up & s_up) if rev else (r_up & ~s_up)))
    return jnp.concatenate(sels, axis=0), qsides, pairs, r == s


def _hgrn_kernel(safe_ref, *refs, layer, rev):
    if rev:
        q_ref, f_ref, v_ref, gam_ref, hg_ref, of_ref, nrm_ref, o_ref = refs[:8]
    else:
        q_ref, f_ref, v_ref, gam_ref, o_ref = refs[:5]
    st_ref, qs_ref, ks_ref, bs_ref, a_ref, qi_ref, dec_ref, up_ref, sb_ref = refs[-9:]

    @pl.when(pl.program_id(1) == 0)
    def _():
        st_ref[...] = jnp.zeros_like(st_ref)

    nc = pl.num_programs(1)
    chunk = (nc - 1 - pl.program_id(1)) if rev else pl.program_id(1)
    safe = safe_ref[pl.program_id(0) * nc + chunk] != 0
    c = HG_SUB
    w = HG_WIDTH
    nsub = q_ref.shape[0] // c
    lb = _lower_bound_row(gam_ref, layer)
    r_i = lax.broadcasted_iota(jnp.int32, (c, c), 0)
    s_i = lax.broadcasted_iota(jnp.int32, (c, c), 1)
    causal = (s_i >= r_i) if rev else (s_i <= r_i)
    tri = jnp.where(causal, 1.0, 0.0).astype(BF16)
    row = lax.broadcasted_iota(jnp.int32, (c, w), 0)
    far = (row < HG_GROUP) if rev else (row >= HG_GROUP)
    ref_row = HG_GROUP if rev else HG_GROUP - 1
    last_row = 0 if rev else c - 1
    heads = [slice(h * HG_DK, (h + 1) * HG_DK) for h in range(HG_HEADS)]

    def rows_of(i):
        ci = (nsub - 1 - i) if rev else i
        return pl.ds(pl.multiple_of(ci * c, c), c)

    def gates_pass(i, carry):
        sl = rows_of(i)
        zq = q_ref[sl, :]
        f = _forget(f_ref[sl, :], lb)
        g = jnp.log(jnp.maximum(f, TINY))
        qs_ref[sl, :] = _silu_of_half(zq) * (HG_DK ** -0.5)
        ks_ref[sl, :] = 1.0 - f
        g1 = g.astype(BF16)
        r1 = g - g1.astype(F32)
        g2 = r1.astype(BF16)
        g3 = (r1 - g2.astype(F32)).astype(BF16)
        bb = _dot(tri, jnp.concatenate([g1, g2, g3], axis=1))
        bs_ref[sl, :] = (bb[:, :w] + bb[:, w:2 * w]) + bb[:, 2 * w:]
        return carry

    lax.fori_loop(0, nsub, gates_pass, 0, unroll=8)

    def gates(i):
        sl = rows_of(i)
        return sl, qs_ref[sl, :], ks_ref[sl, :], bs_ref[sl, :]

    def stage(i, sl, a, q_in, k_dec, dec):
        vb = v_ref[sl, :].astype(BF16)
        for h, hs in enumerate(heads):
            a_ref[i, h] = a[h].astype(BF16)
            up_ref[i, h] = _dot_tn(vb[:, hs], k_dec[:, hs])
        qi_ref[sl, :] = q_in
        dec_ref[pl.ds(pl.multiple_of(i * SUBLANES, SUBLANES), SUBLANES), :] = jnp.broadcast_to(dec, (SUBLANES, w))

    def scan_states():
        for h, hs in enumerate(heads):
            st = st_ref[h]
            for i in range(nsub):
                sb_ref[i, h] = st.astype(BF16)
                st = st * dec_ref[i * SUBLANES:i * SUBLANES + 1, hs] + up_ref[i, h]
            st_ref[h] = st

    def finish_pass(i, carry):
        sl = rows_of(i)
        vb = v_ref[sl, :].astype(BF16)
        q_in = qi_ref[sl, :]
        outs = [_dot_nt(q_in[:, hs], sb_ref[i, h]) + _dot(a_ref[i, h], vb[:, hs]) for h, hs in enumerate(heads)]
        o_all = jnp.concatenate(outs, axis=1)
        if rev:
            tot = of_ref[sl, :] + o_all
            nrm = nrm_ref[...]
            ys = [_rms(tot[:, hs], nrm) for hs in heads]
            o_ref[sl, :] = (jnp.concatenate(ys, axis=1) * _silu_of_half(hg_ref[sl, :])).astype(o_ref.dtype)
        else:
            o_ref[sl, :] = o_all
        return carry

    def factored(i, carry):
        sl, q, k, b = gates(i)
        r = b[ref_row:ref_row + 1, :]
        bl = b[last_row:last_row + 1, :]
        rg = jnp.where(far, r, 0.0)
        qt = q * jnp.exp(b - rg)
        kt = k * jnp.exp(rg - b)
        er = jnp.exp(r)
        qn = jnp.where(far, 0.0, qt).astype(BF16)
        qf = jnp.where(far, qt, 0.0).astype(BF16)
        kc = jnp.where(far, kt, kt * er).astype(BF16)
        ktb = kt.astype(BF16)
        q_in = jnp.where(far, qt * er, qt).astype(BF16)
        k_dec = (kt * jnp.where(far, jnp.exp(bl - r), jnp.exp(bl))).astype(BF16)
        a = []
        for hs in heads:
            lhs = jnp.concatenate([qn[:, hs], qf[:, hs]], axis=1)
            rhs = jnp.concatenate([ktb[:, hs], kc[:, hs]], axis=1)
            a.append(jnp.where(causal, _dot_nt(lhs, rhs), 0.0))
        stage(i, sl, a, q_in, k_dec, jnp.exp(bl))
        return carry

    def levels(i, carry):
        sl, q, k, b = gates(i)
        sel_all, qsides, pairs, eye = _hgrn_level_tables(rev)
        bl = b[last_row:last_row + 1, :]
        q_in = (q * jnp.exp(b)).astype(BF16)
        k_dec = (k * jnp.exp(bl - b)).astype(BF16)
        bref_all = jnp.dot(sel_all, b, precision=HIGHEST, preferred_element_type=F32)
        qb = q.astype(BF16)
        kb = k.astype(BF16)
        a = [jnp.where(eye, _dot_nt(qb[:, hs], kb[:, hs]), 0.0) for hs in heads]
        for lev in range(HG_LEVELS):
            bref = bref_all[lev * c:(lev + 1) * c, :]
            qs = qsides[lev]
            x = jnp.exp(jnp.where(qs, b - bref, bref - b))
            ql = jnp.where(qs, q * x, 0.0).astype(BF16)
            kl = jnp.where(qs, 0.0, k * x).astype(BF16)
            for h, hs in enumerate(heads):
                a[h] = a[h] + jnp.where(pairs[lev], _dot_nt(ql[:, hs], kl[:, hs]), 0.0)
        stage(i, sl, a, q_in, k_dec, jnp.exp(bl))
        return carry

    @pl.when(safe)
    def _():
        lax.fori_loop(0, nsub, factored, 0, unroll=8)

    @pl.when(jnp.logical_not(safe))
    def _():
        lax.fori_loop(0, nsub, levels, 0)

    scan_states()
    lax.fori_loop(0, nsub, finish_pass, 0, unroll=8)


def _hgrn(safe, pa3, pb3, gamma, layer, rev, o_fwd=None, nrm=None):
    bsz, seq, _ = pa3.shape
    cb = min(HG_BLOCK, seq)
    assert seq % cb == 0 and cb % HG_SUB == 0
    nc = seq // cb
    wblk = HG_WIDTH

    def cmap(col):
        if rev:
            return lambda b, c, safe_ref: (b, nc - 1 - c, col // wblk)
        return lambda b, c, safe_ref: (b, c, col // wblk)

    const = lambda b, c, safe_ref: (0, 0)
    in_specs = [
        pl.BlockSpec((None, cb, wblk), cmap(A_HQ)),
        pl.BlockSpec((None, cb, wblk), cmap(A_HF_BWD if rev else A_HF_FWD)),
        pl.BlockSpec((None, cb, wblk), cmap(B_HI)),
        pl.BlockSpec(gamma.shape, const),
    ]
    args = [pa3, pa3, pb3, gamma]
    if rev:
        in_specs += [
            pl.BlockSpec((None, cb, wblk), cmap(A_HG)),
            pl.BlockSpec((None, cb, wblk), cmap(0)),
            pl.BlockSpec((1, HG_DK), const),
        ]
        args += [pa3, o_fwd, nrm]
    out_dtype = BF16 if rev else F32
    grid_spec = pltpu.PrefetchScalarGridSpec(
        num_scalar_prefetch=1,
        grid=(bsz, nc),
        in_specs=in_specs,
        out_specs=pl.BlockSpec((None, cb, wblk), cmap(0)),
        scratch_shapes=[pltpu.VMEM((HG_HEADS, HG_DK, HG_DK), F32)] + [pltpu.VMEM((cb, wblk), F32)] * 3
        + [pltpu.VMEM((cb // HG_SUB, HG_HEADS, HG_SUB, HG_SUB), BF16), pltpu.VMEM((cb, wblk), BF16),
           pltpu.VMEM((cb // HG_SUB * SUBLANES, wblk), F32),
           pltpu.VMEM((cb // HG_SUB, HG_HEADS, HG_DK, HG_DK), F32),
           pltpu.VMEM((cb // HG_SUB, HG_HEADS, HG_DK, HG_DK), BF16)],
    )
    return pl.pallas_call(
        functools.partial(_hgrn_kernel, layer=layer, rev=rev),
        grid_spec=grid_spec,
        out_shape=jax.ShapeDtypeStruct((bsz, seq, HG_WIDTH), out_dtype),
        compiler_params=_params(("arbitrary", "arbitrary")),
        name="hgrn_bwd" if rev else "hgrn_fwd",
    )(safe, *args)


ATT_ROW_HEADS = (0, 2, 1, 3)


def _t5_buckets(rel):
    half = N_BUCKETS // 2
    ret = np.where(rel > 0, half, 0)
    n = np.abs(rel)
    max_exact = half // 2
    large = max_exact + (np.log(np.maximum(n, 1) / max_exact)
                         / np.log(REL_MAX_DIST / max_exact) * (half - max_exact)).astype(np.int32)
    large = np.minimum(large, half - 1)
    return (ret + np.where(n < max_exact, n, large)).astype(np.int32)


def _attn_tables(rel_bias, att_sink):
    c = ATT_BLOCK
    rel = np.arange(3 * c)[None, :] - c - np.arange(c)[:, None]
    onehot = np.equal(_t5_buckets(rel).reshape(-1, 1), np.arange(N_BUCKETS)).astype(np.float32)
    bias = jnp.dot(jnp.asarray(onehot, BF16), rel_bias.astype(F32), precision=HIGHEST)
    bias = bias.reshape(c, 3 * c, ATT_HEADS).transpose(2, 0, 1)
    bias = jnp.where(jnp.asarray(np.abs(rel) <= WINDOW)[None], bias, NEG_BIG)
    col = np.arange(3 * c)
    cases = []
    for case in range(4):
        valid = np.ones(3 * c, bool)
        if case & 1:
            valid &= col >= c
        if case & 2:
            valid &= col < 2 * c
        cases.append(jnp.where(jnp.asarray(valid)[None, None, :], bias, NEG_BIG))
    tab = jnp.stack(cases)
    order = np.array([[ATT_GROUP * g + h for h in ATT_ROW_HEADS] for g in range(ATT_KV)])
    tab = tab[:, order].reshape(4, ATT_KV, ATT_GROUP * c, 3 * c)
    sink = att_sink.astype(F32)[:, order]
    sink = jnp.broadcast_to(sink[..., None, None], sink.shape + (c, LANES))
    return tab, sink.reshape(att_sink.shape[0], ATT_KV, ATT_GROUP * c, LANES)


def _attn_kernel(q_ref, kp_ref, kc_ref, kn_ref, vp_ref, vc_ref, vn_ref, bias_ref, sink_ref, o_ref):
    n = pl.program_id(1)
    nsteps = pl.num_programs(1)
    c = ATT_BLOCK
    nsub = q_ref.shape[0] // c
    pair = 2 * ATT_HD
    kwin = jnp.concatenate([kp_ref[...], kc_ref[...], kn_ref[...]], axis=0).astype(F32)
    vwin = jnp.concatenate([vp_ref[...], vc_ref[...], vn_ref[...]], axis=0).astype(F32)
    kroll = pltpu.roll(kwin, ATT_HD, axis=1)
    vroll = pltpu.roll(vwin, ATT_HD, axis=1)
    lo = lax.broadcasted_iota(jnp.int32, kwin.shape, 1) < ATT_HD
    k_lo = [jnp.where(lo, kwin, 0.0).astype(BF16), jnp.where(lo, kroll, 0.0).astype(BF16)]
    k_hi = [jnp.where(lo, 0.0, kroll).astype(BF16), jnp.where(lo, 0.0, kwin).astype(BF16)]
    v_lo = [jnp.where(lo, vwin, 1.0).astype(BF16), jnp.where(lo, vroll, 1.0).astype(BF16)]
    v_hi = [jnp.where(lo, 1.0, vroll).astype(BF16), jnp.where(lo, 1.0, vwin).astype(BF16)]
    lo_out = lax.broadcasted_iota(jnp.int32, (2 * c, pair), 1) < ATT_HD
    for j in range(nsub):
        case = jnp.int32(0)
        if j == 0:
            case = case + (n == 0).astype(jnp.int32)
        if j == nsub - 1:
            case = case + 2 * (n == nsteps - 1).astype(jnp.int32)
        band = slice(j * c, (j + 3) * c)
        qrows = slice(j * c, (j + 1) * c)
        for g in range(ATT_KV):
            qg = q_ref[qrows, g * 2 * pair:(g + 1) * 2 * pair].astype(F32) * (ATT_HD ** -0.5)
            lhs = jnp.concatenate([qg[:, :pair], qg[:, pair:]], axis=0).astype(BF16)
            rhs = jnp.concatenate([k_lo[g][band], k_hi[g][band]], axis=0)
            lg = _dot_nt(lhs, rhs)
            logits = jnp.concatenate([lg[:, :3 * c], lg[:, 3 * c:]], axis=0) + bias_ref[case, g]
            s = sink_ref[g]
            m = jnp.maximum(jnp.max(logits, axis=-1, keepdims=True), s)
            p = jnp.exp(logits - jnp.concatenate([m] * 3, axis=1)).astype(BF16)
            es = jnp.exp(s - m)
            o_even = _dot(p[:2 * c], v_lo[g][band])
            o_odd = _dot(p[2 * c:], v_hi[g][band])
            o = jnp.where(lo_out,
                          o_even / (pltpu.roll(o_even, ATT_HD, axis=1) + es[:2 * c]),
                          o_odd / (pltpu.roll(o_odd, ATT_HD, axis=1) + es[2 * c:]))
            o_ref[qrows, g * 2 * pair:g * 2 * pair + pair] = o[:c].astype(o_ref.dtype)
            o_ref[qrows, g * 2 * pair + pair:(g + 1) * 2 * pair] = o[c:].astype(o_ref.dtype)


def _attn(proj3, bias, sink):
    bsz, seq, _ = proj3.shape
    c = ATT_BLOCK
    nsub = max(k for k in (4, 2, 1) if seq % (k * c) == 0)
    qb = nsub * c
    nsteps = seq // qb
    assert seq % c == 0
    nb = seq // c
    kcol = B_AK // KV_WIDTH
    vcol = B_AV // KV_WIDTH

    def band(col):
        return [
            pl.BlockSpec((None, c, KV_WIDTH), lambda b, n: (b, jnp.maximum(n * nsub - 1, 0), col)),
            pl.BlockSpec((None, qb, KV_WIDTH), lambda b, n: (b, n, col)),
            pl.BlockSpec((None, c, KV_WIDTH), lambda b, n: (b, jnp.minimum((n + 1) * nsub, nb - 1), col)),
        ]

    return pl.pallas_call(
        _attn_kernel,
        grid=(bsz, nsteps),
        in_specs=[pl.BlockSpec((None, qb, ATT_WIDTH), lambda b, n: (b, n, B_AQ // ATT_WIDTH))]
        + band(kcol) + band(vcol)
        + [pl.BlockSpec(bias.shape, lambda b, n: (0, 0, 0, 0)),
           pl.BlockSpec(sink.shape, lambda b, n: (0, 0, 0))],
        out_specs=pl.BlockSpec((None, qb, ATT_WIDTH), lambda b, n: (b, n, 0)),
        out_shape=jax.ShapeDtypeStruct((bsz, seq, ATT_WIDTH), BF16),
        compiler_params=_params(("arbitrary", "arbitrary")),
        name="window_attn",
    )(proj3, proj3, proj3, proj3, proj3, proj3, proj3, bias, sink)


def _first_argmax(vals):
    best, idx = vals[0], jnp.zeros(vals[0].shape, jnp.int32)
    for j in range(1, len(vals)):
        upd = vals[j] > best
        idx = jnp.where(upd, j, idx)
        best = jnp.where(upd, vals[j], best)
    return best, idx


def _select(vals, idx):
    out = vals[0]
    for j in range(1, len(vals)):
        out = jnp.where(idx == j, vals[j], out)
    return out


def _route(logits_t, rbias):
    m = jnp.max(logits_t, axis=0, keepdims=True)
    e = jnp.exp(logits_t - m)
    scores = e / jnp.sum(e, axis=0, keepdims=True)
    sel = scores + rbias
    srow = [scores[i:i + 1, :] for i in range(N_EXPERTS)]
    lrow = [sel[i:i + 1, :] for i in range(N_EXPERTS)]
    gscore = []
    for g in range(N_GROUPS):
        a, b, c, d = lrow[4 * g:4 * g + 4]
        hi1, lo1 = jnp.maximum(a, b), jnp.minimum(a, b)
        hi2, lo2 = jnp.maximum(c, d), jnp.minimum(c, d)
        gscore.append(jnp.maximum(hi1, hi2) + jnp.maximum(jnp.minimum(hi1, hi2), jnp.maximum(lo1, lo2)))
    _, gi = _first_argmax(gscore)
    ing = [_select([lrow[4 * g + j] for g in range(N_GROUPS)], gi) for j in range(EXPERTS_PER_GROUP)]
    sg = [_select([srow[4 * g + j] for g in range(N_GROUPS)], gi) for j in range(EXPERTS_PER_GROUP)]
    _, i1 = _first_argmax(ing)
    rest = [jnp.where(i1 == j, -jnp.inf, ing[j]) for j in range(EXPERTS_PER_GROUP)]
    _, i2 = _first_argmax(rest)
    s1, s2 = _select(sg, i1), _select(sg, i2)
    tot = s1 + s2
    w1, w2 = s1 / tot, s2 / tot
    idx1 = gi * EXPERTS_PER_GROUP + i1
    idx2 = gi * EXPERTS_PER_GROUP + i2
    return idx1, idx2, w1, w2


def _merge_kernel(oh_ref, oa_ref, gh_ref, ga_ref, x_ref, mod_ref, nf_ref, wbh_ref, wba_ref, wo_ref,
                  wrt_ref, rb_ref, before_ref, x1_ref, h2_ref, ri_ref, rw_ref, cnt_ref, carry_ref, mg_ref):
    @pl.when(pl.program_id(0) == 0)
    def _():
        carry_ref[...] = jnp.zeros_like(carry_ref)

    tm = x_ref.shape[0]
    blocks = [slice(j * MXU_COLS, (j + 1) * MXU_COLS) for j in range(D_MODEL // MXU_COLS)]
    for cols in blocks:
        mh = _dot(oh_ref[...], wbh_ref[:, cols])
        ma = _dot(oa_ref[...], wba_ref[:, cols])
        mg_ref[:, cols] = ((jnp.tanh(gh_ref[:, cols].astype(F32)) + 1.0) * mh
                           + (jnp.tanh(ga_ref[:, cols].astype(F32)) + 1.0) * ma).astype(BF16)
    ssq = jnp.zeros((tm, 1), F32)
    for cols in blocks:
        x1 = x_ref[:, cols] + mod_ref[2:3, cols] * _dot(mg_ref[...], wo_ref[:, cols])
        x1_ref[:, cols] = x1
        ssq = ssq + jnp.sum(x1 * x1, axis=-1, keepdims=True)
    inv = lax.rsqrt(ssq * (1.0 / D_MODEL) + EPS)
    wr = wrt_ref[...]
    w_hi = wr.astype(BF16)
    w_lo = (wr - w_hi.astype(F32)).astype(BF16)
    logits_t = jnp.zeros((N_EXPERTS, tm), F32)
    half = len(blocks) // 2
    for j in range(half):
        pair = []
        for cols in (blocks[j], blocks[half + j]):
            h2 = x1_ref[:, cols] * inv * nf_ref[:, cols] * (1.0 + mod_ref[4:5, cols]) + mod_ref[3:4, cols]
            h_hi = h2.astype(BF16)
            h_lo = (h2 - h_hi.astype(F32)).astype(BF16)
            logits_t = logits_t + ((_dot_nt(w_hi[:, cols], h_hi) + _dot_nt(w_hi[:, cols], h_lo))
                                   + _dot_nt(w_lo[:, cols], h_hi))
            pair.append(h2)
        h2_ref[:, blocks[j]] = _pack_bf16_pairs(jnp.concatenate(pair, axis=1))
    idx1, idx2, w1, w2 = _route(logits_t, rb_ref[:, 0:1])
    erow = lax.broadcasted_iota(jnp.int32, logits_t.shape, 0)
    oh1 = erow == idx1
    oh2 = erow == idx2
    oh = jnp.where(oh1 | oh2, 1.0, 0.0)
    carry = carry_ref[...]
    pref = _dot(oh.astype(BF16), before_ref[...]) + carry[:, 0:1]
    rank1 = jnp.sum(jnp.where(oh1, pref, 0.0), axis=0, keepdims=True)
    rank2 = jnp.sum(jnp.where(oh2, pref, 0.0), axis=0, keepdims=True)
    carry = carry + jnp.sum(oh, axis=1, keepdims=True)
    carry_ref[...] = carry
    cnt_ref[...] = carry
    ri_ref[...] = jnp.concatenate([idx1, idx2, rank1.astype(jnp.int32), rank2.astype(jnp.int32)], axis=0)
    rw_ref[...] = jnp.concatenate([w1, w2], axis=0)


def _merge(o_h, o_a, proj, x2d, mod, nf, wbh, wba, wo, wrt, rb, seq):
    n = x2d.shape[0]
    tm = min(1024, seq)
    per_seq = seq // tm
    const = lambda i: (0, 0)
    before = jnp.asarray(np.arange(tm)[:, None] < np.arange(tm)[None, :], BF16)
    return pl.pallas_call(
        _merge_kernel,
        grid=(n // tm,),
        in_specs=[
            pl.BlockSpec((tm, HG_WIDTH), lambda i: (i, 0)),
            pl.BlockSpec((tm, ATT_WIDTH), lambda i: (i, 0)),
            pl.BlockSpec((tm, D_MODEL), lambda i: (i, B_GATE_H // D_MODEL)),
            pl.BlockSpec((tm, D_MODEL), lambda i: (i, B_GATE_A // D_MODEL)),
            pl.BlockSpec((tm, D_MODEL), lambda i: (i, 0)),
            pl.BlockSpec((None, N_MOD, D_MODEL), lambda i: (i // per_seq, 0, 0)),
            pl.BlockSpec((1, D_MODEL), const),
            pl.BlockSpec(wbh.shape, const),
            pl.BlockSpec(wba.shape, const),
            pl.BlockSpec(wo.shape, const),
            pl.BlockSpec(wrt.shape, const),
            pl.BlockSpec(rb.shape, const),
            pl.BlockSpec(before.shape, const),
        ],
        out_specs=[
            pl.BlockSpec((tm, D_MODEL), lambda i: (i, 0)),
            pl.BlockSpec((tm, D_MODEL // 2), lambda i: (i, 0)),
            pl.BlockSpec((4, tm), lambda i: (0, i)),
            pl.BlockSpec((2, tm), lambda i: (0, i)),
            pl.BlockSpec((N_EXPERTS, LANES), const),
        ],
        out_shape=[
            jax.ShapeDtypeStruct((n, D_MODEL), F32),
            jax.ShapeDtypeStruct((n, D_MODEL // 2), jnp.uint32),
            jax.ShapeDtypeStruct((4, n), jnp.int32),
            jax.ShapeDtypeStruct((2, n), F32),
            jax.ShapeDtypeStruct((N_EXPERTS, LANES), F32),
        ],
        scratch_shapes=[pltpu.VMEM((N_EXPERTS, LANES), F32), pltpu.VMEM((tm, D_MODEL), BF16)],
        compiler_params=_params(("arbitrary",)),
        name="merge_router",
    )(o_h, o_a, proj, proj, x2d, mod, nf, wbh, wba, wo, wrt, rb, before)


MOE_TILE = 512
SC_WINDOW = 64


def _sc_mesh():
    return plsc.VectorSubcoreMesh(core_axis_name="c", subcore_axis_name="s")


def _sc_dispatch(h, pos, p):
    n, d = h.shape
    win = SC_WINDOW
    info = plsc.get_sparse_core_info()
    workers = info.num_cores * info.num_subcores
    assert n % (2 * win * workers) == 0, "each vector subcore walks its windows two at a time"
    wpt = n // (win * workers)
    pos_w = pos.reshape(2, n // win, win).transpose(1, 0, 2)

    @functools.partial(
        pl.kernel, out_type=jax.ShapeDtypeStruct((p, d), h.dtype), mesh=_sc_mesh(),
        scratch_types=[pltpu.VMEM((wpt, 2, win), jnp.int32), pltpu.VMEM((2, win, d), h.dtype),
                       pltpu.SemaphoreType.DMA((2,))],
        name="moe_dispatch")
    def dispatch(h_hbm, pos_hbm, o_hbm, idx_v, rows_v, load_sem):
        wid = lax.axis_index("c") * info.num_subcores + lax.axis_index("s")
        first = wid * wpt
        pltpu.sync_copy(pos_hbm.at[pl.ds(first, wpt)], idx_v)

        def load(j, slot):
            return pltpu.make_async_copy(h_hbm.at[pl.ds((first + j) * win, win)], rows_v.at[slot],
                                         load_sem.at[slot])

        load(0, 0).start()

        @pl.loop(0, wpt, step=2)
        def _(j):
            for slot in range(2):
                jj = j + slot
                load(jj, slot).wait()

                @pl.when(jj + 1 < wpt)
                def _():
                    load(jj + 1, 1 - slot).start()

                pltpu.sync_copy(rows_v.at[slot], o_hbm.at[idx_v.at[jj, 0]])
                pltpu.sync_copy(rows_v.at[slot], o_hbm.at[idx_v.at[jj, 1]])

    return dispatch(h, pos_w)


def _sc_combine(ys, pos):
    n = pos.shape[1]
    d = ys.shape[1]
    win = SC_WINDOW // 2
    info = plsc.get_sparse_core_info()
    workers = info.num_cores * info.num_subcores
    assert n % (2 * win * workers) == 0, "each vector subcore walks its windows two at a time"
    wpt = n // (win * workers)
    pos_w = pos.reshape(2, n // win, win).transpose(1, 0, 2)
    out =jax.ShapeDtypeStruct((n, d), ys.dtype)

    @functools.partial(
        pl.kernel, out_type=(out, out), mesh=_sc_mesh(),
        scratch_types=[pltpu.VMEM((wpt, 2, win), jnp.int32), pltpu.VMEM((2, 2, win, d), ys.dtype),
                       pltpu.SemaphoreType.DMA((2, 2)), pltpu.SemaphoreType.DMA((2,))],
        name="moe_combine")
    def combine(ys_hbm, pos_hbm, a_hbm, b_hbm, idx_v, rows_v, gather_sem, store_sem):
        wid = lax.axis_index("c") * info.num_subcores + lax.axis_index("s")
        first = wid * wpt
        pltpu.sync_copy(pos_hbm.at[pl.ds(first, wpt)], idx_v)
        outs = (a_hbm, b_hbm)

        def gather(j, slot, k):
            return pltpu.make_async_copy(ys_hbm.at[idx_v.at[j, k]], rows_v.at[slot, k], gather_sem.at[slot, k])

        def store(j, slot, k):
            return pltpu.make_async_copy(rows_v.at[slot, k], outs[k].at[pl.ds((first + j) * win, win)],
                                         store_sem.at[k])

        gather(0, 0, 0).start()
        gather(0, 0, 1).start()

        @pl.loop(0, wpt, step=2)
        def _(j):
            for slot in range(2):
                jj = j + slot
                gather(jj, slot, 0).wait()
                gather(jj, slot, 1).wait()

                @pl.when(jj + 1 < wpt)
                def _():
                    gather(jj + 1, 1 - slot, 0).start()
                    gather(jj + 1, 1 - slot, 1).start()

                store(jj, slot, 0).start()
                store(jj, slot, 1).start()
                store(jj, slot, 0).wait()
                store(jj, slot, 1).wait()

    return combine(ys, pos_w)


def _expert_kernel(te_ref, nu_ref, x_ref, wg_ref, wu_ref, wd_ref, o_ref):
    del te_ref
    used = pl.program_id(0) < nu_ref[0]

    @pl.when(used)
    def _():
        x = _unpack_bf16_pairs(x_ref[...]).astype(BF16)
        he = _silu_of_half(_dot(x, wg_ref[...])) * _dot(x, wu_ref[...])
        o_ref[...] = _pack_bf16_pairs(_dot(he.astype(BF16), wd_ref[...]))

    @pl.when(jnp.logical_not(used))
    def _():
        o_ref[...] = jnp.zeros_like(o_ref)


def _experts(xs, tile_expert, n_used, wg, wu, wd, layer):
    p = xs.shape[0]
    tm = MOE_TILE
    grid_spec = pltpu.PrefetchScalarGridSpec(
        num_scalar_prefetch=2,
        grid=(p // tm,),
        in_specs=[
            pl.BlockSpec((tm, D_MODEL // 2), lambda i, te, nu: (i, 0)),
            pl.BlockSpec((None, None, D_MODEL, D_EXPERT), lambda i, te, nu: (layer, te[i], 0, 0)),
            pl.BlockSpec((None, None, D_MODEL, D_EXPERT), lambda i, te, nu: (layer, te[i], 0, 0)),
            pl.BlockSpec((None, None, D_EXPERT, D_MODEL), lambda i, te, nu: (layer, te[i], 0, 0)),
        ],
        out_specs=pl.BlockSpec((tm, D_MODEL // 2), lambda i, te, nu: (i, 0)),
    )
    return pl.pallas_call(
        _expert_kernel,
        grid_spec=grid_spec,
        out_shape=jax.ShapeDtypeStruct((p, D_MODEL // 2), jnp.uint32),
        compiler_params=_params(("arbitrary",)),
        name="moe_experts",
    )(tile_expert, n_used, xs, wg, wu, wd)


def _residual_kernel(ya_ref, yb_ref, w_ref, x1_ref, mod_ref, nfin_ref, o_ref, *, last):
    w = w_ref[...]
    y = w[:, 0:1] * _unpack_bf16_pairs(ya_ref[...]) + w[:, 1:2] * _unpack_bf16_pairs(yb_ref[...])
    x2 = x1_ref[...] + mod_ref[5:6, :] * y
    if last:
        x2 = _rms(x2, nfin_ref[...])
    o_ref[...] = x2


def _residual(ya, yb, w, x1, mod, nfin, seq, last):
    n = x1.shape[0]
    tm = min(1024, seq)
    per_seq = seq // tm
    row = pl.BlockSpec((tm, D_MODEL), lambda i: (i, 0))
    packed = pl.BlockSpec((tm, D_MODEL // 2), lambda i: (i, 0))
    return pl.pallas_call(
        functools.partial(_residual_kernel, last=last),
        grid=(n // tm,),
        in_specs=[packed, packed, pl.BlockSpec((tm, 2), lambda i: (i, 0)), row,
                  pl.BlockSpec((None, N_MOD, D_MODEL), lambda i: (i // per_seq, 0, 0)),
                  pl.BlockSpec((1, D_MODEL), lambda i: (0, 0))],
        out_specs=row,
        out_shape=jax.ShapeDtypeStruct((n, D_MODEL), F32),
        compiler_params=_params(("arbitrary",)),
        name="moe_residual",
    )(ya, yb, w, x1, mod, nfin)


def _moe(h2, ri, rw, cnt, wg, wu, wd, layer, x1, mod, nfin, seq, last):
    n = h2.shape[0]
    tm = MOE_TILE
    p = 2 * n + N_EXPERTS * tm
    counts = cnt[:, 0].astype(jnp.int32)
    padded = (counts + tm - 1) // tm * tm
    ends = jnp.cumsum(padded)
    starts = ends - padded
    base = jnp.zeros_like(ri[0:2])
    for e in range(1, N_EXPERTS):
        base = jnp.where(ri[0:2] == e, starts[e], base)
    pos = base + ri[2:4]
    tile_start = jnp.arange(p // tm, dtype=jnp.int32) * tm
    tile_expert = jnp.minimum(jnp.sum(tile_start[:, None] >= ends[None, :], axis=1), N_EXPERTS - 1)
    n_used = (ends[-1:] // tm).astype(jnp.int32)
    xs = _sc_dispatch(h2, pos, p)
    ys = _experts(xs, tile_expert.astype(jnp.int32), n_used, wg, wu, wd, layer)
    ya, yb = _sc_combine(ys, pos)
    return _residual(ya, yb, rw.T, x1, mod, nfin, seq, last)


def _split_w_in(w):
    hq_hf, hi, hg, att, gates = w[..., :1536], w[..., 1536:2048], w[..., 2048:2560], w[..., 2560:3328], w[..., 3328:]
    wa = (0.5 * jnp.concatenate([hq_hf, hg], axis=-1)).astype(BF16)
    wb = jnp.concatenate([0.5 * gates, hi, att], axis=-1).astype(BF16)
    return wa, wb


def _trunk(x, mod, wts):
    bsz, seq, _ = x.shape
    n = bsz * seq
    depth = wts["w_in_a"].shape[0]
    x2d = x.reshape(n, D_MODEL)
    for l in range(depth):
        mod_l = mod[l]
        pa, pb, span = _inproj(x2d, mod_l, wts["norm_mix"][l:l + 1], wts["w_in_a"][l], wts["w_in_b"][l],
                               wts["hg_lb_fwd"], wts["hg_lb_bwd"], seq, l)
        pa3 = pa.reshape(bsz, seq, A_COLS)
        pb3 = pb.reshape(bsz, seq, B_COLS)
        safe_f = (span[:, 0, 0] <= HG_SAFE_SPAN).astype(jnp.int32)
        safe_b = (span[:, 4, 0] <= HG_SAFE_SPAN).astype(jnp.int32)
        o_f = _hgrn(safe_f, pa3, pb3, wts["hg_lb_fwd"], l, False)
        o_h = _hgrn(safe_b, pa3, pb3, wts["hg_lb_bwd"], l, True, o_f, wts["hg_norm"][l:l + 1])
        o_a = _attn(pb3, wts["bias"], wts["sink"][l])
        x1, h2, ri, rw, cnt = _merge(o_h.reshape(n, HG_WIDTH), o_a.reshape(n, ATT_WIDTH), pb, x2d, mod_l,
                                     wts["norm_ffn"][l:l + 1], wts["w_br_hgrn"][l], wts["w_br_att"][l],
                                     wts["w_out"][l], wts["w_router_t"], wts["router_bias"], seq)
        x2d = _moe(h2, ri, rw, cnt, wts["w_gate"], wts["w_up"], wts["w_down"], l, x1, mod_l,
                   wts["norm_final"], seq, l == depth - 1)
    return x2d.reshape(bsz, seq, D_MODEL)


def kernel(x_prompt, x_sample, c_prompt, c_sample, w_ada, b_ada, norm_mix, norm_ffn, norm_final, w_in, hg_lb_fwd, hg_lb_bwd, hg_norm, att_sink, rel_bias, w_br_hgrn, w_br_att, w_out, w_router, router_bias, w_gate, w_up, w_down):
    depth = w_in.shape[0]
    bp, bs = c_prompt.shape[0], c_sample.shape[0]
    rows = -(-(bp + bs) // SUBLANES) * SUBLANES
    c_all = jnp.concatenate([c_prompt, c_sample, jnp.zeros((rows - bp - bs, D_MODEL), F32)], axis=0)
    mod = _ada(c_all, w_ada, b_ada).reshape(depth, rows, N_MOD, D_MODEL)
    bias, sink = _attn_tables(rel_bias, att_sink)
    w_in_a, w_in_b = _split_w_in(w_in)
    wts = {
        "norm_mix": norm_mix, "norm_ffn": norm_ffn, "norm_final": norm_final.reshape(1, D_MODEL),
        "w_in_a": w_in_a, "w_in_b": w_in_b,
        "hg_lb_fwd": hg_lb_fwd, "hg_lb_bwd": hg_lb_bwd, "hg_norm": hg_norm,
        "sink": sink, "bias": bias,
        "w_br_hgrn": w_br_hgrn.astype(BF16), "w_br_att": w_br_att.astype(BF16), "w_out": (0.5 * w_out).astype(BF16),
        "w_router_t": w_router.T,
        "router_bias": jnp.broadcast_to(router_bias[:, None], (N_EXPERTS, LANES)),
        "w_gate": (0.5 * w_gate).astype(BF16), "w_up": w_up.astype(BF16), "w_down": w_down.astype(BF16),
    }
    y_prompt = _trunk(x_prompt, mod[:, :bp], wts)
    y_sample = _trunk(x_sample, mod[:, bp:bp + bs], wts)
    return (y_prompt, y_sample)
```

```python
import functools

import numpy as np
import jax
import jax.numpy as jnp
from jax import lax
from jax.experimental import pallas as pl
from jax.experimental.pallas import tpu as pltpu
from jax.experimental.pallas import tpu_sc as plsc

D_MODEL = 1024
HG_DK = 128
HG_WIDTH = 512
HG_HEADS = 4
HG_SUB = 64
HG_LEVELS = 6
HG_GROUP = 32
HG_SAFE_SPAN = 80.0
HG_BLOCK = 512
ATT_HD = 64
ATT_HEADS = 8
ATT_KV = 2
ATT_GROUP = 4
ATT_WIDTH = 512
KV_WIDTH = 128
WINDOW = 128
ATT_BLOCK = 128
N_BUCKETS = 32
REL_MAX_DIST = 128
N_EXPERTS = 16
N_GROUPS = 4
EXPERTS_PER_GROUP = 4
D_EXPERT = 512
N_MOD = 6
IN_COLS = 5376
EPS = 1e-6
NEG_BIG = -1e30
TINY = 1e-30

A_COLS = 2048
A_HQ = 0
A_HF_FWD = 512
A_HF_BWD = 1024
A_HG = 1536
B_COLS = 3328
B_GATE_H = 0
B_GATE_A = 1024
B_HI = 2048
B_AQ = 2560
B_AK = 3072
B_AV = 3200

V7X_VMEM_LIMIT = 56 * 1024 * 1024
MXU_COLS = 256
LANES = 128
SUBLANES = 8

F32 = jnp.float32
BF16 = jnp.bfloat16
HIGHEST = lax.Precision.HIGHEST


def _params(sem):
    return pltpu.CompilerParams(dimension_semantics=sem, vmem_limit_bytes=V7X_VMEM_LIMIT)


def _dot(a, b):
    return jnp.dot(a, b, preferred_element_type=F32)


def _dot_nt(a, b):
    return lax.dot_general(a, b, (((1,), (1,)), ((), ())), preferred_element_type=F32)


def _dot_tn(a, b):
    return lax.dot_general(a, b, (((0,), (0,)), ((), ())), preferred_element_type=F32)


def _sigmoid(x):
    return 0.5 * jnp.tanh(0.5 * x) + 0.5


def _pack_bf16_pairs(x):
    c = x.shape[1] // 2
    bits = lax.bitcast_convert_type(x.astype(BF16).astype(F32), jnp.uint32)
    return (bits[:, :c] >> 16) | (bits[:, c:] & jnp.uint32(0xFFFF0000))


def _unpack_bf16_pairs(u):
    lo = lax.bitcast_convert_type(u << 16, F32)
    hi = lax.bitcast_convert_type(u & jnp.uint32(0xFFFF0000), F32)
    return jnp.concatenate([lo, hi], axis=1)


def _silu(x):
    return x * _sigmoid(x)


def _silu_of_half(xh):
    return xh * (jnp.tanh(xh) + 1.0)


def _rms(x, g):
    return x * lax.rsqrt(jnp.mean(x * x, axis=-1, keepdims=True) + EPS) * g


def _ada_kernel(c_ref, w_ref, b_ref, o_ref):
    c = c_ref[...]
    o_ref[...] = jnp.dot(_silu(c), w_ref[...], precision=HIGHEST, preferred_element_type=F32) + b_ref[...]


def _ada(c_all, w_ada, b_ada):
    depth = w_ada.shape[0]
    rows = c_all.shape[0]
    ncol = w_ada.shape[2]
    tn = 1024
    return pl.pallas_call(
        _ada_kernel,
        grid=(depth, ncol // tn),
        in_specs=[
            pl.BlockSpec((rows, D_MODEL), lambda l, j: (0, 0)),
            pl.BlockSpec((None, D_MODEL, tn), lambda l, j: (l, 0, j)),
            pl.BlockSpec((None, 1, tn), lambda l, j: (l, 0, j)),
        ],
        out_specs=pl.BlockSpec((None, rows, tn), lambda l, j: (l, 0, j)),
        out_shape=jax.ShapeDtypeStruct((depth, rows, ncol), F32),
        compiler_params=_params(("arbitrary", "arbitrary")),
        name="ada_mod",
    )(c_all, w_ada, b_ada.reshape(depth, 1, ncol))


def _lower_bound_row(gam_ref, layer):
    rows = [gam_ref[d:d + 1, :] for d in range(gam_ref.shape[0])]
    m = functools.reduce(jnp.maximum, rows)
    es = [jnp.exp(r - m) for r in rows]
    tot = functools.reduce(lambda a, b: a + b, es)
    ps = [e / tot for e in es]
    cum = ps[0]
    for d in range(1, layer + 1):
        cum = cum + ps[d]
    return jnp.clip(cum - ps[0], 0.0, 1.0)


def _forget(zh, lb):
    return 0.5 * (1.0 + lb) + (0.5 * (1.0 - lb)) * jnp.tanh(zh)


def _inproj_kernel(x_ref, mod_ref, g_ref, wa_ref, wb_ref, gf_ref, gb_ref, oa_ref, ob_ref, span_ref, *, layer):
    x = x_ref[...]
    h = (_rms(x, g_ref[...]) * (1.0 + mod_ref[1:2, :]) + mod_ref[0:1, :]).astype(BF16)
    oa = _dot(h, wa_ref[...])
    oa_ref[...] = oa
    ob_ref[...] = _dot(h, wb_ref[...]).astype(BF16)
    spans = []
    for gam_ref, col in ((gf_ref, A_HF_FWD), (gb_ref, A_HF_BWD)):
        f = _forget(oa[:, col:col + HG_WIDTH], _lower_bound_row(gam_ref, layer))
        g = jnp.log(jnp.maximum(f, TINY))
        gsum = jnp.sum(g.reshape(g.shape[0] // HG_GROUP, HG_GROUP, HG_WIDTH), axis=1)
        spans.append(jnp.max(jnp.max(-gsum, axis=0, keepdims=True), axis=1, keepdims=True))
    half = lax.broadcasted_iota(jnp.int32, span_ref.shape, 0) < span_ref.shape[0] // 2
    span_ref[...] = jnp.where(half, spans[0], spans[1])


def _inproj(x2d, mod, g, wa, wb, gam_f, gam_b, seq, layer):
    n = x2d.shape[0]
    tm = min(HG_BLOCK, seq)
    per_seq = seq // tm
    const = lambda i: (0, 0)
    return pl.pallas_call(
        functools.partial(_inproj_kernel, layer=layer),
        grid=(n // tm,),
        in_specs=[
            pl.BlockSpec((tm, D_MODEL), lambda i: (i, 0)),
            pl.BlockSpec((None, N_MOD, D_MODEL), lambda i: (i // per_seq, 0, 0)),
            pl.BlockSpec((1, D_MODEL), const),
            pl.BlockSpec((D_MODEL, A_COLS), const, pipeline_mode=pl.Buffered(1)),
            pl.BlockSpec((D_MODEL, B_COLS), const, pipeline_mode=pl.Buffered(1)),
            pl.BlockSpec(gam_f.shape, const),
            pl.BlockSpec(gam_b.shape, const),
        ],
        out_specs=[pl.BlockSpec((tm, A_COLS), lambda i: (i, 0)),
                   pl.BlockSpec((tm, B_COLS), lambda i: (i, 0)),
                   pl.BlockSpec((None, SUBLANES, LANES), lambda i: (i, 0, 0))],
        out_shape=[jax.ShapeDtypeStruct((n, A_COLS), F32), jax.ShapeDtypeStruct((n, B_COLS), BF16),
                   jax.ShapeDtypeStruct((n // tm, SUBLANES, LANES), F32)],
        compiler_params=_params(("arbitrary",)),
        name="inproj",
    )(x2d, mod, g, wa, wb, gam_f, gam_b)


def _hgrn_level_tables(rev):
    c = HG_SUB
    r = lax.broadcasted_iota(jnp.int32, (c, c), 0)
    s = lax.broadcasted_iota(jnp.int32, (c, c), 1)
    row = lax.broadcasted_iota(jnp.int32, (c, HG_WIDTH), 0)
    sels, qsides, pairs = [], [], []
    for lev in range(HG_LEVELS):
        half = 1 << lev
        blk = 2 * half
        r_up = (r & (blk - 1)) >= half
        s_up = (s & (blk - 1)) >= half
        base = r - (r & (blk - 1))
        mrow = base + (half if rev else half - 1)
        sels.append(jnp.where(s == mrow, 1.0, 0.0).astype(F32))
        row_up = (row & (blk - 1)) >= half
        qsides.append(~row_up if rev else row_up)
        same = (r >> (lev + 1)) == (s >> (lev + 1))
        pairs.append(same & ((~r_up & s_up) if rev else (r_up & ~s_up)))
    return jnp.concatenate(sels, axis=0), qsides, pairs, r == s


def _hgrn_kernel(safe_ref, *refs, layer, rev):
    if rev:
        q_ref, f_ref, v_ref, gam_ref, hg_ref, of_ref, nrm_ref, o_ref = refs[:8]
    else:
        q_ref, f_ref, v_ref, gam_ref, o_ref = refs[:5]
    st_ref, qs_ref, ks_ref, bs_ref, a_ref, qi_ref, dec_ref, up_ref, sb_ref = refs[-9:]

    @pl.when(pl.program_id(1) == 0)
    def _():
        st_ref[...] = jnp.zeros_like(st_ref)

    nc = pl.num_programs(1)
    chunk = (nc - 1 - pl.program_id(1)) if rev else pl.program_id(1)
    safe = safe_ref[pl.program_id(0) * nc + chunk] != 0
    c = HG_SUB
    w = HG_WIDTH
    nsub = q_ref.shape[0] // c
    lb = _lower_bound_row(gam_ref, layer)
    r_i = lax.broadcasted_iota(jnp.int32, (c, c), 0)
    s_i = lax.broadcasted_iota(jnp.int32, (c, c), 1)
    causal = (s_i >= r_i) if rev else (s_i <= r_i)
    tri = jnp.where(causal, 1.0, 0.0).astype(BF16)
    row = lax.broadcasted_iota(jnp.int32, (c, w), 0)
    far = (row < HG_GROUP) if rev else (row >= HG_GROUP)
    ref_row = HG_GROUP if rev else HG_GROUP - 1
    last_row = 0 if rev else c - 1
    heads = [slice(h * HG_DK, (h + 1) * HG_DK) for h in range(HG_HEADS)]

    def rows_of(i):
        ci = (nsub - 1 - i) if rev else i
        return pl.ds(pl.multiple_of(ci * c, c), c)

    def gates_pass(i, carry):
        sl = rows_of(i)
        zq = q_ref[sl, :]
        f = _forget(f_ref[sl, :], lb)
        g = jnp.log(jnp.maximum(f, TINY))
        qs_ref[sl, :] = _silu_of_half(zq) * (HG_DK ** -0.5)
        ks_ref[sl, :] = 1.0 - f
        g1 = g.astype(BF16)
        r1 = g - g1.astype(F32)
        g2 = r1.astype(BF16)
        g3 = (r1 - g2.astype(F32)).astype(BF16)
        bb = _dot(tri, jnp.concatenate([g1, g2, g3], axis=1))
        bs_ref[sl, :] = (bb[:, :w] + bb[:, w:2 * w]) + bb[:, 2 * w:]
        return carry

    lax.fori_loop(0, nsub, gates_pass, 0, unroll=8)

    def gates(i):
        sl = rows_of(i)
        return sl, qs_ref[sl, :], ks_ref[sl, :], bs_ref[sl, :]

    def stage(i, sl, a, q_in, k_dec, dec):
        vb = v_ref[sl, :].astype(BF16)
        for h, hs in enumerate(heads):
            a_ref[i, h] = a[h].astype(BF16)
            up_ref[i, h] = _dot_tn(vb[:, hs], k_dec[:, hs])
        qi_ref[sl, :] = q_in
        dec_ref[pl.ds(pl.multiple_of(i * SUBLANES, SUBLANES), SUBLANES), :] = jnp.broadcast_to(dec, (SUBLANES, w))

    def scan_states():
        for h, hs in enumerate(heads):
            st = st_ref[h]
            for i in range(nsub):
                sb_ref[i, h] = st.astype(BF16)
                st = st * dec_ref[i * SUBLANES:i * SUBLANES + 1, hs] + up_ref[i, h]
            st_ref[h] = st

    def finish_pass(i, carry):
        sl = rows_of(i)
        vb = v_ref[sl, :].astype(BF16)
        q_in = qi_ref[sl, :]
        outs = [_dot_nt(q_in[:, hs], sb_ref[i, h]) + _dot(a_ref[i, h], vb[:, hs]) for h, hs in enumerate(heads)]
        o_all = jnp.concatenate(outs, axis=1)
        if rev:
            tot = of_ref[sl, :] + o_all
            nrm = nrm_ref[...]
            ys = [_rms(tot[:, hs], nrm) for hs in heads]
            o_ref[sl, :] = (jnp.concatenate(ys, axis=1) * _silu_of_half(hg_ref[sl, :])).astype(o_ref.dtype)
        else:
            o_ref[sl, :] = o_all
        return carry

    def factored(i, carry):
        sl, q, k, b = gates(i)
        r = b[ref_row:ref_row + 1, :]
        bl = b[last_row:last_row + 1, :]
        rg = jnp.where(far, r, 0.0)
        qt = q * jnp.exp(b - rg)
        kt = k * jnp.exp(rg - b)
        er = jnp.exp(r)
        qn = jnp.where(far, 0.0, qt).astype(BF16)
        qf = jnp.where(far, qt, 0.0).astype(BF16)
        kc = jnp.where(far, kt, kt * er).astype(BF16)
        ktb = kt.astype(BF16)
        q_in = jnp.where(far, qt * er, qt).astype(BF16)
        k_dec = (kt * jnp.where(far, jnp.exp(bl - r), jnp.exp(bl))).astype(BF16)
        a = []
        for hs in heads:
            lhs = jnp.concatenate([qn[:, hs], qf[:, hs]], axis=1)
            rhs = jnp.concatenate([ktb[:, hs], kc[:, hs]], axis=1)
            a.append(jnp.where(causal, _dot_nt(lhs, rhs), 0.0))
        stage(i, sl, a, q_in, k_dec, jnp.exp(bl))
        return carry

    def levels(i, carry):
        sl, q, k, b = gates(i)
        sel_all, qsides, pairs, eye = _hgrn_level_tables(rev)
        bl = b[last_row:last_row + 1, :]
        q_in = (q * jnp.exp(b)).astype(BF16)
        k_dec = (k * jnp.exp(bl - b)).astype(BF16)
        bref_all = jnp.dot(sel_all, b, precision=HIGHEST, preferred_element_type=F32)
        qb = q.astype(BF16)
        kb = k.astype(BF16)
        a = [jnp.where(eye, _dot_nt(qb[:, hs], kb[:, hs]), 0.0) for hs in heads]
        for lev in range(HG_LEVELS):
            bref = bref_all[lev * c:(lev + 1) * c, :]
            qs = qsides[lev]
            x = jnp.exp(jnp.where(qs, b - bref, bref - b))
            ql = jnp.where(qs, q * x, 0.0).astype(BF16)
            kl = jnp.where(qs, 0.0, k * x).astype(BF16)
            for h, hs in enumerate(heads):
                a[h] = a[h] + jnp.where(pairs[lev], _dot_nt(ql[:, hs], kl[:, hs]), 0.0)
        stage(i, sl, a, q_in, k_dec, jnp.exp(bl))
        return carry

    @pl.when(safe)
    def _():
        lax.fori_loop(0, nsub, factored, 0, unroll=8)

    @pl.when(jnp.logical_not(safe))
    def _():
        lax.fori_loop(0, nsub, levels, 0)

    scan_states()
    lax.fori_loop(0, nsub, finish_pass, 0, unroll=8)


def _hgrn(safe, pa3, pb3, gamma, layer, rev, o_fwd=None, nrm=None):
    bsz, seq, _ = pa3.shape
    cb = min(HG_BLOCK, seq)
    assert seq % cb == 0 and cb % HG_SUB == 0
    nc = seq // cb
    wblk = HG_WIDTH

    def cmap(col):
        if rev:
            return lambda b, c, safe_ref: (b, nc - 1 - c, col // wblk)
        return lambda b, c, safe_ref: (b, c, col // wblk)

    const = lambda b, c, safe_ref: (0, 0)
    in_specs = [
        pl.BlockSpec((None, cb, wblk), cmap(A_HQ)),
        pl.BlockSpec((None, cb, wblk), cmap(A_HF_BWD if rev else A_HF_FWD)),
        pl.BlockSpec((None, cb, wblk), cmap(B_HI)),
        pl.BlockSpec(gamma.shape, const),
    ]
    args = [pa3, pa3, pb3, gamma]
    if rev:
        in_specs += [
            pl.BlockSpec((None, cb, wblk), cmap(A_HG)),
            pl.BlockSpec((None, cb, wblk), cmap(0)),
            pl.BlockSpec((1, HG_DK), const),
        ]
        args += [pa3, o_fwd, nrm]
    out_dtype = BF16 if rev else F32
    grid_spec = pltpu.PrefetchScalarGridSpec(
        num_scalar_prefetch=1,
        grid=(bsz, nc),
        in_specs=in_specs,
        out_specs=pl.BlockSpec((None, cb, wblk), cmap(0)),
        scratch_shapes=[pltpu.VMEM((HG_HEADS, HG_DK, HG_DK), F32)] + [pltpu.VMEM((cb, wblk), F32)] * 3
        + [pltpu.VMEM((cb // HG_SUB, HG_HEADS, HG_SUB, HG_SUB), BF16), pltpu.VMEM((cb, wblk), BF16),
           pltpu.VMEM((cb // HG_SUB * SUBLANES, wblk), F32),
           pltpu.VMEM((cb // HG_SUB, HG_HEADS, HG_DK, HG_DK), F32),
           pltpu.VMEM((cb // HG_SUB, HG_HEADS, HG_DK, HG_DK), BF16)],
    )
    return pl.pallas_call(
        functools.partial(_hgrn_kernel, layer=layer, rev=rev),
        grid_spec=grid_spec,
        out_shape=jax.ShapeDtypeStruct((bsz, seq, HG_WIDTH), out_dtype),
        compiler_params=_params(("arbitrary", "arbitrary")),
        name="hgrn_bwd" if rev else "hgrn_fwd",
    )(safe, *args)


ATT_ROW_HEADS = (0, 2, 1, 3)


def _t5_buckets(rel):
    half = N_BUCKETS // 2
    ret = np.where(rel > 0, half, 0)
    n = np.abs(rel)
    max_exact = half // 2
    large = max_exact + (np.log(np.maximum(n, 1) / max_exact)
                         / np.log(REL_MAX_DIST / max_exact) * (half - max_exact)).astype(np.int32)
    large = np.minimum(large, half - 1)
    return (ret + np.where(n < max_exact, n, large)).astype(np.int32)


def _attn_tables(rel_bias, att_sink):
    c = ATT_BLOCK
    rel = np.arange(3 * c)[None, :] - c - np.arange(c)[:, None]
    onehot = np.equal(_t5_buckets(rel).reshape(-1, 1), np.arange(N_BUCKETS)).astype(np.float32)
    bias = jnp.dot(jnp.asarray(onehot, BF16), rel_bias.astype(F32), precision=HIGHEST)
    bias = bias.reshape(c, 3 * c, ATT_HEADS).transpose(2, 0, 1)
    bias = jnp.where(jnp.asarray(np.abs(rel) <= WINDOW)[None], bias, NEG_BIG)
    col = np.arange(3 * c)
    cases = []
    for case in range(4):
        valid = np.ones(3 * c, bool)
        if case & 1:
            valid &= col >= c
        if case & 2:
            valid &= col < 2 * c
        cases.append(jnp.where(jnp.asarray(valid)[None, None, :], bias, NEG_BIG))
    tab = jnp.stack(cases)
    order = np.array([[ATT_GROUP * g + h for h in ATT_ROW_HEADS] for g in range(ATT_KV)])
    tab = tab[:, order].reshape(4, ATT_KV, ATT_GROUP * c, 3 * c)
    sink = att_sink.astype(F32)[:, order]
    sink = jnp.broadcast_to(sink[..., None, None], sink.shape + (c, LANES))
    return tab, sink.reshape(att_sink.shape[0], ATT_KV, ATT_GROUP * c, LANES)


def _attn_kernel(q_ref, kp_ref, kc_ref, kn_ref, vp_ref, vc_ref, vn_ref, bias_ref, sink_ref, o_ref):
    n = pl.program_id(1)
    nsteps = pl.num_programs(1)
    c = ATT_BLOCK
    nsub = q_ref.shape[0] // c
    pair = 2 * ATT_HD
    kwin = jnp.concatenate([kp_ref[...], kc_ref[...], kn_ref[...]], axis=0).astype(F32)
    vwin = jnp.concatenate([vp_ref[...], vc_ref[...], vn_ref[...]], axis=0).astype(F32)
    kroll = pltpu.roll(kwin, ATT_HD, axis=1)
    vroll = pltpu.roll(vwin, ATT_HD, axis=1)
    lo = lax.broadcasted_iota(jnp.int32, kwin.shape, 1) < ATT_HD
    k_lo = [jnp.where(lo, kwin, 0.0).astype(BF16), jnp.where(lo, kroll, 0.0).astype(BF16)]
    k_hi = [jnp.where(lo, 0.0, kroll).astype(BF16), jnp.where(lo, 0.0, kwin).astype(BF16)]
    v_lo = [jnp.where(lo, vwin, 1.0).astype(BF16), jnp.where(lo, vroll, 1.0).astype(BF16)]
    v_hi = [jnp.where(lo, 1.0, vroll).astype(BF16), jnp.where(lo, 1.0, vwin).astype(BF16)]
    lo_out = lax.broadcasted_iota(jnp.int32, (2 * c, pair), 1) < ATT_HD
    for j in range(nsub):
        case = jnp.int32(0)
        if j == 0:
            case = case + (n == 0).astype(jnp.int32)
        if j == nsub - 1:
            case = case + 2 * (n == nsteps - 1).astype(jnp.int32)
        band = slice(j * c, (j + 3) * c)
        qrows = slice(j * c, (j + 1) * c)
        for g in range(ATT_KV):
            qg = q_ref[qrows, g * 2 * pair:(g + 1) * 2 * pair].astype(F32) * (ATT_HD ** -0.5)
            lhs = jnp.concatenate([qg[:, :pair], qg[:, pair:]], axis=0).astype(BF16)
            rhs = jnp.concatenate([k_lo[g][band], k_hi[g][band]], axis=0)
            lg = _dot_nt(lhs, rhs)
            logits = jnp.concatenate([lg[:, :3 * c], lg[:, 3 * c:]], axis=0) + bias_ref[case, g]
            s = sink_ref[g]
            m = jnp.maximum(jnp.max(logits, axis=-1, keepdims=True), s)
            p = jnp.exp(logits - jnp.concatenate([m] * 3, axis=1)).astype(BF16)
            es = jnp.exp(s - m)
            o_even = _dot(p[:2 * c], v_lo[g][band])
            o_odd = _dot(p[2 * c:], v_hi[g][band])
            o = jnp.where(lo_out,
                          o_even / (pltpu.roll(o_even, ATT_HD, axis=1) + es[:2 * c]),
                          o_odd / (pltpu.roll(o_odd, ATT_HD, axis=1) + es[2 * c:]))
            o_ref[qrows, g * 2 * pair:g * 2 * pair + pair] = o[:c].astype(o_ref.dtype)
            o_ref[qrows, g * 2 * pair + pair:(g + 1) * 2 * pair] = o[c:].astype(o_ref.dtype)


def _attn(proj3, bias, sink):
    bsz, seq, _ = proj3.shape
    c = ATT_BLOCK
    nsub = max(k for k in (8, 4, 2, 1) if seq % (k * c) == 0)
    qb = nsub * c
    nsteps = seq // qb
    assert seq % c == 0
    nb = seq // c
    kcol = B_AK // KV_WIDTH
    vcol = B_AV // KV_WIDTH

    def band(col):
        return [
            pl.BlockSpec((None, c, KV_WIDTH), lambda b, n: (b, jnp.maximum(n * nsub - 1, 0), col)),
            pl.BlockSpec((None, qb, KV_WIDTH), lambda b, n: (b, n, col)),
            pl.BlockSpec((None, c, KV_WIDTH), lambda b, n: (b, jnp.minimum((n + 1) * nsub, nb - 1), col)),
        ]

    return pl.pallas_call(
        _attn_kernel,
        grid=(bsz, nsteps),
        in_specs=[pl.BlockSpec((None, qb, ATT_WIDTH), lambda b, n: (b, n, B_AQ // ATT_WIDTH))]
        + band(kcol) + band(vcol)
        + [pl.BlockSpec(bias.shape, lambda b, n: (0, 0, 0, 0)),
           pl.BlockSpec(sink.shape, lambda b, n: (0, 0, 0))],
        out_specs=pl.BlockSpec((None, qb, ATT_WIDTH), lambda b, n: (b, n, 0)),
        out_shape=jax.ShapeDtypeStruct((bsz, seq, ATT_WIDTH), BF16),
        compiler_params=_params(("arbitrary", "arbitrary")),
        name="window_attn",
    )(proj3, proj3, proj3, proj3, proj3, proj3, proj3, bias, sink)


def _first_argmax(vals):
    best, idx = vals[0], jnp.zeros(vals[0].shape, jnp.int32)
    for j in range(1, len(vals)):
        upd = vals[j] > best
        idx = jnp.where(upd, j, idx)
        best = jnp.where(upd, vals[j], best)
    return best, idx


def _select(vals, idx):
    out = vals[0]
    for j in range(1, len(vals)):
        out = jnp.where(idx == j, vals[j], out)
    return out


def _route(logits_t, rbias):
    m = jnp.max(logits_t, axis=0, keepdims=True)
    e = jnp.exp(logits_t - m)
    scores = e / jnp.sum(e, axis=0, keepdims=True)
    sel = scores + rbias
    srow = [scores[i:i + 1, :] for i in range(N_EXPERTS)]
    lrow = [sel[i:i + 1, :] for i in range(N_EXPERTS)]
    gscore = []
    for g in range(N_GROUPS):
        a, b, c, d = lrow[4 * g:4 * g + 4]
        hi1, lo1 = jnp.maximum(a, b), jnp.minimum(a, b)
        hi2, lo2 = jnp.maximum(c, d), jnp.minimum(c, d)
        gscore.append(jnp.maximum(hi1, hi2) + jnp.maximum(jnp.minimum(hi1, hi2), jnp.maximum(lo1, lo2)))
    _, gi = _first_argmax(gscore)
    ing = [_select([lrow[4 * g + j] for g in range(N_GROUPS)], gi) for j in range(EXPERTS_PER_GROUP)]
    sg = [_select([srow[4 * g + j] for g in range(N_GROUPS)], gi) for j in range(EXPERTS_PER_GROUP)]
    _, i1 = _first_argmax(ing)
    rest = [jnp.where(i1 == j, -jnp.inf, ing[j]) for j in range(EXPERTS_PER_GROUP)]
    _, i2 = _first_argmax(rest)
    s1, s2 = _select(sg, i1), _select(sg, i2)
    tot = s1 + s2
    w1, w2 = s1 / tot, s2 / tot
    idx1 = gi * EXPERTS_PER_GROUP + i1
    idx2 = gi * EXPERTS_PER_GROUP + i2
    return idx1, idx2, w1, w2


def _merge_kernel(oh_ref, oa_ref, gh_ref, ga_ref, x_ref, mod_ref, nf_ref, wbh_ref, wba_ref, wo_ref,
                  wrt_ref, rb_ref, before_ref, x1_ref, h2_ref, ri_ref, rw_ref, cnt_ref, carry_ref, mg_ref):
    @pl.when(pl.program_id(0) == 0)
    def _():
        carry_ref[...] = jnp.zeros_like(carry_ref)

    tm = x_ref.shape[0]
    blocks = [slice(j * MXU_COLS, (j + 1) * MXU_COLS) for j in range(D_MODEL // MXU_COLS)]
    for cols in blocks:
        mh = _dot(oh_ref[...], wbh_ref[:, cols])
        ma = _dot(oa_ref[...], wba_ref[:, cols])
        mg_ref[:, cols] = ((jnp.tanh(gh_ref[:, cols].astype(F32)) + 1.0) * mh
                           + (jnp.tanh(ga_ref[:, cols].astype(F32)) + 1.0) * ma).astype(BF16)
    ssq = jnp.zeros((tm, 1), F32)
    for cols in blocks:
        x1 = x_ref[:, cols] + mod_ref[2:3, cols] * _dot(mg_ref[...], wo_ref[:, cols])
        x1_ref[:, cols] = x1
        ssq = ssq + jnp.sum(x1 * x1, axis=-1, keepdims=True)
    inv = lax.rsqrt(ssq * (1.0 / D_MODEL) + EPS)
    wr = wrt_ref[...]
    w_hi = wr.astype(BF16)
    w_lo = (wr - w_hi.astype(F32)).astype(BF16)
    logits_t = jnp.zeros((N_EXPERTS, tm), F32)
    half = len(blocks) // 2
    for j in range(half):
        pair = []
        for cols in (blocks[j], blocks[half + j]):
            h2 = x1_ref[:, cols] * inv * nf_ref[:, cols] * (1.0 + mod_ref[4:5, cols]) + mod_ref[3:4, cols]
            h_hi = h2.astype(BF16)
            h_lo = (h2 - h_hi.astype(F32)).astype(BF16)
            logits_t = logits_t + ((_dot_nt(w_hi[:, cols], h_hi) + _dot_nt(w_hi[:, cols], h_lo))
                                   + _dot_nt(w_lo[:, cols], h_hi))
            pair.append(h2)
        h2_ref[:, blocks[j]] = _pack_bf16_pairs(jnp.concatenate(pair, axis=1))
    idx1, idx2, w1, w2 = _route(logits_t, rb_ref[:, 0:1])
    erow = lax.broadcasted_iota(jnp.int32, logits_t.shape, 0)
    oh1 = erow == idx1
    oh2 = erow == idx2
    oh = jnp.where(oh1 | oh2, 1.0, 0.0)
    carry = carry_ref[...]
    pref = _dot(oh.astype(BF16), before_ref[...]) + carry[:, 0:1]
    rank1 = jnp.sum(jnp.where(oh1, pref, 0.0), axis=0, keepdims=True)
    rank2 = jnp.sum(jnp.where(oh2, pref, 0.0), axis=0, keepdims=True)
    carry = carry + jnp.sum(oh, axis=1, keepdims=True)
    carry_ref[...] = carry
    cnt_ref[...] = carry
    ri_ref[...] = jnp.concatenate([idx1, idx2, rank1.astype(jnp.int32), rank2.astype(jnp.int32)], axis=0)
    rw_ref[...] = jnp.concatenate([w1, w2], axis=0)


def _merge(o_h, o_a, proj, x2d, mod, nf, wbh, wba, wo, wrt, rb, seq):
    n = x2d.shape[0]
    tm = min(1024, seq)
    per_seq = seq // tm
    const = lambda i: (0, 0)
    before = jnp.asarray(np.arange(tm)[:, None] < np.arange(tm)[None, :], BF16)
    return pl.pallas_call(
        _merge_kernel,
        grid=(n // tm,),
        in_specs=[
            pl.BlockSpec((tm, HG_WIDTH), lambda i: (i, 0)),
            pl.BlockSpec((tm, ATT_WIDTH), lambda i: (i, 0)),
            pl.BlockSpec((tm, D_MODEL), lambda i: (i, B_GATE_H // D_MODEL)),
            pl.BlockSpec((tm, D_MODEL), lambda i: (i, B_GATE_A // D_MODEL)),
            pl.BlockSpec((tm, D_MODEL), lambda i: (i, 0)),
            pl.BlockSpec((None, N_MOD, D_MODEL), lambda i: (i // per_seq, 0, 0)),
            pl.BlockSpec((1, D_MODEL), const),
            pl.BlockSpec(wbh.shape, const),
            pl.BlockSpec(wba.shape, const),
            pl.BlockSpec(wo.shape, const),
            pl.BlockSpec(wrt.shape, const),
            pl.BlockSpec(rb.shape, const),
            pl.BlockSpec(before.shape, const),
        ],
        out_specs=[
            pl.BlockSpec((tm, D_MODEL), lambda i: (i, 0)),
            pl.BlockSpec((tm, D_MODEL // 2), lambda i: (i, 0)),
            pl.BlockSpec((4, tm), lambda i: (0, i)),
            pl.BlockSpec((2, tm), lambda i: (0, i)),
            pl.BlockSpec((N_EXPERTS, LANES), const),
        ],
        out_shape=[
            jax.ShapeDtypeStruct((n, D_MODEL), F32),
            jax.ShapeDtypeStruct((n, D_MODEL // 2), jnp.uint32),
            jax.ShapeDtypeStruct((4, n), jnp.int32),
            jax.ShapeDtypeStruct((2, n), F32),
            jax.ShapeDtypeStruct((N_EXPERTS, LANES), F32),
        ],
        scratch_shapes=[pltpu.VMEM((N_EXPERTS, LANES), F32), pltpu.VMEM((tm, D_MODEL), BF16)],
        compiler_params=_params(("arbitrary",)),
        name="merge_router",
    )(o_h, o_a, proj, proj, x2d, mod, nf, wbh, wba, wo, wrt, rb, before)


MOE_TILE = 512
SC_WINDOW = 64


def _sc_mesh():
    return plsc.VectorSubcoreMesh(core_axis_name="c", subcore_axis_name="s")


def _sc_dispatch(h, pos, p):
    n, d = h.shape
    win = SC_WINDOW
    info = plsc.get_sparse_core_info()
    workers = info.num_cores * info.num_subcores
    assert n % (2 * win * workers) == 0, "each vector subcore walks its windows two at a time"
    wpt = n // (win * workers)
    pos_w = pos.reshape(2, n // win, win).transpose(1, 0, 2)

    @functools.partial(
        pl.kernel, out_type=jax.ShapeDtypeStruct((p, d), h.dtype), mesh=_sc_mesh(),
        scratch_types=[pltpu.VMEM((wpt, 2, win), jnp.int32), pltpu.VMEM((2, win, d), h.dtype),
                       pltpu.SemaphoreType.DMA((2,))],
        name="moe_dispatch")
    def dispatch(h_hbm, pos_hbm, o_hbm, idx_v, rows_v, load_sem):
        wid = lax.axis_index("c") * info.num_subcores + lax.axis_index("s")
        first = wid * wpt
        pltpu.sync_copy(pos_hbm.at[pl.ds(first, wpt)], idx_v)

        def load(j, slot):
            return pltpu.make_async_copy(h_hbm.at[pl.ds((first + j) * win, win)], rows_v.at[slot],
                                         load_sem.at[slot])

        load(0, 0).start()

        @pl.loop(0, wpt, step=2)
        def _(j):
            for slot in range(2):
                jj = j + slot
                load(jj, slot).wait()

                @pl.when(jj + 1 < wpt)
                def _():
                    load(jj + 1, 1 - slot).start()

                pltpu.sync_copy(rows_v.at[slot], o_hbm.at[idx_v.at[jj, 0]])
                pltpu.sync_copy(rows_v.at[slot], o_hbm.at[idx_v.at[jj, 1]])

    return dispatch(h, pos_w)


def _sc_combine(ys, pos):
    n = pos.shape[1]
    d = ys.shape[1]
    win = SC_WINDOW // 2
    info = plsc.get_sparse_core_info()
    workers = info.num_cores * info.num_subcores
    assert n % (2 * win * workers) == 0, "each vector subcore walks its windows two at a time"
    wpt = n // (win * workers)
    pos_w = pos.reshape(2, n // win, win).transpose(1, 0, 2)
    out =jax.ShapeDtypeStruct((n, d), ys.dtype)

    @functools.partial(
        pl.kernel, out_type=(out, out), mesh=_sc_mesh(),
        scratch_types=[pltpu.VMEM((wpt, 2, win), jnp.int32), pltpu.VMEM((2, 2, win, d), ys.dtype),
                       pltpu.SemaphoreType.DMA((2, 2)), pltpu.SemaphoreType.DMA((2,))],
        name="moe_combine")
    def combine(ys_hbm, pos_hbm, a_hbm, b_hbm, idx_v, rows_v, gather_sem, store_sem):
        wid = lax.axis_index("c") * info.num_subcores + lax.axis_index("s")
        first = wid * wpt
        pltpu.sync_copy(pos_hbm.at[pl.ds(first, wpt)], idx_v)
        outs = (a_hbm, b_hbm)

        def gather(j, slot, k):
            return pltpu.make_async_copy(ys_hbm.at[idx_v.at[j, k]], rows_v.at[slot, k], gather_sem.at[slot, k])

        def store(j, slot, k):
            return pltpu.make_async_copy(rows_v.at[slot, k], outs[k].at[pl.ds((first + j) * win, win)],
                                         store_sem.at[k])

        gather(0, 0, 0).start()
        gather(0, 0, 1).start()

        @pl.loop(0, wpt, step=2)
        def _(j):
            for slot in range(2):
                jj = j + slot
                gather(jj, slot, 0).wait()
                gather(jj, slot, 1).wait()

                @pl.when(jj + 1 < wpt)
                def _():
                    gather(jj + 1, 1 - slot, 0).start()
                    gather(jj + 1, 1 - slot, 1).start()

                store(jj, slot, 0).start()
                store(jj, slot, 1).start()
                store(jj, slot, 0).wait()
                store(jj, slot, 1).wait()

    return combine(ys, pos_w)


def _expert_kernel(te_ref, nu_ref, x_ref, wg_ref, wu_ref, wd_ref, o_ref):
    del te_ref
    used = pl.program_id(0) < nu_ref[0]

    @pl.when(used)
    def _():
        x = _unpack_bf16_pairs(x_ref[...]).astype(BF16)
        he = _silu_of_half(_dot(x, wg_ref[...])) * _dot(x, wu_ref[...])
        o_ref[...] = _pack_bf16_pairs(_dot(he.astype(BF16), wd_ref[...]))

    @pl.when(jnp.logical_not(used))
    def _():
        o_ref[...] = jnp.zeros_like(o_ref)


def _experts(xs, tile_expert, n_used, wg, wu, wd, layer):
    p = xs.shape[0]
    tm = MOE_TILE
    grid_spec = pltpu.PrefetchScalarGridSpec(
        num_scalar_prefetch=2,
        grid=(p // tm,),
        in_specs=[
            pl.BlockSpec((tm, D_MODEL // 2), lambda i, te, nu: (i, 0)),
            pl.BlockSpec((None, None, D_MODEL, D_EXPERT), lambda i, te, nu: (layer, te[i], 0, 0)),
            pl.BlockSpec((None, None, D_MODEL, D_EXPERT), lambda i, te, nu: (layer, te[i], 0, 0)),
            pl.BlockSpec((None, None, D_EXPERT, D_MODEL), lambda i, te, nu: (layer, te[i], 0, 0)),
        ],
        out_specs=pl.BlockSpec((tm, D_MODEL // 2), lambda i, te, nu: (i, 0)),
    )
    return pl.pallas_call(
        _expert_kernel,
        grid_spec=grid_spec,
        out_shape=jax.ShapeDtypeStruct((p, D_MODEL // 2), jnp.uint32),
        compiler_params=_params(("arbitrary",)),
        name="moe_experts",
    )(tile_expert, n_used, xs, wg, wu, wd)


def _residual_kernel(ya_ref, yb_ref, w_ref, x1_ref, mod_ref, nfin_ref, o_ref, *, last):
    w = w_ref[...]
    y = w[:, 0:1] * _unpack_bf16_pairs(ya_ref[...]) + w[:, 1:2] * _unpack_bf16_pairs(yb_ref[...])
    x2 = x1_ref[...] + mod_ref[5:6, :] * y
    if last:
        x2 = _rms(x2, nfin_ref[...])
    o_ref[...] = x2


def _residual(ya, yb, w, x1, mod, nfin, seq, last):
    n = x1.shape[0]
    tm = min(1024, seq)
    per_seq = seq // tm
    row = pl.BlockSpec((tm, D_MODEL), lambda i: (i, 0))
    packed = pl.BlockSpec((tm, D_MODEL // 2), lambda i: (i, 0))
    return pl.pallas_call(
        functools.partial(_residual_kernel, last=last),
        grid=(n // tm,),
        in_specs=[packed, packed, pl.BlockSpec((tm, 2), lambda i: (i, 0)), row,
                  pl.BlockSpec((None, N_MOD, D_MODEL), lambda i: (i // per_seq, 0, 0)),
                  pl.BlockSpec((1, D_MODEL), lambda i: (0, 0))],
        out_specs=row,
        out_shape=jax.ShapeDtypeStruct((n, D_MODEL), F32),
        compiler_params=_params(("arbitrary",)),
        name="moe_residual",
    )(ya, yb, w, x1, mod, nfin)


def _moe(h2, ri, rw, cnt, wg, wu, wd, layer, x1, mod, nfin, seq, last):
    n = h2.shape[0]
    tm = MOE_TILE
    p = 2 * n + N_EXPERTS * tm
    counts = cnt[:, 0].astype(jnp.int32)
    padded = (counts + tm - 1) // tm * tm
    ends = jnp.cumsum(padded)
    starts = ends - padded
    base = jnp.zeros_like(ri[0:2])
    for e in range(1, N_EXPERTS):
        base = jnp.where(ri[0:2] == e, starts[e], base)
    pos = base + ri[2:4]
    tile_start = jnp.arange(p // tm, dtype=jnp.int32) * tm
    tile_expert = jnp.minimum(jnp.sum(tile_start[:, None] >= ends[None, :], axis=1), N_EXPERTS - 1)
    n_used = (ends[-1:] // tm).astype(jnp.int32)
    xs = _sc_dispatch(h2, pos, p)
    ys = _experts(xs, tile_expert.astype(jnp.int32), n_used, wg, wu, wd, layer)
    ya, yb = _sc_combine(ys, pos)
    return _residual(ya, yb, rw.T, x1, mod, nfin, seq, last)


def _split_w_in(w):
    hq_hf, hi, hg, att, gates = w[..., :1536], w[..., 1536:2048], w[..., 2048:2560], w[..., 2560:3328], w[..., 3328:]
    wa = (0.5 * jnp.concatenate([hq_hf, hg], axis=-1)).astype(BF16)
    wb = jnp.concatenate([0.5 * gates, hi, att], axis=-1).astype(BF16)
    return wa, wb


def _trunk(x, mod, wts):
    bsz, seq, _ = x.shape
    n = bsz * seq
    depth = wts["w_in_a"].shape[0]
    x2d = x.reshape(n, D_MODEL)
    for l in range(depth):
        mod_l = mod[l]
        pa, pb, span = _inproj(x2d, mod_l, wts["norm_mix"][l:l + 1], wts["w_in_a"][l], wts["w_in_b"][l],
                               wts["hg_lb_fwd"], wts["hg_lb_bwd"], seq, l)
        pa3 = pa.reshape(bsz, seq, A_COLS)
        pb3 = pb.reshape(bsz, seq, B_COLS)
        safe_f = (span[:, 0, 0] <= HG_SAFE_SPAN).astype(jnp.int32)
        safe_b = (span[:, 4, 0] <= HG_SAFE_SPAN).astype(jnp.int32)
        o_f = _hgrn(safe_f, pa3, pb3, wts["hg_lb_fwd"], l, False)
        o_h = _hgrn(safe_b, pa3, pb3, wts["hg_lb_bwd"], l, True, o_f, wts["hg_norm"][l:l + 1])
        o_a = _attn(pb3, wts["bias"], wts["sink"][l])
        x1, h2, ri, rw, cnt = _merge(o_h.reshape(n, HG_WIDTH), o_a.reshape(n, ATT_WIDTH), pb, x2d, mod_l,
                                     wts["norm_ffn"][l:l + 1], wts["w_br_hgrn"][l], wts["w_br_att"][l],
                                     wts["w_out"][l], wts["w_router_t"], wts["router_bias"], seq)
        x2d = _moe(h2, ri, rw, cnt, wts["w_gate"], wts["w_up"], wts["w_down"], l, x1, mod_l,
                   wts["norm_final"], seq, l == depth - 1)
    return x2d.reshape(bsz, seq, D_MODEL)


def kernel(x_prompt, x_sample, c_prompt, c_sample, w_ada, b_ada, norm_mix, norm_ffn, norm_final, w_in, hg_lb_fwd, hg_lb_bwd, hg_norm, att_sink, rel_bias, w_br_hgrn, w_br_att, w_out, w_router, router_bias, w_gate, w_up, w_down):
    depth = w_in.shape[0]
    bp, bs = c_prompt.shape[0], c_sample.shape[0]
    rows = -(-(bp + bs) // SUBLANES) * SUBLANES
    c_all = jnp.concatenate([c_prompt, c_sample, jnp.zeros((rows - bp - bs, D_MODEL), F32)], axis=0)
    mod = _ada(c_all, w_ada, b_ada).reshape(depth, rows, N_MOD, D_MODEL)
    bias, sink = _attn_tables(rel_bias, att_sink)
    w_in_a, w_in_b = _split_w_in(w_in)
    wts = {
        "norm_mix": norm_mix, "norm_ffn": norm_ffn, "norm_final": norm_final.reshape(1, D_MODEL),
        "w_in_a": w_in_a, "w_in_b": w_in_b,
        "hg_lb_fwd": hg_lb_fwd, "hg_lb_bwd": hg_lb_bwd, "hg_norm": hg_norm,
        "sink": sink, "bias": bias,
        "w_br_hgrn": w_br_hgrn.astype(BF16), "w_br_att": w_br_att.astype(BF16), "w_out": (0.5 * w_out).astype(BF16),
        "w_router_t": w_router.T,
        "router_bias": jnp.broadcast_to(router_bias[:, None], (N_EXPERTS, LANES)),
        "w_gate": (0.5 * w_gate).astype(BF16), "w_up": w_up.astype(BF16), "w_down": w_down.astype(BF16),
    }
    y_prompt = _trunk(x_prompt, mod[:, :bp], wts)
    y_sample = _trunk(x_sample, mod[:, bp:bp + bs], wts)
    return (y_prompt, y_sample)
```

```python
import functools

import numpy as np
import jax
import jax.numpy as jnp
from jax import lax
from jax.experimental import pallas as pl
from jax.experimental.pallas import tpu as pltpu
from jax.experimental.pallas import tpu_sc as plsc

D_MODEL = 1024
HG_DK = 128
HG_WIDTH = 512
HG_HEADS = 4
HG_SUB = 64
HG_LEVELS = 6
HG_GROUP = 32
HG_SAFE_SPAN = 80.0
HG_BLOCK = 512
ATT_HD = 64
ATT_HEADS = 8
ATT_KV = 2
ATT_GROUP = 4
ATT_WIDTH = 512
KV_WIDTH = 128
WINDOW = 128
ATT_BLOCK = 128
N_BUCKETS = 32
REL_MAX_DIST = 128
N_EXPERTS = 16
N_GROUPS = 4
EXPERTS_PER_GROUP = 4
D_EXPERT = 512
N_MOD = 6
IN_COLS = 5376
EPS = 1e-6
NEG_BIG = -1e30
TINY = 1e-30

A_COLS = 2048
A_HQ = 0
A_HF_FWD = 512
A_HF_BWD = 1024
A_HG = 1536
B_COLS = 3328
B_GATE_H = 0
B_GATE_A = 1024
B_HI = 2048
B_AQ = 2560
B_AK = 3072
B_AV = 3200

V7X_VMEM_LIMIT = 56 * 1024 * 1024
MXU_COLS = 256
LANES = 128
SUBLANES = 8

F32 = jnp.float32
BF16 = jnp.bfloat16
HIGHEST = lax.Precision.HIGHEST


def _params(sem):
    return pltpu.CompilerParams(dimension_semantics=sem, vmem_limit_bytes=V7X_VMEM_LIMIT)


def _dot(a, b):
    return jnp.dot(a, b, preferred_element_type=F32)


def _dot_nt(a, b):
    return lax.dot_general(a, b, (((1,), (1,)), ((), ())), preferred_element_type=F32)


def _dot_tn(a, b):
    return lax.dot_general(a, b, (((0,), (0,)), ((), ())), preferred_element_type=F32)


def _sigmoid(x):
    return 0.5 * jnp.tanh(0.5 * x) + 0.5


def _pack_bf16_pairs(x):
    c = x.shape[1] // 2
    bits = lax.bitcast_convert_type(x.astype(BF16).astype(F32), jnp.uint32)
    return (bits[:, :c] >> 16) | (bits[:, c:] & jnp.uint32(0xFFFF0000))


def _unpack_bf16_pairs(u):
    lo = lax.bitcast_convert_type(u << 16, F32)
    hi = lax.bitcast_convert_type(u & jnp.uint32(0xFFFF0000), F32)
    return jnp.concatenate([lo, hi], axis=1)


def _silu(x):
    return x * _sigmoid(x)


def _silu_of_half(xh):
    return xh * (jnp.tanh(xh) + 1.0)


def _rms(x, g):
    return x * lax.rsqrt(jnp.mean(x * x, axis=-1, keepdims=True) + EPS) * g


def _ada_kernel(c_ref, w_ref, b_ref, o_ref):
    c = c_ref[...]
    o_ref[...] = jnp.dot(_silu(c), w_ref[...], precision=HIGHEST, preferred_element_type=F32) + b_ref[...]


def _ada(c_all, w_ada, b_ada):
    depth = w_ada.shape[0]
    rows = c_all.shape[0]
    ncol = w_ada.shape[2]
    tn = 1024
    return pl.pallas_call(
        _ada_kernel,
        grid=(depth, ncol // tn),
        in_specs=[
            pl.BlockSpec((rows, D_MODEL), lambda l, j: (0, 0)),
            pl.BlockSpec((None, D_MODEL, tn), lambda l, j: (l, 0, j)),
            pl.BlockSpec((None, 1, tn), lambda l, j: (l, 0, j)),
        ],
        out_specs=pl.BlockSpec((None, rows, tn), lambda l, j: (l, 0, j)),
        out_shape=jax.ShapeDtypeStruct((depth, rows, ncol), F32),
        compiler_params=_params(("arbitrary", "arbitrary")),
        name="ada_mod",
    )(c_all, w_ada, b_ada.reshape(depth, 1, ncol))


def _lower_bound_row(gam_ref, layer):
    rows = [gam_ref[d:d + 1, :] for d in range(gam_ref.shape[0])]
    m = functools.reduce(jnp.maximum, rows)
    es = [jnp.exp(r - m) for r in rows]
    tot = functools.reduce(lambda a, b: a + b, es)
    ps = [e / tot for e in es]
    cum = ps[0]
    for d in range(1, layer + 1):
        cum = cum + ps[d]
    return jnp.clip(cum - ps[0], 0.0, 1.0)


def _forget(zh, lb):
    return 0.5 * (1.0 + lb) + (0.5 * (1.0 - lb)) * jnp.tanh(zh)


def _inproj_kernel(x_ref, mod_ref, g_ref, wa_ref, wb_ref, gf_ref, gb_ref, oa_ref, ob_ref, span_ref, *, layer):
    x = x_ref[...]
    h = (_rms(x, g_ref[...]) * (1.0 + mod_ref[1:2, :]) + mod_ref[0:1, :]).astype(BF16)
    oa = _dot(h, wa_ref[...])
    oa_ref[...] = oa
    ob_ref[...] = _dot(h, wb_ref[...]).astype(BF16)
    spans = []
    for gam_ref, col in ((gf_ref, A_HF_FWD), (gb_ref, A_HF_BWD)):
        f = _forget(oa[:, col:col + HG_WIDTH], _lower_bound_row(gam_ref, layer))
        g = jnp.log(jnp.maximum(f, TINY))
        gsum = jnp.sum(g.reshape(g.shape[0] // HG_GROUP, HG_GROUP, HG_WIDTH), axis=1)
        spans.append(jnp.max(jnp.max(-gsum, axis=0, keepdims=True), axis=1, keepdims=True))
    half = lax.broadcasted_iota(jnp.int32, span_ref.shape, 0) < span_ref.shape[0] // 2
    span_ref[...] = jnp.where(half, spans[0], spans[1])


def _inproj(x2d, mod, g, wa, wb, gam_f, gam_b, seq, layer):
    n = x2d.shape[0]
    tm = min(HG_BLOCK, seq)
    per_seq = seq // tm
    const = lambda i: (0, 0)
    return pl.pallas_call(
        functools.partial(_inproj_kernel, layer=layer),
        grid=(n // tm,),
        in_specs=[
            pl.BlockSpec((tm, D_MODEL), lambda i: (i, 0)),
            pl.BlockSpec((None, N_MOD, D_MODEL), lambda i: (i // per_seq, 0, 0)),
            pl.BlockSpec((1, D_MODEL), const),
            pl.BlockSpec((D_MODEL, A_COLS), const, pipeline_mode=pl.Buffered(1)),
            pl.BlockSpec((D_MODEL, B_COLS), const, pipeline_mode=pl.Buffered(1)),
            pl.BlockSpec(gam_f.shape, const),
            pl.BlockSpec(gam_b.shape, const),
        ],
        out_specs=[pl.BlockSpec((tm, A_COLS), lambda i: (i, 0)),
                   pl.BlockSpec((tm, B_COLS), lambda i: (i, 0)),
                   pl.BlockSpec((None, SUBLANES, LANES), lambda i: (i, 0, 0))],
        out_shape=[jax.ShapeDtypeStruct((n, A_COLS), F32), jax.ShapeDtypeStruct((n, B_COLS), BF16),
                   jax.ShapeDtypeStruct((n // tm, SUBLANES, LANES), F32)],
        compiler_params=_params(("arbitrary",)),
        name="inproj",
    )(x2d, mod, g, wa, wb, gam_f, gam_b)


def _hgrn_level_tables(rev):
    c = HG_SUB
    r = lax.broadcasted_iota(jnp.int32, (c, c), 0)
    s = lax.broadcasted_iota(jnp.int32, (c, c), 1)
    row = lax.broadcasted_iota(jnp.int32, (c, HG_WIDTH), 0)
    sels, qsides, pairs = [], [], []
    for lev in range(HG_LEVELS):
        half = 1 << lev
        blk = 2 * half
        r_up = (r & (blk - 1)) >= half
        s_up = (s & (blk - 1)) >= half
        base = r - (r & (blk - 1))
        mrow = base + (half if rev else half - 1)
        sels.append(jnp.where(s == mrow, 1.0, 0.0).astype(F32))
        row_up = (row & (blk - 1)) >= half
        qsides.append(~row_up if rev else row_up)
        same = (r >> (lev + 1)) == (s >> (lev + 1))
        pairs.append(same & ((~r_up & s_up) if rev else (r_up & ~s_up)))
    return jnp.concatenate(sels, axis=0), qsides, pairs, r == s


def _hgrn_kernel(safe_ref, *refs, layer, rev):
    if rev:
        q_ref, f_ref, v_ref, gam_ref, hg_ref, of_ref, nrm_ref, o_ref = refs[:8]
    else:
        q_ref, f_ref, v_ref, gam_ref, o_ref = refs[:5]
    st_ref, qs_ref, ks_ref, bs_ref, a_ref, qi_ref, dec_ref, up_ref, sb_ref = refs[-9:]

    @pl.when(pl.program_id(1) == 0)
    def _():
        st_ref[...] = jnp.zeros_like(st_ref)

    nc = pl.num_programs(1)
    chunk = (nc - 1 - pl.program_id(1)) if rev else pl.program_id(1)
    safe = safe_ref[pl.program_id(0) * nc + chunk] != 0
    c = HG_SUB
    w = HG_WIDTH
    nsub = q_ref.shape[0] // c
    lb = _lower_bound_row(gam_ref, layer)
    r_i = lax.broadcasted_iota(jnp.int32, (c, c), 0)
    s_i = lax.broadcasted_iota(jnp.int32, (c, c), 1)
    causal = (s_i >= r_i) if rev else (s_i <= r_i)
    tri = jnp.where(causal, 1.0, 0.0).astype(BF16)
    row = lax.broadcasted_iota(jnp.int32, (c, w), 0)
    far = (row < HG_GROUP) if rev else (row >= HG_GROUP)
    ref_row = HG_GROUP if rev else HG_GROUP - 1
    last_row = 0 if rev else c - 1
    heads = [slice(h * HG_DK, (h + 1) * HG_DK) for h in range(HG_HEADS)]

    def rows_of(i):
        ci = (nsub - 1 - i) if rev else i
        return pl.ds(pl.multiple_of(ci * c, c), c)

    def gates_pass(i, carry):
        sl = rows_of(i)
        zq = q_ref[sl, :]
        f = _forget(f_ref[sl, :], lb)
        g = jnp.log(jnp.maximum(f, TINY))
        qs_ref[sl, :] = _silu_of_half(zq) * (HG_DK ** -0.5)
        ks_ref[sl, :] = 1.0 - f
        g1 = g.astype(BF16)
        r1 = g - g1.astype(F32)
        g2 = r1.astype(BF16)
        g3 = (r1 - g2.astype(F32)).astype(BF16)
        bb = _dot(tri, jnp.concatenate([g1, g2, g3], axis=1))
        bs_ref[sl, :] = (bb[:, :w] + bb[:, w:2 * w]) + bb[:, 2 * w:]
        return carry

    lax.fori_loop(0, nsub, gates_pass, 0, unroll=8)

    def gates(i):
        sl = rows_of(i)
        return sl, qs_ref[sl, :], ks_ref[sl, :], bs_ref[sl, :]

    def stage(i, sl, a, q_in, k_dec, dec):
        vb = v_ref[sl, :].astype(BF16)
        for h, hs in enumerate(heads):
            a_ref[i, h] = a[h].astype(BF16)
            up_ref[i, h] = _dot_tn(vb[:, hs], k_dec[:, hs])
        qi_ref[sl, :] = q_in
        dec_ref[pl.ds(pl.multiple_of(i * SUBLANES, SUBLANES), SUBLANES), :] = jnp.broadcast_to(dec, (SUBLANES, w))

    def scan_states():
        for h, hs in enumerate(heads):
            st = st_ref[h]
            for i in range(nsub):
                sb_ref[i, h] = st.astype(BF16)
                st = st * dec_ref[i * SUBLANES:i * SUBLANES + 1, hs] + up_ref[i, h]
            st_ref[h] = st

    def finish_pass(i, carry):
        sl = rows_of(i)
        vb = v_ref[sl, :].astype(BF16)
        q_in = qi_ref[sl, :]
        outs = [_dot_nt(q_in[:, hs], sb_ref[i, h]) + _dot(a_ref[i, h], vb[:, hs]) for h, hs in enumerate(heads)]
        o_all = jnp.concatenate(outs, axis=1)
        if rev:
            tot = of_ref[sl, :] + o_all
            nrm = nrm_ref[...]
            ys = [_rms(tot[:, hs], nrm) for hs in heads]
            o_ref[sl, :] = (jnp.concatenate(ys, axis=1) * _silu_of_half(hg_ref[sl, :])).astype(o_ref.dtype)
        else:
            o_ref[sl, :] = o_all
        return carry

    def factored(i, carry):
        sl, q, k, b = gates(i)
        r = b[ref_row:ref_row + 1, :]
        bl = b[last_row:last_row + 1, :]
        rg = jnp.where(far, r, 0.0)
        qt = q * jnp.exp(b - rg)
        kt = k * jnp.exp(rg - b)
        er = jnp.exp(r)
        qn = jnp.where(far, 0.0, qt).astype(BF16)
        qf = jnp.where(far, qt, 0.0).astype(BF16)
        kc = jnp.where(far, kt, kt * er).astype(BF16)
        ktb = kt.astype(BF16)
        q_in = jnp.where(far, qt * er, qt).astype(BF16)
        k_dec = (kt * jnp.where(far, jnp.exp(bl - r), jnp.exp(bl))).astype(BF16)
        a = []
        for hs in heads:
            lhs = jnp.concatenate([qn[:, hs], qf[:, hs]], axis=1)
            rhs = jnp.concatenate([ktb[:, hs], kc[:, hs]], axis=1)
            a.append(jnp.where(causal, _dot_nt(lhs, rhs), 0.0))
        stage(i, sl, a, q_in, k_dec, jnp.exp(bl))
        return carry

    def levels(i, carry):
        sl, q, k, b = gates(i)
        sel_all, qsides, pairs, eye = _hgrn_level_tables(rev)
        bl = b[last_row:last_row + 1, :]
        q_in = (q * jnp.exp(b)).astype(BF16)
        k_dec = (k * jnp.exp(bl - b)).astype(BF16)
        bref_all = jnp.dot(sel_all, b, precision=HIGHEST, preferred_element_type=F32)
        qb = q.astype(BF16)
        kb = k.astype(BF16)
        a = [jnp.where(eye, _dot_nt(qb[:, hs], kb[:, hs]), 0.0) for hs in heads]
        for lev in range(HG_LEVELS):
            bref = bref_all[lev * c:(lev + 1) * c, :]
            qs = qsides[lev]
            x = jnp.exp(jnp.where(qs, b - bref, bref - b))
            ql = jnp.where(qs, q * x, 0.0).astype(BF16)
            kl = jnp.where(qs, 0.0, k * x).astype(BF16)
            for h, hs in enumerate(heads):
                a[h] = a[h] + jnp.where(pairs[lev], _dot_nt(ql[:, hs], kl[:, hs]), 0.0)
        stage(i, sl, a, q_in, k_dec, jnp.exp(bl))
        return carry

    @pl.when(safe)
    def _():
        lax.fori_loop(0, nsub, factored, 0, unroll=8)

    @pl.when(jnp.logical_not(safe))
    def _():
        lax.fori_loop(0, nsub, levels, 0)

    scan_states()
    lax.fori_loop(0, nsub, finish_pass, 0, unroll=8)


def _hgrn(safe, pa3, pb3, gamma, layer, rev, o_fwd=None, nrm=None):
    bsz, seq, _ = pa3.shape
    cb = min(HG_BLOCK, seq)
    assert seq % cb == 0 and cb % HG_SUB == 0
    nc = seq // cb
    wblk = HG_WIDTH

    def cmap(col):
        if rev:
            return lambda b, c, safe_ref: (b, nc - 1 - c, col // wblk)
        return lambda b, c, safe_ref: (b, c, col // wblk)

    const = lambda b, c, safe_ref: (0, 0)
    in_specs = [
        pl.BlockSpec((None, cb, wblk), cmap(A_HQ)),
        pl.BlockSpec((None, cb, wblk), cmap(A_HF_BWD if rev else A_HF_FWD)),
        pl.BlockSpec((None, cb, wblk), cmap(B_HI)),
        pl.BlockSpec(gamma.shape, const),
    ]
    args = [pa3, pa3, pb3, gamma]
    if rev:
        in_specs += [
            pl.BlockSpec((None, cb, wblk), cmap(A_HG)),
            pl.BlockSpec((None, cb, wblk), cmap(0)),
            pl.BlockSpec((1, HG_DK), const),
        ]
        args += [pa3, o_fwd, nrm]
    out_dtype = BF16 if rev else F32
    grid_spec = pltpu.PrefetchScalarGridSpec(
        num_scalar_prefetch=1,
        grid=(bsz, nc),
        in_specs=in_specs,
        out_specs=pl.BlockSpec((None, cb, wblk), cmap(0)),
        scratch_shapes=[pltpu.VMEM((HG_HEADS, HG_DK, HG_DK), F32)] + [pltpu.VMEM((cb, wblk), F32)] * 3
        + [pltpu.VMEM((cb // HG_SUB, HG_HEADS, HG_SUB, HG_SUB), BF16), pltpu.VMEM((cb, wblk), BF16),
           pltpu.VMEM((cb // HG_SUB * SUBLANES, wblk), F32),
           pltpu.VMEM((cb // HG_SUB, HG_HEADS, HG_DK, HG_DK), F32),
           pltpu.VMEM((cb // HG_SUB, HG_HEADS, HG_DK, HG_DK), BF16)],
    )
    return pl.pallas_call(
        functools.partial(_hgrn_kernel, layer=layer, rev=rev),
        grid_spec=grid_spec,
        out_shape=jax.ShapeDtypeStruct((bsz, seq, HG_WIDTH), out_dtype),
        compiler_params=_params(("arbitrary", "arbitrary")),
        name="hgrn_bwd" if rev else "hgrn_fwd",
    )(safe, *args)


ATT_ROW_HEADS = (0, 2, 1, 3)


def _t5_buckets(rel):
    half = N_BUCKETS // 2
    ret = np.where(rel > 0, half, 0)
    n = np.abs(rel)
    max_exact = half // 2
    large = max_exact + (np.log(np.maximum(n, 1) / max_exact)
                         / np.log(REL_MAX_DIST / max_exact) * (half - max_exact)).astype(np.int32)
    large = np.minimum(large, half - 1)
    return (ret + np.where(n < max_exact, n, large)).astype(np.int32)


def _attn_tables(rel_bias, att_sink):
    c = ATT_BLOCK
    rel = np.arange(3 * c)[None, :] - c - np.arange(c)[:, None]
    onehot = np.equal(_t5_buckets(rel).reshape(-1, 1), np.arange(N_BUCKETS)).astype(np.float32)
    bias = jnp.dot(jnp.asarray(onehot, BF16), rel_bias.astype(F32), precision=HIGHEST)
    bias = bias.reshape(c, 3 * c, ATT_HEADS).transpose(2, 0, 1)
    bias = jnp.where(jnp.asarray(np.abs(rel) <= WINDOW)[None], bias, NEG_BIG)
    col = np.arange(3 * c)
    cases = []
    for case in range(4):
        valid = np.ones(3 * c, bool)
        if case & 1:
            valid &= col >= c
        if case & 2:
            valid &= col < 2 * c
        cases.append(jnp.where(jnp.asarray(valid)[None, None, :], bias, NEG_BIG))
    tab = jnp.stack(cases)
    order = np.array([[ATT_GROUP * g + h for h in ATT_ROW_HEADS] for g in range(ATT_KV)])
    tab = tab[:, order].reshape(4, ATT_KV, ATT_GROUP * c, 3 * c)
    sink = att_sink.astype(F32)[:, order]
    sink = jnp.broadcast_to(sink[..., None, None], sink.shape + (c, LANES))
    return tab, sink.reshape(att_sink.shape[0], ATT_KV, ATT_GROUP * c, LANES)


def _attn_kernel(q_ref, kp_ref, kc_ref, kn_ref, vp_ref, vc_ref, vn_ref, bias_ref, sink_ref, o_ref):
    n = pl.program_id(1)
    nsteps = pl.num_programs(1)
    c = ATT_BLOCK
    nsub = q_ref.shape[0] // c
    pair = 2 * ATT_HD
    kwin = jnp.concatenate([kp_ref[...], kc_ref[...], kn_ref[...]], axis=0).astype(F32)
    vwin = jnp.concatenate([vp_ref[...], vc_ref[...], vn_ref[...]], axis=0).astype(F32)
    kroll = pltpu.roll(kwin, ATT_HD, axis=1)
    vroll = pltpu.roll(vwin, ATT_HD, axis=1)
    lo = lax.broadcasted_iota(jnp.int32, kwin.shape, 1) < ATT_HD
    k_lo = [jnp.where(lo, kwin, 0.0).astype(BF16), jnp.where(lo, kroll, 0.0).astype(BF16)]
    k_hi = [jnp.where(lo, 0.0, kroll).astype(BF16), jnp.where(lo, 0.0, kwin).astype(BF16)]
    v_lo = [jnp.where(lo, vwin, 1.0).astype(BF16), jnp.where(lo, vroll, 1.0).astype(BF16)]
    v_hi = [jnp.where(lo, 1.0, vroll).astype(BF16), jnp.where(lo, 1.0, vwin).astype(BF16)]
    lo_out = lax.broadcasted_iota(jnp.int32, (2 * c, pair), 1) < ATT_HD
    for j in range(nsub):
        case = jnp.int32(0)
        if j == 0:
            case = case + (n == 0).astype(jnp.int32)
        if j == nsub - 1:
            case = case + 2 * (n == nsteps - 1).astype(jnp.int32)
        band = slice(j * c, (j + 3) * c)
        qrows = slice(j * c, (j + 1) * c)
        for g in range(ATT_KV):
            qg = q_ref[qrows, g * 2 * pair:(g + 1) * 2 * pair].astype(F32) * (ATT_HD ** -0.5)
            lhs = jnp.concatenate([qg[:, :pair], qg[:, pair:]], axis=0).astype(BF16)
            rhs = jnp.concatenate([k_lo[g][band], k_hi[g][band]], axis=0)
            lg = _dot_nt(lhs, rhs)
            logits = jnp.concatenate([lg[:, :3 * c], lg[:, 3 * c:]], axis=0) + bias_ref[case, g]
            s = sink_ref[g]
            m = jnp.maximum(jnp.max(logits, axis=-1, keepdims=True), s)
            p = jnp.exp(logits - jnp.concatenate([m] * 3, axis=1)).astype(BF16)
            es = jnp.exp(s - m)
            o_even = _dot(p[:2 * c], v_lo[g][band])
            o_odd = _dot(p[2 * c:], v_hi[g][band])
            o = jnp.where(lo_out,
                          o_even / (pltpu.roll(o_even, ATT_HD, axis=1) + es[:2 * c]),
                          o_odd / (pltpu.roll(o_odd, ATT_HD, axis=1) + es[2 * c:]))
            o_ref[qrows, g * 2 * pair:g * 2 * pair + pair] = o[:c].astype(o_ref.dtype)
            o_ref[qrows, g * 2 * pair + pair:(g + 1) * 2 * pair] = o[c:].astype(o_ref.dtype)


def _attn(proj3, bias, sink):
    bsz, seq, _ = proj3.shape
    c = ATT_BLOCK
    nsub = max(k for k in (8, 4, 2, 1) if seq % (k * c) == 0)
    qb = nsub * c
    nsteps = seq // qb
    assert seq % c == 0
    nb = seq // c
    kcol = B_AK // KV_WIDTH
    vcol = B_AV // KV_WIDTH

    def band(col):
        return [
            pl.BlockSpec((None, c, KV_WIDTH), lambda b, n: (b, jnp.maximum(n * nsub - 1, 0), col)),
            pl.BlockSpec((None, qb, KV_WIDTH), lambda b, n: (b, n, col)),
            pl.BlockSpec((None, c, KV_WIDTH), lambda b, n: (b, jnp.minimum((n + 1) * nsub, nb - 1), col)),
        ]

    return pl.pallas_call(
        _attn_kernel,
        grid=(bsz, nsteps),
        in_specs=[pl.BlockSpec((None, qb, ATT_WIDTH), lambda b, n: (b, n, B_AQ // ATT_WIDTH))]
        + band(kcol) + band(vcol)
        + [pl.BlockSpec(bias.shape, lambda b, n: (0, 0, 0, 0)),
           pl.BlockSpec(sink.shape, lambda b, n: (0, 0, 0))],
        out_specs=pl.BlockSpec((None, qb, ATT_WIDTH), lambda b, n: (b, n, 0)),
        out_shape=jax.ShapeDtypeStruct((bsz, seq, ATT_WIDTH), BF16),
        compiler_params=_params(("arbitrary", "arbitrary")),
        name="window_attn",
    )(proj3, proj3, proj3, proj3, proj3, proj3, proj3, bias, sink)


def _first_argmax(vals):
    best, idx = vals[0], jnp.zeros(vals[0].shape, jnp.int32)
    for j in range(1, len(vals)):
        upd = vals[j] > best
        idx = jnp.where(upd, j, idx)
        best = jnp.where(upd, vals[j], best)
    return best, idx


def _select(vals, idx):
    out = vals[0]
    for j in range(1, len(vals)):
        out = jnp.where(idx == j, vals[j], out)
    return out


def _route(logits_t, rbias):
    m = jnp.max(logits_t, axis=0, keepdims=True)
    e = jnp.exp(logits_t - m)
    scores = e / jnp.sum(e, axis=0, keepdims=True)
    sel = scores + rbias
    srow = [scores[i:i + 1, :] for i in range(N_EXPERTS)]
    lrow = [sel[i:i + 1, :] for i in range(N_EXPERTS)]
    gscore = []
    for g in range(N_GROUPS):
        a, b, c, d = lrow[4 * g:4 * g + 4]
        hi1, lo1 = jnp.maximum(a, b), jnp.minimum(a, b)
        hi2, lo2 = jnp.maximum(c, d), jnp.minimum(c, d)
        gscore.append(jnp.maximum(hi1, hi2) + jnp.maximum(jnp.minimum(hi1, hi2), jnp.maximum(lo1, lo2)))
    _, gi = _first_argmax(gscore)
    ing = [_select([lrow[4 * g + j] for g in range(N_GROUPS)], gi) for j in range(EXPERTS_PER_GROUP)]
    sg = [_select([srow[4 * g + j] for g in range(N_GROUPS)], gi) for j in range(EXPERTS_PER_GROUP)]
    _, i1 = _first_argmax(ing)
    rest = [jnp.where(i1 == j, -jnp.inf, ing[j]) for j in range(EXPERTS_PER_GROUP)]
    _, i2 = _first_argmax(rest)
    s1, s2 = _select(sg, i1), _select(sg, i2)
    tot = s1 + s2
    w1, w2 = s1 / tot, s2 / tot
    idx1 = gi * EXPERTS_PER_GROUP + i1
    idx2 = gi * EXPERTS_PER_GROUP + i2
    return idx1, idx2, w1, w2


def _merge_kernel(oh_ref, oa_ref, gh_ref, ga_ref, x_ref, mod_ref, nf_ref, wbh_ref, wba_ref, wo_ref,
                  wrt_ref, rb_ref, before_ref, x1_ref, h2_ref, ri_ref, rw_ref, cnt_ref, carry_ref, mg_ref):
    @pl.when(pl.program_id(0) == 0)
    def _():
        carry_ref[...] = jnp.zeros_like(carry_ref)

    tm = x_ref.shape[0]
    blocks = [slice(j * MXU_COLS, (j + 1) * MXU_COLS) for j in range(D_MODEL // MXU_COLS)]
    for cols in blocks:
        mh = _dot(oh_ref[...], wbh_ref[:, cols])
        ma = _dot(oa_ref[...], wba_ref[:, cols])
        mg_ref[:, cols] = ((jnp.tanh(gh_ref[:, cols].astype(F32)) + 1.0) * mh
                           + (jnp.tanh(ga_ref[:, cols].astype(F32)) + 1.0) * ma).astype(BF16)
    ssq = jnp.zeros((tm, 1), F32)
    for cols in blocks:
        x1 = x_ref[:, cols] + mod_ref[2:3, cols] * _dot(mg_ref[...], wo_ref[:, cols])
        x1_ref[:, cols] = x1
        ssq = ssq + jnp.sum(x1 * x1, axis=-1, keepdims=True)
    inv = lax.rsqrt(ssq * (1.0 / D_MODEL) + EPS)
    wr = wrt_ref[...]
    w_hi = wr.astype(BF16)
    w_lo = (wr - w_hi.astype(F32)).astype(BF16)
    logits_t = jnp.zeros((N_EXPERTS, tm), F32)
    half = len(blocks) // 2
    for j in range(half):
        pair = []
        for cols in (blocks[j], blocks[half + j]):
            h2 = x1_ref[:, cols] * inv * nf_ref[:, cols] * (1.0 + mod_ref[4:5, cols]) + mod_ref[3:4, cols]
            h_hi = h2.astype(BF16)
            h_lo = (h2 - h_hi.astype(F32)).astype(BF16)
            logits_t = logits_t + ((_dot_nt(w_hi[:, cols], h_hi) + _dot_nt(w_hi[:, cols], h_lo))
                                   + _dot_nt(w_lo[:, cols], h_hi))
            pair.append(h2)
        h2_ref[:, blocks[j]] = _pack_bf16_pairs(jnp.concatenate(pair, axis=1))
    idx1, idx2, w1, w2 = _route(logits_t, rb_ref[:, 0:1])
    erow = lax.broadcasted_iota(jnp.int32, logits_t.shape, 0)
    oh1 = erow == idx1
    oh2 = erow == idx2
    oh = jnp.where(oh1 | oh2, 1.0, 0.0)
    carry = carry_ref[...]
    pref = _dot(oh.astype(BF16), before_ref[...]) + carry[:, 0:1]
    rank1 = jnp.sum(jnp.where(oh1, pref, 0.0), axis=0, keepdims=True)
    rank2 = jnp.sum(jnp.where(oh2, pref, 0.0), axis=0, keepdims=True)
    carry = carry + jnp.sum(oh, axis=1, keepdims=True)
    carry_ref[...] = carry
    cnt_ref[...] = carry
    ri_ref[...] = jnp.concatenate([idx1, idx2, rank1.astype(jnp.int32), rank2.astype(jnp.int32)], axis=0)
    rw_ref[...] = jnp.concatenate([w1, w2], axis=0)


def _merge(o_h, o_a, proj, x2d, mod, nf, wbh, wba, wo, wrt, rb, seq):
    n = x2d.shape[0]
    tm = min(1024, seq)
    per_seq = seq // tm
    const = lambda i: (0, 0)
    before = jnp.asarray(np.arange(tm)[:, None] < np.arange(tm)[None, :], BF16)
    return pl.pallas_call(
        _merge_kernel,
        grid=(n // tm,),
        in_specs=[
            pl.BlockSpec((tm, HG_WIDTH), lambda i: (i, 0)),
            pl.BlockSpec((tm, ATT_WIDTH), lambda i: (i, 0)),
            pl.BlockSpec((tm, D_MODEL), lambda i: (i, B_GATE_H // D_MODEL)),
            pl.BlockSpec((tm, D_MODEL), lambda i: (i, B_GATE_A // D_MODEL)),
            pl.BlockSpec((tm, D_MODEL), lambda i: (i, 0)),
            pl.BlockSpec((None, N_MOD, D_MODEL), lambda i: (i // per_seq, 0, 0)),
            pl.BlockSpec((1, D_MODEL), const),
            pl.BlockSpec(wbh.shape, const),
            pl.BlockSpec(wba.shape, const),
            pl.BlockSpec(wo.shape, const),
            pl.BlockSpec(wrt.shape, const),
            pl.BlockSpec(rb.shape, const),
            pl.BlockSpec(before.shape, const),
        ],
        out_specs=[
            pl.BlockSpec((tm, D_MODEL), lambda i: (i, 0)),
            pl.BlockSpec((tm, D_MODEL // 2), lambda i: (i, 0)),
            pl.BlockSpec((4, tm), lambda i: (0, i)),
            pl.BlockSpec((2, tm), lambda i: (0, i)),
            pl.BlockSpec((N_EXPERTS, LANES), const),
        ],
        out_shape=[
            jax.ShapeDtypeStruct((n, D_MODEL), F32),
            jax.ShapeDtypeStruct((n, D_MODEL // 2), jnp.uint32),
            jax.ShapeDtypeStruct((4, n), jnp.int32),
            jax.ShapeDtypeStruct((2, n), F32),
            jax.ShapeDtypeStruct((N_EXPERTS, LANES), F32),
        ],
        scratch_shapes=[pltpu.VMEM((N_EXPERTS, LANES), F32), pltpu.VMEM((tm, D_MODEL), BF16)],
        compiler_params=_params(("arbitrary",)),
        name="merge_router",
    )(o_h, o_a, proj, proj, x2d, mod, nf, wbh, wba, wo, wrt, rb, before)


MOE_TILE = 512
SC_WINDOW = 64


def _sc_mesh():
    return plsc.VectorSubcoreMesh(core_axis_name="c", subcore_axis_name="s")


def _sc_dispatch(h, pos, p):
    n, d = h.shape
    win = SC_WINDOW
    info = plsc.get_sparse_core_info()
    workers = info.num_cores * info.num_subcores
    assert n % (2 * win * workers) == 0, "each vector subcore walks its windows two at a time"
    wpt = n // (win * workers)
    pos_w = pos.reshape(2, n // win, win).transpose(1, 0, 2)

    @functools.partial(
        pl.kernel, out_type=jax.ShapeDtypeStruct((p, d), h.dtype), mesh=_sc_mesh(),
        scratch_types=[pltpu.VMEM((wpt, 2, win), jnp.int32), pltpu.VMEM((2, win, d), h.dtype),
                       pltpu.SemaphoreType.DMA((2,))],
        name="moe_dispatch")
    def dispatch(h_hbm, pos_hbm, o_hbm, idx_v, rows_v, load_sem):
        wid = lax.axis_index("c") * info.num_subcores + lax.axis_index("s")
        first = wid * wpt
        pltpu.sync_copy(pos_hbm.at[pl.ds(first, wpt)], idx_v)

        def load(j, slot):
            return pltpu.make_async_copy(h_hbm.at[pl.ds((first + j) * win, win)], rows_v.at[slot],
                                         load_sem.at[slot])

        load(0, 0).start()

        @pl.loop(0, wpt, step=2)
        def _(j):
            for slot in range(2):
                jj = j + slot
                load(jj, slot).wait()

                @pl.when(jj + 1 < wpt)
                def _():
                    load(jj + 1, 1 - slot).start()

                pltpu.sync_copy(rows_v.at[slot], o_hbm.at[idx_v.at[jj, 0]])
                pltpu.sync_copy(rows_v.at[slot], o_hbm.at[idx_v.at[jj, 1]])

    return dispatch(h, pos_w)


def _sc_combine(ys, pos):
    n = pos.shape[1]
    d = ys.shape[1]
    win = SC_WINDOW // 2
    info = plsc.get_sparse_core_info()
    workers = info.num_cores * info.num_subcores
    assert n % (2 * win * workers) == 0, "each vector subcore walks its windows two at a time"
    wpt = n // (win * workers)
    pos_w = pos.reshape(2, n // win, win).transpose(1, 0, 2)
    out =jax.ShapeDtypeStruct((n, d), ys.dtype)

    @functools.partial(
        pl.kernel, out_type=(out, out), mesh=_sc_mesh(),
        scratch_types=[pltpu.VMEM((wpt, 2, win), jnp.int32), pltpu.VMEM((2, 2, win, d), ys.dtype),
                       pltpu.SemaphoreType.DMA((2, 2)), pltpu.SemaphoreType.DMA((2,))],
        name="moe_combine")
    def combine(ys_hbm, pos_hbm, a_hbm, b_hbm, idx_v, rows_v, gather_sem, store_sem):
        wid = lax.axis_index("c") * info.num_subcores + lax.axis_index("s")
        first = wid * wpt
        pltpu.sync_copy(pos_hbm.at[pl.ds(first, wpt)], idx_v)
        outs = (a_hbm, b_hbm)

        def gather(j, slot, k):
            return pltpu.make_async_copy(ys_hbm.at[idx_v.at[j, k]], rows_v.at[slot, k], gather_sem.at[slot, k])

        def store(j, slot, k):
            return pltpu.make_async_copy(rows_v.at[slot, k], outs[k].at[pl.ds((first + j) * win, win)],
                                         store_sem.at[k])

        gather(0, 0, 0).start()
        gather(0, 0, 1).start()

        @pl.loop(0, wpt, step=2)
        def _(j):
            for slot in range(2):
                jj = j + slot
                gather(jj, slot, 0).wait()
                gather(jj, slot, 1).wait()

                @pl.when(jj + 1 < wpt)
                def _():
                    gather(jj + 1, 1 - slot, 0).start()
                    gather(jj + 1, 1 - slot, 1).start()

                store(jj, slot, 0).start()
                store(jj, slot, 1).start()
                store(jj, slot, 0).wait()
                store(jj, slot, 1).wait()

    return combine(ys, pos_w)


def _expert_kernel(te_ref, nu_ref, x_ref, wg_ref, wu_ref, wd_ref, o_ref):
    del te_ref
    used = pl.program_id(0) < nu_ref[0]

    @pl.when(used)
    def _():
        x = _unpack_bf16_pairs(x_ref[...]).astype(BF16)
        he = _silu_of_half(_dot(x, wg_ref[...])) * _dot(x, wu_ref[...])
        o_ref[...] = _pack_bf16_pairs(_dot(he.astype(BF16), wd_ref[...]))

    @pl.when(jnp.logical_not(used))
    def _():
        o_ref[...] = jnp.zeros_like(o_ref)


def _experts(xs, tile_expert, n_used, wg, wu, wd, layer):
    p = xs.shape[0]
    tm = MOE_TILE
    grid_spec = pltpu.PrefetchScalarGridSpec(
        num_scalar_prefetch=2,
        grid=(p // tm,),
        in_specs=[
            pl.BlockSpec((tm, D_MODEL // 2), lambda i, te, nu: (i, 0)),
            pl.BlockSpec((None, None, D_MODEL, D_EXPERT), lambda i, te, nu: (layer, te[i], 0, 0)),
            pl.BlockSpec((None, None, D_MODEL, D_EXPERT), lambda i, te, nu: (layer, te[i], 0, 0)),
            pl.BlockSpec((None, None, D_EXPERT, D_MODEL), lambda i, te, nu: (layer, te[i], 0, 0)),
        ],
        out_specs=pl.BlockSpec((tm, D_MODEL // 2), lambda i, te, nu: (i, 0)),
    )
    return pl.pallas_call(
        _expert_kernel,
        grid_spec=grid_spec,
        out_shape=jax.ShapeDtypeStruct((p, D_MODEL // 2), jnp.uint32),
        compiler_params=_params(("arbitrary",)),
        name="moe_experts",
    )(tile_expert, n_used, xs, wg, wu, wd)


def _residual_kernel(ya_ref, yb_ref, w_ref, x1_ref, mod_ref, nfin_ref, o_ref, *, last):
    w = w_ref[...]
    y = w[:, 0:1] * _unpack_bf16_pairs(ya_ref[...]) + w[:, 1:2] * _unpack_bf16_pairs(yb_ref[...])
    x2 = x1_ref[...] + mod_ref[5:6, :] * y
    if last:
        x2 = _rms(x2, nfin_ref[...])
    o_ref[...] = x2


def _residual(ya, yb, w, x1, mod, nfin, seq, last):
    n = x1.shape[0]
    tm = min(1024, seq)
    per_seq = seq // tm
    row = pl.BlockSpec((tm, D_MODEL), lambda i: (i, 0))
    packed = pl.BlockSpec((tm, D_MODEL // 2), lambda i: (i, 0))
    return pl.pallas_call(
        functools.partial(_residual_kernel, last=last),
        grid=(n // tm,),
        in_specs=[packed, packed, pl.BlockSpec((tm, 2), lambda i: (i, 0)), row,
                  pl.BlockSpec((None, N_MOD, D_MODEL), lambda i: (i // per_seq, 0, 0)),
                  pl.BlockSpec((1, D_MODEL), lambda i: (0, 0))],
        out_specs=row,
        out_shape=jax.ShapeDtypeStruct((n, D_MODEL), F32),
        compiler_params=_params(("arbitrary",)),
        name="moe_residual",
    )(ya, yb, w, x1, mod, nfin)


def _moe(h2, ri, rw, cnt, wg, wu, wd, layer, x1, mod, nfin, seq, last):
    n = h2.shape[0]
    tm = MOE_TILE
    p = 2 * n + N_EXPERTS * tm
    counts = cnt[:, 0].astype(jnp.int32)
    padded = (counts + tm - 1) // tm * tm
    ends = jnp.cumsum(padded)
    starts = ends - padded
    base = jnp.zeros_like(ri[0:2])
    for e in range(1, N_EXPERTS):
        base = jnp.where(ri[0:2] == e, starts[e], base)
    pos = base + ri[2:4]
    tile_start = jnp.arange(p // tm, dtype=jnp.int32) * tm
    tile_expert = jnp.minimum(jnp.sum(tile_start[:, None] >= ends[None, :], axis=1), N_EXPERTS - 1)
    n_used = (ends[-1:] // tm).astype(jnp.int32)
    xs = _sc_dispatch(h2, pos, p)
    ys = _experts(xs, tile_expert.astype(jnp.int32), n_used, wg, wu, wd, layer)
    ya, yb = _sc_combine(ys, pos)
    return _residual(ya, yb, rw.T, x1, mod, nfin, seq, last)


def _split_w_in(w):
    hq_hf, hi, hg, att, gates = w[..., :1536], w[..., 1536:2048], w[..., 2048:2560], w[..., 2560:3328], w[..., 3328:]
    wa = (0.5 * jnp.concatenate([hq_hf, hg], axis=-1)).astype(BF16)
    wb = jnp.concatenate([0.5 * gates, hi, att], axis=-1).astype(BF16)
    return wa, wb


def _layer(x2d, mod_l, wts, l, bsz, seq, last):
    n = bsz * seq
    pa, pb, span = _inproj(x2d, mod_l, wts["norm_mix"][l:l + 1], wts["w_in_a"][l], wts["w_in_b"][l],
                           wts["hg_lb_fwd"], wts["hg_lb_bwd"], seq, l)
    pa3 = pa.reshape(bsz, seq, A_COLS)
    pb3 = pb.reshape(bsz, seq, B_COLS)
    safe_f = (span[:, 0, 0] <= HG_SAFE_SPAN).astype(jnp.int32)
    safe_b = (span[:, 4, 0] <= HG_SAFE_SPAN).astype(jnp.int32)
    o_f = _hgrn(safe_f, pa3, pb3, wts["hg_lb_fwd"], l, False)
    o_h = _hgrn(safe_b, pa3, pb3, wts["hg_lb_bwd"], l, True, o_f, wts["hg_norm"][l:l + 1])
    o_a = _attn(pb3, wts["bias"], wts["sink"][l])
    x1, h2, ri, rw, cnt = _merge(o_h.reshape(n, HG_WIDTH), o_a.reshape(n, ATT_WIDTH), pb, x2d, mod_l,
                                 wts["norm_ffn"][l:l + 1], wts["w_br_hgrn"][l], wts["w_br_att"][l],
                                 wts["w_out"][l], wts["w_router_t"], wts["router_bias"], seq)
    return _moe(h2, ri, rw, cnt, wts["w_gate"], wts["w_up"], wts["w_down"], l, x1, mod_l,
                wts["norm_final"], seq, last)


def kernel(x_prompt, x_sample, c_prompt, c_sample, w_ada, b_ada, norm_mix, norm_ffn, norm_final, w_in, hg_lb_fwd, hg_lb_bwd, hg_norm, att_sink, rel_bias, w_br_hgrn, w_br_att, w_out, w_router, router_bias, w_gate, w_up, w_down):
    depth = w_in.shape[0]
    bp, bs = c_prompt.shape[0], c_sample.shape[0]
    rows = -(-(bp + bs) // SUBLANES) * SUBLANES
    c_all = jnp.concatenate([c_prompt, c_sample, jnp.zeros((rows - bp - bs, D_MODEL), F32)], axis=0)
    mod = _ada(c_all, w_ada, b_ada).reshape(depth, rows, N_MOD, D_MODEL)
    bias, sink = _attn_tables(rel_bias, att_sink)
    w_in_a, w_in_b = _split_w_in(w_in)
    wts = {
        "norm_mix": norm_mix, "norm_ffn": norm_ffn, "norm_final": norm_final.reshape(1, D_MODEL),
        "w_in_a": w_in_a, "w_in_b": w_in_b,
        "hg_lb_fwd": hg_lb_fwd, "hg_lb_bwd": hg_lb_bwd, "hg_norm": hg_norm,
        "sink": sink, "bias": bias,
        "w_br_hgrn": w_br_hgrn.astype(BF16), "w_br_att": w_br_att.astype(BF16), "w_out": (0.5 * w_out).astype(BF16),
        "w_router_t": w_router.T,
        "router_bias": jnp.broadcast_to(router_bias[:, None], (N_EXPERTS, LANES)),
        "w_gate": (0.5 * w_gate).astype(BF16), "w_up": w_up.astype(BF16), "w_down": w_down.astype(BF16),
    }
    groups = [(x_prompt, mod[:, :bp]), (x_sample, mod[:, bp:bp + bs])]
    xs = [x.reshape(-1, D_MODEL) for x, _ in groups]
    for l in range(depth):
        for gi, (x, mod_g) in enumerate(groups):
            xs[gi] = _layer(xs[gi], mod_g[l], wts, l, x.shape[0], x.shape[1], l == depth - 1)
    return tuple(y.reshape(x.shape) for y, (x, _) in zip(xs, groups))
```

```python
import functools

import numpy as np
import jax
import jax.numpy as jnp
from jax import lax
from jax.experimental import pallas as pl
from jax.experimental.pallas import tpu as pltpu
from jax.experimental.pallas import tpu_sc as plsc

D_MODEL = 1024
HG_DK = 128
HG_WIDTH = 512
HG_HEADS = 4
HG_SUB = 64
HG_LEVELS = 6
HG_GROUP = 32
HG_SAFE_SPAN = 80.0
HG_BLOCK = 512
ATT_HD = 64
ATT_HEADS = 8
ATT_KV = 2
ATT_GROUP = 4
ATT_WIDTH = 512
KV_WIDTH = 128
WINDOW = 128
ATT_BLOCK = 128
N_BUCKETS = 32
REL_MAX_DIST = 128
N_EXPERTS = 16
N_GROUPS = 4
EXPERTS_PER_GROUP = 4
D_EXPERT = 512
N_MOD = 6
IN_COLS = 5376
EPS = 1e-6
NEG_BIG = -1e30
TINY = 1e-30

A_COLS = 2048
A_HQ = 0
A_HF_FWD = 512
A_HF_BWD = 1024
A_HG = 1536
B_COLS = 3328
B_GATE_H = 0
B_GATE_A = 1024
B_HI = 2048
B_AQ = 2560
B_AK = 3072
B_AV = 3200

V7X_VMEM_LIMIT = 56 * 1024 * 1024
MXU_COLS = 256
LANES = 128
SUBLANES = 8

F32 = jnp.float32
BF16 = jnp.bfloat16
HIGHEST = lax.Precision.HIGHEST


def _params(sem):
    return pltpu.CompilerParams(dimension_semantics=sem, vmem_limit_bytes=V7X_VMEM_LIMIT)


def _dot(a, b):
    return jnp.dot(a, b, preferred_element_type=F32)


def _dot_nt(a, b):
    return lax.dot_general(a, b, (((1,), (1,)), ((), ())), preferred_element_type=F32)


def _dot_tn(a, b):
    return lax.dot_general(a, b, (((0,), (0,)), ((), ())), preferred_element_type=F32)


def _sigmoid(x):
    return 0.5 * jnp.tanh(0.5 * x) + 0.5


def _pack_bf16_pairs(x):
    c = x.shape[1] // 2
    bits = lax.bitcast_convert_type(x.astype(BF16).astype(F32), jnp.uint32)
    return (bits[:, :c] >> 16) | (bits[:, c:] & jnp.uint32(0xFFFF0000))


def _unpack_bf16_pairs(u):
    lo = lax.bitcast_convert_type(u << 16, F32)
    hi = lax.bitcast_convert_type(u & jnp.uint32(0xFFFF0000), F32)
    return jnp.concatenate([lo, hi], axis=1)


def _silu(x):
    return x * _sigmoid(x)


def _silu_of_half(xh):
    return xh * (jnp.tanh(xh) + 1.0)


def _rms(x, g):
    return x * lax.rsqrt(jnp.mean(x * x, axis=-1, keepdims=True) + EPS) * g


def _ada_kernel(c_ref, w_ref, b_ref, o_ref):
    c = c_ref[...]
    o_ref[...] = jnp.dot(_silu(c), w_ref[...], precision=HIGHEST, preferred_element_type=F32) + b_ref[...]


def _ada(c_all, w_ada, b_ada):
    depth = w_ada.shape[0]
    rows = c_all.shape[0]
    ncol = w_ada.shape[2]
    tn = 1024
    return pl.pallas_call(
        _ada_kernel,
        grid=(depth, ncol // tn),
        in_specs=[
            pl.BlockSpec((rows, D_MODEL), lambda l, j: (0, 0)),
            pl.BlockSpec((None, D_MODEL, tn), lambda l, j: (l, 0, j)),
            pl.BlockSpec((None, 1, tn), lambda l, j: (l, 0, j)),
        ],
        out_specs=pl.BlockSpec((None, rows, tn), lambda l, j: (l, 0, j)),
        out_shape=jax.ShapeDtypeStruct((depth, rows, ncol), F32),
        compiler_params=_params(("arbitrary", "arbitrary")),
        name="ada_mod",
    )(c_all, w_ada, b_ada.reshape(depth, 1, ncol))


def _lower_bound_row(gam_ref, layer):
    rows = [gam_ref[d:d + 1, :] for d in range(gam_ref.shape[0])]
    m = functools.reduce(jnp.maximum, rows)
    es = [jnp.exp(r - m) for r in rows]
    tot = functools.reduce(lambda a, b: a + b, es)
    ps = [e / tot for e in es]
    cum = ps[0]
    for d in range(1, layer + 1):
        cum = cum + ps[d]
    return jnp.clip(cum - ps[0], 0.0, 1.0)


def _forget(zh, lb):
    return 0.5 * (1.0 + lb) + (0.5 * (1.0 - lb)) * jnp.tanh(zh)


def _inproj_kernel(x_ref, mod_ref, g_ref, wa_ref, wb_ref, gf_ref, gb_ref, oa_ref, ob_ref, span_ref, *, layer):
    x = x_ref[...]
    h = (_rms(x, g_ref[...]) * (1.0 + mod_ref[1:2, :]) + mod_ref[0:1, :]).astype(BF16)
    oa = _dot(h, wa_ref[...])
    oa_ref[...] = oa
    ob_ref[...] = _dot(h, wb_ref[...]).astype(BF16)
    spans = []
    for gam_ref, col in ((gf_ref, A_HF_FWD), (gb_ref, A_HF_BWD)):
        f = _forget(oa[:, col:col + HG_WIDTH], _lower_bound_row(gam_ref, layer))
        g = jnp.log(jnp.maximum(f, TINY))
        gsum = jnp.sum(g.reshape(g.shape[0] // HG_GROUP, HG_GROUP, HG_WIDTH), axis=1)
        spans.append(jnp.max(jnp.max(-gsum, axis=0, keepdims=True), axis=1, keepdims=True))
    half = lax.broadcasted_iota(jnp.int32, span_ref.shape, 0) < span_ref.shape[0] // 2
    span_ref[...] = jnp.where(half, spans[0], spans[1])


def _inproj(x2d, mod, g, wa, wb, gam_f, gam_b, seq, layer):
    n = x2d.shape[0]
    tm = min(HG_BLOCK, seq)
    per_seq = seq // tm
    const = lambda i: (0, 0)
    return pl.pallas_call(
        functools.partial(_inproj_kernel, layer=layer),
        grid=(n // tm,),
        in_specs=[
            pl.BlockSpec((tm, D_MODEL), lambda i: (i, 0)),
            pl.BlockSpec((None, N_MOD, D_MODEL), lambda i: (i // per_seq, 0, 0)),
            pl.BlockSpec((1, D_MODEL), const),
            pl.BlockSpec((D_MODEL, A_COLS), const, pipeline_mode=pl.Buffered(1)),
            pl.BlockSpec((D_MODEL, B_COLS), const, pipeline_mode=pl.Buffered(1)),
            pl.BlockSpec(gam_f.shape, const),
            pl.BlockSpec(gam_b.shape, const),
        ],
        out_specs=[pl.BlockSpec((tm, A_COLS), lambda i: (i, 0)),
                   pl.BlockSpec((tm, B_COLS), lambda i: (i, 0)),
                   pl.BlockSpec((None, SUBLANES, LANES), lambda i: (i, 0, 0))],
        out_shape=[jax.ShapeDtypeStruct((n, A_COLS), F32), jax.ShapeDtypeStruct((n, B_COLS), BF16),
                   jax.ShapeDtypeStruct((n // tm, SUBLANES, LANES), F32)],
        compiler_params=_params(("arbitrary",)),
        name="inproj",
    )(x2d, mod, g, wa, wb, gam_f, gam_b)


def _hgrn_level_tables(rev):
    c = HG_SUB
    r = lax.broadcasted_iota(jnp.int32, (c, c), 0)
    s = lax.broadcasted_iota(jnp.int32, (c, c), 1)
    row = lax.broadcasted_iota(jnp.int32, (c, HG_WIDTH), 0)
    sels, qsides, pairs = [], [], []
    for lev in range(HG_LEVELS):
        half = 1 << lev
        blk = 2 * half
        r_up = (r & (blk - 1)) >= half
        s_up = (s & (blk - 1)) >= half
        base = r - (r & (blk - 1))
        mrow = base + (half if rev else half - 1)
        sels.append(jnp.where(s == mrow, 1.0, 0.0).astype(F32))
        row_up = (row & (blk - 1)) >= half
        qsides.append(~row_up if rev else row_up)
        same = (r >> (lev + 1)) == (s >> (lev + 1))
        pairs.append(same & ((~r_up & s_up) if rev else (r_up & ~s_up)))
    return jnp.concatenate(sels, axis=0), qsides, pairs, r == s


def _hgrn_kernel(safe_ref, *refs, layer, rev):
    if rev:
        q_ref, f_ref, v_ref, gam_ref, hg_ref, of_ref, nrm_ref, o_ref = refs[:8]
    else:
        q_ref, f_ref, v_ref, gam_ref, o_ref = refs[:5]
    st_ref, qs_ref, ks_ref, bs_ref, a_ref, qi_ref, dec_ref, up_ref, sb_ref = refs[-9:]

    @pl.when(pl.program_id(1) == 0)
    def _():
        st_ref[...] = jnp.zeros_like(st_ref)

    nc = pl.num_programs(1)
    chunk = (nc - 1 - pl.program_id(1)) if rev else pl.program_id(1)
    safe = safe_ref[pl.program_id(0) * nc + chunk] != 0
    c = HG_SUB
    w = HG_WIDTH
    nsub = q_ref.shape[0] // c
    lb = _lower_bound_row(gam_ref, layer)
    r_i = lax.broadcasted_iota(jnp.int32, (c, c), 0)
    s_i = lax.broadcasted_iota(jnp.int32, (c, c), 1)
    causal = (s_i >= r_i) if rev else (s_i <= r_i)
    tri = jnp.where(causal, 1.0, 0.0).astype(BF16)
    row = lax.broadcasted_iota(jnp.int32, (c, w), 0)
    far = (row < HG_GROUP) if rev else (row >= HG_GROUP)
    ref_row = HG_GROUP if rev else HG_GROUP - 1
    last_row = 0 if rev else c - 1
    heads = [slice(h * HG_DK, (h + 1) * HG_DK) for h in range(HG_HEADS)]

    def rows_of(i):
        ci = (nsub - 1 - i) if rev else i
        return pl.ds(pl.multiple_of(ci * c, c), c)

    def gates_pass(i, carry):
        sl = rows_of(i)
        zq = q_ref[sl, :]
        f = _forget(f_ref[sl, :], lb)
        g = jnp.log(jnp.maximum(f, TINY))
        qs_ref[sl, :] = _silu_of_half(zq) * (HG_DK ** -0.5)
        ks_ref[sl, :] = 1.0 - f
        g1 = g.astype(BF16)
        r1 = g - g1.astype(F32)
        g2 = r1.astype(BF16)
        g3 = (r1 - g2.astype(F32)).astype(BF16)
        bb = _dot(tri, jnp.concatenate([g1, g2, g3], axis=1))
        bs_ref[sl, :] = (bb[:, :w] + bb[:, w:2 * w]) + bb[:, 2 * w:]
        return carry

    lax.fori_loop(0, nsub, gates_pass, 0, unroll=8)

    def gates(i):
        sl = rows_of(i)
        return sl, qs_ref[sl, :], ks_ref[sl, :], bs_ref[sl, :]

    def stage(i, sl, a, q_in, k_dec, dec):
        vb = v_ref[sl, :].astype(BF16)
        for h, hs in enumerate(heads):
            a_ref[i, h] = a[h].astype(BF16)
            up_ref[i, h] = _dot_tn(vb[:, hs], k_dec[:, hs])
        qi_ref[sl, :] = q_in
        dec_ref[pl.ds(pl.multiple_of(i * SUBLANES, SUBLANES), SUBLANES), :] = jnp.broadcast_to(dec, (SUBLANES, w))

    def scan_states():
        for h, hs in enumerate(heads):
            st = st_ref[h]
            for i in range(nsub):
                sb_ref[i, h] = st.astype(BF16)
                st = st * dec_ref[i * SUBLANES:i * SUBLANES + 1, hs] + up_ref[i, h]
            st_ref[h] = st

    def finish_pass(i, carry):
        sl = rows_of(i)
        vb = v_ref[sl, :].astype(BF16)
        q_in = qi_ref[sl, :]
        outs = [_dot_nt(q_in[:, hs], sb_ref[i, h]) + _dot(a_ref[i, h], vb[:, hs]) for h, hs in enumerate(heads)]
        o_all = jnp.concatenate(outs, axis=1)
        if rev:
            tot = of_ref[sl, :] + o_all
            nrm = nrm_ref[...]
            ys = [_rms(tot[:, hs], nrm) for hs in heads]
            o_ref[sl, :] = (jnp.concatenate(ys, axis=1) * _silu_of_half(hg_ref[sl, :])).astype(o_ref.dtype)
        else:
            o_ref[sl, :] = o_all
        return carry

    def factored(i, carry):
        sl, q, k, b = gates(i)
        r = b[ref_row:ref_row + 1, :]
        bl = b[last_row:last_row + 1, :]
        rg = jnp.where(far, r, 0.0)
        qt = q * jnp.exp(b - rg)
        kt = k * jnp.exp(rg - b)
        er = jnp.exp(r)
        qn = jnp.where(far, 0.0, qt).astype(BF16)
        qf = jnp.where(far, qt, 0.0).astype(BF16)
        kc = jnp.where(far, kt, kt * er).astype(BF16)
        ktb = kt.astype(BF16)
        q_in = jnp.where(far, qt * er, qt).astype(BF16)
        k_dec = (kt * jnp.where(far, jnp.exp(bl - r), jnp.exp(bl))).astype(BF16)
        a = []
        for hs in heads:
            lhs = jnp.concatenate([qn[:, hs], qf[:, hs]], axis=1)
            rhs = jnp.concatenate([ktb[:, hs], kc[:, hs]], axis=1)
            a.append(jnp.where(causal, _dot_nt(lhs, rhs), 0.0))
        stage(i, sl, a, q_in, k_dec, jnp.exp(bl))
        return carry

    def levels(i, carry):
        sl, q, k, b = gates(i)
        sel_all, qsides, pairs, eye = _hgrn_level_tables(rev)
        bl = b[last_row:last_row + 1, :]
        q_in = (q * jnp.exp(b)).astype(BF16)
        k_dec = (k * jnp.exp(bl - b)).astype(BF16)
        bref_all = jnp.dot(sel_all, b, precision=HIGHEST, preferred_element_type=F32)
        qb = q.astype(BF16)
        kb = k.astype(BF16)
        a = [jnp.where(eye, _dot_nt(qb[:, hs], kb[:, hs]), 0.0) for hs in heads]
        for lev in range(HG_LEVELS):
            bref = bref_all[lev * c:(lev + 1) * c, :]
            qs = qsides[lev]
            x = jnp.exp(jnp.where(qs, b - bref, bref - b))
            ql = jnp.where(qs, q * x, 0.0).astype(BF16)
            kl = jnp.where(qs, 0.0, k * x).astype(BF16)
            for h, hs in enumerate(heads):
                a[h] = a[h] + jnp.where(pairs[lev], _dot_nt(ql[:, hs], kl[:, hs]), 0.0)
        stage(i, sl, a, q_in, k_dec, jnp.exp(bl))
        return carry

    @pl.when(safe)
    def _():
        lax.fori_loop(0, nsub, factored, 0, unroll=8)

    @pl.when(jnp.logical_not(safe))
    def _():
        lax.fori_loop(0, nsub, levels, 0)

    scan_states()
    lax.fori_loop(0, nsub, finish_pass, 0, unroll=8)


def _hgrn(safe, pa3, pb3, gamma, layer, rev, o_fwd=None, nrm=None):
    bsz, seq, _ = pa3.shape
    cb = min(HG_BLOCK, seq)
    assert seq % cb == 0 and cb % HG_SUB == 0
    nc = seq // cb
    wblk = HG_WIDTH

    def cmap(col):
        if rev:
            return lambda b, c, safe_ref: (b, nc - 1 - c, col // wblk)
        return lambda b, c, safe_ref: (b, c, col // wblk)

    const = lambda b, c, safe_ref: (0, 0)
    in_specs = [
        pl.BlockSpec((None, cb, wblk), cmap(A_HQ)),
        pl.BlockSpec((None, cb, wblk), cmap(A_HF_BWD if rev else A_HF_FWD)),
        pl.BlockSpec((None, cb, wblk), cmap(B_HI)),
        pl.BlockSpec(gamma.shape, const),
    ]
    args = [pa3, pa3, pb3, gamma]
    if rev:
        in_specs += [
            pl.BlockSpec((None, cb, wblk), cmap(A_HG)),
            pl.BlockSpec((None, cb, wblk), cmap(0)),
            pl.BlockSpec((1, HG_DK), const),
        ]
        args += [pa3, o_fwd, nrm]
    out_dtype = BF16 if rev else F32
    grid_spec = pltpu.PrefetchScalarGridSpec(
        num_scalar_prefetch=1,
        grid=(bsz, nc),
        in_specs=in_specs,
        out_specs=pl.BlockSpec((None, cb, wblk), cmap(0)),
        scratch_shapes=[pltpu.VMEM((HG_HEADS, HG_DK, HG_DK), F32)] + [pltpu.VMEM((cb, wblk), F32)] * 3
        + [pltpu.VMEM((cb // HG_SUB, HG_HEADS, HG_SUB, HG_SUB), BF16), pltpu.VMEM((cb, wblk), BF16),
           pltpu.VMEM((cb // HG_SUB * SUBLANES, wblk), F32),
           pltpu.VMEM((cb // HG_SUB, HG_HEADS, HG_DK, HG_DK), F32),
           pltpu.VMEM((cb // HG_SUB, HG_HEADS, HG_DK, HG_DK), BF16)],
    )
    return pl.pallas_call(
        functools.partial(_hgrn_kernel, layer=layer, rev=rev),
        grid_spec=grid_spec,
        out_shape=jax.ShapeDtypeStruct((bsz, seq, HG_WIDTH), out_dtype),
        compiler_params=_params(("arbitrary", "arbitrary")),
        name="hgrn_bwd" if rev else "hgrn_fwd",
    )(safe, *args)


ATT_ROW_HEADS = (0, 2, 1, 3)


def _t5_buckets(rel):
    half = N_BUCKETS // 2
    ret = np.where(rel > 0, half, 0)
    n = np.abs(rel)
    max_exact = half // 2
    large = max_exact + (np.log(np.maximum(n, 1) / max_exact)
                         / np.log(REL_MAX_DIST / max_exact) * (half - max_exact)).astype(np.int32)
    large = np.minimum(large, half - 1)
    return (ret + np.where(n < max_exact, n, large)).astype(np.int32)


def _attn_tables(rel_bias, att_sink):
    c = ATT_BLOCK
    rel = np.arange(3 * c)[None, :] - c - np.arange(c)[:, None]
    onehot = np.equal(_t5_buckets(rel).reshape(-1, 1), np.arange(N_BUCKETS)).astype(np.float32)
    bias = jnp.dot(jnp.asarray(onehot, BF16), rel_bias.astype(F32), precision=HIGHEST)
    bias = bias.reshape(c, 3 * c, ATT_HEADS).transpose(2, 0, 1)
    bias = jnp.where(jnp.asarray(np.abs(rel) <= WINDOW)[None], bias, NEG_BIG)
    col = np.arange(3 * c)
    cases = []
    for case in range(4):
        valid = np.ones(3 * c, bool)
        if case & 1:
            valid &= col >= c
        if case & 2:
            valid &= col < 2 * c
        cases.append(jnp.where(jnp.asarray(valid)[None, None, :], bias, NEG_BIG))
    tab = jnp.stack(cases)
    order = np.array([[ATT_GROUP * g + h for h in ATT_ROW_HEADS] for g in range(ATT_KV)])
    tab = tab[:, order].reshape(4, ATT_KV, ATT_GROUP * c, 3 * c)
    sink = att_sink.astype(F32)[:, order]
    sink = jnp.broadcast_to(sink[..., None, None], sink.shape + (c, LANES))
    return tab, sink.reshape(att_sink.shape[0], ATT_KV, ATT_GROUP * c, LANES)


def _attn_kernel(q_ref, kp_ref, kc_ref, kn_ref, vp_ref, vc_ref, vn_ref, bias_ref, sink_ref, o_ref):
    n = pl.program_id(1)
    nsteps = pl.num_programs(1)
    c = ATT_BLOCK
    nsub = q_ref.shape[0] // c
    pair = 2 * ATT_HD
    kwin = jnp.concatenate([kp_ref[...], kc_ref[...], kn_ref[...]], axis=0).astype(F32)
    vwin = jnp.concatenate([vp_ref[...], vc_ref[...], vn_ref[...]], axis=0).astype(F32)
    kroll = pltpu.roll(kwin, ATT_HD, axis=1)
    vroll = pltpu.roll(vwin, ATT_HD, axis=1)
    lo = lax.broadcasted_iota(jnp.int32, kwin.shape, 1) < ATT_HD
    k_lo = [jnp.where(lo, kwin, 0.0).astype(BF16), jnp.where(lo, kroll, 0.0).astype(BF16)]
    k_hi = [jnp.where(lo, 0.0, kroll).astype(BF16), jnp.where(lo, 0.0, kwin).astype(BF16)]
    v_lo = [jnp.where(lo, vwin, 1.0).astype(BF16), jnp.where(lo, vroll, 1.0).astype(BF16)]
    v_hi = [jnp.where(lo, 1.0, vroll).astype(BF16), jnp.where(lo, 1.0, vwin).astype(BF16)]
    lo_out = lax.broadcasted_iota(jnp.int32, (2 * c, pair), 1) < ATT_HD
    for j in range(nsub):
        case = jnp.int32(0)
        if j == 0:
            case = case + (n == 0).astype(jnp.int32)
        if j == nsub - 1:
            case = case + 2 * (n == nsteps - 1).astype(jnp.int32)
        band = slice(j * c, (j + 3) * c)
        qrows = slice(j * c, (j + 1) * c)
        for g in range(ATT_KV):
            qg = q_ref[qrows, g * 2 * pair:(g + 1) * 2 * pair].astype(F32) * (ATT_HD ** -0.5)
            lhs = jnp.concatenate([qg[:, :pair], qg[:, pair:]], axis=0).astype(BF16)
            rhs = jnp.concatenate([k_lo[g][band], k_hi[g][band]], axis=0)
            lg = _dot_nt(lhs, rhs)
            logits = jnp.concatenate([lg[:, :3 * c], lg[:, 3 * c:]], axis=0) + bias_ref[case, g]
            s = sink_ref[g]
            m = jnp.maximum(jnp.max(logits, axis=-1, keepdims=True), s)
            p = jnp.exp(logits - jnp.concatenate([m] * 3, axis=1)).astype(BF16)
            es = jnp.exp(s - m)
            o_even = _dot(p[:2 * c], v_lo[g][band])
            o_odd = _dot(p[2 * c:], v_hi[g][band])
            o = jnp.where(lo_out,
                          o_even / (pltpu.roll(o_even, ATT_HD, axis=1) + es[:2 * c]),
                          o_odd / (pltpu.roll(o_odd, ATT_HD, axis=1) + es[2 * c:]))
            o_ref[qrows, g * 2 * pair:g * 2 * pair + pair] = o[:c].astype(o_ref.dtype)
            o_ref[qrows, g * 2 * pair + pair:(g + 1) * 2 * pair] = o[c:].astype(o_ref.dtype)


def _attn(proj3, bias, sink):
    bsz, seq, _ = proj3.shape
    c = ATT_BLOCK
    nsub = max(k for k in (8, 4, 2, 1) if seq % (k * c) == 0)
    qb = nsub * c
    nsteps = seq // qb
    assert seq % c == 0
    nb = seq // c
    kcol = B_AK // KV_WIDTH
    vcol = B_AV // KV_WIDTH

    def band(col):
        return [
            pl.BlockSpec((None, c, KV_WIDTH), lambda b, n: (b, jnp.maximum(n * nsub - 1, 0), col)),
            pl.BlockSpec((None, qb, KV_WIDTH), lambda b, n: (b, n, col)),
            pl.BlockSpec((None, c, KV_WIDTH), lambda b, n: (b, jnp.minimum((n + 1) * nsub, nb - 1), col)),
        ]

    return pl.pallas_call(
        _attn_kernel,
        grid=(bsz, nsteps),
        in_specs=[pl.BlockSpec((None, qb, ATT_WIDTH), lambda b, n: (b, n, B_AQ // ATT_WIDTH))]
        + band(kcol) + band(vcol)
        + [pl.BlockSpec(bias.shape, lambda b, n: (0, 0, 0, 0)),
           pl.BlockSpec(sink.shape, lambda b, n: (0, 0, 0))],
        out_specs=pl.BlockSpec((None, qb, ATT_WIDTH), lambda b, n: (b, n, 0)),
        out_shape=jax.ShapeDtypeStruct((bsz, seq, ATT_WIDTH), BF16),
        compiler_params=_params(("arbitrary", "arbitrary")),
        name="window_attn",
    )(proj3, proj3, proj3, proj3, proj3, proj3, proj3, bias, sink)


def _first_argmax(vals):
    best, idx = vals[0], jnp.zeros(vals[0].shape, jnp.int32)
    for j in range(1, len(vals)):
        upd = vals[j] > best
        idx = jnp.where(upd, j, idx)
        best = jnp.where(upd, vals[j], best)
    return best, idx


def _select(vals, idx):
    out = vals[0]
    for j in range(1, len(vals)):
        out = jnp.where(idx == j, vals[j], out)
    return out


def _route(logits_t, rbias):
    m = jnp.max(logits_t, axis=0, keepdims=True)
    e = jnp.exp(logits_t - m)
    scores = e / jnp.sum(e, axis=0, keepdims=True)
    sel = scores + rbias
    srow = [scores[i:i + 1, :] for i in range(N_EXPERTS)]
    lrow = [sel[i:i + 1, :] for i in range(N_EXPERTS)]
    gscore = []
    for g in range(N_GROUPS):
        a, b, c, d = lrow[4 * g:4 * g + 4]
        hi1, lo1 = jnp.maximum(a, b), jnp.minimum(a, b)
        hi2, lo2 = jnp.maximum(c, d), jnp.minimum(c, d)
        gscore.append(jnp.maximum(hi1, hi2) + jnp.maximum(jnp.minimum(hi1, hi2), jnp.maximum(lo1, lo2)))
    _, gi = _first_argmax(gscore)
    ing = [_select([lrow[4 * g + j] for g in range(N_GROUPS)], gi) for j in range(EXPERTS_PER_GROUP)]
    sg = [_select([srow[4 * g + j] for g in range(N_GROUPS)], gi) for j in range(EXPERTS_PER_GROUP)]
    _, i1 = _first_argmax(ing)
    rest = [jnp.where(i1 == j, -jnp.inf, ing[j]) for j in range(EXPERTS_PER_GROUP)]
    _, i2 = _first_argmax(rest)
    s1, s2 = _select(sg, i1), _select(sg, i2)
    tot = s1 + s2
    w1, w2 = s1 / tot, s2 / tot
    idx1 = gi * EXPERTS_PER_GROUP + i1
    idx2 = gi * EXPERTS_PER_GROUP + i2
    return idx1, idx2, w1, w2


def _merge_kernel(oh_ref, oa_ref, gh_ref, ga_ref, x_ref, mod_ref, nf_ref, wbh_ref, wba_ref, wo_ref,
                  wrt_ref, rb_ref, before_ref, x1_ref, h2_ref, ri_ref, rw_ref, cnt_ref, carry_ref, mg_ref):
    @pl.when(pl.program_id(0) == 0)
    def _():
        carry_ref[...] = jnp.zeros_like(carry_ref)

    tm = x_ref.shape[0]
    blocks = [slice(j * MXU_COLS, (j + 1) * MXU_COLS) for j in range(D_MODEL // MXU_COLS)]
    for cols in blocks:
        mh = _dot(oh_ref[...], wbh_ref[:, cols])
        ma = _dot(oa_ref[...], wba_ref[:, cols])
        mg_ref[:, cols] = ((jnp.tanh(gh_ref[:, cols].astype(F32)) + 1.0) * mh
                           + (jnp.tanh(ga_ref[:, cols].astype(F32)) + 1.0) * ma).astype(BF16)
    ssq = jnp.zeros((tm, 1), F32)
    for cols in blocks:
        x1 = x_ref[:, cols] + mod_ref[2:3, cols] * _dot(mg_ref[...], wo_ref[:, cols])
        x1_ref[:, cols] = x1
        ssq = ssq + jnp.sum(x1 * x1, axis=-1, keepdims=True)
    inv = lax.rsqrt(ssq * (1.0 / D_MODEL) + EPS)
    wr = wrt_ref[...]
    w_hi = wr.astype(BF16)
    w_lo = (wr - w_hi.astype(F32)).astype(BF16)
    logits_t = jnp.zeros((N_EXPERTS, tm), F32)
    half = len(blocks) // 2
    for j in range(half):
        pair = []
        for cols in (blocks[j], blocks[half + j]):
            h2 = x1_ref[:, cols] * inv * nf_ref[:, cols] * (1.0 + mod_ref[4:5, cols]) + mod_ref[3:4, cols]
            h_hi = h2.astype(BF16)
            h_lo = (h2 - h_hi.astype(F32)).astype(BF16)
            logits_t = logits_t + ((_dot_nt(w_hi[:, cols], h_hi) + _dot_nt(w_hi[:, cols], h_lo))
                                   + _dot_nt(w_lo[:, cols], h_hi))
            pair.append(h2)
        h2_ref[:, blocks[j]] = _pack_bf16_pairs(jnp.concatenate(pair, axis=1))
    idx1, idx2, w1, w2 = _route(logits_t, rb_ref[:, 0:1])
    erow = lax.broadcasted_iota(jnp.int32, logits_t.shape, 0)
    oh1 = erow == idx1
    oh2 = erow == idx2
    oh = jnp.where(oh1 | oh2, 1.0, 0.0)
    carry = carry_ref[...]
    pref = _dot(oh.astype(BF16), before_ref[...]) + carry[:, 0:1]
    rank1 = jnp.sum(jnp.where(oh1, pref, 0.0), axis=0, keepdims=True)
    rank2 = jnp.sum(jnp.where(oh2, pref, 0.0), axis=0, keepdims=True)
    carry = carry + jnp.sum(oh, axis=1, keepdims=True)
    carry_ref[...] = carry
    cnt_ref[...] = carry
    ri_ref[...] = jnp.concatenate([idx1, idx2, rank1.astype(jnp.int32), rank2.astype(jnp.int32)], axis=0)
    rw_ref[...] = jnp.concatenate([w1, w2], axis=0)


def _merge(o_h, o_a, proj, x2d, mod, nf, wbh, wba, wo, wrt, rb, seq):
    n = x2d.shape[0]
    tm = min(1024, seq)
    per_seq = seq // tm
    const = lambda i: (0, 0)
    before = jnp.asarray(np.arange(tm)[:, None] < np.arange(tm)[None, :], BF16)
    return pl.pallas_call(
        _merge_kernel,
        grid=(n // tm,),
        in_specs=[
            pl.BlockSpec((tm, HG_WIDTH), lambda i: (i, 0)),
            pl.BlockSpec((tm, ATT_WIDTH), lambda i: (i, 0)),
            pl.BlockSpec((tm, D_MODEL), lambda i: (i, B_GATE_H // D_MODEL)),
            pl.BlockSpec((tm, D_MODEL), lambda i: (i, B_GATE_A // D_MODEL)),
            pl.BlockSpec((tm, D_MODEL), lambda i: (i, 0)),
            pl.BlockSpec((None, N_MOD, D_MODEL), lambda i: (i // per_seq, 0, 0)),
            pl.BlockSpec((1, D_MODEL), const),
            pl.BlockSpec(wbh.shape, const),
            pl.BlockSpec(wba.shape, const),
            pl.BlockSpec(wo.shape, const),
            pl.BlockSpec(wrt.shape, const),
            pl.BlockSpec(rb.shape, const),
            pl.BlockSpec(before.shape, const),
        ],
        out_specs=[
            pl.BlockSpec((tm, D_MODEL), lambda i: (i, 0)),
            pl.BlockSpec((tm, D_MODEL // 2), lambda i: (i, 0)),
            pl.BlockSpec((4, tm), lambda i: (0, i)),
            pl.BlockSpec((2, tm), lambda i: (0, i)),
            pl.BlockSpec((N_EXPERTS, LANES), const),
        ],
        out_shape=[
            jax.ShapeDtypeStruct((n, D_MODEL), F32),
            jax.ShapeDtypeStruct((n, D_MODEL // 2), jnp.uint32),
            jax.ShapeDtypeStruct((4, n), jnp.int32),
            jax.ShapeDtypeStruct((2, n), F32),
            jax.ShapeDtypeStruct((N_EXPERTS, LANES), F32),
        ],
        scratch_shapes=[pltpu.VMEM((N_EXPERTS, LANES), F32), pltpu.VMEM((tm, D_MODEL), BF16)],
        compiler_params=_params(("arbitrary",)),
        name="merge_router",
    )(o_h, o_a, proj, proj, x2d, mod, nf, wbh, wba, wo, wrt, rb, before)


MOE_TILE = 512
SC_WINDOW = 64


def _sc_mesh():
    return plsc.VectorSubcoreMesh(core_axis_name="c", subcore_axis_name="s")


def _sc_dispatch(h, pos, p):
    n, d = h.shape
    win = SC_WINDOW
    info = plsc.get_sparse_core_info()
    workers = info.num_cores * info.num_subcores
    assert n % (2 * win * workers) == 0, "each vector subcore walks its windows two at a time"
    wpt = n // (win * workers)
    pos_w = pos.reshape(2, n // win, win).transpose(1, 0, 2)

    @functools.partial(
        pl.kernel, out_type=jax.ShapeDtypeStruct((p, d), h.dtype), mesh=_sc_mesh(),
        scratch_types=[pltpu.VMEM((wpt, 2, win), jnp.int32), pltpu.VMEM((2, win, d), h.dtype),
                       pltpu.SemaphoreType.DMA((2,))],
        name="moe_dispatch")
    def dispatch(h_hbm, pos_hbm, o_hbm, idx_v, rows_v, load_sem):
        wid = lax.axis_index("c") * info.num_subcores + lax.axis_index("s")
        first = wid * wpt
        pltpu.sync_copy(pos_hbm.at[pl.ds(first, wpt)], idx_v)

        def load(j, slot):
            return pltpu.make_async_copy(h_hbm.at[pl.ds((first + j) * win, win)], rows_v.at[slot],
                                         load_sem.at[slot])

        load(0, 0).start()

        @pl.loop(0, wpt, step=2)
        def _(j):
            for slot in range(2):
                jj = j + slot
                load(jj, slot).wait()

                @pl.when(jj + 1 < wpt)
                def _():
                    load(jj + 1, 1 - slot).start()

                pltpu.sync_copy(rows_v.at[slot], o_hbm.at[idx_v.at[jj, 0]])
                pltpu.sync_copy(rows_v.at[slot], o_hbm.at[idx_v.at[jj, 1]])

    return dispatch(h, pos_w)


def _sc_combine(ys, pos):
    n = pos.shape[1]
    d = ys.shape[1]
    win = SC_WINDOW // 2
    info = plsc.get_sparse_core_info()
    workers = info.num_cores * info.num_subcores
    assert n % (2 * win * workers) == 0, "each vector subcore walks its windows two at a time"
    wpt = n // (win * workers)
    pos_w = pos.reshape(2, n // win, win).transpose(1, 0, 2)
    out =jax.ShapeDtypeStruct((n, d), ys.dtype)

    @functools.partial(
        pl.kernel, out_type=(out, out), mesh=_sc_mesh(),
        scratch_types=[pltpu.VMEM((wpt, 2, win), jnp.int32), pltpu.VMEM((2, 2, win, d), ys.dtype),
                       pltpu.SemaphoreType.DMA((2, 2)), pltpu.SemaphoreType.DMA((2,))],
        name="moe_combine")
    def combine(ys_hbm, pos_hbm, a_hbm, b_hbm, idx_v, rows_v, gather_sem, store_sem):
        wid = lax.axis_index("c") * info.num_subcores + lax.axis_index("s")
        first = wid * wpt
        pltpu.sync_copy(pos_hbm.at[pl.ds(first, wpt)], idx_v)
        outs = (a_hbm, b_hbm)

        def gather(j, slot, k):
            return pltpu.make_async_copy(ys_hbm.at[idx_v.at[j, k]], rows_v.at[slot, k], gather_sem.at[slot, k])

        def store(j, slot, k):
            return pltpu.make_async_copy(rows_v.at[slot, k], outs[k].at[pl.ds((first + j) * win, win)],
                                         store_sem.at[k])

        gather(0, 0, 0).start()
        gather(0, 0, 1).start()

        @pl.loop(0, wpt, step=2)
        def _(j):
            for slot in range(2):
                jj = j + slot
                gather(jj, slot, 0).wait()
                gather(jj, slot, 1).wait()

                @pl.when(jj + 1 < wpt)
                def _():
                    gather(jj + 1, 1 - slot, 0).start()
                    gather(jj + 1, 1 - slot, 1).start()

                store(jj, slot, 0).start()
                store(jj, slot, 1).start()
                store(jj, slot, 0).wait()
                store(jj, slot, 1).wait()

    return combine(ys, pos_w)


def _expert_kernel(te_ref, nu_ref, x_ref, wg_ref, wu_ref, wd_ref, o_ref):
    del te_ref
    used = pl.program_id(0) < nu_ref[0]

    @pl.when(used)
    def _():
        x = _unpack_bf16_pairs(x_ref[...]).astype(BF16)
        he = _silu_of_half(_dot(x, wg_ref[...])) * _dot(x, wu_ref[...])
        o_ref[...] = _pack_bf16_pairs(_dot(he.astype(BF16), wd_ref[...]))

    @pl.when(jnp.logical_not(used))
    def _():
        o_ref[...] = jnp.zeros_like(o_ref)


def _experts(xs, tile_expert, n_used, wg, wu, wd, layer):
    p = xs.shape[0]
    tm = MOE_TILE
    grid_spec = pltpu.PrefetchScalarGridSpec(
        num_scalar_prefetch=2,
        grid=(p // tm,),
        in_specs=[
            pl.BlockSpec((tm, D_MODEL // 2), lambda i, te, nu: (i, 0)),
            pl.BlockSpec((None, None, D_MODEL, D_EXPERT), lambda i, te, nu: (layer, te[i], 0, 0)),
            pl.BlockSpec((None, None, D_MODEL, D_EXPERT), lambda i, te, nu: (layer, te[i], 0, 0)),
            pl.BlockSpec((None, None, D_EXPERT, D_MODEL), lambda i, te, nu: (layer, te[i], 0, 0)),
        ],
        out_specs=pl.BlockSpec((tm, D_MODEL // 2), lambda i, te, nu: (i, 0)),
    )
    return pl.pallas_call(
        _expert_kernel,
        grid_spec=grid_spec,
        out_shape=jax.ShapeDtypeStruct((p, D_MODEL // 2), jnp.uint32),
        compiler_params=_params(("arbitrary",)),
        name="moe_experts",
    )(tile_expert, n_used, xs, wg, wu, wd)


def _residual_kernel(ya_ref, yb_ref, w_ref, x1_ref, mod_ref, nfin_ref, o_ref, *, last):
    w = w_ref[...]
    y = w[:, 0:1] * _unpack_bf16_pairs(ya_ref[...]) + w[:, 1:2] * _unpack_bf16_pairs(yb_ref[...])
    x2 = x1_ref[...] + mod_ref[5:6, :] * y
    if last:
        x2 = _rms(x2, nfin_ref[...])
    o_ref[...] = x2


def _residual(ya, yb, w, x1, mod, nfin, seq, last):
    n = x1.shape[0]
    tm = min(1024, seq)
    per_seq = seq // tm
    row = pl.BlockSpec((tm, D_MODEL), lambda i: (i, 0))
    packed = pl.BlockSpec((tm, D_MODEL // 2), lambda i: (i, 0))
    return pl.pallas_call(
        functools.partial(_residual_kernel, last=last),
        grid=(n // tm,),
        in_specs=[packed, packed, pl.BlockSpec((tm, 2), lambda i: (i, 0)), row,
                  pl.BlockSpec((None, N_MOD, D_MODEL), lambda i: (i // per_seq, 0, 0)),
                  pl.BlockSpec((1, D_MODEL), lambda i: (0, 0))],
        out_specs=row,
        out_shape=jax.ShapeDtypeStruct((n, D_MODEL), F32),
        compiler_params=_params(("arbitrary",)),
        name="moe_residual",
    )(ya, yb, w, x1, mod, nfin)


def _moe(h2, ri, rw, cnt, wg, wu, wd, layer, x1, mod, nfin, seq, last):
    n = h2.shape[0]
    tm = MOE_TILE
    p = 2 * n + N_EXPERTS * tm
    counts = cnt[:, 0].astype(jnp.int32)
    padded = (counts + tm - 1) // tm * tm
    ends = jnp.cumsum(padded)
    starts = ends - padded
    base = jnp.zeros_like(ri[0:2])
    for e in range(1, N_EXPERTS):
        base = jnp.where(ri[0:2] == e, starts[e], base)
    pos = base + ri[2:4]
    tile_start = jnp.arange(p // tm, dtype=jnp.int32) * tm
    tile_expert = jnp.minimum(jnp.sum(tile_start[:, None] >= ends[None, :], axis=1), N_EXPERTS - 1)
    n_used = (ends[-1:] // tm).astype(jnp.int32)
    xs = _sc_dispatch(h2, pos, p)
    ys = _experts(xs, tile_expert.astype(jnp.int32), n_used, wg, wu, wd, layer)
    ya, yb = _sc_combine(ys, pos)
    return _residual(ya, yb, rw.T, x1, mod, nfin, seq, last)


def _split_w_in(w):
    hq_hf, hi, hg, att, gates = w[..., :1536], w[..., 1536:2048], w[..., 2048:2560], w[..., 2560:3328], w[..., 3328:]
    wa = (0.5 * jnp.concatenate([hq_hf, hg], axis=-1)).astype(BF16)
    wb = jnp.concatenate([0.5 * gates, hi, att], axis=-1).astype(BF16)
    return wa, wb


def _mixer(x2d, mod_l, wts, l, bsz, seq):
    n = bsz * seq
    pa, pb, span = _inproj(x2d, mod_l, wts["norm_mix"][l:l + 1], wts["w_in_a"][l], wts["w_in_b"][l],
                           wts["hg_lb_fwd"], wts["hg_lb_bwd"], seq, l)
    pa3 = pa.reshape(bsz, seq, A_COLS)
    pb3 = pb.reshape(bsz, seq, B_COLS)
    safe_f = (span[:, 0, 0] <= HG_SAFE_SPAN).astype(jnp.int32)
    safe_b = (span[:, 4, 0] <= HG_SAFE_SPAN).astype(jnp.int32)
    o_f = _hgrn(safe_f, pa3, pb3, wts["hg_lb_fwd"], l, False)
    o_h = _hgrn(safe_b, pa3, pb3, wts["hg_lb_bwd"], l, True, o_f, wts["hg_norm"][l:l + 1])
    o_a = _attn(pb3, wts["bias"], wts["sink"][l])
    return _merge(o_h.reshape(n, HG_WIDTH), o_a.reshape(n, ATT_WIDTH), pb, x2d, mod_l,
                  wts["norm_ffn"][l:l + 1], wts["w_br_hgrn"][l], wts["w_br_att"][l],
                  wts["w_out"][l], wts["w_router_t"], wts["router_bias"], seq)


def kernel(x_prompt, x_sample, c_prompt, c_sample, w_ada, b_ada, norm_mix, norm_ffn, norm_final, w_in, hg_lb_fwd, hg_lb_bwd, hg_norm, att_sink, rel_bias, w_br_hgrn, w_br_att, w_out, w_router, router_bias, w_gate, w_up, w_down):
    depth = w_in.shape[0]
    bp, bs = c_prompt.shape[0], c_sample.shape[0]
    rows = -(-(bp + bs) // SUBLANES) * SUBLANES
    c_all = jnp.concatenate([c_prompt, c_sample, jnp.zeros((rows - bp - bs, D_MODEL), F32)], axis=0)
    mod = _ada(c_all, w_ada, b_ada).reshape(depth, rows, N_MOD, D_MODEL)
    bias, sink = _attn_tables(rel_bias, att_sink)
    w_in_a, w_in_b = _split_w_in(w_in)
    wts = {
        "norm_mix": norm_mix, "norm_ffn": norm_ffn, "norm_final": norm_final.reshape(1, D_MODEL),
        "w_in_a": w_in_a, "w_in_b": w_in_b,
        "hg_lb_fwd": hg_lb_fwd, "hg_lb_bwd": hg_lb_bwd, "hg_norm": hg_norm,
        "sink": sink, "bias": bias,
        "w_br_hgrn": w_br_hgrn.astype(BF16), "w_br_att": w_br_att.astype(BF16), "w_out": (0.5 * w_out).astype(BF16),
        "w_router_t": w_router.T,
        "router_bias": jnp.broadcast_to(router_bias[:, None], (N_EXPERTS, LANES)),
        "w_gate": (0.5 * w_gate).astype(BF16), "w_up": w_up.astype(BF16), "w_down": w_down.astype(BF16),
    }
    groups = [(x_prompt, mod[:, :bp]), (x_sample, mod[:, bp:bp + bs])]
    xs = [x.reshape(-1, D_MODEL) for x, _ in groups]
    for l in range(depth):
        mixed = [_mixer(xs[gi], mod_g[l], wts, l, x.shape[0], x.shape[1]) for gi, (x, mod_g) in enumerate(groups)]
        for gi, (x, mod_g) in enumerate(groups):
            x1, h2, ri, rw, cnt = mixed[gi]
            xs[gi] = _moe(h2, ri, rw, cnt, wts["w_gate"], wts["w_up"], wts["w_down"], l, x1, mod_g[l],
                          wts["norm_final"], x.shape[1], l == depth - 1)
    return tuple(y.reshape(x.shape) for y, (x, _) in zip(xs, groups))
```
